```python
import jax
import jax.numpy as jnp
from jax import lax
import numpy as np

D_MODEL = 1024
BATCH = 4
SEQ = 8192
DEPTH = 1

D_MIX = D_MODEL
N_HEADS = 8
N_KV = 2
HPG = N_HEADS // N_KV
HEAD_DIM = 64
D_ATTN = N_HEADS * HEAD_DIM
D_KV = N_KV * HEAD_DIM
D_CONV = D_MIX - D_ATTN
CONV_WIDTH = 31
CMP_LEN = 32
CMP_STRIDE = 16
CMP_HID = 256
SEL_LEN = 64
SEL_TOPN = 16
WINDOW = 512
Q_BLOCK = 128
N_EXPERTS = 256
TOP_K = 8
N_GROUPS = 8
TOPK_GROUPS = 4
D_EXPERT = 256
ROUTED_SCALE = 2.5
EXPERT_BLOCK = 128
LN_EPS = 1e-5
DEEPNORM_ALPHA = (2 * DEPTH) ** 0.25
DEEPNORM_BETA = (8 * DEPTH) ** -0.25
IN_SIZES = (D_ATTN, D_KV, D_KV, D_KV, D_KV, D_KV, D_KV, 3 * N_HEADS, 2 * D_CONV)
IN_SPLITS = tuple(int(v) for v in np.cumsum(IN_SIZES)[:-1])
D_IN = int(sum(IN_SIZES))

kernel_name = "nsa_conformer_moe_hybrid_layer"


def layer_norm(x, g, b):
    xf = x.astype(jnp.float32)
    mu = jnp.mean(xf, axis=-1, keepdims=True)
    var = jnp.mean(jnp.square(xf - mu), axis=-1, keepdims=True)
    return ((xf - mu) * lax.rsqrt(var + LN_EPS) * g + b).astype(x.dtype)


def masked_softmax(s, mask):
    s = jnp.where(mask, s.astype(jnp.float32), -jnp.inf)
    m = jnp.max(s, axis=-1, keepdims=True)
    m = jnp.where(jnp.isfinite(m), m, 0.0)
    p = jnp.exp(s - m)
    d = jnp.sum(p, axis=-1, keepdims=True)
    return p / jnp.where(d > 0, d, 1.0)


def compress(kv, pe, w1, w2):
    b, s = kv.shape[0], kv.shape[1]
    n_cmp = (s - CMP_LEN) // CMP_STRIDE + 1
    idx = np.arange(n_cmp)[:, None] * CMP_STRIDE + np.arange(CMP_LEN)[None, :]
    blocks = kv[:, idx] + pe[None, None, :, None, :]
    blocks = blocks.transpose(0, 3, 1, 2, 4).reshape(b, N_KV, n_cmp, CMP_LEN * HEAD_DIM)
    return jax.nn.gelu(blocks @ w1) @ w2


def nsa_attention(q, gates, k_cmp, v_cmp, k_slc, v_slc, k_win, v_win,
                  pe_k, pe_v, w_cmp_k1, w_cmp_k2, w_cmp_v1, w_cmp_v2):
    b, s = q.shape[0], q.shape[1]
    k_cmp = k_cmp.reshape(b, s, N_KV, HEAD_DIM)
    v_cmp = v_cmp.reshape(b, s, N_KV, HEAD_DIM)
    kc = compress(k_cmp, pe_k, w_cmp_k1, w_cmp_k2)
    vc = compress(v_cmp, pe_v, w_cmp_v1, w_cmp_v2)
    n_cmp = kc.shape[2]
    cmp_end = jnp.arange(n_cmp) * CMP_STRIDE + CMP_LEN - 1

    n_sel = s // SEL_LEN
    top_n = min(SEL_TOPN, n_sel)
    cs = np.arange(n_cmp) * CMP_STRIDE
    ss = np.arange(n_sel) * SEL_LEN
    overlap = jnp.asarray(((cs[:, None] < ss[None, :] + SEL_LEN) &
                           (cs[:, None] + CMP_LEN > ss[None, :])).astype(np.float32))
    kb = k_slc.reshape(b, n_sel, SEL_LEN, N_KV, HEAD_DIM).transpose(0, 3, 1, 2, 4)
    vb = v_slc.reshape(b, n_sel, SEL_LEN, N_KV, HEAD_DIM).transpose(0, 3, 1, 2, 4)
    pad = ((0, 0), (WINDOW, 0), (0, 0), (0, 0))
    kw = jnp.pad(k_win.reshape(b, s, N_KV, HEAD_DIM), pad)
    vw = jnp.pad(v_win.reshape(b, s, N_KV, HEAD_DIM), pad)

    n_qb = s // Q_BLOCK
    qb = q.reshape(b, n_qb, Q_BLOCK, N_KV, HPG, HEAD_DIM).transpose(1, 0, 2, 3, 4, 5)
    gb = gates.reshape(b, n_qb, Q_BLOCK, N_KV, HPG, 3).transpose(1, 0, 2, 3, 4, 5)
    scale = HEAD_DIM ** -0.5
    bidx = jnp.arange(b)[:, None, None, None]
    gidx = jnp.arange(N_KV)[None, :, None, None]
    blk = jnp.arange(n_sel)
    wrel = jnp.arange(WINDOW + Q_BLOCK)

    def query_block(args):
        i, q_i, g_i = args
        s0 = i * Q_BLOCK
        t = s0 + jnp.arange(Q_BLOCK)
        s_c = jnp.einsum('bqgpd,bgnd->bgpqn', q_i, kc) * scale
        p_c = masked_softmax(s_c, cmp_end[None, :] <= t[:, None])
        o_c = jnp.einsum('bgpqn,bgnd->bqgpd', p_c.astype(vc.dtype), vc)
        imp = jnp.einsum('bgpqn,ns->bgqs', p_c, overlap)
        cur = t // SEL_LEN
        forced = (blk[None, :] == 0) | (blk[None, :] == cur[:, None]) | (blk[None, :] == cur[:, None] - 1)
        causal = blk[None, :] <= cur[:, None]
        imp = jnp.where(forced, jnp.inf, jnp.where(causal, imp, -jnp.inf))
        top_s, top_i = lax.top_k(imp, top_n)
        k_sel = kb[bidx, gidx, top_i]
        v_sel = vb[bidx, gidx, top_i]
        kpos = top_i[..., None] * SEL_LEN + jnp.arange(SEL_LEN)
        m_s = (top_s[..., None] > -jnp.inf) & (kpos <= t[:, None, None])
        s_s = jnp.einsum('bqgpd,bgqnld->bgpqnl', q_i, k_sel) * scale
        sh = s_s.shape
        p_s = masked_softmax(s_s.reshape(sh[0], sh[1], sh[2], sh[3], top_n * SEL_LEN),
                             m_s.reshape(sh[0], sh[1], 1, sh[3], top_n * SEL_LEN)).reshape(sh)
        o_s = jnp.einsum('bgpqnl,bgqnld->bqgpd', p_s.astype(v_sel.dtype), v_sel)
        k_w = lax.dynamic_slice_in_dim(kw, s0, WINDOW + Q_BLOCK, axis=1)
        v_w = lax.dynamic_slice_in_dim(vw, s0, WINDOW + Q_BLOCK, axis=1)
        wpos = s0 - WINDOW + wrel
        m_w = (wpos[None, :] >= 0) & (wpos[None, :] <= t[:, None]) & (wpos[None, :] > t[:, None] - WINDOW)
        s_w = jnp.einsum('bqgpd,bkgd->bgpqk', q_i, k_w) * scale
        p_w = masked_softmax(s_w, m_w)
        o_w = jnp.einsum('bgpqk,bkgd->bqgpd', p_w.astype(v_w.dtype), v_w)
        g = jax.nn.sigmoid(g_i.astype(jnp.float32)).astype(o_c.dtype)
        return g[..., 0:1] * o_c + g[..., 1:2] * o_s + g[..., 2:3] * o_w

    out = lax.map(query_block, (jnp.arange(n_qb), qb, gb))
    return out.transpose(1, 0, 2, 3, 4, 5).reshape(b, s, D_ATTN)


def conformer_conv(u, w_dw, b_dw, ln_g, ln_b):
    a, gte = jnp.split(u, 2, axis=-1)
    glu = a * jax.nn.sigmoid(gte)
    y = lax.conv_general_dilated(glu, w_dw, window_strides=(1,), padding=[(CONV_WIDTH - 1, 0)],
                                 dimension_numbers=('NWC', 'WIO', 'NWC'),
                                 feature_group_count=D_CONV) + b_dw
    return jax.nn.silu(layer_norm(y, ln_g, ln_b))


def route(h, w_router, router_bias):
    t = h.shape[0]
    s = jax.nn.sigmoid(h.astype(jnp.float32) @ w_router.astype(jnp.float32))
    sel = s + router_bias.astype(jnp.float32)
    grp_score = jnp.sum(lax.top_k(sel.reshape(t, N_GROUPS, N_EXPERTS // N_GROUPS), 2)[0], axis=-1)
    _, grp_idx = lax.top_k(grp_score, TOPK_GROUPS)
    grp_mask = jnp.any(grp_idx[..., None] == jnp.arange(N_GROUPS), axis=1)
    sel = jnp.where(jnp.repeat(grp_mask, N_EXPERTS // N_GROUPS, axis=1), sel, -jnp.inf)
    _, idx = lax.top_k(sel, TOP_K)
    w = jnp.take_along_axis(s, idx, axis=-1)
    w = w / jnp.sum(w, axis=-1, keepdims=True) * ROUTED_SCALE
    return idx, w


def routed_experts(h, idx, w, w_gate, w_up, w_down):
    t, d = h.shape
    n_assign = t * TOP_K
    flat_e = idx.reshape(-1)
    order = jnp.argsort(flat_e)
    sorted_e = flat_e[order]
    counts = jnp.bincount(flat_e, length=N_EXPERTS)
    padded = (counts + EXPERT_BLOCK - 1) // EXPERT_BLOCK * EXPERT_BLOCK
    pad_end = jnp.cumsum(padded)
    pad_start = pad_end - padded
    grp_start = jnp.cumsum(counts) - counts
    dest = pad_start[sorted_e] + jnp.arange(n_assign) - grp_start[sorted_e]
    n_blk = -(-(n_assign + N_EXPERTS * (EXPERT_BLOCK - 1)) // EXPERT_BLOCK)
    n_buf = n_blk * EXPERT_BLOCK
    buf_tok = jnp.zeros((n_buf,), jnp.int32).at[dest].set((order // TOP_K).astype(jnp.int32))
    buf_w = jnp.zeros((n_buf,), h.dtype).at[dest].set(w.reshape(-1)[order].astype(h.dtype))
    blk_e = jnp.minimum(jnp.searchsorted(pad_end, jnp.arange(n_blk) * EXPERT_BLOCK, side='right'),
                        N_EXPERTS - 1)

    def expert_block(args):
        e, tok, wt = args
        xb = h[tok]
        hid = jax.nn.silu(xb @ w_gate[e]) * (xb @ w_up[e])
        return (hid @ w_down[e]) * wt[:, None]

    y = lax.map(expert_block, (blk_e, buf_tok.reshape(n_blk, EXPERT_BLOCK),
                               buf_w.reshape(n_blk, EXPERT_BLOCK)))
    return jax.ops.segment_sum(y.reshape(n_buf, d), buf_tok, num_segments=t)


def setup_inputs(seed: int = 0) -> dict:
    key = jax.random.key(seed)
    ks = jax.random.split(key, 32)
    L, D, E, F = DEPTH, D_MODEL, N_EXPERTS, D_EXPERT
    nrm = lambda k, shape, sc: jax.random.normal(k, shape, jnp.float32) * sc
    starts = (0,) + IN_SPLITS
    col_scale = np.ones((D_IN,), np.float32)
    for j in (2, 4, 6):
        col_scale[starts[j]:starts[j] + D_KV] = DEEPNORM_BETA
    return {
        "x": nrm(ks[0], (BATCH, SEQ, D), 1.0),
        "c": nrm(ks[1], (BATCH, D), 1.0),
        "w_ada": nrm(ks[2], (L, D, 6 * D), D ** -0.5),
        "b_ada": nrm(ks[3], (L, 6 * D), 0.02),
        "w_in": nrm(ks[4], (L, D, D_IN), D ** -0.5) * jnp.asarray(col_scale),
        "pe_k": nrm(ks[5], (L, CMP_LEN, HEAD_DIM), 0.1),
        "pe_v": nrm(ks[6], (L, CMP_LEN, HEAD_DIM), 0.1),
        "w_cmp_k1": nrm(ks[7], (L, CMP_LEN * HEAD_DIM, CMP_HID), (CMP_LEN * HEAD_DIM) ** -0.5),
        "w_cmp_k2": nrm(ks[8], (L, CMP_HID, HEAD_DIM), CMP_HID ** -0.5),
        "w_cmp_v1": nrm(ks[9], (L, CMP_LEN * HEAD_DIM, CMP_HID), (CMP_LEN * HEAD_DIM) ** -0.5),
        "w_cmp_v2": nrm(ks[10], (L, CMP_HID, HEAD_DIM), CMP_HID ** -0.5),
        "w_dw": nrm(ks[11], (L, CONV_WIDTH, 1, D_CONV), CONV_WIDTH ** -0.5),
        "b_dw": nrm(ks[12], (L, D_CONV), 0.02),
        "conv_ln_g": 1.0 + nrm(ks[13], (L, D_CONV), 0.02),
        "conv_ln_b": nrm(ks[14], (L, D_CONV), 0.02),
        "w_out": nrm(ks[15], (L, D_MIX, D), D_MIX ** -0.5 * DEEPNORM_BETA),
        "ln1_g": 1.0 + nrm(ks[16], (L, D), 0.02),
        "ln1_b": nrm(ks[17], (L, D), 0.02),
        "w_router": nrm(ks[18], (L, D, E), D ** -0.5),
        "router_bias": nrm(ks[19], (L, E), 0.01),
        "w_e_gate": nrm(ks[20], (L, E, D, F), D ** -0.5),
        "w_e_up": nrm(ks[21], (L, E, D, F), D ** -0.5),
        "w_e_down": nrm(ks[22], (L, E, F, D), F ** -0.5 * DEEPNORM_BETA),
        "w_s_gate": nrm(ks[23], (L, D, F), D ** -0.5),
        "w_s_up": nrm(ks[24], (L, D, F), D ** -0.5),
        "w_s_down": nrm(ks[25], (L, F, D), F ** -0.5 * DEEPNORM_BETA),
        "ln2_g": 1.0 + nrm(ks[26], (L, D), 0.02),
        "ln2_b": nrm(ks[27], (L, D), 0.02),
    }


def reference(x, c, w_ada, b_ada, w_in, pe_k, pe_v, w_cmp_k1, w_cmp_k2, w_cmp_v1, w_cmp_v2,
              w_dw, b_dw, conv_ln_g, conv_ln_b, w_out, ln1_g, ln1_b, w_router, router_bias,
              w_e_gate, w_e_up, w_e_down, w_s_gate, w_s_up, w_s_down, ln2_g, ln2_b):
    b, s, d = x.shape
    for l in range(DEPTH):
        mod = jax.nn.silu(c) @ w_ada[l] + b_ada[l]
        sh1, sc1, g1, sh2, sc2, g2 = [m[:, None, :] for m in jnp.split(mod, 6, axis=-1)]
        h = x * (1 + sc1) + sh1
        u = h @ w_in[l]
        q, k_c, v_c, k_s, v_s, k_w, v_w, gates, conv_in = jnp.split(u, IN_SPLITS, axis=-1)
        attn = nsa_attention(q, gates, k_c, v_c, k_s, v_s, k_w, v_w,
                             pe_k[l], pe_v[l], w_cmp_k1[l], w_cmp_k2[l], w_cmp_v1[l], w_cmp_v2[l])
        conv = conformer_conv(conv_in, w_dw[l], b_dw[l], conv_ln_g[l], conv_ln_b[l])
        mix = jnp.concatenate([attn, conv], axis=-1) @ w_out[l]
        x = layer_norm(DEEPNORM_ALPHA * x + g1 * mix, ln1_g[l], ln1_b[l])
        h = (x * (1 + sc2) + sh2).reshape(b * s, d)
        idx, wts = route(h, w_router[l], router_bias[l])
        shared = (jax.nn.silu(h @ w_s_gate[l]) * (h @ w_s_up[l])) @ w_s_down[l]
        routed = routed_experts(h, idx, wts, w_e_gate[l], w_e_up[l], w_e_down[l])
        y = (shared + routed.astype(shared.dtype)).reshape(b, s, d)
        x = layer_norm(DEEPNORM_ALPHA * x + g2 * y, ln2_g[l], ln2_b[l])
    return x
```

```python
import functools

import jax
import jax.numpy as jnp
import numpy as np
from jax import lax
from jax.experimental import pallas as pl
from jax.experimental.pallas import tpu as pltpu

N_HEADS = 8
N_KV = 2
HPG = N_HEADS // N_KV
HEAD_DIM = 64
D_ATTN = N_HEADS * HEAD_DIM
D_KV = N_KV * HEAD_DIM
CONV_WIDTH = 31
CMP_LEN = 32
CMP_STRIDE = 16
CMP_HID = 256
SEL_LEN = 64
SEL_TOPN = 16
WINDOW = 512
Q_BLOCK = 128
N_EXPERTS = 256
TOP_K = 8
N_GROUPS = 8
TOPK_GROUPS = 4
ROUTED_SCALE = 2.5
LN_EPS = 1e-5
DEPTH = 1
DEEPNORM_ALPHA = (2 * DEPTH) ** 0.25

LANES = 128
ROW_BLOCK = 256
NEG = -1e30
HIGHEST = lax.Precision.HIGHEST
VMEM_LIMIT = 48 * 1024 * 1024

f32 = jnp.float32
bf16 = jnp.bfloat16
i32 = jnp.int32


def _params(sem, vmem=VMEM_LIMIT):
    return pltpu.CompilerParams(dimension_semantics=sem, vmem_limit_bytes=vmem)


def _sigmoid(v):
    return 1.0 / (1.0 + jnp.exp(-v))


def _silu(v):
    return v * _sigmoid(v)


def _layer_norm(v, g, b):
    mu = jnp.mean(v, axis=-1, keepdims=True)
    var = jnp.mean(jnp.square(v - mu), axis=-1, keepdims=True)
    return (v - mu) * lax.rsqrt(var + LN_EPS) * g + b


def _dot_nt(a, b):
    return lax.dot_general(a, b, (((1,), (1,)), ((), ())), preferred_element_type=f32)


def _ada_kernel(c_ref, w_ref, b_ref, o_ref):
    c = c_ref[...]
    o_ref[...] = jnp.dot(_silu(c), w_ref[...], precision=HIGHEST,
                         preferred_element_type=f32) + b_ref[...]


def _ada(c, w_ada, b_ada):
    b, d = c.shape
    n = w_ada.shape[1]
    rows = 8
    c_pad = jnp.zeros((rows, d), f32).at[:b].set(c)
    tn = 1024
    out = pl.pallas_call(
        _ada_kernel,
        grid=(n // tn,),
        in_specs=[pl.BlockSpec((rows, d), lambda j: (0, 0)),
                  pl.BlockSpec((d, tn), lambda j: (0, j)),
                  pl.BlockSpec((1, tn), lambda j: (0, j))],
        out_specs=pl.BlockSpec((rows, tn), lambda j: (0, j)),
        out_shape=jax.ShapeDtypeStruct((rows, n), f32),
        compiler_params=_params(("arbitrary",)),
        name="ada",
    )(c_pad, w_ada, b_ada.reshape(1, n))
    return out[:b].reshape(b, 6, d)


def _in_proj_kernel(x_ref, mod_ref, wq_ref, wc_ref, ws_ref, wg_ref, wv_ref,
                    q_ref, kvc_ref, kv4_ref, g_ref, cv_ref):
    m = mod_ref[0]
    h = (x_ref[0] * (1.0 + m[1:2]) + m[0:1]).astype(bf16)
    q = jnp.dot(h, wq_ref[...], preferred_element_type=f32)
    q_ref[0] = (q * (HEAD_DIM ** -0.5)).astype(bf16)
    kvc_ref[0] = jnp.dot(h, wc_ref[...], preferred_element_type=f32)
    kv4 = jnp.dot(h, ws_ref[...], preferred_element_type=f32).astype(bf16)
    for j in range(4):
        for g in range(N_KV):
            off = (j * N_KV + g) * HEAD_DIM
            kv4_ref[0, j, g] = kv4[:, off:off + HEAD_DIM]
    g_ref[0] = jnp.dot(h, wg_ref[...], preferred_element_type=f32)
    cv_ref[0] = jnp.dot(h, wv_ref[...], preferred_element_type=f32)


def _in_proj(x, mod, w_in, tm):
    b, s, d = x.shape
    o = 0
    wq = w_in[:, o:o + D_ATTN]; o += D_ATTN
    wkc = w_in[:, o:o + 2 * D_KV]; o += 2 * D_KV
    wks = w_in[:, o:o + 4 * D_KV]; o += 4 * D_KV
    wg = w_in[:, o:o + 3 * N_HEADS]; o += 3 * N_HEADS
    wcv = w_in[:, o:]
    d_conv2 = wcv.shape[1]
    wg = jnp.zeros((d, LANES), f32).at[:, :3 * N_HEADS].set(wg)
    ws = [w.astype(bf16) for w in (wq, wkc, wks, wg, wcv)]
    full = lambda a: pl.BlockSpec(a.shape, lambda bi, i: (0, 0))
    return pl.pallas_call(
        _in_proj_kernel,
        grid=(b, s // tm),
        in_specs=[pl.BlockSpec((1, tm, d), lambda bi, i: (bi, i, 0)),
                  pl.BlockSpec((1, 6, d), lambda bi, i: (bi, 0, 0))] + [full(w) for w in ws],
        out_specs=[pl.BlockSpec((1, tm, D_ATTN), lambda bi, i: (bi, i, 0)),
                   pl.BlockSpec((1, tm, 2 * D_KV), lambda bi, i: (bi, i, 0)),
                   pl.BlockSpec((1, 4, N_KV, tm, HEAD_DIM), lambda bi, i: (bi, 0, 0, i, 0)),
                   pl.BlockSpec((1, tm, LANES), lambda bi, i: (bi, i, 0)),
                   pl.BlockSpec((1, tm, d_conv2), lambda bi, i: (bi, i, 0))],
        out_shape=[jax.ShapeDtypeStruct((b, s, D_ATTN), bf16),
                   jax.ShapeDtypeStruct((b, s, 2 * D_KV), f32),
                   jax.ShapeDtypeStruct((b, 4, N_KV, s, HEAD_DIM), bf16),
                   jax.ShapeDtypeStruct((b, s, LANES), f32),
                   jax.ShapeDtypeStruct((b, s, d_conv2), f32)],
        compiler_params=_params(("parallel", "parallel")),
        name="in_proj",
    )(x, mod, *ws)


def _compress_kernel(c_ref, pe_ref, w1_ref, w2_ref, o_ref):
    c = c_ref[0, 0, 0]
    n_chunk = c.shape[0]
    a = jnp.dot((c + pe_ref[0, 0]).astype(bf16), w1_ref[0, 0], preferred_element_type=f32)
    bm = jnp.dot((c + pe_ref[0, 1]).astype(bf16), w1_ref[0, 1], preferred_element_type=f32)
    hid = a + pltpu.roll(bm, n_chunk - 1, 0)
    act = 0.5 * hid * (1.0 + jnp.tanh(0.7978845608028654 * (hid + 0.044715 * (hid * hid * hid))))
    o_ref[0, 0, 0] = jnp.dot(act.astype(bf16), w2_ref[0], preferred_element_type=f32).astype(bf16)


def _compress(kvc, pe, w1, w2):
    b, s, _ = kvc.shape
    n_chunk = s // CMP_STRIDE
    half = CMP_STRIDE * HEAD_DIM
    c = kvc.reshape(b, n_chunk, CMP_STRIDE, 2, N_KV, HEAD_DIM).transpose(0, 3, 4, 1, 2, 5)
    c = c.reshape(b, 2, N_KV, n_chunk, half)
    pe2 = pe.reshape(2, 2, 1, half)
    w1h = w1.reshape(2, 2, half, CMP_HID).astype(bf16)
    w2b = w2.astype(bf16)
    return pl.pallas_call(
        _compress_kernel,
        grid=(b, 2, N_KV),
        in_specs=[pl.BlockSpec((1, 1, 1, n_chunk, half), lambda bi, j, g: (bi, j, g, 0, 0)),
                  pl.BlockSpec((1, 2, 1, half), lambda bi, j, g: (j, 0, 0, 0)),
                  pl.BlockSpec((1, 2, half, CMP_HID), lambda bi, j, g: (j, 0, 0, 0)),
                  pl.BlockSpec((1, CMP_HID, HEAD_DIM), lambda bi, j, g: (j, 0, 0))],
        out_specs=pl.BlockSpec((1, 1, 1, n_chunk, HEAD_DIM), lambda bi, j, g: (bi, j, g, 0, 0)),
        out_shape=jax.ShapeDtypeStruct((b, 2, N_KV, n_chunk, HEAD_DIM), bf16),
        compiler_params=_params(("parallel", "parallel", "parallel")),
        name="compress",
    )(c, pe2, w1h, w2b)


def _masked_softmax3(s3, mask):
    sm = jnp.where(mask[None], s3, NEG)
    mx = jnp.max(sm, axis=-1, keepdims=True)
    p = jnp.where(mask[None], jnp.exp(sm - mx), 0.0)
    d = jnp.sum(p, axis=-1, keepdims=True)
    return p / jnp.where(d > 0, d, 1.0)


def _attn_kernel(q_ref, g_ref, c_ref, kv_ref, ov_ref, o_ref, *, seq, tk, top_n):
    i = pl.program_id(2)
    s0 = i * Q_BLOCK
    n_cmp_rows = c_ref.shape[3]
    n_sel = seq // SEL_LEN
    rows = HPG * Q_BLOCK

    qb = q_ref[0]
    q_all = jnp.concatenate([qb[:, p * HEAD_DIM:(p + 1) * HEAD_DIM] for p in range(HPG)], axis=0)
    t_col = s0 + lax.broadcasted_iota(i32, (Q_BLOCK, 1), 0)

    kc = c_ref[0, 0, 0]
    vc = c_ref[0, 1, 0]
    s_c = _dot_nt(q_all, kc).reshape(HPG, Q_BLOCK, n_cmp_rows)
    cmp_end = lax.broadcasted_iota(i32, (Q_BLOCK, n_cmp_rows), 1) * CMP_STRIDE + (CMP_LEN - 1)
    p_c = _masked_softmax3(s_c, cmp_end <= t_col)
    o_c = jnp.dot(p_c.reshape(rows, n_cmp_rows).astype(bf16), vc, preferred_element_type=f32)

    p_sum = p_c[0]
    for p in range(1, HPG):
        p_sum = p_sum + p_c[p]
    imp = jnp.dot(p_sum, ov_ref[...], precision=HIGHEST, preferred_element_type=f32)
    blk = lax.broadcasted_iota(i32, (Q_BLOCK, n_sel), 1)
    cur = t_col >> 6
    forced = (blk == 0) | (blk == cur) | (blk == cur - 1)
    vals = jnp.where(forced, jnp.inf, jnp.where(blk <= cur, imp, -jnp.inf))
    sel = jnp.zeros((Q_BLOCK, n_sel), f32)
    for _ in range(top_n):
        mx = jnp.max(vals, axis=-1, keepdims=True)
        first = jnp.min(jnp.where(vals == mx, blk, n_sel), axis=-1, keepdims=True)
        pick = blk == first
        sel = jnp.where(pick & (mx > -jnp.inf), 1.0, sel)
        vals = jnp.where(pick, -jnp.inf, vals)
    sel_b = sel.astype(bf16)

    blocks_per_tile = tk // SEL_LEN

    def sel_body(kt, carry):
        m_i, l_i, acc = carry
        k0 = pl.multiple_of(kt * tk, tk)
        k = kv_ref[0, 0, 0, pl.ds(k0, tk), :]
        v = kv_ref[0, 1, 0, pl.ds(k0, tk), :]
        s3 = _dot_nt(q_all, k).reshape(HPG, Q_BLOCK, tk)
        e_row = lax.broadcasted_iota(i32, (n_sel, tk), 0)
        e_col = lax.broadcasted_iota(i32, (n_sel, tk), 1)
        expand = jnp.where(e_row == kt * blocks_per_tile + (e_col >> 6), 1.0, 0.0).astype(bf16)
        chosen = jnp.dot(sel_b, expand, preferred_element_type=f32) > 0.5
        kpos = k0 + lax.broadcasted_iota(i32, (Q_BLOCK, tk), 1)
        msk = (chosen & (kpos <= t_col))[None]
        sm = jnp.where(msk, s3, NEG)
        m_new = jnp.maximum(m_i, jnp.max(sm, axis=-1, keepdims=True))
        p = jnp.where(msk, jnp.exp(sm - m_new), 0.0)
        alpha = jnp.exp(m_i - m_new)
        l_new = alpha * l_i + jnp.sum(p, axis=-1, keepdims=True)
        pv = jnp.dot(p.reshape(rows, tk).astype(bf16), v, preferred_element_type=f32)
        acc_new = alpha.reshape(rows, 1) * acc + pv
        return m_new, l_new, acc_new

    n_kt = (s0 + Q_BLOCK + tk - 1) // tk
    init = (jnp.full((HPG, Q_BLOCK, 1), NEG, f32), jnp.zeros((HPG, Q_BLOCK, 1), f32),
            jnp.zeros((rows, HEAD_DIM), f32))
    _, l_s, acc_s = lax.fori_loop(0, n_kt, sel_body, init)
    l_s = l_s.reshape(rows, 1)
    o_s = acc_s / jnp.where(l_s > 0, l_s, 1.0)

    span = WINDOW + Q_BLOCK
    w0 = pl.multiple_of(jnp.maximum(s0 - WINDOW, 0), Q_BLOCK)
    kw = kv_ref[0, 2, 0, pl.ds(w0, span), :]
    vw = kv_ref[0, 3, 0, pl.ds(w0, span), :]
    wpos = w0 + lax.broadcasted_iota(i32, (Q_BLOCK, span), 1)
    s_w = _dot_nt(q_all, kw).reshape(HPG, Q_BLOCK, span)
    p_w = _masked_softmax3(s_w, (wpos <= t_col) & (wpos > t_col - WINDOW))
    o_w = jnp.dot(p_w.reshape(rows, span).astype(bf16), vw, preferred_element_type=f32)

    gate = _sigmoid(g_ref[0, 0])
    outs = []
    for p in range(HPG):
        r = slice(p * Q_BLOCK, (p + 1) * Q_BLOCK)
        outs.append(gate[:, 3 * p:3 * p + 1] * o_c[r] + gate[:, 3 * p + 1:3 * p + 2] * o_s[r]
                    + gate[:, 3 * p + 2:3 * p + 3] * o_w[r])
    o_ref[0] = jnp.concatenate(outs, axis=1).astype(bf16)


def _attention(q, gates, cmp_kv, kv4):
    b, s, _ = q.shape
    n_chunk = cmp_kv.shape[3]
    n_sel = s // SEL_LEN
    top_n = min(SEL_TOPN, n_sel)
    tk = min(512, s)
    cs = np.arange(n_chunk) * CMP_STRIDE
    ss = np.arange(n_sel) * SEL_LEN
    overlap = ((cs[:, None] < ss[None, :] + SEL_LEN) & (cs[:, None] + CMP_LEN > ss[None, :]))
    overlap[(s - CMP_LEN) // CMP_STRIDE + 1:] = False
    overlap = jnp.asarray(overlap.astype(np.float32))
    gw = HPG * HEAD_DIM
    return pl.pallas_call(
        functools.partial(_attn_kernel, seq=s, tk=tk, top_n=top_n),
        grid=(b, N_KV, s // Q_BLOCK),
        in_specs=[pl.BlockSpec((1, Q_BLOCK, gw), lambda bi, g, i: (bi, i, g)),
                  pl.BlockSpec((1, 1, Q_BLOCK, 3 * HPG), lambda bi, g, i: (bi, g, i, 0)),
                  pl.BlockSpec((1, 2, 1, n_chunk, HEAD_DIM), lambda bi, g, i: (bi, 0, g, 0, 0)),
                  pl.BlockSpec((1, 4, 1, s, HEAD_DIM), lambda bi, g, i: (bi, 0, g, 0, 0)),
                  pl.BlockSpec((n_chunk, n_sel), lambda bi, g, i: (0, 0))],
        out_specs=pl.BlockSpec((1, Q_BLOCK, gw), lambda bi, g, i: (bi, i, g)),
        out_shape=jax.ShapeDtypeStruct((b, s, D_ATTN), bf16),
        compiler_params=_params(("parallel", "parallel", "arbitrary")),
        name="attention",
    )(q, gates, cmp_kv, kv4, overlap)


CONV_HALO = 32
CONV_ROWS = 32


def _conv_kernel(cur_ref, prev_ref, w_ref, b_ref, g_ref, bb_ref, o_ref, glu_ref):
    i = pl.program_id(1)
    ts = cur_ref.shape[1]
    dc = o_ref.shape[2]
    cur = cur_ref[0]
    prev = prev_ref[0]
    glu_prev = prev[:, :dc] * _sigmoid(prev[:, dc:])
    glu_ref[0:CONV_HALO] = jnp.where(i == 0, 0.0, glu_prev)
    glu_ref[CONV_HALO:CONV_HALO + ts] = cur[:, :dc] * _sigmoid(cur[:, dc:])
    lead = CONV_HALO - (CONV_WIDTH - 1)

    def chunk(r, _):
        r0 = pl.multiple_of(r * CONV_ROWS, CONV_ROWS)
        win = glu_ref[pl.ds(r0, CONV_ROWS + CONV_HALO), :]
        acc = jnp.zeros((CONV_ROWS, dc), f32)
        for j in range(CONV_WIDTH):
            acc = acc + w_ref[j:j + 1, :] * win[lead + j:lead + j + CONV_ROWS]
        y = _layer_norm(acc + b_ref[...], g_ref[...], bb_ref[...])
        o_ref[0, pl.ds(r0, CONV_ROWS), :] = _silu(y).astype(bf16)
        return 0

    lax.fori_loop(0, ts // CONV_ROWS, chunk, 0)


def _conv(conv_in, w_dw, b_dw, ln_g, ln_b, ts):
    b, s, dc2 = conv_in.shape
    dc = dc2 // 2
    per = ts // CONV_HALO
    row = lambda a: a.reshape(1, dc)
    return pl.pallas_call(
        _conv_kernel,
        grid=(b, s // ts),
        in_specs=[pl.BlockSpec((1, ts, dc2), lambda bi, i: (bi, i, 0)),
                  pl.BlockSpec((1, CONV_HALO, dc2), lambda bi, i: (bi, jnp.maximum(i * per - 1, 0), 0)),
                  pl.BlockSpec((CONV_WIDTH, dc), lambda bi, i: (0, 0)),
                  pl.BlockSpec((1, dc), lambda bi, i: (0, 0)),
                  pl.BlockSpec((1, dc), lambda bi, i: (0, 0)),
                  pl.BlockSpec((1, dc), lambda bi, i: (0, 0))],
        out_specs=pl.BlockSpec((1, ts, dc), lambda bi, i: (bi, i, 0)),
        out_shape=jax.ShapeDtypeStruct((b, s, dc), bf16),
        scratch_shapes=[pltpu.VMEM((CONV_HALO + ts, dc), f32)],
        compiler_params=_params(("parallel", "parallel")),
        name="conv",
    )(conv_in, conv_in, w_dw.reshape(CONV_WIDTH, dc), row(b_dw), row(ln_g), row(ln_b))


def _mix_route_kernel(a_ref, cv_ref, x_ref, mod_ref, wo_ref, lg_ref, lb_ref, wr_ref, rb_ref,
                      x1_ref, h2_ref, idx_ref, wt_ref, cnt_ref):
    step = pl.program_id(0)
    tm = x_ref.shape[0]
    m = mod_ref[0]
    da = a_ref.shape[1]
    mix = (jnp.dot(a_ref[...], wo_ref[0:da, :], preferred_element_type=f32)
           + jnp.dot(cv_ref[...], wo_ref[da:, :], preferred_element_type=f32))
    x1 = _layer_norm(DEEPNORM_ALPHA * x_ref[...] + m[2:3] * mix, lg_ref[...], lb_ref[...])
    x1_ref[...] = x1
    h2 = x1 * (1.0 + m[4:5]) + m[3:4]
    h2_ref[...] = h2

    score = _sigmoid(jnp.dot(h2, wr_ref[...], precision=HIGHEST, preferred_element_type=f32))
    sel = score + rb_ref[...]
    lane = lax.broadcasted_iota(i32, (tm, N_EXPERTS), 1)
    grp = lane >> 5
    gs = []
    for g in range(N_GROUPS):
        v = jnp.where(grp == g, sel, -jnp.inf)
        m1 = jnp.max(v, axis=-1, keepdims=True)
        i1 = jnp.min(jnp.where(v == m1, lane, N_EXPERTS), axis=-1, keepdims=True)
        m2 = jnp.max(jnp.where(lane == i1, -jnp.inf, v), axis=-1, keepdims=True)
        gs.append(m1 + m2)
    gmask = jnp.zeros((tm, N_EXPERTS), jnp.bool_)
    for g in range(N_GROUPS):
        rank = jnp.zeros((tm, 1), i32)
        for o in range(N_GROUPS):
            if o == g:
                continue
            beats = (gs[o] > gs[g]) | (gs[o] == gs[g]) if o < g else gs[o] > gs[g]
            rank = rank + beats.astype(i32)
        gmask = gmask | ((grp == g) & (rank < TOPK_GROUPS))
    cand = jnp.where(gmask, sel, -jnp.inf)
    lane_o = lax.broadcasted_iota(i32, (tm, LANES), 1)
    idx_out = jnp.zeros((tm, LANES), i32)
    wt_out = jnp.zeros((tm, LANES), f32)
    picked = jnp.zeros((tm, N_EXPERTS), f32)
    w_sum = jnp.zeros((tm, 1), f32)
    for k in range(TOP_K):
        mx = jnp.max(cand, axis=-1, keepdims=True)
        ik = jnp.min(jnp.where(cand == mx, lane, N_EXPERTS), axis=-1, keepdims=True)
        pick = lane == ik
        wk = jnp.sum(jnp.where(pick, score, 0.0), axis=-1, keepdims=True)
        cand = jnp.where(pick, -jnp.inf, cand)
        picked = jnp.where(pick, 1.0, picked)
        idx_out = jnp.where(lane_o == k, ik, idx_out)
        wt_out = jnp.where(lane_o == k, wk, wt_out)
        w_sum = w_sum + wk
    idx_ref[...] = idx_out
    wt_ref[...] = wt_out / w_sum * ROUTED_SCALE

    @pl.when(step == 0)
    def _():
        cnt_ref[...] = jnp.zeros_like(cnt_ref)

    cnt_ref[...] += jnp.sum(picked, axis=0, keepdims=True)


def _mix_route(attn, conv, x2, mod, w_out, ln_g, ln_b, w_router, router_bias, s, tm):
    t, d = x2.shape
    per = s // tm
    da = attn.shape[1]
    row = lambda a: a.reshape(1, -1)
    tile = lambda w: pl.BlockSpec((tm, w), lambda i: (i, 0))
    full = lambda a: pl.BlockSpec(a.shape, lambda i: (0,) * a.ndim)
    args = (attn, conv, x2, mod, w_out.astype(bf16), row(ln_g), row(ln_b), w_router, row(router_bias))
    return pl.pallas_call(
        _mix_route_kernel,
        grid=(t // tm,),
        in_specs=[tile(da), tile(conv.shape[1]), tile(d),
                  pl.BlockSpec((1, 6, d), lambda i: (i // per, 0, 0))] + [full(a) for a in args[4:]],
        out_specs=[tile(d), tile(d), tile(LANES), tile(LANES),
                   pl.BlockSpec((1, N_EXPERTS), lambda i: (0, 0))],
        out_shape=[jax.ShapeDtypeStruct((t, d), f32), jax.ShapeDtypeStruct((t, d), f32),
                   jax.ShapeDtypeStruct((t, LANES), i32), jax.ShapeDtypeStruct((t, LANES), f32),
                   jax.ShapeDtypeStruct((1, N_EXPERTS), f32)],
        compiler_params=_params(("arbitrary",)),
        name="mix_route",
    )(*args)


def _positions_kernel(idx_ref, start_ref, o_ref, run_ref):
    step = pl.program_id(0)
    tm = idx_ref.shape[0]

    @pl.when(step == 0)
    def _():
        run_ref[...] = jnp.zeros_like(run_ref)

    idx = idx_ref[...]
    lane = lax.broadcasted_iota(i32, (tm, N_EXPERTS), 1)
    onehot = jnp.zeros((tm, N_EXPERTS), f32)
    for k in range(TOP_K):
        onehot = jnp.where(lane == idx[:, k:k + 1], 1.0, onehot)
    r = lax.broadcasted_iota(i32, (tm, tm), 0)
    c = lax.broadcasted_iota(i32, (tm, tm), 1)
    below = jnp.where(r > c, 1.0, 0.0).astype(bf16)
    prior = jnp.dot(below, onehot.astype(bf16), preferred_element_type=f32)
    pos = prior + run_ref[...] + start_ref[...]
    lane_o = lax.broadcasted_iota(i32, (tm, LANES), 1)
    out = jnp.zeros((tm, LANES), i32)
    for k in range(TOP_K):
        dk = jnp.sum(jnp.where(lane == idx[:, k:k + 1], pos, 0.0), axis=-1, keepdims=True)
        out = jnp.where(lane_o == k, dk.astype(i32), out)
    o_ref[...] = out
    run_ref[...] += jnp.sum(onehot, axis=0, keepdims=True)


def _positions(idx, seg_start, tm):
    t = idx.shape[0]
    return pl.pallas_call(
        _positions_kernel,
        grid=(t // tm,),
        in_specs=[pl.BlockSpec((tm, LANES), lambda i: (i, 0)),
                  pl.BlockSpec((1, N_EXPERTS), lambda i: (0, 0))],
        out_specs=pl.BlockSpec((tm, LANES), lambda i: (i, 0)),
        out_shape=jax.ShapeDtypeStruct((t, LANES), i32),
        scratch_shapes=[pltpu.VMEM((1, N_EXPERTS), f32)],
        compiler_params=_params(("arbitrary",)),
        name="positions",
    )(idx, seg_start)


def _dispatch_kernel(dest_hbm, h_ref, zero_hbm, o_hbm, dest_smem, sem_i, sem):
    del zero_hbm
    step = pl.program_id(0)
    n = dest_smem.shape[0]
    load = pltpu.make_async_copy(dest_hbm.at[pl.ds(pl.multiple_of(step * n, n), n)], dest_smem, sem_i)
    load.start()
    load.wait()

    def row_copy(j):
        return pltpu.make_async_copy(h_ref.at[pl.ds(j >> 3, 1)], o_hbm.at[pl.ds(dest_smem[j], 1)], sem)

    def issue(j, _):
        row_copy(j).start()
        return 0

    def drain(j, _):
        row_copy(j).wait()
        return 0

    lax.fori_loop(0, n, issue, 0)
    lax.fori_loop(0, n, drain, 0)


def _dispatch(dest_flat, h2, n_buf, tm):
    t, d = h2.shape
    zeros = jnp.zeros((n_buf, d), f32)
    return pl.pallas_call(
        _dispatch_kernel,
        grid=(t // tm,),
        in_specs=[pl.BlockSpec(memory_space=pl.ANY),
                  pl.BlockSpec((tm, d), lambda i: (i, 0)),
                  pl.BlockSpec(memory_space=pl.ANY)],
        out_specs=pl.BlockSpec(memory_space=pl.ANY),
        out_shape=jax.ShapeDtypeStruct((n_buf, d), f32),
        scratch_shapes=[pltpu.SMEM((tm * TOP_K,), i32), pltpu.SemaphoreType.DMA, pltpu.SemaphoreType.DMA],
        input_output_aliases={2: 0},
        compiler_params=_params(("arbitrary",)),
        name="dispatch",
    )(dest_flat, h2, zeros)


def _experts_kernel(be_ref, nu_ref, x_ref, wg_ref, wu_ref, wd_ref, o_ref, wg_s, wu_s, wd_s):
    j = pl.program_id(0)
    prev = be_ref[jnp.maximum(j - 1, 0)]
    used = j < nu_ref[0]

    @pl.when(used & ((j == 0) | (be_ref[j] != prev)))
    def _():
        wg_s[...] = wg_ref[0].astype(bf16)
        wu_s[...] = wu_ref[0].astype(bf16)
        wd_s[...] = wd_ref[0].astype(bf16)

    @pl.when(used)
    def _():
        x = x_ref[...].astype(bf16)
        hg = jnp.dot(x, wg_s[...], preferred_element_type=f32)
        hu = jnp.dot(x, wu_s[...], preferred_element_type=f32)
        hid = (_silu(hg) * hu).astype(bf16)
        o_ref[...] = jnp.dot(hid, wd_s[...], preferred_element_type=f32)


def _experts(blk_e, n_used, xs, w_gate, w_up, w_down):
    n_buf, d = xs.shape
    f = w_gate.shape[2]
    n_blk = n_buf // ROW_BLOCK
    rows = lambda j, be, nu: (jnp.minimum(j, nu[0] - 1), 0)
    wsel = lambda j, be, nu: (be[j], 0, 0)
    return pl.pallas_call(
        _experts_kernel,
        grid_spec=pltpu.PrefetchScalarGridSpec(
            num_scalar_prefetch=2,
            grid=(n_blk,),
            in_specs=[pl.BlockSpec((ROW_BLOCK, d), rows),
                      pl.BlockSpec((1, d, f), wsel),
                      pl.BlockSpec((1, d, f), wsel),
                      pl.BlockSpec((1, f, d), wsel)],
            out_specs=pl.BlockSpec((ROW_BLOCK, d), rows),
            scratch_shapes=[pltpu.VMEM((d, f), bf16), pltpu.VMEM((d, f), bf16), pltpu.VMEM((f, d), bf16)]),
        out_shape=jax.ShapeDtypeStruct((n_buf, d), f32),
        compiler_params=_params(("arbitrary",)),
        name="experts",
    )(blk_e, n_used, xs, w_gate, w_up, w_down)


def _combine_kernel(dest_hbm, ys_hbm, wt_ref, h_ref, x1_ref, mod_ref, wsg_ref, wsu_ref, wsd_ref,
                    lg_ref, lb_ref, o_ref, dest_smem, rows_ref, sem_i, sem):
    step = pl.program_id(0)
    n = dest_smem.shape[0]
    load = pltpu.make_async_copy(dest_hbm.at[pl.ds(pl.multiple_of(step * n, n), n)], dest_smem, sem_i)
    load.start()
    load.wait()

    def row_copy(j):
        return pltpu.make_async_copy(ys_hbm.at[pl.ds(dest_smem[j], 1)],
                                     rows_ref.at[j & 7, pl.ds(j >> 3, 1)], sem)

    def issue(j, _):
        row_copy(j).start()
        return 0

    def drain(j, _):
        row_copy(j).wait()
        return 0

    lax.fori_loop(0, n, issue, 0)

    h = h_ref[...].astype(bf16)
    hg = jnp.dot(h, wsg_ref[...], preferred_element_type=f32)
    hu = jnp.dot(h, wsu_ref[...], preferred_element_type=f32)
    y = jnp.dot((_silu(hg) * hu).astype(bf16), wsd_ref[...], preferred_element_type=f32)

    lax.fori_loop(0, n, drain, 0)
    wt = wt_ref[...]
    for k in range(TOP_K):
        y = y + wt[:, k:k + 1] * rows_ref[k]
    m = mod_ref[0]
    o_ref[...] = _layer_norm(DEEPNORM_ALPHA * x1_ref[...] + m[5:6] * y, lg_ref[...], lb_ref[...])


def _combine(dest_flat, ys, wts, h2, x1, mod, w_s_gate, w_s_up, w_s_down, ln_g, ln_b, s, tm):
    t, d = x1.shape
    per = s // tm
    row = lambda a: a.reshape(1, -1)
    tile = lambda w: pl.BlockSpec((tm, w), lambda i: (i, 0))
    full = lambda a: pl.BlockSpec(a.shape, lambda i: (0,) * a.ndim)
    tail = (w_s_gate.astype(bf16), w_s_up.astype(bf16), w_s_down.astype(bf16), row(ln_g), row(ln_b))
    return pl.pallas_call(
        _combine_kernel,
        grid=(t // tm,),
        in_specs=[pl.BlockSpec(memory_space=pl.ANY), pl.BlockSpec(memory_space=pl.ANY),
                  tile(LANES), tile(d), tile(d),
                  pl.BlockSpec((1, 6, d), lambda i: (i // per, 0, 0))] + [full(a) for a in tail],
        out_specs=tile(d),
        out_shape=jax.ShapeDtypeStruct((t, d), f32),
        scratch_shapes=[pltpu.SMEM((tm * TOP_K,), i32), pltpu.VMEM((TOP_K, tm, d), f32),
                        pltpu.SemaphoreType.DMA, pltpu.SemaphoreType.DMA],
        compiler_params=_params(("arbitrary",)),
        name="combine",
    )(dest_flat, ys, wts, h2, x1, mod, *tail)


def _layer(x, mod, w_in, pe, w_cmp1, w_cmp2, w_dw, b_dw, conv_ln_g, conv_ln_b, w_out, ln1_g, ln1_b,
           w_router, router_bias, w_e_gate, w_e_up, w_e_down, w_s_gate, w_s_up, w_s_down, ln2_g, ln2_b):
    b, s, d = x.shape
    t = b * s
    tm = min(512, s)
    q, kvc, kv4, gates, conv_in = _in_proj(x, mod, w_in, tm)
    cmp_kv = _compress(kvc, pe, w_cmp1, w_cmp2)
    gates = gates[:, :, :3 * N_HEADS].reshape(b, s, N_KV, 3 * HPG).transpose(0, 2, 1, 3)
    attn = _attention(q, gates, cmp_kv, kv4)
    conv = _conv(conv_in, w_dw, b_dw, conv_ln_g, conv_ln_b, tm)

    tr = min(256, s)
    x1, h2, idx, wts, counts = _mix_route(attn.reshape(t, -1), conv.reshape(t, -1), x.reshape(t, d), mod,
                                          w_out, ln1_g, ln1_b, w_router, router_bias, s, tr)
    counts = counts[0].astype(i32)
    padded = (counts + ROW_BLOCK - 1) // ROW_BLOCK * ROW_BLOCK
    seg_end = jnp.cumsum(padded)
    seg_start = seg_end - padded
    n_blk = -(-(t * TOP_K + N_EXPERTS * (ROW_BLOCK - 1)) // ROW_BLOCK)
    blk_e = jnp.minimum(jnp.searchsorted(seg_end, jnp.arange(n_blk, dtype=i32) * ROW_BLOCK, side='right'),
                        N_EXPERTS - 1).astype(i32)
    n_used = (seg_end[-1:] // ROW_BLOCK).astype(i32)

    dest = _positions(idx, seg_start.astype(f32).reshape(1, N_EXPERTS), tr)
    dest_flat = dest[:, :TOP_K].reshape(-1)
    td = min(128, s)
    xs = _dispatch(dest_flat, h2, n_blk * ROW_BLOCK, td)
    ys = _experts(blk_e, n_used, xs, w_e_gate, w_e_up, w_e_down)
    out = _combine(dest_flat, ys, wts, h2, x1, mod, w_s_gate, w_s_up, w_s_down, ln2_g, ln2_b, s, td)
    return out.reshape(b, s, d)


def kernel(x, c, w_ada, b_ada, w_in, pe_k, pe_v, w_cmp_k1, w_cmp_k2, w_cmp_v1, w_cmp_v2, w_dw, b_dw,
           conv_ln_g, conv_ln_b, w_out, ln1_g, ln1_b, w_router, router_bias, w_e_gate, w_e_up, w_e_down,
           w_s_gate, w_s_up, w_s_down, ln2_g, ln2_b):
    assert w_ada.shape[0] == DEPTH
    layer = lambda a: a.reshape(a.shape[1:])
    mod = _ada(c, layer(w_ada), layer(b_ada))
    return _layer(x, mod, layer(w_in), jnp.concatenate([pe_k, pe_v]),
                  jnp.concatenate([w_cmp_k1, w_cmp_v1]), jnp.concatenate([w_cmp_k2, w_cmp_v2]),
                  *[layer(a) for a in (w_dw, b_dw, conv_ln_g, conv_ln_b, w_out, ln1_g, ln1_b, w_router,
                                       router_bias, w_e_gate, w_e_up, w_e_down, w_s_gate, w_s_up, w_s_down,
                                       ln2_g, ln2_b)])
```

```python
import functools

import jax
import jax.numpy as jnp
import numpy as np
from jax import lax
from jax.experimental import pallas as pl
from jax.experimental.pallas import tpu as pltpu

N_HEADS = 8
N_KV = 2
HPG = N_HEADS // N_KV
HEAD_DIM = 64
D_ATTN = N_HEADS * HEAD_DIM
D_KV = N_KV * HEAD_DIM
CONV_WIDTH = 31
CMP_LEN = 32
CMP_STRIDE = 16
CMP_HID = 256
SEL_LEN = 64
SEL_TOPN = 16
WINDOW = 512
Q_BLOCK = 128
N_EXPERTS = 256
TOP_K = 8
N_GROUPS = 8
TOPK_GROUPS = 4
ROUTED_SCALE = 2.5
LN_EPS = 1e-5
DEPTH = 1
DEEPNORM_ALPHA = (2 * DEPTH) ** 0.25

LANES = 128
SUBLANES = 8
ROW_BLOCK = 256
NEG = -1e30
HIGHEST = lax.Precision.HIGHEST
VMEM_LIMIT = 48 * 1024 * 1024

f32 = jnp.float32
bf16 = jnp.bfloat16
i32 = jnp.int32


def _params(sem, vmem=VMEM_LIMIT):
    return pltpu.CompilerParams(dimension_semantics=sem, vmem_limit_bytes=vmem)


def _sigmoid(v):
    return 1.0 / (1.0 + jnp.exp(-v))


def _silu(v):
    return v * _sigmoid(v)


def _layer_norm(v, g, b):
    mu = jnp.mean(v, axis=-1, keepdims=True)
    var = jnp.mean(jnp.square(v - mu), axis=-1, keepdims=True)
    return (v - mu) * lax.rsqrt(var + LN_EPS) * g + b


def _dot_nt(a, b):
    return lax.dot_general(a, b, (((1,), (1,)), ((), ())), preferred_element_type=f32)


def _ada_kernel(c_ref, w_ref, b_ref, o_ref):
    c = c_ref[...]
    o_ref[...] = jnp.dot(_silu(c), w_ref[...], precision=HIGHEST,
                         preferred_element_type=f32) + b_ref[...]


def _ada(c, w_ada, b_ada):
    b, d = c.shape
    n = w_ada.shape[1]
    rows = 8
    c_pad = jnp.zeros((rows, d), f32).at[:b].set(c)
    tn = 1024
    out = pl.pallas_call(
        _ada_kernel,
        grid=(n // tn,),
        in_specs=[pl.BlockSpec((rows, d), lambda j: (0, 0)),
                  pl.BlockSpec((d, tn), lambda j: (0, j)),
                  pl.BlockSpec((1, tn), lambda j: (0, j))],
        out_specs=pl.BlockSpec((rows, tn), lambda j: (0, j)),
        out_shape=jax.ShapeDtypeStruct((rows, n), f32),
        compiler_params=_params(("arbitrary",)),
        name="ada",
    )(c_pad, w_ada, b_ada.reshape(1, n))
    return out[:b].reshape(b, 6, d)


def _in_proj_kernel(x_ref, mod_ref, wq_ref, wc_ref, ws_ref, wg_ref, wv_ref,
                    q_ref, kvc_ref, kv4_ref, g_ref, cv_ref):
    m = mod_ref[0]
    h = (x_ref[0] * (1.0 + m[1:2]) + m[0:1]).astype(bf16)
    q = jnp.dot(h, wq_ref[...], preferred_element_type=f32)
    q_ref[0] = (q * (HEAD_DIM ** -0.5)).astype(bf16)
    kvc_ref[0] = jnp.dot(h, wc_ref[...], preferred_element_type=f32)
    kv4 = jnp.dot(h, ws_ref[...], preferred_element_type=f32).astype(bf16)
    for j in range(4):
        for g in range(N_KV):
            off = (j * N_KV + g) * HEAD_DIM
            kv4_ref[0, j, g] = kv4[:, off:off + HEAD_DIM]
    g_ref[0] = jnp.dot(h, wg_ref[...], preferred_element_type=f32)
    cv_ref[0] = jnp.dot(h, wv_ref[...], preferred_element_type=f32)


def _in_proj(x, mod, w_in, tm):
    b, s, d = x.shape
    o = 0
    wq = w_in[:, o:o + D_ATTN]; o += D_ATTN
    wkc = w_in[:, o:o + 2 * D_KV]; o += 2 * D_KV
    wks = w_in[:, o:o + 4 * D_KV]; o += 4 * D_KV
    wg = w_in[:, o:o + 3 * N_HEADS]; o += 3 * N_HEADS
    wcv = w_in[:, o:]
    d_conv2 = wcv.shape[1]
    wg = jnp.zeros((d, LANES), f32).at[:, :3 * N_HEADS].set(wg)
    ws = [w.astype(bf16) for w in (wq, wkc, wks, wg, wcv)]
    full = lambda a: pl.BlockSpec(a.shape, lambda bi, i: (0, 0))
    return pl.pallas_call(
        _in_proj_kernel,
        grid=(b, s // tm),
        in_specs=[pl.BlockSpec((1, tm, d), lambda bi, i: (bi, i, 0)),
                  pl.BlockSpec((1, 6, d), lambda bi, i: (bi, 0, 0))] + [full(w) for w in ws],
        out_specs=[pl.BlockSpec((1, tm, D_ATTN), lambda bi, i: (bi, i, 0)),
                   pl.BlockSpec((1, tm, 2 * D_KV), lambda bi, i: (bi, i, 0)),
                   pl.BlockSpec((1, 4, N_KV, tm, HEAD_DIM), lambda bi, i: (bi, 0, 0, i, 0)),
                   pl.BlockSpec((1, tm, LANES), lambda bi, i: (bi, i, 0)),
                   pl.BlockSpec((1, tm, d_conv2), lambda bi, i: (bi, i, 0))],
        out_shape=[jax.ShapeDtypeStruct((b, s, D_ATTN), bf16),
                   jax.ShapeDtypeStruct((b, s, 2 * D_KV), f32),
                   jax.ShapeDtypeStruct((b, 4, N_KV, s, HEAD_DIM), bf16),
                   jax.ShapeDtypeStruct((b, s, LANES), f32),
                   jax.ShapeDtypeStruct((b, s, d_conv2), f32)],
        compiler_params=_params(("parallel", "parallel")),
        name="in_proj",
    )(x, mod, *ws)


def _compress_kernel(c_ref, pe_ref, w1_ref, w2_ref, o_ref):
    c = c_ref[0, 0, 0]
    n_chunk = c.shape[0]
    a = jnp.dot((c + pe_ref[0, 0]).astype(bf16), w1_ref[0, 0], preferred_element_type=f32)
    bm = jnp.dot((c + pe_ref[0, 1]).astype(bf16), w1_ref[0, 1], preferred_element_type=f32)
    hid = a + pltpu.roll(bm, n_chunk - 1, 0)
    act = 0.5 * hid * (1.0 + jnp.tanh(0.7978845608028654 * (hid + 0.044715 * (hid * hid * hid))))
    o_ref[0, 0, 0] = jnp.dot(act.astype(bf16), w2_ref[0], preferred_element_type=f32).astype(bf16)


def _compress(kvc, pe, w1, w2):
    b, s, _ = kvc.shape
    n_chunk = s // CMP_STRIDE
    half = CMP_STRIDE * HEAD_DIM
    c = kvc.reshape(b, n_chunk, CMP_STRIDE, 2, N_KV, HEAD_DIM).transpose(0, 3, 4, 1, 2, 5)
    c = c.reshape(b, 2, N_KV, n_chunk, half)
    pe2 = pe.reshape(2, 2, 1, half)
    w1h = w1.reshape(2, 2, half, CMP_HID).astype(bf16)
    w2b = w2.astype(bf16)
    return pl.pallas_call(
        _compress_kernel,
        grid=(b, 2, N_KV),
        in_specs=[pl.BlockSpec((1, 1, 1, n_chunk, half), lambda bi, j, g: (bi, j, g, 0, 0)),
                  pl.BlockSpec((1, 2, 1, half), lambda bi, j, g: (j, 0, 0, 0)),
                  pl.BlockSpec((1, 2, half, CMP_HID), lambda bi, j, g: (j, 0, 0, 0)),
                  pl.BlockSpec((1, CMP_HID, HEAD_DIM), lambda bi, j, g: (j, 0, 0))],
        out_specs=pl.BlockSpec((1, 1, 1, n_chunk, HEAD_DIM), lambda bi, j, g: (bi, j, g, 0, 0)),
        out_shape=jax.ShapeDtypeStruct((b, 2, N_KV, n_chunk, HEAD_DIM), bf16),
        compiler_params=_params(("parallel", "parallel", "parallel")),
        name="compress",
    )(c, pe2, w1h, w2b)


def _masked_softmax3(s3, mask):
    sm = jnp.where(mask[None], s3, NEG)
    mx = jnp.max(sm, axis=-1, keepdims=True)
    p = jnp.where(mask[None], jnp.exp(sm - mx), 0.0)
    d = jnp.sum(p, axis=-1, keepdims=True)
    return p / jnp.where(d > 0, d, 1.0)


def _attn_kernel(q_ref, g_ref, c_ref, kv_ref, ov_ref, o_ref, *, seq, tk, top_n):
    i = pl.program_id(2)
    s0 = i * Q_BLOCK
    n_cmp_rows = c_ref.shape[3]
    n_sel = seq // SEL_LEN
    rows = HPG * Q_BLOCK

    qb = q_ref[0]
    q_all = jnp.concatenate([qb[:, p * HEAD_DIM:(p + 1) * HEAD_DIM] for p in range(HPG)], axis=0)
    t_col = s0 + lax.broadcasted_iota(i32, (Q_BLOCK, 1), 0)

    kc = c_ref[0, 0, 0]
    vc = c_ref[0, 1, 0]
    s_c = _dot_nt(q_all, kc).reshape(HPG, Q_BLOCK, n_cmp_rows)
    cmp_end = lax.broadcasted_iota(i32, (Q_BLOCK, n_cmp_rows), 1) * CMP_STRIDE + (CMP_LEN - 1)
    p_c = _masked_softmax3(s_c, cmp_end <= t_col)
    o_c = jnp.dot(p_c.reshape(rows, n_cmp_rows).astype(bf16), vc, preferred_element_type=f32)

    p_sum = p_c[0]
    for p in range(1, HPG):
        p_sum = p_sum + p_c[p]
    imp = jnp.dot(p_sum, ov_ref[...], precision=HIGHEST, preferred_element_type=f32)
    blk = lax.broadcasted_iota(i32, (Q_BLOCK, n_sel), 1)
    cur = t_col >> 6
    forced = (blk == 0) | (blk == cur) | (blk == cur - 1)
    vals = jnp.where(forced, jnp.inf, jnp.where(blk <= cur, imp, -jnp.inf))
    sel = jnp.zeros((Q_BLOCK, n_sel), f32)
    for _ in range(top_n):
        mx = jnp.max(vals, axis=-1, keepdims=True)
        first = jnp.min(jnp.where(vals == mx, blk, n_sel), axis=-1, keepdims=True)
        pick = blk == first
        sel = jnp.where(pick & (mx > -jnp.inf), 1.0, sel)
        vals = jnp.where(pick, -jnp.inf, vals)
    sel_b = sel.astype(bf16)

    blocks_per_tile = tk // SEL_LEN

    def sel_body(kt, carry):
        m_i, l_i, acc = carry
        k0 = pl.multiple_of(kt * tk, tk)
        k = kv_ref[0, 0, 0, pl.ds(k0, tk), :]
        v = kv_ref[0, 1, 0, pl.ds(k0, tk), :]
        s3 = _dot_nt(q_all, k).reshape(HPG, Q_BLOCK, tk)
        e_row = lax.broadcasted_iota(i32, (n_sel, tk), 0)
        e_col = lax.broadcasted_iota(i32, (n_sel, tk), 1)
        expand = jnp.where(e_row == kt * blocks_per_tile + (e_col >> 6), 1.0, 0.0).astype(bf16)
        chosen = jnp.dot(sel_b, expand, preferred_element_type=f32) > 0.5
        kpos = k0 + lax.broadcasted_iota(i32, (Q_BLOCK, tk), 1)
        msk = (chosen & (kpos <= t_col))[None]
        sm = jnp.where(msk, s3, NEG)
        m_new = jnp.maximum(m_i, jnp.max(sm, axis=-1, keepdims=True))
        p = jnp.where(msk, jnp.exp(sm - m_new), 0.0)
        alpha = jnp.exp(m_i - m_new)
        l_new = alpha * l_i + jnp.sum(p, axis=-1, keepdims=True)
        pv = jnp.dot(p.reshape(rows, tk).astype(bf16), v, preferred_element_type=f32)
        acc_new = alpha.reshape(rows, 1) * acc + pv
        return m_new, l_new, acc_new

    n_kt = (s0 + Q_BLOCK + tk - 1) // tk
    init = (jnp.full((HPG, Q_BLOCK, 1), NEG, f32), jnp.zeros((HPG, Q_BLOCK, 1), f32),
            jnp.zeros((rows, HEAD_DIM), f32))
    _, l_s, acc_s = lax.fori_loop(0, n_kt, sel_body, init)
    l_s = l_s.reshape(rows, 1)
    o_s = acc_s / jnp.where(l_s > 0, l_s, 1.0)

    span = WINDOW + Q_BLOCK
    w0 = pl.multiple_of(jnp.maximum(s0 - WINDOW, 0), Q_BLOCK)
    kw = kv_ref[0, 2, 0, pl.ds(w0, span), :]
    vw = kv_ref[0, 3, 0, pl.ds(w0, span), :]
    wpos = w0 + lax.broadcasted_iota(i32, (Q_BLOCK, span), 1)
    s_w = _dot_nt(q_all, kw).reshape(HPG, Q_BLOCK, span)
    p_w = _masked_softmax3(s_w, (wpos <= t_col) & (wpos > t_col - WINDOW))
    o_w = jnp.dot(p_w.reshape(rows, span).astype(bf16), vw, preferred_element_type=f32)

    gate = _sigmoid(g_ref[0, 0])
    outs = []
    for p in range(HPG):
        r = slice(p * Q_BLOCK, (p + 1) * Q_BLOCK)
        outs.append(gate[:, 3 * p:3 * p + 1] * o_c[r] + gate[:, 3 * p + 1:3 * p + 2] * o_s[r]
                    + gate[:, 3 * p + 2:3 * p + 3] * o_w[r])
    o_ref[0] = jnp.concatenate(outs, axis=1).astype(bf16)


def _attention(q, gates, cmp_kv, kv4):
    b, s, _ = q.shape
    n_chunk = cmp_kv.shape[3]
    n_sel = s // SEL_LEN
    top_n = min(SEL_TOPN, n_sel)
    tk = min(512, s)
    cs = np.arange(n_chunk) * CMP_STRIDE
    ss = np.arange(n_sel) * SEL_LEN
    overlap = ((cs[:, None] < ss[None, :] + SEL_LEN) & (cs[:, None] + CMP_LEN > ss[None, :]))
    overlap[(s - CMP_LEN) // CMP_STRIDE + 1:] = False
    overlap = jnp.asarray(overlap.astype(np.float32))
    gw = HPG * HEAD_DIM
    return pl.pallas_call(
        functools.partial(_attn_kernel, seq=s, tk=tk, top_n=top_n),
        grid=(b, N_KV, s // Q_BLOCK),
        in_specs=[pl.BlockSpec((1, Q_BLOCK, gw), lambda bi, g, i: (bi, i, g)),
                  pl.BlockSpec((1, 1, Q_BLOCK, 3 * HPG), lambda bi, g, i: (bi, g, i, 0)),
                  pl.BlockSpec((1, 2, 1, n_chunk, HEAD_DIM), lambda bi, g, i: (bi, 0, g, 0, 0)),
                  pl.BlockSpec((1, 4, 1, s, HEAD_DIM), lambda bi, g, i: (bi, 0, g, 0, 0)),
                  pl.BlockSpec((n_chunk, n_sel), lambda bi, g, i: (0, 0))],
        out_specs=pl.BlockSpec((1, Q_BLOCK, gw), lambda bi, g, i: (bi, i, g)),
        out_shape=jax.ShapeDtypeStruct((b, s, D_ATTN), bf16),
        compiler_params=_params(("parallel", "parallel", "arbitrary")),
        name="attention",
    )(q, gates, cmp_kv, kv4, overlap)


CONV_HALO = 32
CONV_ROWS = 32


def _conv_kernel(cur_ref, prev_ref, w_ref, b_ref, g_ref, bb_ref, o_ref, glu_ref):
    i = pl.program_id(1)
    ts = cur_ref.shape[1]
    dc = o_ref.shape[2]
    cur = cur_ref[0]
    prev = prev_ref[0]
    glu_prev = prev[:, :dc] * _sigmoid(prev[:, dc:])
    glu_ref[0:CONV_HALO] = jnp.where(i == 0, 0.0, glu_prev)
    glu_ref[CONV_HALO:CONV_HALO + ts] = cur[:, :dc] * _sigmoid(cur[:, dc:])
    lead = CONV_HALO - (CONV_WIDTH - 1)

    def chunk(r, _):
        r0 = pl.multiple_of(r * CONV_ROWS, CONV_ROWS)
        win = glu_ref[pl.ds(r0, CONV_ROWS + CONV_HALO), :]
        acc = jnp.zeros((CONV_ROWS, dc), f32)
        for j in range(CONV_WIDTH):
            acc = acc + w_ref[j:j + 1, :] * win[lead + j:lead + j + CONV_ROWS]
        y = _layer_norm(acc + b_ref[...], g_ref[...], bb_ref[...])
        o_ref[0, pl.ds(r0, CONV_ROWS), :] = _silu(y).astype(bf16)
        return 0

    lax.fori_loop(0, ts // CONV_ROWS, chunk, 0)


def _conv(conv_in, w_dw, b_dw, ln_g, ln_b, ts):
    b, s, dc2 = conv_in.shape
    dc = dc2 // 2
    per = ts // CONV_HALO
    row = lambda a: a.reshape(1, dc)
    return pl.pallas_call(
        _conv_kernel,
        grid=(b, s // ts),
        in_specs=[pl.BlockSpec((1, ts, dc2), lambda bi, i: (bi, i, 0)),
                  pl.BlockSpec((1, CONV_HALO, dc2), lambda bi, i: (bi, jnp.maximum(i * per - 1, 0), 0)),
                  pl.BlockSpec((CONV_WIDTH, dc), lambda bi, i: (0, 0)),
                  pl.BlockSpec((1, dc), lambda bi, i: (0, 0)),
                  pl.BlockSpec((1, dc), lambda bi, i: (0, 0)),
                  pl.BlockSpec((1, dc), lambda bi, i: (0, 0))],
        out_specs=pl.BlockSpec((1, ts, dc), lambda bi, i: (bi, i, 0)),
        out_shape=jax.ShapeDtypeStruct((b, s, dc), bf16),
        scratch_shapes=[pltpu.VMEM((CONV_HALO + ts, dc), f32)],
        compiler_params=_params(("parallel", "parallel")),
        name="conv",
    )(conv_in, conv_in, w_dw.reshape(CONV_WIDTH, dc), row(b_dw), row(ln_g), row(ln_b))


def _mix_route_kernel(a_ref, cv_ref, x_ref, mod_ref, wo_ref, lg_ref, lb_ref, wr_ref, rb_ref,
                      x1_ref, h2_ref, h2r_ref, idx_ref, wt_ref, cnt_ref):
    step = pl.program_id(0)
    tm = x_ref.shape[0]
    m = mod_ref[0]
    da = a_ref.shape[1]
    mix = (jnp.dot(a_ref[...], wo_ref[0:da, :], preferred_element_type=f32)
           + jnp.dot(cv_ref[...], wo_ref[da:, :], preferred_element_type=f32))
    x1 = _layer_norm(DEEPNORM_ALPHA * x_ref[...] + m[2:3] * mix, lg_ref[...], lb_ref[...])
    x1_ref[...] = x1
    h2 = x1 * (1.0 + m[4:5]) + m[3:4]
    h2_ref[...] = h2
    for cch in range(h2.shape[1] // LANES):
        h2r_ref[pl.ds(cch, tm, stride=SUBLANES), :] = h2[:, cch * LANES:(cch + 1) * LANES]

    score = _sigmoid(jnp.dot(h2, wr_ref[...], precision=HIGHEST, preferred_element_type=f32))
    sel = score + rb_ref[...]
    lane = lax.broadcasted_iota(i32, (tm, N_EXPERTS), 1)
    grp = lane >> 5
    gs = []
    for g in range(N_GROUPS):
        v = jnp.where(grp == g, sel, -jnp.inf)
        m1 = jnp.max(v, axis=-1, keepdims=True)
        i1 = jnp.min(jnp.where(v == m1, lane, N_EXPERTS), axis=-1, keepdims=True)
        m2 = jnp.max(jnp.where(lane == i1, -jnp.inf, v), axis=-1, keepdims=True)
        gs.append(m1 + m2)
    gmask = jnp.zeros((tm, N_EXPERTS), jnp.bool_)
    for g in range(N_GROUPS):
        rank = jnp.zeros((tm, 1), i32)
        for o in range(N_GROUPS):
            if o == g:
                continue
            beats = (gs[o] > gs[g]) | (gs[o] == gs[g]) if o < g else gs[o] > gs[g]
            rank = rank + beats.astype(i32)
        gmask = gmask | ((grp == g) & (rank < TOPK_GROUPS))
    cand = jnp.where(gmask, sel, -jnp.inf)
    lane_o = lax.broadcasted_iota(i32, (tm, LANES), 1)
    idx_out = jnp.zeros((tm, LANES), i32)
    wt_out = jnp.zeros((tm, LANES), f32)
    picked = jnp.zeros((tm, N_EXPERTS), f32)
    w_sum = jnp.zeros((tm, 1), f32)
    for k in range(TOP_K):
        mx = jnp.max(cand, axis=-1, keepdims=True)
        ik = jnp.min(jnp.where(cand == mx, lane, N_EXPERTS), axis=-1, keepdims=True)
        pick = lane == ik
        wk = jnp.sum(jnp.where(pick, score, 0.0), axis=-1, keepdims=True)
        cand = jnp.where(pick, -jnp.inf, cand)
        picked = jnp.where(pick, 1.0, picked)
        idx_out = jnp.where(lane_o == k, ik, idx_out)
        wt_out = jnp.where(lane_o == k, wk, wt_out)
        w_sum = w_sum + wk
    idx_ref[...] = idx_out
    wt_ref[...] = wt_out / w_sum * ROUTED_SCALE

    @pl.when(step == 0)
    def _():
        cnt_ref[...] = jnp.zeros_like(cnt_ref)

    cnt_ref[...] += jnp.sum(picked, axis=0, keepdims=True)


def _mix_route(attn, conv, x2, mod, w_out, ln_g, ln_b, w_router, router_bias, s, tm):
    t, d = x2.shape
    per = s // tm
    da = attn.shape[1]
    row = lambda a: a.reshape(1, -1)
    tile = lambda w: pl.BlockSpec((tm, w), lambda i: (i, 0))
    full = lambda a: pl.BlockSpec(a.shape, lambda i: (0,) * a.ndim)
    args = (attn, conv, x2, mod, w_out.astype(bf16), row(ln_g), row(ln_b), w_router, row(router_bias))
    return pl.pallas_call(
        _mix_route_kernel,
        grid=(t // tm,),
        in_specs=[tile(da), tile(conv.shape[1]), tile(d),
                  pl.BlockSpec((1, 6, d), lambda i: (i // per, 0, 0))] + [full(a) for a in args[4:]],
        out_specs=[tile(d), tile(d), pl.BlockSpec((tm * d // LANES, LANES), lambda i: (i, 0)),
                   tile(LANES), tile(LANES), pl.BlockSpec((1, N_EXPERTS), lambda i: (0, 0))],
        out_shape=[jax.ShapeDtypeStruct((t, d), f32), jax.ShapeDtypeStruct((t, d), f32),
                   jax.ShapeDtypeStruct((t * d // LANES, LANES), f32),
                   jax.ShapeDtypeStruct((t, LANES), i32), jax.ShapeDtypeStruct((t, LANES), f32),
                   jax.ShapeDtypeStruct((1, N_EXPERTS), f32)],
        compiler_params=_params(("arbitrary",)),
        name="mix_route",
    )(*args)


def _positions_kernel(idx_ref, start_ref, o_ref, run_ref):
    step = pl.program_id(0)
    tm = idx_ref.shape[0]

    @pl.when(step == 0)
    def _():
        run_ref[...] = jnp.zeros_like(run_ref)

    idx = idx_ref[...]
    lane = lax.broadcasted_iota(i32, (tm, N_EXPERTS), 1)
    onehot = jnp.zeros((tm, N_EXPERTS), f32)
    for k in range(TOP_K):
        onehot = jnp.where(lane == idx[:, k:k + 1], 1.0, onehot)
    r = lax.broadcasted_iota(i32, (tm, tm), 0)
    c = lax.broadcasted_iota(i32, (tm, tm), 1)
    below = jnp.where(r > c, 1.0, 0.0).astype(bf16)
    prior = jnp.dot(below, onehot.astype(bf16), preferred_element_type=f32)
    pos = prior + run_ref[...] + start_ref[...]
    lane_o = lax.broadcasted_iota(i32, (tm, LANES), 1)
    out = jnp.zeros((tm, LANES), i32)
    for k in range(TOP_K):
        dk = jnp.sum(jnp.where(lane == idx[:, k:k + 1], pos, 0.0), axis=-1, keepdims=True)
        out = jnp.where(lane_o == k, dk.astype(i32), out)
    o_ref[...] = out
    run_ref[...] += jnp.sum(onehot, axis=0, keepdims=True)


def _positions(idx, seg_start, tm):
    t = idx.shape[0]
    return pl.pallas_call(
        _positions_kernel,
        grid=(t // tm,),
        in_specs=[pl.BlockSpec((tm, LANES), lambda i: (i, 0)),
                  pl.BlockSpec((1, N_EXPERTS), lambda i: (0, 0))],
        out_specs=pl.BlockSpec((tm, LANES), lambda i: (i, 0)),
        out_shape=jax.ShapeDtypeStruct((t, LANES), i32),
        scratch_shapes=[pltpu.VMEM((1, N_EXPERTS), f32)],
        compiler_params=_params(("arbitrary",)),
        name="positions",
    )(idx, seg_start)


ISSUE_UNROLL = 8


def _dispatch_kernel(dest_hbm, h_ref, o_hbm, dest_smem, sem_i, sem):
    step = pl.program_id(0)
    n = dest_smem.shape[0]
    load = pltpu.make_async_copy(dest_hbm.at[pl.ds(pl.multiple_of(step * n, n), n)], dest_smem, sem_i)
    load.start()
    load.wait()

    def issue(j, _):
        src = pl.multiple_of((j >> 3) * SUBLANES, SUBLANES)
        dst = pl.multiple_of(dest_smem[j] * SUBLANES, SUBLANES)
        pltpu.make_async_copy(h_ref.at[pl.ds(src, SUBLANES)], o_hbm.at[pl.ds(dst, SUBLANES)], sem).start()
        return 0

    lax.fori_loop(0, n, issue, 0, unroll=ISSUE_UNROLL)
    everything = o_hbm.at[pl.ds(0, n * SUBLANES)]
    pltpu.make_async_copy(everything, everything, sem).wait()


def _dispatch(dest_flat, h2r, n_buf, tm):
    per_tok = SUBLANES
    t = h2r.shape[0] // per_tok
    return pl.pallas_call(
        _dispatch_kernel,
        grid=(t // tm,),
        in_specs=[pl.BlockSpec(memory_space=pl.ANY),
                  pl.BlockSpec((tm * per_tok, LANES), lambda i: (i, 0))],
        out_specs=pl.BlockSpec(memory_space=pl.ANY),
        out_shape=jax.ShapeDtypeStruct((n_buf * per_tok, LANES), f32),
        scratch_shapes=[pltpu.SMEM((tm * TOP_K,), i32), pltpu.SemaphoreType.DMA, pltpu.SemaphoreType.DMA],
        compiler_params=_params(("arbitrary",)),
        name="dispatch",
    )(dest_flat, h2r)


def _experts_kernel(be_ref, bv_ref, nu_ref, x_ref, wg_ref, wu_ref, wd_ref, o_ref, wg_s, wu_s, wd_s):
    j = pl.program_id(0)
    prev = be_ref[jnp.maximum(j - 1, 0)]
    used = j < nu_ref[0]
    d = wg_s.shape[0]

    @pl.when(used & ((j == 0) | (be_ref[j] != prev)))
    def _():
        wg_s[...] = wg_ref[0].astype(bf16)
        wu_s[...] = wu_ref[0].astype(bf16)
        wd_s[...] = wd_ref[0].astype(bf16)

    @pl.when(used)
    def _():
        live = lax.broadcasted_iota(i32, (ROW_BLOCK, 1), 0) < bv_ref[j]
        hg = jnp.zeros((ROW_BLOCK, wg_s.shape[1]), f32)
        hu = jnp.zeros((ROW_BLOCK, wg_s.shape[1]), f32)
        for c in range(d // LANES):
            xc = jnp.where(live, x_ref[pl.ds(c, ROW_BLOCK, stride=SUBLANES), :], 0.0).astype(bf16)
            hg = hg + jnp.dot(xc, wg_s[c * LANES:(c + 1) * LANES, :], preferred_element_type=f32)
            hu = hu + jnp.dot(xc, wu_s[c * LANES:(c + 1) * LANES, :], preferred_element_type=f32)
        hid = (_silu(hg) * hu).astype(bf16)
        o_ref[...] = jnp.dot(hid, wd_s[...], preferred_element_type=f32)


def _experts(blk_e, blk_valid, n_used, xs, w_gate, w_up, w_down):
    d, f = w_gate.shape[1], w_gate.shape[2]
    per_tok = d // LANES
    n_buf = xs.shape[0] // per_tok
    n_blk = n_buf // ROW_BLOCK
    rows = lambda j, be, bv, nu: (jnp.minimum(j, nu[0] - 1), 0)
    wsel = lambda j, be, bv, nu: (be[j], 0, 0)
    return pl.pallas_call(
        _experts_kernel,
        grid_spec=pltpu.PrefetchScalarGridSpec(
            num_scalar_prefetch=3,
            grid=(n_blk,),
            in_specs=[pl.BlockSpec((ROW_BLOCK * per_tok, LANES), rows),
                      pl.BlockSpec((1, d, f), wsel),
                      pl.BlockSpec((1, d, f), wsel),
                      pl.BlockSpec((1, f, d), wsel)],
            out_specs=pl.BlockSpec((ROW_BLOCK, d), rows),
            scratch_shapes=[pltpu.VMEM((d, f), bf16), pltpu.VMEM((d, f), bf16), pltpu.VMEM((f, d), bf16)]),
        out_shape=jax.ShapeDtypeStruct((n_buf, d), f32),
        compiler_params=_params(("arbitrary",)),
        name="experts",
    )(blk_e, blk_valid, n_used, xs, w_gate, w_up, w_down)


def _combine_kernel(dest_hbm, ys_hbm, wt_ref, h_ref, x1_ref, mod_ref, wsg_ref, wsu_ref, wsd_ref,
                    lg_ref, lb_ref, o_ref, dest_smem, rows_ref, sem_i, sem):
    step = pl.program_id(0)
    n = dest_smem.shape[0]
    load = pltpu.make_async_copy(dest_hbm.at[pl.ds(pl.multiple_of(step * n, n), n)], dest_smem, sem_i)
    load.start()
    load.wait()

    def issue(j, _):
        pltpu.make_async_copy(ys_hbm.at[pl.ds(dest_smem[j], 1)],
                              rows_ref.at[j & 7, pl.ds(j >> 3, 1)], sem).start()
        return 0

    lax.fori_loop(0, n, issue, 0, unroll=ISSUE_UNROLL)

    h = h_ref[...].astype(bf16)
    hg = jnp.dot(h, wsg_ref[...], preferred_element_type=f32)
    hu = jnp.dot(h, wsu_ref[...], preferred_element_type=f32)
    y = jnp.dot((_silu(hg) * hu).astype(bf16), wsd_ref[...], preferred_element_type=f32)

    pltpu.make_async_copy(rows_ref, rows_ref, sem).wait()
    wt = wt_ref[...]
    for k in range(TOP_K):
        y = y + wt[:, k:k + 1] * rows_ref[k]
    m = mod_ref[0]
    o_ref[...] = _layer_norm(DEEPNORM_ALPHA * x1_ref[...] + m[5:6] * y, lg_ref[...], lb_ref[...])


def _combine(dest_flat, ys, wts, h2, x1, mod, w_s_gate, w_s_up, w_s_down, ln_g, ln_b, s, tm):
    t, d = x1.shape
    per = s // tm
    row = lambda a: a.reshape(1, -1)
    tile = lambda w: pl.BlockSpec((tm, w), lambda i: (i, 0))
    full = lambda a: pl.BlockSpec(a.shape, lambda i: (0,) * a.ndim)
    tail = (w_s_gate.astype(bf16), w_s_up.astype(bf16), w_s_down.astype(bf16), row(ln_g), row(ln_b))
    return pl.pallas_call(
        _combine_kernel,
        grid=(t // tm,),
        in_specs=[pl.BlockSpec(memory_space=pl.ANY), pl.BlockSpec(memory_space=pl.ANY),
                  tile(LANES), tile(d), tile(d),
                  pl.BlockSpec((1, 6, d), lambda i: (i // per, 0, 0))] + [full(a) for a in tail],
        out_specs=tile(d),
        out_shape=jax.ShapeDtypeStruct((t, d), f32),
        scratch_shapes=[pltpu.SMEM((tm * TOP_K,), i32), pltpu.VMEM((TOP_K, tm, d), f32),
                        pltpu.SemaphoreType.DMA, pltpu.SemaphoreType.DMA],
        compiler_params=_params(("arbitrary",)),
        name="combine",
    )(dest_flat, ys, wts, h2, x1, mod, *tail)


def _layer(x, mod, w_in, pe, w_cmp1, w_cmp2, w_dw, b_dw, conv_ln_g, conv_ln_b, w_out, ln1_g, ln1_b,
           w_router, router_bias, w_e_gate, w_e_up, w_e_down, w_s_gate, w_s_up, w_s_down, ln2_g, ln2_b):
    b, s, d = x.shape
    t = b * s
    tm = min(512, s)
    q, kvc, kv4, gates, conv_in = _in_proj(x, mod, w_in, tm)
    cmp_kv = _compress(kvc, pe, w_cmp1, w_cmp2)
    gates = gates[:, :, :3 * N_HEADS].reshape(b, s, N_KV, 3 * HPG).transpose(0, 2, 1, 3)
    attn = _attention(q, gates, cmp_kv, kv4)
    conv = _conv(conv_in, w_dw, b_dw, conv_ln_g, conv_ln_b, tm)

    tr = min(256, s)
    x1, h2, h2r, idx, wts, counts = _mix_route(attn.reshape(t, -1), conv.reshape(t, -1), x.reshape(t, d), mod,
                                               w_out, ln1_g, ln1_b, w_router, router_bias, s, tr)
    counts = counts[0].astype(i32)
    padded = (counts + ROW_BLOCK - 1) // ROW_BLOCK * ROW_BLOCK
    seg_end = jnp.cumsum(padded)
    seg_start = seg_end - padded
    n_blk = -(-(t * TOP_K + N_EXPERTS * (ROW_BLOCK - 1)) // ROW_BLOCK)
    blk_row0 = jnp.arange(n_blk, dtype=i32) * ROW_BLOCK
    blk_e = jnp.minimum(jnp.searchsorted(seg_end, blk_row0, side='right'), N_EXPERTS - 1).astype(i32)
    blk_valid = jnp.clip(seg_start[blk_e] + counts[blk_e] - blk_row0, 0, ROW_BLOCK).astype(i32)
    n_used = (seg_end[-1:] // ROW_BLOCK).astype(i32)

    dest = _positions(idx, seg_start.astype(f32).reshape(1, N_EXPERTS), tr)
    dest_flat = dest[:, :TOP_K].reshape(-1)
    xs = _dispatch(dest_flat, h2r, n_blk * ROW_BLOCK, min(512, s))
    ys = _experts(blk_e, blk_valid, n_used, xs, w_e_gate, w_e_up, w_e_down)
    out = _combine(dest_flat, ys, wts, h2, x1, mod, w_s_gate, w_s_up, w_s_down, ln2_g, ln2_b, s, min(128, s))
    return out.reshape(b, s, d)


def kernel(x, c, w_ada, b_ada, w_in, pe_k, pe_v, w_cmp_k1, w_cmp_k2, w_cmp_v1, w_cmp_v2, w_dw, b_dw,
           conv_ln_g, conv_ln_b, w_out, ln1_g, ln1_b, w_router, router_bias, w_e_gate, w_e_up, w_e_down,
           w_s_gate, w_s_up, w_s_down, ln2_g, ln2_b):
    assert w_ada.shape[0] == DEPTH
    layer = lambda a: a.reshape(a.shape[1:])
    mod = _ada(c, layer(w_ada), layer(b_ada))
    return _layer(x, mod, layer(w_in), jnp.concatenate([pe_k, pe_v]),
                  jnp.concatenate([w_cmp_k1, w_cmp_v1]), jnp.concatenate([w_cmp_k2, w_cmp_v2]),
                  *[layer(a) for a in (w_dw, b_dw, conv_ln_g, conv_ln_b, w_out, ln1_g, ln1_b, w_router,
                                       router_bias, w_e_gate, w_e_up, w_e_down, w_s_gate, w_s_up, w_s_down,
                                       ln2_g, ln2_b)])
```

```python
import functools

import jax
import jax.numpy as jnp
import numpy as np
from jax import lax
from jax.experimental import pallas as pl
from jax.experimental.pallas import tpu as pltpu

N_HEADS = 8
N_KV = 2
HPG = N_HEADS // N_KV
HEAD_DIM = 64
D_ATTN = N_HEADS * HEAD_DIM
D_KV = N_KV * HEAD_DIM
CONV_WIDTH = 31
CMP_LEN = 32
CMP_STRIDE = 16
CMP_HID = 256
SEL_LEN = 64
SEL_TOPN = 16
WINDOW = 512
Q_BLOCK = 128
N_EXPERTS = 256
TOP_K = 8
N_GROUPS = 8
TOPK_GROUPS = 4
ROUTED_SCALE = 2.5
LN_EPS = 1e-5
DEPTH = 1
DEEPNORM_ALPHA = (2 * DEPTH) ** 0.25

LANES = 128
SUBLANES = 8
ROW_BLOCK = 256
NEG = -1e30
HIGHEST = lax.Precision.HIGHEST
VMEM_LIMIT = 48 * 1024 * 1024

f32 = jnp.float32
bf16 = jnp.bfloat16
i32 = jnp.int32


def _params(sem, vmem=VMEM_LIMIT):
    return pltpu.CompilerParams(dimension_semantics=sem, vmem_limit_bytes=vmem)


def _sigmoid(v):
    return 1.0 / (1.0 + jnp.exp(-v))


def _silu(v):
    return v * _sigmoid(v)


def _layer_norm(v, g, b):
    mu = jnp.mean(v, axis=-1, keepdims=True)
    var = jnp.mean(jnp.square(v - mu), axis=-1, keepdims=True)
    return (v - mu) * lax.rsqrt(var + LN_EPS) * g + b


def _dot_nt(a, b):
    return lax.dot_general(a, b, (((1,), (1,)), ((), ())), preferred_element_type=f32)


def _ada_kernel(c_ref, w_ref, b_ref, o_ref):
    c = c_ref[...]
    o_ref[...] = jnp.dot(_silu(c), w_ref[...], precision=HIGHEST,
                         preferred_element_type=f32) + b_ref[...]


def _ada(c, w_ada, b_ada):
    b, d = c.shape
    n = w_ada.shape[1]
    rows = 8
    c_pad = jnp.zeros((rows, d), f32).at[:b].set(c)
    tn = 1024
    out = pl.pallas_call(
        _ada_kernel,
        grid=(n // tn,),
        in_specs=[pl.BlockSpec((rows, d), lambda j: (0, 0)),
                  pl.BlockSpec((d, tn), lambda j: (0, j)),
                  pl.BlockSpec((1, tn), lambda j: (0, j))],
        out_specs=pl.BlockSpec((rows, tn), lambda j: (0, j)),
        out_shape=jax.ShapeDtypeStruct((rows, n), f32),
        compiler_params=_params(("arbitrary",)),
        name="ada",
    )(c_pad, w_ada, b_ada.reshape(1, n))
    return out[:b].reshape(b, 6, d)


GATE_ROWS = 32
V_ROWS = HEAD_DIM + 16
Q_SCALE = HEAD_DIM ** -0.5 * 1.4426950408889634


def _in_proj_kernel(x_ref, mod_ref, wt_ref, wc_ref, wk_ref, wv_ref,
                    qt_ref, vt_ref, gt_ref, kvc_ref, kn_ref, cv_ref):
    m = mod_ref[0]
    h = (x_ref[0] * (1.0 + m[1:2]) + m[0:1]).astype(bf16)
    res_t = _dot_nt(wt_ref[...], h)
    qt_ref[0] = (res_t[0:D_ATTN] * Q_SCALE).astype(bf16)
    ones = jnp.ones((V_ROWS - HEAD_DIM, res_t.shape[1]), bf16)
    for j in range(2):
        for g in range(N_KV):
            off = D_ATTN + (j * N_KV + g) * HEAD_DIM
            vt_ref[0, j, g, 0:HEAD_DIM, :] = res_t[off:off + HEAD_DIM].astype(bf16)
            vt_ref[0, j, g, HEAD_DIM:V_ROWS, :] = ones
    gt_ref[0] = res_t[D_ATTN + 2 * D_KV:]
    kvc_ref[0] = jnp.dot(h, wc_ref[...], preferred_element_type=f32)
    kn = jnp.dot(h, wk_ref[...], preferred_element_type=f32).astype(bf16)
    for j in range(2):
        for g in range(N_KV):
            off = (j * N_KV + g) * HEAD_DIM
            kn_ref[0, j, g] = kn[:, off:off + HEAD_DIM]
    cv_ref[0] = jnp.dot(h, wv_ref[...], preferred_element_type=f32)


def _in_proj(x, mod, w_in, tm):
    b, s, d = x.shape
    o = 0
    wq = w_in[:, o:o + D_ATTN]; o += D_ATTN
    wkc = w_in[:, o:o + 2 * D_KV]; o += 2 * D_KV
    wk_s = w_in[:, o:o + D_KV]; o += D_KV
    wv_s = w_in[:, o:o + D_KV]; o += D_KV
    wk_w = w_in[:, o:o + D_KV]; o += D_KV
    wv_w = w_in[:, o:o + D_KV]; o += D_KV
    wg = w_in[:, o:o + 3 * N_HEADS]; o += 3 * N_HEADS
    wcv = w_in[:, o:]
    d_conv2 = wcv.shape[1]
    wg = jnp.zeros((d, GATE_ROWS), f32).at[:, :3 * N_HEADS].set(wg)
    wt = jnp.concatenate([wq, wv_s, wv_w, wg], axis=1).T
    ws = [w.astype(bf16) for w in (wt, wkc, jnp.concatenate([wk_s, wk_w], axis=1), wcv)]
    full = lambda a: pl.BlockSpec(a.shape, lambda bi, i: (0, 0))
    return pl.pallas_call(
        _in_proj_kernel,
        grid=(b, s // tm),
        in_specs=[pl.BlockSpec((1, tm, d), lambda bi, i: (bi, i, 0)),
                  pl.BlockSpec((1, 6, d), lambda bi, i: (bi, 0, 0))] + [full(w) for w in ws],
        out_specs=[pl.BlockSpec((1, D_ATTN, tm), lambda bi, i: (bi, 0, i)),
                   pl.BlockSpec((1, 2, N_KV, V_ROWS, tm), lambda bi, i: (bi, 0, 0, 0, i)),
                   pl.BlockSpec((1, GATE_ROWS, tm), lambda bi, i: (bi, 0, i)),
                   pl.BlockSpec((1, tm, 2 * D_KV), lambda bi, i: (bi, i, 0)),
                   pl.BlockSpec((1, 2, N_KV, tm, HEAD_DIM), lambda bi, i: (bi, 0, 0, i, 0)),
                   pl.BlockSpec((1, tm, d_conv2), lambda bi, i: (bi, i, 0))],
        out_shape=[jax.ShapeDtypeStruct((b, D_ATTN, s), bf16),
                   jax.ShapeDtypeStruct((b, 2, N_KV, V_ROWS, s), bf16),
                   jax.ShapeDtypeStruct((b, GATE_ROWS, s), f32),
                   jax.ShapeDtypeStruct((b, s, 2 * D_KV), f32),
                   jax.ShapeDtypeStruct((b, 2, N_KV, s, HEAD_DIM), bf16),
                   jax.ShapeDtypeStruct((b, s, d_conv2), f32)],
        compiler_params=_params(("parallel", "parallel")),
        name="in_proj",
    )(x, mod, *ws)


def _compress_kernel(c_ref, pe_ref, w1_ref, w2_ref, w2t_ref, o_ref, ot_ref):
    c = c_ref[0, 0, 0]
    n_chunk = c.shape[0]
    a = jnp.dot((c + pe_ref[0, 0]).astype(bf16), w1_ref[0, 0], preferred_element_type=f32)
    bm = jnp.dot((c + pe_ref[0, 1]).astype(bf16), w1_ref[0, 1], preferred_element_type=f32)
    hid = a + pltpu.roll(bm, n_chunk - 1, 0)
    act = 0.5 * hid * (1.0 + jnp.tanh(0.7978845608028654 * (hid + 0.044715 * (hid * hid * hid))))
    act = act.astype(bf16)
    o_ref[0, 0, 0] = jnp.dot(act, w2_ref[0], preferred_element_type=f32).astype(bf16)
    ot_ref[0, 0, 0] = _dot_nt(w2t_ref[0], act).astype(bf16)


def _compress(kvc, pe, w1, w2):
    b, s, _ = kvc.shape
    n_chunk = s // CMP_STRIDE
    half = CMP_STRIDE * HEAD_DIM
    c = kvc.reshape(b, n_chunk, CMP_STRIDE, 2, N_KV, HEAD_DIM).transpose(0, 3, 4, 1, 2, 5)
    c = c.reshape(b, 2, N_KV, n_chunk, half)
    pe2 = pe.reshape(2, 2, 1, half)
    w1h = w1.reshape(2, 2, half, CMP_HID).astype(bf16)
    w2b = w2.astype(bf16)
    w2t = w2b.transpose(0, 2, 1)
    return pl.pallas_call(
        _compress_kernel,
        grid=(b, 2, N_KV),
        in_specs=[pl.BlockSpec((1, 1, 1, n_chunk, half), lambda bi, j, g: (bi, j, g, 0, 0)),
                  pl.BlockSpec((1, 2, 1, half), lambda bi, j, g: (j, 0, 0, 0)),
                  pl.BlockSpec((1, 2, half, CMP_HID), lambda bi, j, g: (j, 0, 0, 0)),
                  pl.BlockSpec((1, CMP_HID, HEAD_DIM), lambda bi, j, g: (j, 0, 0)),
                  pl.BlockSpec((1, HEAD_DIM, CMP_HID), lambda bi, j, g: (j, 0, 0))],
        out_specs=[pl.BlockSpec((1, 1, 1, n_chunk, HEAD_DIM), lambda bi, j, g: (bi, j, g, 0, 0)),
                   pl.BlockSpec((1, 1, 1, HEAD_DIM, n_chunk), lambda bi, j, g: (bi, j, g, 0, 0))],
        out_shape=[jax.ShapeDtypeStruct((b, 2, N_KV, n_chunk, HEAD_DIM), bf16),
                   jax.ShapeDtypeStruct((b, 2, N_KV, HEAD_DIM, n_chunk), bf16)],
        compiler_params=_params(("parallel", "parallel", "parallel")),
        name="compress",
    )(c, pe2, w1h, w2b, w2t)


def _attn_kernel(q_ref, g_ref, kc_ref, vc_ref, kn_ref, vt_ref, ov_ref, o_ref, selbias_ref, *, seq, tk, top_n):
    i = pl.program_id(2)
    s0 = i * Q_BLOCK
    n_cmp_rows = kc_ref.shape[3]
    n_sel = seq // SEL_LEN

    q4 = q_ref[0]
    qt = jnp.concatenate([q4[p * HEAD_DIM:(p + 1) * HEAD_DIM, :] for p in range(HPG)], axis=1)
    t_row = s0 + lax.broadcasted_iota(i32, (1, Q_BLOCK), 1)

    s_c = jnp.dot(kc_ref[0, 0, 0], qt, preferred_element_type=f32)
    cmp_end = lax.broadcasted_iota(i32, (n_cmp_rows, 1), 0) * CMP_STRIDE + (CMP_LEN - 1)
    bias_c = jnp.where(cmp_end <= t_row, 0.0, NEG)
    any_c = t_row >= CMP_LEN - 1
    p_sum = jnp.zeros((n_cmp_rows, Q_BLOCK), f32)
    pcs = []
    for p in range(HPG):
        sp = s_c[:, p * Q_BLOCK:(p + 1) * Q_BLOCK] + bias_c
        e = jnp.exp2(sp - jnp.max(sp, axis=0, keepdims=True))
        pn = e * jnp.where(any_c, 1.0 / jnp.sum(e, axis=0, keepdims=True), 0.0)
        p_sum = p_sum + pn
        pcs.append(pn.astype(bf16))
    o_c = jnp.dot(vc_ref[0, 0, 0], jnp.concatenate(pcs, axis=1), preferred_element_type=f32)

    imp = jnp.dot(ov_ref[...], p_sum, precision=HIGHEST, preferred_element_type=f32)
    blk = lax.broadcasted_iota(i32, (n_sel, Q_BLOCK), 0)
    cur = t_row >> 6
    forced = (blk == 0) | (blk == cur) | (blk == cur - 1)
    vals = jnp.where(forced, jnp.inf, jnp.where(blk <= cur, imp, -jnp.inf))
    sel = jnp.zeros((n_sel, Q_BLOCK), f32)
    for _ in range(top_n):
        mx = jnp.max(vals, axis=0, keepdims=True)
        first = jnp.min(jnp.where(vals == mx, blk, n_sel), axis=0, keepdims=True)
        pick = blk == first
        sel = jnp.where(pick & (mx > -jnp.inf), 1.0, sel)
        vals = jnp.where(pick, -jnp.inf, vals)
    selbias_ref[...] = jnp.where(sel > 0.5, 0.0, NEG)

    blocks_per_tile = tk // SEL_LEN

    def sel_tile(kt, carry, diagonal):
        m_i, acc = carry
        k0 = pl.multiple_of(kt * tk, tk)
        s_t = jnp.dot(kn_ref[0, 0, 0, pl.ds(k0, tk), :], qt, preferred_element_type=f32)
        sb = selbias_ref[pl.ds(pl.multiple_of(kt * blocks_per_tile, blocks_per_tile), blocks_per_tile), :]
        bias = jnp.concatenate([jnp.broadcast_to(sb[j:j + 1], (SEL_LEN, Q_BLOCK))
                                for j in range(blocks_per_tile)], axis=0)
        if diagonal:
            kpos = k0 + lax.broadcasted_iota(i32, (tk, 1), 0)
            bias = jnp.where(kpos <= t_row, bias, NEG)
        es, ms, alphas = [], [], []
        for p in range(HPG):
            c = slice(p * Q_BLOCK, (p + 1) * Q_BLOCK)
            sp = s_t[:, c] + bias
            m_new = jnp.maximum(m_i[:, c], jnp.max(sp, axis=0, keepdims=True))
            es.append(jnp.exp2(sp - m_new).astype(bf16))
            alphas.append(jnp.exp2(m_i[:, c] - m_new))
            ms.append(m_new)
        pv = jnp.dot(vt_ref[0, 0, 0, :, pl.ds(k0, tk)], jnp.concatenate(es, axis=1),
                     preferred_element_type=f32)
        return jnp.concatenate(ms, axis=1), jnp.concatenate(alphas, axis=1) * acc + pv

    n_kt = (s0 + Q_BLOCK + tk - 1) // tk
    cols = HPG * Q_BLOCK
    init = (jnp.full((1, cols), NEG, f32), jnp.zeros((V_ROWS, cols), f32))
    carry = lax.fori_loop(0, n_kt - 1, lambda kt, c: sel_tile(kt, c, False), init)
    _, acc_s = sel_tile(n_kt - 1, carry, True)
    o_s = acc_s[0:HEAD_DIM] * (1.0 / acc_s[HEAD_DIM:HEAD_DIM + 1])

    span = WINDOW + Q_BLOCK
    w0 = pl.multiple_of(jnp.maximum(s0 - WINDOW, 0), Q_BLOCK)
    s_w = jnp.dot(kn_ref[0, 1, 0, pl.ds(w0, span), :], qt, preferred_element_type=f32)
    wpos = w0 + lax.broadcasted_iota(i32, (span, 1), 0)
    bias_w = jnp.where((wpos <= t_row) & (wpos > t_row - WINDOW), 0.0, NEG)
    pws = []
    for p in range(HPG):
        sp = s_w[:, p * Q_BLOCK:(p + 1) * Q_BLOCK] + bias_w
        pws.append(jnp.exp2(sp - jnp.max(sp, axis=0, keepdims=True)).astype(bf16))
    acc_w = jnp.dot(vt_ref[0, 1, 0, :, pl.ds(w0, span)], jnp.concatenate(pws, axis=1),
                    preferred_element_type=f32)
    o_w = acc_w[0:HEAD_DIM] * (1.0 / acc_w[HEAD_DIM:HEAD_DIM + 1])

    gate = _sigmoid(g_ref[0, 0])
    outs = []
    for p in range(HPG):
        c = slice(p * Q_BLOCK, (p + 1) * Q_BLOCK)
        outs.append(gate[3 * p:3 * p + 1] * o_c[:, c] + gate[3 * p + 1:3 * p + 2] * o_s[:, c]
                    + gate[3 * p + 2:3 * p + 3] * o_w[:, c])
    o_ref[0] = jnp.concatenate(outs, axis=0).T.astype(bf16)


GATE_GROUP_ROWS = 16


def _attention(qt, gates, cmp_n, cmp_t, kn, vt):
    b, _, s = qt.shape
    n_chunk = cmp_n.shape[3]
    n_sel = s // SEL_LEN
    top_n = min(SEL_TOPN, n_sel)
    tk = min(512, s)
    cs = np.arange(n_chunk) * CMP_STRIDE
    ss = np.arange(n_sel) * SEL_LEN
    overlap = ((cs[None, :] < ss[:, None] + SEL_LEN) & (cs[None, :] + CMP_LEN > ss[:, None]))
    overlap[:, (s - CMP_LEN) // CMP_STRIDE + 1:] = False
    overlap = jnp.asarray(overlap.astype(np.float32))
    gw = HPG * HEAD_DIM
    return pl.pallas_call(
        functools.partial(_attn_kernel, seq=s, tk=tk, top_n=top_n),
        grid=(b, N_KV, s // Q_BLOCK),
        in_specs=[pl.BlockSpec((1, gw, Q_BLOCK), lambda bi, g, i: (bi, g, i)),
                  pl.BlockSpec((1, 1, GATE_GROUP_ROWS, Q_BLOCK), lambda bi, g, i: (bi, g, 0, i)),
                  pl.BlockSpec((1, 1, 1, n_chunk, HEAD_DIM), lambda bi, g, i: (bi, 0, g, 0, 0)),
                  pl.BlockSpec((1, 1, 1, HEAD_DIM, n_chunk), lambda bi, g, i: (bi, 1, g, 0, 0)),
                  pl.BlockSpec((1, 2, 1, s, HEAD_DIM), lambda bi, g, i: (bi, 0, g, 0, 0)),
                  pl.BlockSpec((1, 2, 1, V_ROWS, s), lambda bi, g, i: (bi, 0, g, 0, 0)),
                  pl.BlockSpec((n_sel, n_chunk), lambda bi, g, i: (0, 0))],
        out_specs=pl.BlockSpec((1, Q_BLOCK, gw), lambda bi, g, i: (bi, i, g)),
        out_shape=jax.ShapeDtypeStruct((b, s, D_ATTN), bf16),
        scratch_shapes=[pltpu.VMEM((n_sel, Q_BLOCK), f32)],
        compiler_params=_params(("parallel", "parallel", "arbitrary")),
        name="attention",
    )(qt, gates, cmp_n, cmp_t, kn, vt, overlap)


CONV_HALO = 32
CONV_ROWS = 32


def _conv_kernel(cur_ref, prev_ref, w_ref, b_ref, g_ref, bb_ref, o_ref, glu_ref):
    i = pl.program_id(1)
    ts = cur_ref.shape[1]
    dc = o_ref.shape[2]
    cur = cur_ref[0]
    prev = prev_ref[0]
    glu_prev = prev[:, :dc] * _sigmoid(prev[:, dc:])
    glu_ref[0:CONV_HALO] = jnp.where(i == 0, 0.0, glu_prev)
    glu_ref[CONV_HALO:CONV_HALO + ts] = cur[:, :dc] * _sigmoid(cur[:, dc:])
    lead = CONV_HALO - (CONV_WIDTH - 1)

    def chunk(r, _):
        r0 = pl.multiple_of(r * CONV_ROWS, CONV_ROWS)
        win = glu_ref[pl.ds(r0, CONV_ROWS + CONV_HALO), :]
        acc = jnp.zeros((CONV_ROWS, dc), f32)
        for j in range(CONV_WIDTH):
            acc = acc + w_ref[j:j + 1, :] * win[lead + j:lead + j + CONV_ROWS]
        y = _layer_norm(acc + b_ref[...], g_ref[...], bb_ref[...])
        o_ref[0, pl.ds(r0, CONV_ROWS), :] = _silu(y).astype(bf16)
        return 0

    lax.fori_loop(0, ts // CONV_ROWS, chunk, 0)


def _conv(conv_in, w_dw, b_dw, ln_g, ln_b, ts):
    b, s, dc2 = conv_in.shape
    dc = dc2 // 2
    per = ts // CONV_HALO
    row = lambda a: a.reshape(1, dc)
    return pl.pallas_call(
        _conv_kernel,
        grid=(b, s // ts),
        in_specs=[pl.BlockSpec((1, ts, dc2), lambda bi, i: (bi, i, 0)),
                  pl.BlockSpec((1, CONV_HALO, dc2), lambda bi, i: (bi, jnp.maximum(i * per - 1, 0), 0)),
                  pl.BlockSpec((CONV_WIDTH, dc), lambda bi, i: (0, 0)),
                  pl.BlockSpec((1, dc), lambda bi, i: (0, 0)),
                  pl.BlockSpec((1, dc), lambda bi, i: (0, 0)),
                  pl.BlockSpec((1, dc), lambda bi, i: (0, 0))],
        out_specs=pl.BlockSpec((1, ts, dc), lambda bi, i: (bi, i, 0)),
        out_shape=jax.ShapeDtypeStruct((b, s, dc), bf16),
        scratch_shapes=[pltpu.VMEM((CONV_HALO + ts, dc), f32)],
        compiler_params=_params(("parallel", "parallel")),
        name="conv",
    )(conv_in, conv_in, w_dw.reshape(CONV_WIDTH, dc), row(b_dw), row(ln_g), row(ln_b))


def _mix_route_kernel(a_ref, cv_ref, x_ref, mod_ref, wo_ref, lg_ref, lb_ref, wr_ref, rb_ref,
                      x1_ref, h2_ref, h2r_ref, idx_ref, wt_ref, cnt_ref):
    step = pl.program_id(0)
    tm = x_ref.shape[0]
    m = mod_ref[0]
    da = a_ref.shape[1]
    mix = (jnp.dot(a_ref[...], wo_ref[0:da, :], preferred_element_type=f32)
           + jnp.dot(cv_ref[...], wo_ref[da:, :], preferred_element_type=f32))
    x1 = _layer_norm(DEEPNORM_ALPHA * x_ref[...] + m[2:3] * mix, lg_ref[...], lb_ref[...])
    x1_ref[...] = x1
    h2 = x1 * (1.0 + m[4:5]) + m[3:4]
    h2_ref[...] = h2
    for cch in range(h2.shape[1] // LANES):
        h2r_ref[pl.ds(cch, tm, stride=SUBLANES), :] = h2[:, cch * LANES:(cch + 1) * LANES]

    score = _sigmoid(jnp.dot(h2, wr_ref[...], precision=HIGHEST, preferred_element_type=f32))
    sel = score + rb_ref[...]
    lane = lax.broadcasted_iota(i32, (tm, N_EXPERTS), 1)
    grp = lane >> 5
    gs = []
    for g in range(N_GROUPS):
        v = jnp.where(grp == g, sel, -jnp.inf)
        m1 = jnp.max(v, axis=-1, keepdims=True)
        i1 = jnp.min(jnp.where(v == m1, lane, N_EXPERTS), axis=-1, keepdims=True)
        m2 = jnp.max(jnp.where(lane == i1, -jnp.inf, v), axis=-1, keepdims=True)
        gs.append(m1 + m2)
    gmask = jnp.zeros((tm, N_EXPERTS), jnp.bool_)
    for g in range(N_GROUPS):
        rank = jnp.zeros((tm, 1), i32)
        for o in range(N_GROUPS):
            if o == g:
                continue
            beats = (gs[o] > gs[g]) | (gs[o] == gs[g]) if o < g else gs[o] > gs[g]
            rank = rank + beats.astype(i32)
        gmask = gmask | ((grp == g) & (rank < TOPK_GROUPS))
    cand = jnp.where(gmask, sel, -jnp.inf)
    lane_o = lax.broadcasted_iota(i32, (tm, LANES), 1)
    idx_out = jnp.zeros((tm, LANES), i32)
    wt_out = jnp.zeros((tm, LANES), f32)
    picked = jnp.zeros((tm, N_EXPERTS), f32)
    w_sum = jnp.zeros((tm, 1), f32)
    for k in range(TOP_K):
        mx = jnp.max(cand, axis=-1, keepdims=True)
        ik = jnp.min(jnp.where(cand == mx, lane, N_EXPERTS), axis=-1, keepdims=True)
        pick = lane == ik
        wk = jnp.sum(jnp.where(pick, score, 0.0), axis=-1, keepdims=True)
        cand = jnp.where(pick, -jnp.inf, cand)
        picked = jnp.where(pick, 1.0, picked)
        idx_out = jnp.where(lane_o == k, ik, idx_out)
        wt_out = jnp.where(lane_o == k, wk, wt_out)
        w_sum = w_sum + wk
    idx_ref[...] = idx_out
    wt_ref[...] = wt_out / w_sum * ROUTED_SCALE

    @pl.when(step == 0)
    def _():
        cnt_ref[...] = jnp.zeros_like(cnt_ref)

    cnt_ref[...] += jnp.sum(picked, axis=0, keepdims=True)


def _mix_route(attn, conv, x2, mod, w_out, ln_g, ln_b, w_router, router_bias, s, tm):
    t, d = x2.shape
    per = s // tm
    da = attn.shape[1]
    row = lambda a: a.reshape(1, -1)
    tile = lambda w: pl.BlockSpec((tm, w), lambda i: (i, 0))
    full = lambda a: pl.BlockSpec(a.shape, lambda i: (0,) * a.ndim)
    args = (attn, conv, x2, mod, w_out.astype(bf16), row(ln_g), row(ln_b), w_router, row(router_bias))
    return pl.pallas_call(
        _mix_route_kernel,
        grid=(t // tm,),
        in_specs=[tile(da), tile(conv.shape[1]), tile(d),
                  pl.BlockSpec((1, 6, d), lambda i: (i // per, 0, 0))] + [full(a) for a in args[4:]],
        out_specs=[tile(d), tile(d), pl.BlockSpec((tm * d // LANES, LANES), lambda i: (i, 0)),
                   tile(LANES), tile(LANES), pl.BlockSpec((1, N_EXPERTS), lambda i: (0, 0))],
        out_shape=[jax.ShapeDtypeStruct((t, d), f32), jax.ShapeDtypeStruct((t, d), f32),
                   jax.ShapeDtypeStruct((t * d // LANES, LANES), f32),
                   jax.ShapeDtypeStruct((t, LANES), i32), jax.ShapeDtypeStruct((t, LANES), f32),
                   jax.ShapeDtypeStruct((1, N_EXPERTS), f32)],
        compiler_params=_params(("arbitrary",)),
        name="mix_route",
    )(*args)


def _positions_kernel(idx_ref, start_ref, o_ref, run_ref):
    step = pl.program_id(0)
    tm = idx_ref.shape[0]

    @pl.when(step == 0)
    def _():
        run_ref[...] = jnp.zeros_like(run_ref)

    idx = idx_ref[...]
    lane = lax.broadcasted_iota(i32, (tm, N_EXPERTS), 1)
    onehot = jnp.zeros((tm, N_EXPERTS), f32)
    for k in range(TOP_K):
        onehot = jnp.where(lane == idx[:, k:k + 1], 1.0, onehot)
    r = lax.broadcasted_iota(i32, (tm, tm), 0)
    c = lax.broadcasted_iota(i32, (tm, tm), 1)
    below = jnp.where(r > c, 1.0, 0.0).astype(bf16)
    prior = jnp.dot(below, onehot.astype(bf16), preferred_element_type=f32)
    pos = prior + run_ref[...] + start_ref[...]
    lane_o = lax.broadcasted_iota(i32, (tm, LANES), 1)
    out = jnp.zeros((tm, LANES), i32)
    for k in range(TOP_K):
        dk = jnp.sum(jnp.where(lane == idx[:, k:k + 1], pos, 0.0), axis=-1, keepdims=True)
        out = jnp.where(lane_o == k, dk.astype(i32), out)
    o_ref[...] = out
    run_ref[...] += jnp.sum(onehot, axis=0, keepdims=True)


def _positions(idx, seg_start, tm):
    t = idx.shape[0]
    return pl.pallas_call(
        _positions_kernel,
        grid=(t // tm,),
        in_specs=[pl.BlockSpec((tm, LANES), lambda i: (i, 0)),
                  pl.BlockSpec((1, N_EXPERTS), lambda i: (0, 0))],
        out_specs=pl.BlockSpec((tm, LANES), lambda i: (i, 0)),
        out_shape=jax.ShapeDtypeStruct((t, LANES), i32),
        scratch_shapes=[pltpu.VMEM((1, N_EXPERTS), f32)],
        compiler_params=_params(("arbitrary",)),
        name="positions",
    )(idx, seg_start)


ISSUE_UNROLL = 8


def _dispatch_kernel(dest_hbm, h_ref, o_hbm, dest_smem, sem_i, sem):
    step = pl.program_id(0)
    n = dest_smem.shape[0]
    load = pltpu.make_async_copy(dest_hbm.at[pl.ds(pl.multiple_of(step * n, n), n)], dest_smem, sem_i)
    load.start()
    load.wait()

    def issue(j, _):
        src = pl.multiple_of((j >> 3) * SUBLANES, SUBLANES)
        dst = pl.multiple_of(dest_smem[j] * SUBLANES, SUBLANES)
        pltpu.make_async_copy(h_ref.at[pl.ds(src, SUBLANES)], o_hbm.at[pl.ds(dst, SUBLANES)], sem).start()
        return 0

    lax.fori_loop(0, n, issue, 0, unroll=ISSUE_UNROLL)
    everything = o_hbm.at[pl.ds(0, n * SUBLANES)]
    pltpu.make_async_copy(everything, everything, sem).wait()


def _dispatch(dest_flat, h2r, n_buf, tm):
    per_tok = SUBLANES
    t = h2r.shape[0] // per_tok
    return pl.pallas_call(
        _dispatch_kernel,
        grid=(t // tm,),
        in_specs=[pl.BlockSpec(memory_space=pl.ANY),
                  pl.BlockSpec((tm * per_tok, LANES), lambda i: (i, 0))],
        out_specs=pl.BlockSpec(memory_space=pl.ANY),
        out_shape=jax.ShapeDtypeStruct((n_buf * per_tok, LANES), f32),
        scratch_shapes=[pltpu.SMEM((tm * TOP_K,), i32), pltpu.SemaphoreType.DMA, pltpu.SemaphoreType.DMA],
        compiler_params=_params(("arbitrary",)),
        name="dispatch",
    )(dest_flat, h2r)


def _experts_kernel(be_ref, bv_ref, nu_ref, x_ref, wg_ref, wu_ref, wd_ref, o_ref, wg_s, wu_s, wd_s):
    j = pl.program_id(0)
    prev = be_ref[jnp.maximum(j - 1, 0)]
    used = j < nu_ref[0]
    d = wg_s.shape[0]

    @pl.when(used & ((j == 0) | (be_ref[j] != prev)))
    def _():
        wg_s[...] = wg_ref[0].astype(bf16)
        wu_s[...] = wu_ref[0].astype(bf16)
        wd_s[...] = wd_ref[0].astype(bf16)

    @pl.when(used)
    def _():
        live = lax.broadcasted_iota(i32, (ROW_BLOCK, 1), 0) < bv_ref[j]
        hg = jnp.zeros((ROW_BLOCK, wg_s.shape[1]), f32)
        hu = jnp.zeros((ROW_BLOCK, wg_s.shape[1]), f32)
        for c in range(d // LANES):
            xc = jnp.where(live, x_ref[pl.ds(c, ROW_BLOCK, stride=SUBLANES), :], 0.0).astype(bf16)
            hg = hg + jnp.dot(xc, wg_s[c * LANES:(c + 1) * LANES, :], preferred_element_type=f32)
            hu = hu + jnp.dot(xc, wu_s[c * LANES:(c + 1) * LANES, :], preferred_element_type=f32)
        hid = (_silu(hg) * hu).astype(bf16)
        o_ref[...] = jnp.dot(hid, wd_s[...], preferred_element_type=f32)


def _experts(blk_e, blk_valid, n_used, xs, w_gate, w_up, w_down):
    d, f = w_gate.shape[1], w_gate.shape[2]
    per_tok = d // LANES
    n_buf = xs.shape[0] // per_tok
    n_blk = n_buf // ROW_BLOCK
    rows = lambda j, be, bv, nu: (jnp.minimum(j, nu[0] - 1), 0)
    wsel = lambda j, be, bv, nu: (be[j], 0, 0)
    return pl.pallas_call(
        _experts_kernel,
        grid_spec=pltpu.PrefetchScalarGridSpec(
            num_scalar_prefetch=3,
            grid=(n_blk,),
            in_specs=[pl.BlockSpec((ROW_BLOCK * per_tok, LANES), rows),
                      pl.BlockSpec((1, d, f), wsel),
                      pl.BlockSpec((1, d, f), wsel),
                      pl.BlockSpec((1, f, d), wsel)],
            out_specs=pl.BlockSpec((ROW_BLOCK, d), rows),
            scratch_shapes=[pltpu.VMEM((d, f), bf16), pltpu.VMEM((d, f), bf16), pltpu.VMEM((f, d), bf16)]),
        out_shape=jax.ShapeDtypeStruct((n_buf, d), f32),
        compiler_params=_params(("arbitrary",)),
        name="experts",
    )(blk_e, blk_valid, n_used, xs, w_gate, w_up, w_down)


def _combine_kernel(dest_hbm, ys_hbm, wt_ref, h_ref, x1_ref, mod_ref, wsg_ref, wsu_ref, wsd_ref,
                    lg_ref, lb_ref, o_ref, dest_smem, rows_ref, sem_i, sem):
    step = pl.program_id(0)
    n = dest_smem.shape[0]
    load = pltpu.make_async_copy(dest_hbm.at[pl.ds(pl.multiple_of(step * n, n), n)], dest_smem, sem_i)
    load.start()
    load.wait()

    def issue(j, _):
        pltpu.make_async_copy(ys_hbm.at[pl.ds(dest_smem[j], 1)],
                              rows_ref.at[j & 7, pl.ds(j >> 3, 1)], sem).start()
        return 0

    lax.fori_loop(0, n, issue, 0, unroll=ISSUE_UNROLL)

    h = h_ref[...].astype(bf16)
    hg = jnp.dot(h, wsg_ref[...], preferred_element_type=f32)
    hu = jnp.dot(h, wsu_ref[...], preferred_element_type=f32)
    y = jnp.dot((_silu(hg) * hu).astype(bf16), wsd_ref[...], preferred_element_type=f32)

    pltpu.make_async_copy(rows_ref, rows_ref, sem).wait()
    wt = wt_ref[...]
    for k in range(TOP_K):
        y = y + wt[:, k:k + 1] * rows_ref[k]
    m = mod_ref[0]
    o_ref[...] = _layer_norm(DEEPNORM_ALPHA * x1_ref[...] + m[5:6] * y, lg_ref[...], lb_ref[...])


def _combine(dest_flat, ys, wts, h2, x1, mod, w_s_gate, w_s_up, w_s_down, ln_g, ln_b, s, tm):
    t, d = x1.shape
    per = s // tm
    row = lambda a: a.reshape(1, -1)
    tile = lambda w: pl.BlockSpec((tm, w), lambda i: (i, 0))
    full = lambda a: pl.BlockSpec(a.shape, lambda i: (0,) * a.ndim)
    tail = (w_s_gate.astype(bf16), w_s_up.astype(bf16), w_s_down.astype(bf16), row(ln_g), row(ln_b))
    return pl.pallas_call(
        _combine_kernel,
        grid=(t // tm,),
        in_specs=[pl.BlockSpec(memory_space=pl.ANY), pl.BlockSpec(memory_space=pl.ANY),
                  tile(LANES), tile(d), tile(d),
                  pl.BlockSpec((1, 6, d), lambda i: (i // per, 0, 0))] + [full(a) for a in tail],
        out_specs=tile(d),
        out_shape=jax.ShapeDtypeStruct((t, d), f32),
        scratch_shapes=[pltpu.SMEM((tm * TOP_K,), i32), pltpu.VMEM((TOP_K, tm, d), f32),
                        pltpu.SemaphoreType.DMA, pltpu.SemaphoreType.DMA],
        compiler_params=_params(("arbitrary",)),
        name="combine",
    )(dest_flat, ys, wts, h2, x1, mod, *tail)


def _layer(x, mod, w_in, pe, w_cmp1, w_cmp2, w_dw, b_dw, conv_ln_g, conv_ln_b, w_out, ln1_g, ln1_b,
           w_router, router_bias, w_e_gate, w_e_up, w_e_down, w_s_gate, w_s_up, w_s_down, ln2_g, ln2_b):
    b, s, d = x.shape
    t = b * s
    tm = min(512, s)
    qt, vt, gt, kvc, kn, conv_in = _in_proj(x, mod, w_in, tm)
    cmp_n, cmp_t = _compress(kvc, pe, w_cmp1, w_cmp2)
    gates = gt[:, :3 * N_HEADS].reshape(b, N_KV, 3 * HPG, s)
    gates = jnp.pad(gates, ((0, 0), (0, 0), (0, GATE_GROUP_ROWS - 3 * HPG), (0, 0)))
    attn = _attention(qt, gates, cmp_n, cmp_t, kn, vt)
    conv = _conv(conv_in, w_dw, b_dw, conv_ln_g, conv_ln_b, tm)

    tr = min(256, s)
    x1, h2, h2r, idx, wts, counts = _mix_route(attn.reshape(t, -1), conv.reshape(t, -1), x.reshape(t, d), mod,
                                               w_out, ln1_g, ln1_b, w_router, router_bias, s, tr)
    counts = counts[0].astype(i32)
    padded = (counts + ROW_BLOCK - 1) // ROW_BLOCK * ROW_BLOCK
    seg_end = jnp.cumsum(padded)
    seg_start = seg_end - padded
    n_blk = -(-(t * TOP_K + N_EXPERTS * (ROW_BLOCK - 1)) // ROW_BLOCK)
    blk_row0 = jnp.arange(n_blk, dtype=i32) * ROW_BLOCK
    blk_e = jnp.minimum(jnp.searchsorted(seg_end, blk_row0, side='right'), N_EXPERTS - 1).astype(i32)
    blk_valid = jnp.clip(seg_start[blk_e] + counts[blk_e] - blk_row0, 0, ROW_BLOCK).astype(i32)
    n_used = (seg_end[-1:] // ROW_BLOCK).astype(i32)

    dest = _positions(idx, seg_start.astype(f32).reshape(1, N_EXPERTS), tr)
    dest_flat = dest[:, :TOP_K].reshape(-1)
    xs = _dispatch(dest_flat, h2r, n_blk * ROW_BLOCK, min(512, s))
    ys = _experts(blk_e, blk_valid, n_used, xs, w_e_gate, w_e_up, w_e_down)
    out = _combine(dest_flat, ys, wts, h2, x1, mod, w_s_gate, w_s_up, w_s_down, ln2_g, ln2_b, s, min(128, s))
    return out.reshape(b, s, d)


def kernel(x, c, w_ada, b_ada, w_in, pe_k, pe_v, w_cmp_k1, w_cmp_k2, w_cmp_v1, w_cmp_v2, w_dw, b_dw,
           conv_ln_g, conv_ln_b, w_out, ln1_g, ln1_b, w_router, router_bias, w_e_gate, w_e_up, w_e_down,
           w_s_gate, w_s_up, w_s_down, ln2_g, ln2_b):
    assert w_ada.shape[0] == DEPTH
    layer = lambda a: a.reshape(a.shape[1:])
    mod = _ada(c, layer(w_ada), layer(b_ada))
    return _layer(x, mod, layer(w_in), jnp.concatenate([pe_k, pe_v]),
                  jnp.concatenate([w_cmp_k1, w_cmp_v1]), jnp.concatenate([w_cmp_k2, w_cmp_v2]),
                  *[layer(a) for a in (w_dw, b_dw, conv_ln_g, conv_ln_b, w_out, ln1_g, ln1_b, w_router,
                                       router_bias, w_e_gate, w_e_up, w_e_down, w_s_gate, w_s_up, w_s_down,
                                       ln2_g, ln2_b)])
```

```python
import functools

import jax
import jax.numpy as jnp
import numpy as np
from jax import lax
from jax.experimental import pallas as pl
from jax.experimental.pallas import tpu as pltpu

N_HEADS = 8
N_KV = 2
HPG = N_HEADS // N_KV
HEAD_DIM = 64
D_ATTN = N_HEADS * HEAD_DIM
D_KV = N_KV * HEAD_DIM
CONV_WIDTH = 31
CMP_LEN = 32
CMP_STRIDE = 16
CMP_HID = 256
SEL_LEN = 64
SEL_TOPN = 16
WINDOW = 512
Q_BLOCK = 128
N_EXPERTS = 256
TOP_K = 8
N_GROUPS = 8
TOPK_GROUPS = 4
ROUTED_SCALE = 2.5
LN_EPS = 1e-5
DEPTH = 1
DEEPNORM_ALPHA = (2 * DEPTH) ** 0.25

LANES = 128
SUBLANES = 8
ROW_BLOCK = 256
NEG = -1e30
HIGHEST = lax.Precision.HIGHEST
VMEM_LIMIT = 48 * 1024 * 1024

f32 = jnp.float32
bf16 = jnp.bfloat16
i32 = jnp.int32
u32 = jnp.uint32


def _params(sem, vmem=VMEM_LIMIT):
    return pltpu.CompilerParams(dimension_semantics=sem, vmem_limit_bytes=vmem)


def _sigmoid(v):
    return 1.0 / (1.0 + jnp.exp(-v))


def _silu(v):
    return v * _sigmoid(v)


def _layer_norm(v, g, b):
    mu = jnp.mean(v, axis=-1, keepdims=True)
    var = jnp.mean(jnp.square(v - mu), axis=-1, keepdims=True)
    return (v - mu) * lax.rsqrt(var + LN_EPS) * g + b


def _dot_nt(a, b):
    return lax.dot_general(a, b, (((1,), (1,)), ((), ())), preferred_element_type=f32)


def _store_packed_rows(ref, v):
    n, d = v.shape
    half = d // 2
    bits = lax.bitcast_convert_type(v.astype(bf16).astype(f32), u32)
    words = bits[:, half:] | (bits[:, :half] >> 16)
    r = half // LANES
    for c in range(r):
        ref[pl.ds(c, n, stride=r), :] = words[:, c * LANES:(c + 1) * LANES]


def _unpack_words(w):
    return (lax.bitcast_convert_type(w << 16, f32),
            lax.bitcast_convert_type(w & jnp.uint32(0xFFFF0000), f32))


def _ada_kernel(c_ref, w_ref, b_ref, o_ref):
    c = c_ref[...]
    o_ref[...] = jnp.dot(_silu(c), w_ref[...], precision=HIGHEST,
                         preferred_element_type=f32) + b_ref[...]


def _ada(c, w_ada, b_ada):
    b, d = c.shape
    n = w_ada.shape[1]
    rows = 8
    c_pad = jnp.zeros((rows, d), f32).at[:b].set(c)
    tn = 1024
    out = pl.pallas_call(
        _ada_kernel,
        grid=(n // tn,),
        in_specs=[pl.BlockSpec((rows, d), lambda j: (0, 0)),
                  pl.BlockSpec((d, tn), lambda j: (0, j)),
                  pl.BlockSpec((1, tn), lambda j: (0, j))],
        out_specs=pl.BlockSpec((rows, tn), lambda j: (0, j)),
        out_shape=jax.ShapeDtypeStruct((rows, n), f32),
        compiler_params=_params(("arbitrary",)),
        name="ada",
    )(c_pad, w_ada, b_ada.reshape(1, n))
    return out[:b].reshape(b, 6, d)


GATE_ROWS = 32
V_ROWS = HEAD_DIM + 16
Q_SCALE = HEAD_DIM ** -0.5 * 1.4426950408889634


def _in_proj_kernel(x_ref, mod_ref, wt_ref, wc_ref, wk_ref, wv_ref,
                    qt_ref, vt_ref, gt_ref, kvc_ref, kn_ref, cv_ref):
    m = mod_ref[0]
    h = (x_ref[0] * (1.0 + m[1:2]) + m[0:1]).astype(bf16)
    res_t = _dot_nt(wt_ref[...], h)
    qt_ref[0] = (res_t[0:D_ATTN] * Q_SCALE).astype(bf16)
    ones = jnp.ones((V_ROWS - HEAD_DIM, res_t.shape[1]), bf16)
    for j in range(2):
        for g in range(N_KV):
            off = D_ATTN + (j * N_KV + g) * HEAD_DIM
            vt_ref[0, j, g, 0:HEAD_DIM, :] = res_t[off:off + HEAD_DIM].astype(bf16)
            vt_ref[0, j, g, HEAD_DIM:V_ROWS, :] = ones
    gt_ref[0] = res_t[D_ATTN + 2 * D_KV:]
    kvc_ref[0] = jnp.dot(h, wc_ref[...], preferred_element_type=f32)
    kn = jnp.dot(h, wk_ref[...], preferred_element_type=f32).astype(bf16)
    for j in range(2):
        for g in range(N_KV):
            off = (j * N_KV + g) * HEAD_DIM
            kn_ref[0, j, g] = kn[:, off:off + HEAD_DIM]
    cv_ref[0] = jnp.dot(h, wv_ref[...], preferred_element_type=f32)


def _in_proj(x, mod, w_in, tm):
    b, s, d = x.shape
    o = 0
    wq = w_in[:, o:o + D_ATTN]; o += D_ATTN
    wkc = w_in[:, o:o + 2 * D_KV]; o += 2 * D_KV
    wk_s = w_in[:, o:o + D_KV]; o += D_KV
    wv_s = w_in[:, o:o + D_KV]; o += D_KV
    wk_w = w_in[:, o:o + D_KV]; o += D_KV
    wv_w = w_in[:, o:o + D_KV]; o += D_KV
    wg = w_in[:, o:o + 3 * N_HEADS]; o += 3 * N_HEADS
    wcv = w_in[:, o:]
    d_conv2 = wcv.shape[1]
    wg = jnp.zeros((d, GATE_ROWS), f32).at[:, :3 * N_HEADS].set(wg)
    wt = jnp.concatenate([wq, wv_s, wv_w, wg], axis=1).T
    ws = [w.astype(bf16) for w in (wt, wkc, jnp.concatenate([wk_s, wk_w], axis=1), wcv)]
    full = lambda a: pl.BlockSpec(a.shape, lambda bi, i: (0, 0))
    return pl.pallas_call(
        _in_proj_kernel,
        grid=(b, s // tm),
        in_specs=[pl.BlockSpec((1, tm, d), lambda bi, i: (bi, i, 0)),
                  pl.BlockSpec((1, 6, d), lambda bi, i: (bi, 0, 0))] + [full(w) for w in ws],
        out_specs=[pl.BlockSpec((1, D_ATTN, tm), lambda bi, i: (bi, 0, i)),
                   pl.BlockSpec((1, 2, N_KV, V_ROWS, tm), lambda bi, i: (bi, 0, 0, 0, i)),
                   pl.BlockSpec((1, GATE_ROWS, tm), lambda bi, i: (bi, 0, i)),
                   pl.BlockSpec((1, tm, 2 * D_KV), lambda bi, i: (bi, i, 0)),
                   pl.BlockSpec((1, 2, N_KV, tm, HEAD_DIM), lambda bi, i: (bi, 0, 0, i, 0)),
                   pl.BlockSpec((1, tm, d_conv2), lambda bi, i: (bi, i, 0))],
        out_shape=[jax.ShapeDtypeStruct((b, D_ATTN, s), bf16),
                   jax.ShapeDtypeStruct((b, 2, N_KV, V_ROWS, s), bf16),
                   jax.ShapeDtypeStruct((b, GATE_ROWS, s), f32),
                   jax.ShapeDtypeStruct((b, s, 2 * D_KV), f32),
                   jax.ShapeDtypeStruct((b, 2, N_KV, s, HEAD_DIM), bf16),
                   jax.ShapeDtypeStruct((b, s, d_conv2), f32)],
        compiler_params=_params(("parallel", "parallel")),
        name="in_proj",
    )(x, mod, *ws)


def _compress_kernel(c_ref, pe_ref, w1_ref, w2_ref, w2t_ref, o_ref, ot_ref):
    c = c_ref[0, 0, 0]
    n_chunk = c.shape[0]
    a = jnp.dot((c + pe_ref[0, 0]).astype(bf16), w1_ref[0, 0], preferred_element_type=f32)
    bm = jnp.dot((c + pe_ref[0, 1]).astype(bf16), w1_ref[0, 1], preferred_element_type=f32)
    hid = a + pltpu.roll(bm, n_chunk - 1, 0)
    act = 0.5 * hid * (1.0 + jnp.tanh(0.7978845608028654 * (hid + 0.044715 * (hid * hid * hid))))
    act = act.astype(bf16)
    o_ref[0, 0, 0] = jnp.dot(act, w2_ref[0], preferred_element_type=f32).astype(bf16)
    ot_ref[0, 0, 0] = _dot_nt(w2t_ref[0], act).astype(bf16)


def _compress(kvc, pe, w1, w2):
    b, s, _ = kvc.shape
    n_chunk = s // CMP_STRIDE
    half = CMP_STRIDE * HEAD_DIM
    c = kvc.reshape(b, n_chunk, CMP_STRIDE, 2, N_KV, HEAD_DIM).transpose(0, 3, 4, 1, 2, 5)
    c = c.reshape(b, 2, N_KV, n_chunk, half)
    pe2 = pe.reshape(2, 2, 1, half)
    w1h = w1.reshape(2, 2, half, CMP_HID).astype(bf16)
    w2b = w2.astype(bf16)
    w2t = w2b.transpose(0, 2, 1)
    return pl.pallas_call(
        _compress_kernel,
        grid=(b, 2, N_KV),
        in_specs=[pl.BlockSpec((1, 1, 1, n_chunk, half), lambda bi, j, g: (bi, j, g, 0, 0)),
                  pl.BlockSpec((1, 2, 1, half), lambda bi, j, g: (j, 0, 0, 0)),
                  pl.BlockSpec((1, 2, half, CMP_HID), lambda bi, j, g: (j, 0, 0, 0)),
                  pl.BlockSpec((1, CMP_HID, HEAD_DIM), lambda bi, j, g: (j, 0, 0)),
                  pl.BlockSpec((1, HEAD_DIM, CMP_HID), lambda bi, j, g: (j, 0, 0))],
        out_specs=[pl.BlockSpec((1, 1, 1, n_chunk, HEAD_DIM), lambda bi, j, g: (bi, j, g, 0, 0)),
                   pl.BlockSpec((1, 1, 1, HEAD_DIM, n_chunk), lambda bi, j, g: (bi, j, g, 0, 0))],
        out_shape=[jax.ShapeDtypeStruct((b, 2, N_KV, n_chunk, HEAD_DIM), bf16),
                   jax.ShapeDtypeStruct((b, 2, N_KV, HEAD_DIM, n_chunk), bf16)],
        compiler_params=_params(("parallel", "parallel", "parallel")),
        name="compress",
    )(c, pe2, w1h, w2b, w2t)


def _attn_kernel(q_ref, g_ref, kc_ref, vc_ref, kn_ref, vt_ref, ov_ref, o_ref, selbias_ref, *, seq, tk, top_n):
    i = pl.program_id(2)
    s0 = i * Q_BLOCK
    n_cmp_rows = kc_ref.shape[3]
    n_sel = seq // SEL_LEN

    q4 = q_ref[0]
    qt = jnp.concatenate([q4[p * HEAD_DIM:(p + 1) * HEAD_DIM, :] for p in range(HPG)], axis=1)
    t_row = s0 + lax.broadcasted_iota(i32, (1, Q_BLOCK), 1)

    s_c = jnp.dot(kc_ref[0, 0, 0], qt, preferred_element_type=f32)
    cmp_end = lax.broadcasted_iota(i32, (n_cmp_rows, 1), 0) * CMP_STRIDE + (CMP_LEN - 1)
    bias_c = jnp.where(cmp_end <= t_row, 0.0, NEG)
    any_c = t_row >= CMP_LEN - 1
    p_sum = jnp.zeros((n_cmp_rows, Q_BLOCK), f32)
    pcs = []
    for p in range(HPG):
        sp = s_c[:, p * Q_BLOCK:(p + 1) * Q_BLOCK] + bias_c
        e = jnp.exp2(sp - jnp.max(sp, axis=0, keepdims=True))
        pn = e * jnp.where(any_c, 1.0 / jnp.sum(e, axis=0, keepdims=True), 0.0)
        p_sum = p_sum + pn
        pcs.append(pn.astype(bf16))
    o_c = jnp.dot(vc_ref[0, 0, 0], jnp.concatenate(pcs, axis=1), preferred_element_type=f32)

    imp = jnp.dot(ov_ref[...], p_sum, precision=HIGHEST, preferred_element_type=f32)
    blk = lax.broadcasted_iota(i32, (n_sel, Q_BLOCK), 0)
    cur = t_row >> 6
    forced = (blk == 0) | (blk == cur) | (blk == cur - 1)
    vals = jnp.where(forced, jnp.inf, jnp.where(blk <= cur, imp, -jnp.inf))
    sel = jnp.zeros((n_sel, Q_BLOCK), f32)
    for _ in range(top_n):
        mx = jnp.max(vals, axis=0, keepdims=True)
        first = jnp.min(jnp.where(vals == mx, blk, n_sel), axis=0, keepdims=True)
        pick = blk == first
        sel = jnp.where(pick & (mx > -jnp.inf), 1.0, sel)
        vals = jnp.where(pick, -jnp.inf, vals)
    selbias_ref[...] = jnp.where(sel > 0.5, 0.0, NEG)

    blocks_per_tile = tk // SEL_LEN

    def sel_tile(kt, carry, diagonal):
        m_i, acc = carry
        k0 = pl.multiple_of(kt * tk, tk)
        s_t = jnp.dot(kn_ref[0, 0, 0, pl.ds(k0, tk), :], qt, preferred_element_type=f32)
        sb = selbias_ref[pl.ds(pl.multiple_of(kt * blocks_per_tile, blocks_per_tile), blocks_per_tile), :]
        bias = jnp.concatenate([jnp.broadcast_to(sb[j:j + 1], (SEL_LEN, Q_BLOCK))
                                for j in range(blocks_per_tile)], axis=0)
        if diagonal:
            kpos = k0 + lax.broadcasted_iota(i32, (tk, 1), 0)
            bias = jnp.where(kpos <= t_row, bias, NEG)
        es, ms, alphas = [], [], []
        for p in range(HPG):
            c = slice(p * Q_BLOCK, (p + 1) * Q_BLOCK)
            sp = s_t[:, c] + bias
            m_new = jnp.maximum(m_i[:, c], jnp.max(sp, axis=0, keepdims=True))
            es.append(jnp.exp2(sp - m_new).astype(bf16))
            alphas.append(jnp.exp2(m_i[:, c] - m_new))
            ms.append(m_new)
        pv = jnp.dot(vt_ref[0, 0, 0, :, pl.ds(k0, tk)], jnp.concatenate(es, axis=1),
                     preferred_element_type=f32)
        return jnp.concatenate(ms, axis=1), jnp.concatenate(alphas, axis=1) * acc + pv

    n_kt = (s0 + Q_BLOCK + tk - 1) // tk
    cols = HPG * Q_BLOCK
    init = (jnp.full((1, cols), NEG, f32), jnp.zeros((V_ROWS, cols), f32))
    carry = lax.fori_loop(0, n_kt - 1, lambda kt, c: sel_tile(kt, c, False), init)
    _, acc_s = sel_tile(n_kt - 1, carry, True)
    o_s = acc_s[0:HEAD_DIM] * (1.0 / acc_s[HEAD_DIM:HEAD_DIM + 1])

    span = WINDOW + Q_BLOCK
    w0 = pl.multiple_of(jnp.maximum(s0 - WINDOW, 0), Q_BLOCK)
    s_w = jnp.dot(kn_ref[0, 1, 0, pl.ds(w0, span), :], qt, preferred_element_type=f32)
    wpos = w0 + lax.broadcasted_iota(i32, (span, 1), 0)
    bias_w = jnp.where((wpos <= t_row) & (wpos > t_row - WINDOW), 0.0, NEG)
    pws = []
    for p in range(HPG):
        sp = s_w[:, p * Q_BLOCK:(p + 1) * Q_BLOCK] + bias_w
        pws.append(jnp.exp2(sp - jnp.max(sp, axis=0, keepdims=True)).astype(bf16))
    acc_w = jnp.dot(vt_ref[0, 1, 0, :, pl.ds(w0, span)], jnp.concatenate(pws, axis=1),
                    preferred_element_type=f32)
    o_w = acc_w[0:HEAD_DIM] * (1.0 / acc_w[HEAD_DIM:HEAD_DIM + 1])

    gate = _sigmoid(g_ref[0, 0])
    outs = []
    for p in range(HPG):
        c = slice(p * Q_BLOCK, (p + 1) * Q_BLOCK)
        outs.append(gate[3 * p:3 * p + 1] * o_c[:, c] + gate[3 * p + 1:3 * p + 2] * o_s[:, c]
                    + gate[3 * p + 2:3 * p + 3] * o_w[:, c])
    o_ref[0] = jnp.concatenate(outs, axis=0).T.astype(bf16)


GATE_GROUP_ROWS = 16


def _attention(qt, gates, cmp_n, cmp_t, kn, vt):
    b, _, s = qt.shape
    n_chunk = cmp_n.shape[3]
    n_sel = s // SEL_LEN
    top_n = min(SEL_TOPN, n_sel)
    tk = min(512, s)
    cs = np.arange(n_chunk) * CMP_STRIDE
    ss = np.arange(n_sel) * SEL_LEN
    overlap = ((cs[None, :] < ss[:, None] + SEL_LEN) & (cs[None, :] + CMP_LEN > ss[:, None]))
    overlap[:, (s - CMP_LEN) // CMP_STRIDE + 1:] = False
    overlap = jnp.asarray(overlap.astype(np.float32))
    gw = HPG * HEAD_DIM
    return pl.pallas_call(
        functools.partial(_attn_kernel, seq=s, tk=tk, top_n=top_n),
        grid=(b, N_KV, s // Q_BLOCK),
        in_specs=[pl.BlockSpec((1, gw, Q_BLOCK), lambda bi, g, i: (bi, g, i)),
                  pl.BlockSpec((1, 1, GATE_GROUP_ROWS, Q_BLOCK), lambda bi, g, i: (bi, g, 0, i)),
                  pl.BlockSpec((1, 1, 1, n_chunk, HEAD_DIM), lambda bi, g, i: (bi, 0, g, 0, 0)),
                  pl.BlockSpec((1, 1, 1, HEAD_DIM, n_chunk), lambda bi, g, i: (bi, 1, g, 0, 0)),
                  pl.BlockSpec((1, 2, 1, s, HEAD_DIM), lambda bi, g, i: (bi, 0, g, 0, 0)),
                  pl.BlockSpec((1, 2, 1, V_ROWS, s), lambda bi, g, i: (bi, 0, g, 0, 0)),
                  pl.BlockSpec((n_sel, n_chunk), lambda bi, g, i: (0, 0))],
        out_specs=pl.BlockSpec((1, Q_BLOCK, gw), lambda bi, g, i: (bi, i, g)),
        out_shape=jax.ShapeDtypeStruct((b, s, D_ATTN), bf16),
        scratch_shapes=[pltpu.VMEM((n_sel, Q_BLOCK), f32)],
        compiler_params=_params(("parallel", "parallel", "arbitrary")),
        name="attention",
    )(qt, gates, cmp_n, cmp_t, kn, vt, overlap)


CONV_HALO = 32
CONV_ROWS = 32


def _conv_kernel(cur_ref, prev_ref, w_ref, b_ref, g_ref, bb_ref, o_ref, glu_ref):
    i = pl.program_id(1)
    ts = cur_ref.shape[1]
    dc = o_ref.shape[2]
    cur = cur_ref[0]
    prev = prev_ref[0]
    glu_prev = prev[:, :dc] * _sigmoid(prev[:, dc:])
    glu_ref[0:CONV_HALO] = jnp.where(i == 0, 0.0, glu_prev)
    glu_ref[CONV_HALO:CONV_HALO + ts] = cur[:, :dc] * _sigmoid(cur[:, dc:])
    lead = CONV_HALO - (CONV_WIDTH - 1)

    def chunk(r, _):
        r0 = pl.multiple_of(r * CONV_ROWS, CONV_ROWS)
        win = glu_ref[pl.ds(r0, CONV_ROWS + CONV_HALO), :]
        acc = jnp.zeros((CONV_ROWS, dc), f32)
        for j in range(CONV_WIDTH):
            acc = acc + w_ref[j:j + 1, :] * win[lead + j:lead + j + CONV_ROWS]
        y = _layer_norm(acc + b_ref[...], g_ref[...], bb_ref[...])
        o_ref[0, pl.ds(r0, CONV_ROWS), :] = _silu(y).astype(bf16)
        return 0

    lax.fori_loop(0, ts // CONV_ROWS, chunk, 0)


def _conv(conv_in, w_dw, b_dw, ln_g, ln_b, ts):
    b, s, dc2 = conv_in.shape
    dc = dc2 // 2
    per = ts // CONV_HALO
    row = lambda a: a.reshape(1, dc)
    return pl.pallas_call(
        _conv_kernel,
        grid=(b, s // ts),
        in_specs=[pl.BlockSpec((1, ts, dc2), lambda bi, i: (bi, i, 0)),
                  pl.BlockSpec((1, CONV_HALO, dc2), lambda bi, i: (bi, jnp.maximum(i * per - 1, 0), 0)),
                  pl.BlockSpec((CONV_WIDTH, dc), lambda bi, i: (0, 0)),
                  pl.BlockSpec((1, dc), lambda bi, i: (0, 0)),
                  pl.BlockSpec((1, dc), lambda bi, i: (0, 0)),
                  pl.BlockSpec((1, dc), lambda bi, i: (0, 0))],
        out_specs=pl.BlockSpec((1, ts, dc), lambda bi, i: (bi, i, 0)),
        out_shape=jax.ShapeDtypeStruct((b, s, dc), bf16),
        scratch_shapes=[pltpu.VMEM((CONV_HALO + ts, dc), f32)],
        compiler_params=_params(("parallel", "parallel")),
        name="conv",
    )(conv_in, conv_in, w_dw.reshape(CONV_WIDTH, dc), row(b_dw), row(ln_g), row(ln_b))


def _mix_route_kernel(a_ref, cv_ref, x_ref, mod_ref, wo_ref, lg_ref, lb_ref, wr_ref, rb_ref,
                      x1_ref, h2p_ref, idx_ref, wt_ref, cnt_ref):
    step = pl.program_id(0)
    tm = x_ref.shape[0]
    m = mod_ref[0]
    da = a_ref.shape[1]
    mix = (jnp.dot(a_ref[...], wo_ref[0:da, :], preferred_element_type=f32)
           + jnp.dot(cv_ref[...], wo_ref[da:, :], preferred_element_type=f32))
    x1 = _layer_norm(DEEPNORM_ALPHA * x_ref[...] + m[2:3] * mix, lg_ref[...], lb_ref[...])
    x1_ref[...] = x1
    h2 = x1 * (1.0 + m[4:5]) + m[3:4]
    _store_packed_rows(h2p_ref, h2)

    score = _sigmoid(jnp.dot(h2, wr_ref[...], precision=HIGHEST, preferred_element_type=f32))
    sel = score + rb_ref[...]
    lane = lax.broadcasted_iota(i32, (tm, N_EXPERTS), 1)
    grp = lane >> 5
    gs = []
    for g in range(N_GROUPS):
        v = jnp.where(grp == g, sel, -jnp.inf)
        m1 = jnp.max(v, axis=-1, keepdims=True)
        i1 = jnp.min(jnp.where(v == m1, lane, N_EXPERTS), axis=-1, keepdims=True)
        m2 = jnp.max(jnp.where(lane == i1, -jnp.inf, v), axis=-1, keepdims=True)
        gs.append(m1 + m2)
    gmask = jnp.zeros((tm, N_EXPERTS), jnp.bool_)
    for g in range(N_GROUPS):
        rank = jnp.zeros((tm, 1), i32)
        for o in range(N_GROUPS):
            if o == g:
                continue
            beats = (gs[o] > gs[g]) | (gs[o] == gs[g]) if o < g else gs[o] > gs[g]
            rank = rank + beats.astype(i32)
        gmask = gmask | ((grp == g) & (rank < TOPK_GROUPS))
    cand = jnp.where(gmask, sel, -jnp.inf)
    lane_o = lax.broadcasted_iota(i32, (tm, LANES), 1)
    idx_out = jnp.zeros((tm, LANES), i32)
    wt_out = jnp.zeros((tm, LANES), f32)
    picked = jnp.zeros((tm, N_EXPERTS), f32)
    w_sum = jnp.zeros((tm, 1), f32)
    for k in range(TOP_K):
        mx = jnp.max(cand, axis=-1, keepdims=True)
        ik = jnp.min(jnp.where(cand == mx, lane, N_EXPERTS), axis=-1, keepdims=True)
        pick = lane == ik
        wk = jnp.sum(jnp.where(pick, score, 0.0), axis=-1, keepdims=True)
        cand = jnp.where(pick, -jnp.inf, cand)
        picked = jnp.where(pick, 1.0, picked)
        idx_out = jnp.where(lane_o == k, ik, idx_out)
        wt_out = jnp.where(lane_o == k, wk, wt_out)
        w_sum = w_sum + wk
    idx_ref[...] = idx_out
    wt_ref[...] = wt_out / w_sum * ROUTED_SCALE

    @pl.when(step == 0)
    def _():
        cnt_ref[...] = jnp.zeros_like(cnt_ref)

    cnt_ref[...] += jnp.sum(picked, axis=0, keepdims=True)


def _mix_route(attn, conv, x2, mod, w_out, ln_g, ln_b, w_router, router_bias, s, tm):
    t, d = x2.shape
    per = s // tm
    da = attn.shape[1]
    pack_rows = d // 2 // LANES
    row = lambda a: a.reshape(1, -1)
    tile = lambda w: pl.BlockSpec((tm, w), lambda i: (i, 0))
    full = lambda a: pl.BlockSpec(a.shape, lambda i: (0,) * a.ndim)
    args = (attn, conv, x2, mod, w_out.astype(bf16), row(ln_g), row(ln_b), w_router, row(router_bias))
    return pl.pallas_call(
        _mix_route_kernel,
        grid=(t // tm,),
        in_specs=[tile(da), tile(conv.shape[1]), tile(d),
                  pl.BlockSpec((1, 6, d), lambda i: (i // per, 0, 0))] + [full(a) for a in args[4:]],
        out_specs=[tile(d), pl.BlockSpec((tm * pack_rows, LANES), lambda i: (i, 0)),
                   tile(LANES), tile(LANES), pl.BlockSpec((1, N_EXPERTS), lambda i: (0, 0))],
        out_shape=[jax.ShapeDtypeStruct((t, d), f32),
                   jax.ShapeDtypeStruct((t * pack_rows, LANES), u32),
                   jax.ShapeDtypeStruct((t, LANES), i32), jax.ShapeDtypeStruct((t, LANES), f32),
                   jax.ShapeDtypeStruct((1, N_EXPERTS), f32)],
        compiler_params=_params(("arbitrary",)),
        name="mix_route",
    )(*args)


def _positions_kernel(idx_ref, start_ref, o_ref, run_ref):
    step = pl.program_id(0)
    tm = idx_ref.shape[0]

    @pl.when(step == 0)
    def _():
        run_ref[...] = jnp.zeros_like(run_ref)

    idx = idx_ref[...]
    lane = lax.broadcasted_iota(i32, (tm, N_EXPERTS), 1)
    onehot = jnp.zeros((tm, N_EXPERTS), f32)
    for k in range(TOP_K):
        onehot = jnp.where(lane == idx[:, k:k + 1], 1.0, onehot)
    r = lax.broadcasted_iota(i32, (tm, tm), 0)
    c = lax.broadcasted_iota(i32, (tm, tm), 1)
    below = jnp.where(r > c, 1.0, 0.0).astype(bf16)
    prior = jnp.dot(below, onehot.astype(bf16), preferred_element_type=f32)
    pos = prior + run_ref[...] + start_ref[...]
    lane_o = lax.broadcasted_iota(i32, (tm, LANES), 1)
    out = jnp.zeros((tm, LANES), i32)
    for k in range(TOP_K):
        dk = jnp.sum(jnp.where(lane == idx[:, k:k + 1], pos, 0.0), axis=-1, keepdims=True)
        out = jnp.where(lane_o == k, dk.astype(i32), out)
    o_ref[...] = out
    run_ref[...] += jnp.sum(onehot, axis=0, keepdims=True)


def _positions(idx, seg_start, tm):
    t = idx.shape[0]
    return pl.pallas_call(
        _positions_kernel,
        grid=(t // tm,),
        in_specs=[pl.BlockSpec((tm, LANES), lambda i: (i, 0)),
                  pl.BlockSpec((1, N_EXPERTS), lambda i: (0, 0))],
        out_specs=pl.BlockSpec((tm, LANES), lambda i: (i, 0)),
        out_shape=jax.ShapeDtypeStruct((t, LANES), i32),
        scratch_shapes=[pltpu.VMEM((1, N_EXPERTS), f32)],
        compiler_params=_params(("arbitrary",)),
        name="positions",
    )(idx, seg_start)


ISSUE_UNROLL = 8


def _dispatch_kernel(dest_hbm, h_ref, o_hbm, dest_smem, sem_i, sem, *, r):
    step = pl.program_id(0)
    n = dest_smem.shape[0]
    load = pltpu.make_async_copy(dest_hbm.at[pl.ds(pl.multiple_of(step * n, n), n)], dest_smem, sem_i)
    load.start()
    load.wait()

    def issue(j, _):
        src = pl.multiple_of((j >> 3) * r, r)
        dst = pl.multiple_of(dest_smem[j] * r, r)
        pltpu.make_async_copy(h_ref.at[pl.ds(src, r)], o_hbm.at[pl.ds(dst, r)], sem).start()
        return 0

    lax.fori_loop(0, n, issue, 0, unroll=ISSUE_UNROLL)
    everything = o_hbm.at[pl.ds(0, n * r)]
    pltpu.make_async_copy(everything, everything, sem).wait()


def _dispatch(dest_flat, h2p, n_buf, r, tm):
    t = h2p.shape[0] // r
    return pl.pallas_call(
        functools.partial(_dispatch_kernel, r=r),
        grid=(t // tm,),
        in_specs=[pl.BlockSpec(memory_space=pl.ANY),
                  pl.BlockSpec((tm * r, LANES), lambda i: (i, 0))],
        out_specs=pl.BlockSpec(memory_space=pl.ANY),
        out_shape=jax.ShapeDtypeStruct((n_buf * r, LANES), u32),
        scratch_shapes=[pltpu.SMEM((tm * TOP_K,), i32), pltpu.SemaphoreType.DMA, pltpu.SemaphoreType.DMA],
        compiler_params=_params(("arbitrary",)),
        name="dispatch",
    )(dest_flat, h2p)


def _experts_kernel(be_ref, bv_ref, nu_ref, x_ref, wg_ref, wu_ref, wd_ref, o_ref, wg_s, wu_s, wd_s):
    j = pl.program_id(0)
    prev = be_ref[jnp.maximum(j - 1, 0)]
    used = j < nu_ref[0]
    d = wg_s.shape[0]

    @pl.when(used & ((j == 0) | (be_ref[j] != prev)))
    def _():
        wg_s[...] = wg_ref[0].astype(bf16)
        wu_s[...] = wu_ref[0].astype(bf16)
        wd_s[...] = wd_ref[0].astype(bf16)

    @pl.when(used)
    def _():
        live = lax.broadcasted_iota(i32, (ROW_BLOCK, 1), 0) < bv_ref[j]
        half = d // 2
        r = half // LANES
        hg = jnp.zeros((ROW_BLOCK, wg_s.shape[1]), f32)
        hu = jnp.zeros((ROW_BLOCK, wg_s.shape[1]), f32)
        for c in range(r):
            parts = _unpack_words(x_ref[pl.ds(c, ROW_BLOCK, stride=r), :])
            for off, part in zip((0, half), parts):
                xc = jnp.where(live, part, 0.0).astype(bf16)
                rows = slice(off + c * LANES, off + (c + 1) * LANES)
                hg = hg + jnp.dot(xc, wg_s[rows, :], preferred_element_type=f32)
                hu = hu + jnp.dot(xc, wu_s[rows, :], preferred_element_type=f32)
        hid = (_silu(hg) * hu).astype(bf16)
        _store_packed_rows(o_ref, jnp.dot(hid, wd_s[...], preferred_element_type=f32))


def _experts(blk_e, blk_valid, n_used, xs, w_gate, w_up, w_down):
    d, f = w_gate.shape[1], w_gate.shape[2]
    r = d // 2 // LANES
    n_blk = xs.shape[0] // r // ROW_BLOCK
    rows = lambda j, be, bv, nu: (jnp.minimum(j, nu[0] - 1), 0)
    wsel = lambda j, be, bv, nu: (be[j], 0, 0)
    return pl.pallas_call(
        _experts_kernel,
        grid_spec=pltpu.PrefetchScalarGridSpec(
            num_scalar_prefetch=3,
            grid=(n_blk,),
            in_specs=[pl.BlockSpec((ROW_BLOCK * r, LANES), rows),
                      pl.BlockSpec((1, d, f), wsel),
                      pl.BlockSpec((1, d, f), wsel),
                      pl.BlockSpec((1, f, d), wsel)],
            out_specs=pl.BlockSpec((ROW_BLOCK * r, LANES), rows),
            scratch_shapes=[pltpu.VMEM((d, f), bf16), pltpu.VMEM((d, f), bf16), pltpu.VMEM((f, d), bf16)]),
        out_shape=jax.ShapeDtypeStruct(xs.shape, u32),
        compiler_params=_params(("arbitrary",)),
        name="experts",
    )(blk_e, blk_valid, n_used, xs, w_gate, w_up, w_down)


def _combine_kernel(dest_hbm, ys_hbm, wt_ref, h_ref, x1_ref, mod_ref, wsg_ref, wsu_ref, wsd_ref,
                    lg_ref, lb_ref, o_ref, dest_smem, rows_ref, sem_i, sem):
    step = pl.program_id(0)
    n = dest_smem.shape[0]
    load = pltpu.make_async_copy(dest_hbm.at[pl.ds(pl.multiple_of(step * n, n), n)], dest_smem, sem_i)
    load.start()
    load.wait()

    tm, d = x1_ref.shape
    half = d // 2
    r = half // LANES

    def issue(j, _):
        src = pl.multiple_of(dest_smem[j] * r, r)
        dst = pl.multiple_of((j >> 3) * r, r)
        pltpu.make_async_copy(ys_hbm.at[pl.ds(src, r)], rows_ref.at[j & 7, pl.ds(dst, r)], sem).start()
        return 0

    lax.fori_loop(0, n, issue, 0, unroll=ISSUE_UNROLL)

    hg = jnp.zeros((tm, wsg_ref.shape[1]), f32)
    hu = jnp.zeros((tm, wsg_ref.shape[1]), f32)
    for c in range(r):
        parts = _unpack_words(h_ref[pl.ds(c, tm, stride=r), :])
        for off, part in zip((0, half), parts):
            rows = slice(off + c * LANES, off + (c + 1) * LANES)
            hg = hg + jnp.dot(part.astype(bf16), wsg_ref[rows, :], preferred_element_type=f32)
            hu = hu + jnp.dot(part.astype(bf16), wsu_ref[rows, :], preferred_element_type=f32)
    y = jnp.dot((_silu(hg) * hu).astype(bf16), wsd_ref[...], preferred_element_type=f32)

    pltpu.make_async_copy(rows_ref, rows_ref, sem).wait()
    wt = wt_ref[...]
    lo = [jnp.zeros((tm, LANES), f32) for _ in range(r)]
    hi = [jnp.zeros((tm, LANES), f32) for _ in range(r)]
    for k in range(TOP_K):
        wk = wt[:, k:k + 1]
        for c in range(r):
            pl_, ph_ = _unpack_words(rows_ref[k, pl.ds(c, tm, stride=r), :])
            lo[c] = lo[c] + wk * pl_
            hi[c] = hi[c] + wk * ph_
    y = y + jnp.concatenate(lo + hi, axis=1)
    m = mod_ref[0]
    o_ref[...] = _layer_norm(DEEPNORM_ALPHA * x1_ref[...] + m[5:6] * y, lg_ref[...], lb_ref[...])


def _combine(dest_flat, ys, wts, h2p, x1, mod, w_s_gate, w_s_up, w_s_down, ln_g, ln_b, s, tm):
    t, d = x1.shape
    per = s // tm
    r = d // 2 // LANES
    row = lambda a: a.reshape(1, -1)
    tile = lambda w: pl.BlockSpec((tm, w), lambda i: (i, 0))
    full = lambda a: pl.BlockSpec(a.shape, lambda i: (0,) * a.ndim)
    tail = (w_s_gate.astype(bf16), w_s_up.astype(bf16), w_s_down.astype(bf16), row(ln_g), row(ln_b))
    return pl.pallas_call(
        _combine_kernel,
        grid=(t // tm,),
        in_specs=[pl.BlockSpec(memory_space=pl.ANY), pl.BlockSpec(memory_space=pl.ANY),
                  tile(LANES), pl.BlockSpec((tm * r, LANES), lambda i: (i, 0)), tile(d),
                  pl.BlockSpec((1, 6, d), lambda i: (i // per, 0, 0))] + [full(a) for a in tail],
        out_specs=tile(d),
        out_shape=jax.ShapeDtypeStruct((t, d), f32),
        scratch_shapes=[pltpu.SMEM((tm * TOP_K,), i32), pltpu.VMEM((TOP_K, tm * r, LANES), u32),
                        pltpu.SemaphoreType.DMA, pltpu.SemaphoreType.DMA],
        compiler_params=_params(("arbitrary",)),
        name="combine",
    )(dest_flat, ys, wts, h2p, x1, mod, *tail)


def _layer(x, mod, w_in, pe, w_cmp1, w_cmp2, w_dw, b_dw, conv_ln_g, conv_ln_b, w_out, ln1_g, ln1_b,
           w_router, router_bias, w_e_gate, w_e_up, w_e_down, w_s_gate, w_s_up, w_s_down, ln2_g, ln2_b):
    b, s, d = x.shape
    t = b * s
    tm = min(512, s)
    qt, vt, gt, kvc, kn, conv_in = _in_proj(x, mod, w_in, tm)
    cmp_n, cmp_t = _compress(kvc, pe, w_cmp1, w_cmp2)
    gates = gt[:, :3 * N_HEADS].reshape(b, N_KV, 3 * HPG, s)
    gates = jnp.pad(gates, ((0, 0), (0, 0), (0, GATE_GROUP_ROWS - 3 * HPG), (0, 0)))
    attn = _attention(qt, gates, cmp_n, cmp_t, kn, vt)
    conv = _conv(conv_in, w_dw, b_dw, conv_ln_g, conv_ln_b, tm)

    tr = min(256, s)
    x1, h2p, idx, wts, counts = _mix_route(attn.reshape(t, -1), conv.reshape(t, -1), x.reshape(t, d), mod,
                                           w_out, ln1_g, ln1_b, w_router, router_bias, s, tr)
    counts = counts[0].astype(i32)
    padded = (counts + ROW_BLOCK - 1) // ROW_BLOCK * ROW_BLOCK
    seg_end = jnp.cumsum(padded)
    seg_start = seg_end - padded
    n_blk = -(-(t * TOP_K + N_EXPERTS * (ROW_BLOCK - 1)) // ROW_BLOCK)
    blk_row0 = jnp.arange(n_blk, dtype=i32) * ROW_BLOCK
    blk_e = jnp.minimum(jnp.searchsorted(seg_end, blk_row0, side='right'), N_EXPERTS - 1).astype(i32)
    blk_valid = jnp.clip(seg_start[blk_e] + counts[blk_e] - blk_row0, 0, ROW_BLOCK).astype(i32)
    n_used = (seg_end[-1:] // ROW_BLOCK).astype(i32)

    dest = _positions(idx, seg_start.astype(f32).reshape(1, N_EXPERTS), tr)
    dest_flat = dest[:, :TOP_K].reshape(-1)
    xs = _dispatch(dest_flat, h2p, n_blk * ROW_BLOCK, d // 2 // LANES, min(512, s))
    ys = _experts(blk_e, blk_valid, n_used, xs, w_e_gate, w_e_up, w_e_down)
    out = _combine(dest_flat, ys, wts, h2p, x1, mod, w_s_gate, w_s_up, w_s_down, ln2_g, ln2_b, s, min(256, s))
    return out.reshape(b, s, d)


def kernel(x, c, w_ada, b_ada, w_in, pe_k, pe_v, w_cmp_k1, w_cmp_k2, w_cmp_v1, w_cmp_v2, w_dw, b_dw,
           conv_ln_g, conv_ln_b, w_out, ln1_g, ln1_b, w_router, router_bias, w_e_gate, w_e_up, w_e_down,
           w_s_gate, w_s_up, w_s_down, ln2_g, ln2_b):
    assert w_ada.shape[0] == DEPTH
    layer = lambda a: a.reshape(a.shape[1:])
    mod = _ada(c, layer(w_ada), layer(b_ada))
    return _layer(x, mod, layer(w_in), jnp.concatenate([pe_k, pe_v]),
                  jnp.concatenate([w_cmp_k1, w_cmp_v1]), jnp.concatenate([w_cmp_k2, w_cmp_v2]),
                  *[layer(a) for a in (w_dw, b_dw, conv_ln_g, conv_ln_b, w_out, ln1_g, ln1_b, w_router,
                                       router_bias, w_e_gate, w_e_up, w_e_down, w_s_gate, w_s_up, w_s_down,
                                       ln2_g, ln2_b)])
```

```python
import functools

import jax
import jax.numpy as jnp
import numpy as np
from jax import lax
from jax.experimental import pallas as pl
from jax.experimental.pallas import tpu as pltpu

N_HEADS = 8
N_KV = 2
HPG = N_HEADS // N_KV
HEAD_DIM = 64
D_ATTN = N_HEADS * HEAD_DIM
D_KV = N_KV * HEAD_DIM
CONV_WIDTH = 31
CMP_LEN = 32
CMP_STRIDE = 16
CMP_HID = 256
SEL_LEN = 64
SEL_TOPN = 16
WINDOW = 512
Q_BLOCK = 128
N_EXPERTS = 256
TOP_K = 8
N_GROUPS = 8
TOPK_GROUPS = 4
ROUTED_SCALE = 2.5
LN_EPS = 1e-5
DEPTH = 1
DEEPNORM_ALPHA = (2 * DEPTH) ** 0.25

LANES = 128
SUBLANES = 8
ROW_BLOCK = 256
NEG = -1e30
HIGHEST = lax.Precision.HIGHEST
VMEM_LIMIT = 48 * 1024 * 1024

f32 = jnp.float32
bf16 = jnp.bfloat16
i32 = jnp.int32
u32 = jnp.uint32


def _params(sem, vmem=VMEM_LIMIT):
    return pltpu.CompilerParams(dimension_semantics=sem, vmem_limit_bytes=vmem)


def _sigmoid(v):
    return 1.0 / (1.0 + jnp.exp(-v))


def _silu(v):
    return v * _sigmoid(v)


def _layer_norm(v, g, b):
    mu = jnp.mean(v, axis=-1, keepdims=True)
    var = jnp.mean(jnp.square(v - mu), axis=-1, keepdims=True)
    return (v - mu) * lax.rsqrt(var + LN_EPS) * g + b


def _dot_nt(a, b):
    return lax.dot_general(a, b, (((1,), (1,)), ((), ())), preferred_element_type=f32)


def _store_packed_rows(ref, v):
    n, d = v.shape
    half = d // 2
    bits = lax.bitcast_convert_type(v.astype(bf16).astype(f32), u32)
    words = bits[:, half:] | (bits[:, :half] >> 16)
    r = half // LANES
    for c in range(r):
        ref[pl.ds(c, n, stride=r), :] = words[:, c * LANES:(c + 1) * LANES]


def _unpack_words(w):
    return (lax.bitcast_convert_type(w << 16, f32),
            lax.bitcast_convert_type(w & jnp.uint32(0xFFFF0000), f32))


def _ada_kernel(c_ref, w_ref, b_ref, o_ref):
    c = c_ref[...]
    o_ref[...] = jnp.dot(_silu(c), w_ref[...], precision=HIGHEST,
                         preferred_element_type=f32) + b_ref[...]


def _ada(c, w_ada, b_ada):
    b, d = c.shape
    n = w_ada.shape[1]
    rows = 8
    c_pad = jnp.zeros((rows, d), f32).at[:b].set(c)
    tn = 1024
    out = pl.pallas_call(
        _ada_kernel,
        grid=(n // tn,),
        in_specs=[pl.BlockSpec((rows, d), lambda j: (0, 0)),
                  pl.BlockSpec((d, tn), lambda j: (0, j)),
                  pl.BlockSpec((1, tn), lambda j: (0, j))],
        out_specs=pl.BlockSpec((rows, tn), lambda j: (0, j)),
        out_shape=jax.ShapeDtypeStruct((rows, n), f32),
        compiler_params=_params(("arbitrary",)),
        name="ada",
    )(c_pad, w_ada, b_ada.reshape(1, n))
    return out[:b].reshape(b, 6, d)


GATE_ROWS = 32
V_ROWS = HEAD_DIM + 16
Q_SCALE = HEAD_DIM ** -0.5 * 1.4426950408889634


def _in_proj_kernel(x_ref, mod_ref, wt_ref, wc_ref, wk_ref, wv_ref,
                    qt_ref, vt_ref, gt_ref, kvc_ref, kn_ref, cv_ref):
    m = mod_ref[0]
    h = (x_ref[0] * (1.0 + m[1:2]) + m[0:1]).astype(bf16)
    res_t = _dot_nt(wt_ref[...], h)
    qt_ref[0] = (res_t[0:D_ATTN] * Q_SCALE).astype(bf16)
    ones = jnp.ones((V_ROWS - HEAD_DIM, res_t.shape[1]), bf16)
    for j in range(2):
        for g in range(N_KV):
            off = D_ATTN + (j * N_KV + g) * HEAD_DIM
            vt_ref[0, j, g, 0:HEAD_DIM, :] = res_t[off:off + HEAD_DIM].astype(bf16)
            vt_ref[0, j, g, HEAD_DIM:V_ROWS, :] = ones
    gt_ref[0] = res_t[D_ATTN + 2 * D_KV:]
    kvc_ref[0] = jnp.dot(h, wc_ref[...], preferred_element_type=f32)
    kn = jnp.dot(h, wk_ref[...], preferred_element_type=f32).astype(bf16)
    for j in range(2):
        for g in range(N_KV):
            off = (j * N_KV + g) * HEAD_DIM
            kn_ref[0, j, g] = kn[:, off:off + HEAD_DIM]
    cv_ref[0] = jnp.dot(h, wv_ref[...], preferred_element_type=f32)


def _in_proj(x, mod, w_in, tm):
    b, s, d = x.shape
    o = 0
    wq = w_in[:, o:o + D_ATTN]; o += D_ATTN
    wkc = w_in[:, o:o + 2 * D_KV]; o += 2 * D_KV
    wk_s = w_in[:, o:o + D_KV]; o += D_KV
    wv_s = w_in[:, o:o + D_KV]; o += D_KV
    wk_w = w_in[:, o:o + D_KV]; o += D_KV
    wv_w = w_in[:, o:o + D_KV]; o += D_KV
    wg = w_in[:, o:o + 3 * N_HEADS]; o += 3 * N_HEADS
    wcv = w_in[:, o:]
    d_conv2 = wcv.shape[1]
    wg = jnp.zeros((d, GATE_ROWS), f32).at[:, :3 * N_HEADS].set(wg)
    wt = jnp.concatenate([wq, wv_s, wv_w, wg], axis=1).T
    ws = [w.astype(bf16) for w in (wt, wkc, jnp.concatenate([wk_s, wk_w], axis=1), wcv)]
    full = lambda a: pl.BlockSpec(a.shape, lambda bi, i: (0, 0))
    return pl.pallas_call(
        _in_proj_kernel,
        grid=(b, s // tm),
        in_specs=[pl.BlockSpec((1, tm, d), lambda bi, i: (bi, i, 0)),
                  pl.BlockSpec((1, 6, d), lambda bi, i: (bi, 0, 0))] + [full(w) for w in ws],
        out_specs=[pl.BlockSpec((1, D_ATTN, tm), lambda bi, i: (bi, 0, i)),
                   pl.BlockSpec((1, 2, N_KV, V_ROWS, tm), lambda bi, i: (bi, 0, 0, 0, i)),
                   pl.BlockSpec((1, GATE_ROWS, tm), lambda bi, i: (bi, 0, i)),
                   pl.BlockSpec((1, tm, 2 * D_KV), lambda bi, i: (bi, i, 0)),
                   pl.BlockSpec((1, 2, N_KV, tm, HEAD_DIM), lambda bi, i: (bi, 0, 0, i, 0)),
                   pl.BlockSpec((1, tm, d_conv2), lambda bi, i: (bi, i, 0))],
        out_shape=[jax.ShapeDtypeStruct((b, D_ATTN, s), bf16),
                   jax.ShapeDtypeStruct((b, 2, N_KV, V_ROWS, s), bf16),
                   jax.ShapeDtypeStruct((b, GATE_ROWS, s), f32),
                   jax.ShapeDtypeStruct((b, s, 2 * D_KV), f32),
                   jax.ShapeDtypeStruct((b, 2, N_KV, s, HEAD_DIM), bf16),
                   jax.ShapeDtypeStruct((b, s, d_conv2), f32)],
        compiler_params=_params(("parallel", "parallel")),
        name="in_proj",
    )(x, mod, *ws)


def _compress_kernel(c_ref, pe_ref, w1_ref, w2_ref, w2t_ref, o_ref, ot_ref):
    c = c_ref[0, 0, 0]
    n_chunk = c.shape[0]
    a = jnp.dot((c + pe_ref[0, 0]).astype(bf16), w1_ref[0, 0], preferred_element_type=f32)
    bm = jnp.dot((c + pe_ref[0, 1]).astype(bf16), w1_ref[0, 1], preferred_element_type=f32)
    hid = a + pltpu.roll(bm, n_chunk - 1, 0)
    act = 0.5 * hid * (1.0 + jnp.tanh(0.7978845608028654 * (hid + 0.044715 * (hid * hid * hid))))
    act = act.astype(bf16)
    o_ref[0, 0, 0] = jnp.dot(act, w2_ref[0], preferred_element_type=f32).astype(bf16)
    ot_ref[0, 0, 0] = _dot_nt(w2t_ref[0], act).astype(bf16)


def _compress(kvc, pe, w1, w2):
    b, s, _ = kvc.shape
    n_chunk = s // CMP_STRIDE
    half = CMP_STRIDE * HEAD_DIM
    c = kvc.reshape(b, n_chunk, CMP_STRIDE, 2, N_KV, HEAD_DIM).transpose(0, 3, 4, 1, 2, 5)
    c = c.reshape(b, 2, N_KV, n_chunk, half)
    pe2 = pe.reshape(2, 2, 1, half)
    w1h = w1.reshape(2, 2, half, CMP_HID).astype(bf16)
    w2b = w2.astype(bf16)
    w2t = w2b.transpose(0, 2, 1)
    return pl.pallas_call(
        _compress_kernel,
        grid=(b, 2, N_KV),
        in_specs=[pl.BlockSpec((1, 1, 1, n_chunk, half), lambda bi, j, g: (bi, j, g, 0, 0)),
                  pl.BlockSpec((1, 2, 1, half), lambda bi, j, g: (j, 0, 0, 0)),
                  pl.BlockSpec((1, 2, half, CMP_HID), lambda bi, j, g: (j, 0, 0, 0)),
                  pl.BlockSpec((1, CMP_HID, HEAD_DIM), lambda bi, j, g: (j, 0, 0)),
                  pl.BlockSpec((1, HEAD_DIM, CMP_HID), lambda bi, j, g: (j, 0, 0))],
        out_specs=[pl.BlockSpec((1, 1, 1, n_chunk, HEAD_DIM), lambda bi, j, g: (bi, j, g, 0, 0)),
                   pl.BlockSpec((1, 1, 1, HEAD_DIM, n_chunk), lambda bi, j, g: (bi, j, g, 0, 0))],
        out_shape=[jax.ShapeDtypeStruct((b, 2, N_KV, n_chunk, HEAD_DIM), bf16),
                   jax.ShapeDtypeStruct((b, 2, N_KV, HEAD_DIM, n_chunk), bf16)],
        compiler_params=_params(("parallel", "parallel", "parallel")),
        name="compress",
    )(c, pe2, w1h, w2b, w2t)


def _attn_kernel(q_ref, g_ref, kc_ref, vc_ref, kn_ref, vt_ref, ov_ref, o_ref, selbias_ref, *, seq, tk, top_n):
    i = pl.program_id(2)
    s0 = i * Q_BLOCK
    n_cmp_rows = kc_ref.shape[3]
    n_sel = seq // SEL_LEN

    q4 = q_ref[0]
    qt = jnp.concatenate([q4[p * HEAD_DIM:(p + 1) * HEAD_DIM, :] for p in range(HPG)], axis=1)
    t_row = s0 + lax.broadcasted_iota(i32, (1, Q_BLOCK), 1)

    s_c = jnp.dot(kc_ref[0, 0, 0], qt, preferred_element_type=f32)
    cmp_end = lax.broadcasted_iota(i32, (n_cmp_rows, 1), 0) * CMP_STRIDE + (CMP_LEN - 1)
    bias_c = jnp.where(cmp_end <= t_row, 0.0, NEG)
    any_c = t_row >= CMP_LEN - 1
    p_sum = jnp.zeros((n_cmp_rows, Q_BLOCK), f32)
    pcs = []
    for p in range(HPG):
        sp = s_c[:, p * Q_BLOCK:(p + 1) * Q_BLOCK] + bias_c
        e = jnp.exp2(sp - jnp.max(sp, axis=0, keepdims=True))
        pn = e * jnp.where(any_c, 1.0 / jnp.sum(e, axis=0, keepdims=True), 0.0)
        p_sum = p_sum + pn
        pcs.append(pn.astype(bf16))
    o_c = jnp.dot(vc_ref[0, 0, 0], jnp.concatenate(pcs, axis=1), preferred_element_type=f32)

    imp = jnp.dot(ov_ref[...], p_sum, precision=HIGHEST, preferred_element_type=f32)
    blk = lax.broadcasted_iota(i32, (n_sel, Q_BLOCK), 0)
    cur = t_row >> 6
    forced = (blk == 0) | (blk == cur) | (blk == cur - 1)
    vals = jnp.where(forced, jnp.inf, jnp.where(blk <= cur, imp, -jnp.inf))
    sel = jnp.zeros((n_sel, Q_BLOCK), f32)
    for _ in range(top_n):
        mx = jnp.max(vals, axis=0, keepdims=True)
        first = jnp.min(jnp.where(vals == mx, blk, n_sel), axis=0, keepdims=True)
        pick = blk == first
        sel = jnp.where(pick & (mx > -jnp.inf), 1.0, sel)
        vals = jnp.where(pick, -jnp.inf, vals)
    selbias_ref[...] = jnp.where(sel > 0.5, 0.0, NEG)

    blocks_per_tile = tk // SEL_LEN

    def sel_tile(kt, carry, diagonal):
        m_i, acc = carry
        k0 = pl.multiple_of(kt * tk, tk)
        s_t = jnp.dot(kn_ref[0, 0, 0, pl.ds(k0, tk), :], qt, preferred_element_type=f32)
        sb = selbias_ref[pl.ds(pl.multiple_of(kt * blocks_per_tile, blocks_per_tile), blocks_per_tile), :]
        bias = jnp.concatenate([jnp.broadcast_to(sb[j:j + 1], (SEL_LEN, Q_BLOCK))
                                for j in range(blocks_per_tile)], axis=0)
        if diagonal:
            kpos = k0 + lax.broadcasted_iota(i32, (tk, 1), 0)
            bias = jnp.where(kpos <= t_row, bias, NEG)
        es, ms, alphas = [], [], []
        for p in range(HPG):
            c = slice(p * Q_BLOCK, (p + 1) * Q_BLOCK)
            sp = s_t[:, c] + bias
            m_new = jnp.maximum(m_i[:, c], jnp.max(sp, axis=0, keepdims=True))
            es.append(jnp.exp2(sp - m_new).astype(bf16))
            alphas.append(jnp.exp2(m_i[:, c] - m_new))
            ms.append(m_new)
        pv = jnp.dot(vt_ref[0, 0, 0, :, pl.ds(k0, tk)], jnp.concatenate(es, axis=1),
                     preferred_element_type=f32)
        return jnp.concatenate(ms, axis=1), jnp.concatenate(alphas, axis=1) * acc + pv

    n_kt = (s0 + Q_BLOCK + tk - 1) // tk
    cols = HPG * Q_BLOCK
    init = (jnp.full((1, cols), NEG, f32), jnp.zeros((V_ROWS, cols), f32))
    carry = lax.fori_loop(0, n_kt - 1, lambda kt, c: sel_tile(kt, c, False), init)
    _, acc_s = sel_tile(n_kt - 1, carry, True)
    o_s = acc_s[0:HEAD_DIM] * (1.0 / acc_s[HEAD_DIM:HEAD_DIM + 1])

    span = WINDOW + Q_BLOCK
    w0 = pl.multiple_of(jnp.maximum(s0 - WINDOW, 0), Q_BLOCK)
    s_w = jnp.dot(kn_ref[0, 1, 0, pl.ds(w0, span), :], qt, preferred_element_type=f32)
    wpos = w0 + lax.broadcasted_iota(i32, (span, 1), 0)
    bias_w = jnp.where((wpos <= t_row) & (wpos > t_row - WINDOW), 0.0, NEG)
    pws = []
    for p in range(HPG):
        sp = s_w[:, p * Q_BLOCK:(p + 1) * Q_BLOCK] + bias_w
        pws.append(jnp.exp2(sp - jnp.max(sp, axis=0, keepdims=True)).astype(bf16))
    acc_w = jnp.dot(vt_ref[0, 1, 0, :, pl.ds(w0, span)], jnp.concatenate(pws, axis=1),
                    preferred_element_type=f32)
    o_w = acc_w[0:HEAD_DIM] * (1.0 / acc_w[HEAD_DIM:HEAD_DIM + 1])

    gate = _sigmoid(g_ref[0, 0])
    outs = []
    for p in range(HPG):
        c = slice(p * Q_BLOCK, (p + 1) * Q_BLOCK)
        outs.append(gate[3 * p:3 * p + 1] * o_c[:, c] + gate[3 * p + 1:3 * p + 2] * o_s[:, c]
                    + gate[3 * p + 2:3 * p + 3] * o_w[:, c])
    o_ref[0] = jnp.concatenate(outs, axis=0).T.astype(bf16)


GATE_GROUP_ROWS = 16


def _attention(qt, gates, cmp_n, cmp_t, kn, vt):
    b, _, s = qt.shape
    n_chunk = cmp_n.shape[3]
    n_sel = s // SEL_LEN
    top_n = min(SEL_TOPN, n_sel)
    tk = min(512, s)
    cs = np.arange(n_chunk) * CMP_STRIDE
    ss = np.arange(n_sel) * SEL_LEN
    overlap = ((cs[None, :] < ss[:, None] + SEL_LEN) & (cs[None, :] + CMP_LEN > ss[:, None]))
    overlap[:, (s - CMP_LEN) // CMP_STRIDE + 1:] = False
    overlap = jnp.asarray(overlap.astype(np.float32))
    gw = HPG * HEAD_DIM
    return pl.pallas_call(
        functools.partial(_attn_kernel, seq=s, tk=tk, top_n=top_n),
        grid=(b, N_KV, s // Q_BLOCK),
        in_specs=[pl.BlockSpec((1, gw, Q_BLOCK), lambda bi, g, i: (bi, g, i)),
                  pl.BlockSpec((1, 1, GATE_GROUP_ROWS, Q_BLOCK), lambda bi, g, i: (bi, g, 0, i)),
                  pl.BlockSpec((1, 1, 1, n_chunk, HEAD_DIM), lambda bi, g, i: (bi, 0, g, 0, 0)),
                  pl.BlockSpec((1, 1, 1, HEAD_DIM, n_chunk), lambda bi, g, i: (bi, 1, g, 0, 0)),
                  pl.BlockSpec((1, 2, 1, s, HEAD_DIM), lambda bi, g, i: (bi, 0, g, 0, 0)),
                  pl.BlockSpec((1, 2, 1, V_ROWS, s), lambda bi, g, i: (bi, 0, g, 0, 0)),
                  pl.BlockSpec((n_sel, n_chunk), lambda bi, g, i: (0, 0))],
        out_specs=pl.BlockSpec((1, Q_BLOCK, gw), lambda bi, g, i: (bi, i, g)),
        out_shape=jax.ShapeDtypeStruct((b, s, D_ATTN), bf16),
        scratch_shapes=[pltpu.VMEM((n_sel, Q_BLOCK), f32)],
        compiler_params=_params(("parallel", "parallel", "arbitrary")),
        name="attention",
    )(qt, gates, cmp_n, cmp_t, kn, vt, overlap)


CONV_HALO = 32
CONV_ROWS = 32


def _conv_kernel(cur_ref, prev_ref, w_ref, b_ref, g_ref, bb_ref, o_ref, glu_ref):
    i = pl.program_id(1)
    ts = cur_ref.shape[1]
    dc = o_ref.shape[2]
    cur = cur_ref[0]
    prev = prev_ref[0]
    glu_prev = prev[:, :dc] * _sigmoid(prev[:, dc:])
    glu_ref[0:CONV_HALO] = jnp.where(i == 0, 0.0, glu_prev)
    glu_ref[CONV_HALO:CONV_HALO + ts] = cur[:, :dc] * _sigmoid(cur[:, dc:])
    lead = CONV_HALO - (CONV_WIDTH - 1)

    def chunk(r, _):
        r0 = pl.multiple_of(r * CONV_ROWS, CONV_ROWS)
        win = glu_ref[pl.ds(r0, CONV_ROWS + CONV_HALO), :]
        acc = jnp.zeros((CONV_ROWS, dc), f32)
        for j in range(CONV_WIDTH):
            acc = acc + w_ref[j:j + 1, :] * win[lead + j:lead + j + CONV_ROWS]
        y = _layer_norm(acc + b_ref[...], g_ref[...], bb_ref[...])
        o_ref[0, pl.ds(r0, CONV_ROWS), :] = _silu(y).astype(bf16)
        return 0

    lax.fori_loop(0, ts // CONV_ROWS, chunk, 0)


def _conv(conv_in, w_dw, b_dw, ln_g, ln_b, ts):
    b, s, dc2 = conv_in.shape
    dc = dc2 // 2
    per = ts // CONV_HALO
    row = lambda a: a.reshape(1, dc)
    return pl.pallas_call(
        _conv_kernel,
        grid=(b, s // ts),
        in_specs=[pl.BlockSpec((1, ts, dc2), lambda bi, i: (bi, i, 0)),
                  pl.BlockSpec((1, CONV_HALO, dc2), lambda bi, i: (bi, jnp.maximum(i * per - 1, 0), 0)),
                  pl.BlockSpec((CONV_WIDTH, dc), lambda bi, i: (0, 0)),
                  pl.BlockSpec((1, dc), lambda bi, i: (0, 0)),
                  pl.BlockSpec((1, dc), lambda bi, i: (0, 0)),
                  pl.BlockSpec((1, dc), lambda bi, i: (0, 0))],
        out_specs=pl.BlockSpec((1, ts, dc), lambda bi, i: (bi, i, 0)),
        out_shape=jax.ShapeDtypeStruct((b, s, dc), bf16),
        scratch_shapes=[pltpu.VMEM((CONV_HALO + ts, dc), f32)],
        compiler_params=_params(("parallel", "parallel")),
        name="conv",
    )(conv_in, conv_in, w_dw.reshape(CONV_WIDTH, dc), row(b_dw), row(ln_g), row(ln_b))


def _mix_route_kernel(a_ref, cv_ref, x_ref, mod_ref, wo_ref, lg_ref, lb_ref, wr_ref, rb_ref,
                      x1_ref, h2p_ref, idx_ref, wt_ref, cnt_ref):
    step = pl.program_id(0)
    tm = x_ref.shape[0]
    m = mod_ref[0]
    da = a_ref.shape[1]
    mix = (jnp.dot(a_ref[...], wo_ref[0:da, :], preferred_element_type=f32)
           + jnp.dot(cv_ref[...], wo_ref[da:, :], preferred_element_type=f32))
    x1 = _layer_norm(DEEPNORM_ALPHA * x_ref[...] + m[2:3] * mix, lg_ref[...], lb_ref[...])
    x1_ref[...] = x1
    h2 = x1 * (1.0 + m[4:5]) + m[3:4]
    _store_packed_rows(h2p_ref, h2)

    score = _sigmoid(jnp.dot(h2, wr_ref[...], precision=HIGHEST, preferred_element_type=f32))
    sel = score + rb_ref[...]
    lane = lax.broadcasted_iota(i32, (tm, N_EXPERTS), 1)
    grp = lane >> 5
    gs = []
    for g in range(N_GROUPS):
        v = jnp.where(grp == g, sel, -jnp.inf)
        m1 = jnp.max(v, axis=-1, keepdims=True)
        i1 = jnp.min(jnp.where(v == m1, lane, N_EXPERTS), axis=-1, keepdims=True)
        m2 = jnp.max(jnp.where(lane == i1, -jnp.inf, v), axis=-1, keepdims=True)
        gs.append(m1 + m2)
    gmask = jnp.zeros((tm, N_EXPERTS), jnp.bool_)
    for g in range(N_GROUPS):
        rank = jnp.zeros((tm, 1), i32)
        for o in range(N_GROUPS):
            if o == g:
                continue
            beats = (gs[o] > gs[g]) | (gs[o] == gs[g]) if o < g else gs[o] > gs[g]
            rank = rank + beats.astype(i32)
        gmask = gmask | ((grp == g) & (rank < TOPK_GROUPS))
    cand = jnp.where(gmask, sel, -jnp.inf)
    lane_o = lax.broadcasted_iota(i32, (tm, LANES), 1)
    idx_out = jnp.zeros((tm, LANES), i32)
    wt_out = jnp.zeros((tm, LANES), f32)
    picked = jnp.zeros((tm, N_EXPERTS), f32)
    w_sum = jnp.zeros((tm, 1), f32)
    for k in range(TOP_K):
        mx = jnp.max(cand, axis=-1, keepdims=True)
        ik = jnp.min(jnp.where(cand == mx, lane, N_EXPERTS), axis=-1, keepdims=True)
        pick = lane == ik
        wk = jnp.sum(jnp.where(pick, score, 0.0), axis=-1, keepdims=True)
        cand = jnp.where(pick, -jnp.inf, cand)
        picked = jnp.where(pick, 1.0, picked)
        idx_out = jnp.where(lane_o == k, ik, idx_out)
        wt_out = jnp.where(lane_o == k, wk, wt_out)
        w_sum = w_sum + wk
    idx_ref[...] = idx_out
    wt_ref[...] = wt_out / w_sum * ROUTED_SCALE

    @pl.when(step == 0)
    def _():
        cnt_ref[...] = jnp.zeros_like(cnt_ref)

    cnt_ref[...] += jnp.sum(picked, axis=0, keepdims=True)


def _mix_route(attn, conv, x2, mod, w_out, ln_g, ln_b, w_router, router_bias, s, tm):
    t, d = x2.shape
    per = s // tm
    da = attn.shape[1]
    pack_rows = d // 2 // LANES
    row = lambda a: a.reshape(1, -1)
    tile = lambda w: pl.BlockSpec((tm, w), lambda i: (i, 0))
    full = lambda a: pl.BlockSpec(a.shape, lambda i: (0,) * a.ndim)
    args = (attn, conv, x2, mod, w_out.astype(bf16), row(ln_g), row(ln_b), w_router, row(router_bias))
    return pl.pallas_call(
        _mix_route_kernel,
        grid=(t // tm,),
        in_specs=[tile(da), tile(conv.shape[1]), tile(d),
                  pl.BlockSpec((1, 6, d), lambda i: (i // per, 0, 0))] + [full(a) for a in args[4:]],
        out_specs=[tile(d), pl.BlockSpec((tm * pack_rows, LANES), lambda i: (i, 0)),
                   tile(LANES), tile(LANES), pl.BlockSpec((1, N_EXPERTS), lambda i: (0, 0))],
        out_shape=[jax.ShapeDtypeStruct((t, d), f32),
                   jax.ShapeDtypeStruct((t * pack_rows, LANES), u32),
                   jax.ShapeDtypeStruct((t, LANES), i32), jax.ShapeDtypeStruct((t, LANES), f32),
                   jax.ShapeDtypeStruct((1, N_EXPERTS), f32)],
        compiler_params=_params(("arbitrary",)),
        name="mix_route",
    )(*args)


def _positions_kernel(idx_ref, start_ref, o_ref, run_ref):
    step = pl.program_id(0)
    tm = idx_ref.shape[0]

    @pl.when(step == 0)
    def _():
        run_ref[...] = jnp.zeros_like(run_ref)

    idx = idx_ref[...]
    lane = lax.broadcasted_iota(i32, (tm, N_EXPERTS), 1)
    onehot = jnp.zeros((tm, N_EXPERTS), f32)
    for k in range(TOP_K):
        onehot = jnp.where(lane == idx[:, k:k + 1], 1.0, onehot)
    r = lax.broadcasted_iota(i32, (tm, tm), 0)
    c = lax.broadcasted_iota(i32, (tm, tm), 1)
    below = jnp.where(r > c, 1.0, 0.0).astype(bf16)
    prior = jnp.dot(below, onehot.astype(bf16), preferred_element_type=f32)
    pos = prior + run_ref[...] + start_ref[...]
    lane_o = lax.broadcasted_iota(i32, (tm, LANES), 1)
    out = jnp.zeros((tm, LANES), i32)
    for k in range(TOP_K):
        dk = jnp.sum(jnp.where(lane == idx[:, k:k + 1], pos, 0.0), axis=-1, keepdims=True)
        out = jnp.where(lane_o == k, dk.astype(i32), out)
    o_ref[...] = out
    run_ref[...] += jnp.sum(onehot, axis=0, keepdims=True)


def _positions(idx, seg_start, tm):
    t = idx.shape[0]
    return pl.pallas_call(
        _positions_kernel,
        grid=(t // tm,),
        in_specs=[pl.BlockSpec((tm, LANES), lambda i: (i, 0)),
                  pl.BlockSpec((1, N_EXPERTS), lambda i: (0, 0))],
        out_specs=pl.BlockSpec((tm, LANES), lambda i: (i, 0)),
        out_shape=jax.ShapeDtypeStruct((t, LANES), i32),
        scratch_shapes=[pltpu.VMEM((1, N_EXPERTS), f32)],
        compiler_params=_params(("arbitrary",)),
        name="positions",
    )(idx, seg_start)


def _dispatch_kernel(dest_hbm, h_ref, o_hbm, dest_smem, sem_i, sem, *, r):
    step = pl.program_id(0)
    n = dest_smem.shape[0]
    load = pltpu.make_async_copy(dest_hbm.at[pl.ds(pl.multiple_of(step * n, n), n)], dest_smem, sem_i)
    load.start()
    load.wait()

    def issue(tok, _):
        src = pl.multiple_of(tok * r, r)
        for k in range(TOP_K):
            dst = pl.multiple_of(dest_smem[tok * TOP_K + k] * r, r)
            pltpu.make_async_copy(h_ref.at[pl.ds(src, r)], o_hbm.at[pl.ds(dst, r)], sem).start(priority=k % 2)
        return 0

    lax.fori_loop(0, n // TOP_K, issue, 0)
    everything = o_hbm.at[pl.ds(0, n * r)]
    pltpu.make_async_copy(everything, everything, sem).wait()


def _dispatch(dest_flat, h2p, n_buf, r, tm):
    t = h2p.shape[0] // r
    return pl.pallas_call(
        functools.partial(_dispatch_kernel, r=r),
        grid=(t // tm,),
        in_specs=[pl.BlockSpec(memory_space=pl.ANY),
                  pl.BlockSpec((tm * r, LANES), lambda i: (i, 0))],
        out_specs=pl.BlockSpec(memory_space=pl.ANY),
        out_shape=jax.ShapeDtypeStruct((n_buf * r, LANES), u32),
        scratch_shapes=[pltpu.SMEM((tm * TOP_K,), i32), pltpu.SemaphoreType.DMA, pltpu.SemaphoreType.DMA],
        compiler_params=_params(("arbitrary",)),
        name="dispatch",
    )(dest_flat, h2p)


def _experts_kernel(be_ref, bv_ref, nu_ref, x_ref, wg_ref, wu_ref, wd_ref, o_ref, wg_s, wu_s, wd_s):
    j = pl.program_id(0)
    prev = be_ref[jnp.maximum(j - 1, 0)]
    used = j < nu_ref[0]
    d = wg_s.shape[0]

    @pl.when(used & ((j == 0) | (be_ref[j] != prev)))
    def _():
        wg_s[...] = wg_ref[0].astype(bf16)
        wu_s[...] = wu_ref[0].astype(bf16)
        wd_s[...] = wd_ref[0].astype(bf16)

    @pl.when(used)
    def _():
        live = lax.broadcasted_iota(i32, (ROW_BLOCK, 1), 0) < bv_ref[j]
        half = d // 2
        r = half // LANES
        hg = jnp.zeros((ROW_BLOCK, wg_s.shape[1]), f32)
        hu = jnp.zeros((ROW_BLOCK, wg_s.shape[1]), f32)
        for c in range(r):
            parts = _unpack_words(x_ref[pl.ds(c, ROW_BLOCK, stride=r), :])
            for off, part in zip((0, half), parts):
                xc = jnp.where(live, part, 0.0).astype(bf16)
                rows = slice(off + c * LANES, off + (c + 1) * LANES)
                hg = hg + jnp.dot(xc, wg_s[rows, :], preferred_element_type=f32)
                hu = hu + jnp.dot(xc, wu_s[rows, :], preferred_element_type=f32)
        hid = (_silu(hg) * hu).astype(bf16)
        _store_packed_rows(o_ref, jnp.dot(hid, wd_s[...], preferred_element_type=f32))


def _experts(blk_e, blk_valid, n_used, xs, w_gate, w_up, w_down):
    d, f = w_gate.shape[1], w_gate.shape[2]
    r = d // 2 // LANES
    n_blk = xs.shape[0] // r // ROW_BLOCK
    rows = lambda j, be, bv, nu: (jnp.minimum(j, nu[0] - 1), 0)
    wsel = lambda j, be, bv, nu: (be[j], 0, 0)
    return pl.pallas_call(
        _experts_kernel,
        grid_spec=pltpu.PrefetchScalarGridSpec(
            num_scalar_prefetch=3,
            grid=(n_blk,),
            in_specs=[pl.BlockSpec((ROW_BLOCK * r, LANES), rows),
                      pl.BlockSpec((1, d, f), wsel),
                      pl.BlockSpec((1, d, f), wsel),
                      pl.BlockSpec((1, f, d), wsel)],
            out_specs=pl.BlockSpec((ROW_BLOCK * r, LANES), rows),
            scratch_shapes=[pltpu.VMEM((d, f), bf16), pltpu.VMEM((d, f), bf16), pltpu.VMEM((f, d), bf16)]),
        out_shape=jax.ShapeDtypeStruct(xs.shape, u32),
        compiler_params=_params(("arbitrary",)),
        name="experts",
    )(blk_e, blk_valid, n_used, xs, w_gate, w_up, w_down)


def _combine_kernel(dest_hbm, ys_hbm, wt_ref, h_ref, x1_ref, mod_ref, wsg_ref, wsu_ref, wsd_ref,
                    lg_ref, lb_ref, o_ref, dest_smem, rows_ref, sem_i, sem):
    step = pl.program_id(0)
    n = dest_smem.shape[0]
    load = pltpu.make_async_copy(dest_hbm.at[pl.ds(pl.multiple_of(step * n, n), n)], dest_smem, sem_i)
    load.start()
    load.wait()

    tm, d = x1_ref.shape
    half = d // 2
    r = half // LANES

    def issue(tok, _):
        dst = pl.multiple_of(tok * r, r)
        for k in range(TOP_K):
            src = pl.multiple_of(dest_smem[tok * TOP_K + k] * r, r)
            pltpu.make_async_copy(ys_hbm.at[pl.ds(src, r)], rows_ref.at[k, pl.ds(dst, r)],
                                  sem).start(priority=k % 2)
        return 0

    lax.fori_loop(0, tm, issue, 0)

    hg = jnp.zeros((tm, wsg_ref.shape[1]), f32)
    hu = jnp.zeros((tm, wsg_ref.shape[1]), f32)
    for c in range(r):
        parts = _unpack_words(h_ref[pl.ds(c, tm, stride=r), :])
        for off, part in zip((0, half), parts):
            rows = slice(off + c * LANES, off + (c + 1) * LANES)
            hg = hg + jnp.dot(part.astype(bf16), wsg_ref[rows, :], preferred_element_type=f32)
            hu = hu + jnp.dot(part.astype(bf16), wsu_ref[rows, :], preferred_element_type=f32)
    y = jnp.dot((_silu(hg) * hu).astype(bf16), wsd_ref[...], preferred_element_type=f32)

    pltpu.make_async_copy(rows_ref, rows_ref, sem).wait()
    wt = wt_ref[...]
    lo = [jnp.zeros((tm, LANES), f32) for _ in range(r)]
    hi = [jnp.zeros((tm, LANES), f32) for _ in range(r)]
    for k in range(TOP_K):
        wk = wt[:, k:k + 1]
        for c in range(r):
            pl_, ph_ = _unpack_words(rows_ref[k, pl.ds(c, tm, stride=r), :])
            lo[c] = lo[c] + wk * pl_
            hi[c] = hi[c] + wk * ph_
    y = y + jnp.concatenate(lo + hi, axis=1)
    m = mod_ref[0]
    o_ref[...] = _layer_norm(DEEPNORM_ALPHA * x1_ref[...] + m[5:6] * y, lg_ref[...], lb_ref[...])


def _combine(dest_flat, ys, wts, h2p, x1, mod, w_s_gate, w_s_up, w_s_down, ln_g, ln_b, s, tm):
    t, d = x1.shape
    per = s // tm
    r = d // 2 // LANES
    row = lambda a: a.reshape(1, -1)
    tile = lambda w: pl.BlockSpec((tm, w), lambda i: (i, 0))
    full = lambda a: pl.BlockSpec(a.shape, lambda i: (0,) * a.ndim)
    tail = (w_s_gate.astype(bf16), w_s_up.astype(bf16), w_s_down.astype(bf16), row(ln_g), row(ln_b))
    return pl.pallas_call(
        _combine_kernel,
        grid=(t // tm,),
        in_specs=[pl.BlockSpec(memory_space=pl.ANY), pl.BlockSpec(memory_space=pl.ANY),
                  tile(LANES), pl.BlockSpec((tm * r, LANES), lambda i: (i, 0)), tile(d),
                  pl.BlockSpec((1, 6, d), lambda i: (i // per, 0, 0))] + [full(a) for a in tail],
        out_specs=tile(d),
        out_shape=jax.ShapeDtypeStruct((t, d), f32),
        scratch_shapes=[pltpu.SMEM((tm * TOP_K,), i32), pltpu.VMEM((TOP_K, tm * r, LANES), u32),
                        pltpu.SemaphoreType.DMA, pltpu.SemaphoreType.DMA],
        compiler_params=_params(("arbitrary",)),
        name="combine",
    )(dest_flat, ys, wts, h2p, x1, mod, *tail)


def _layer(x, mod, w_in, pe, w_cmp1, w_cmp2, w_dw, b_dw, conv_ln_g, conv_ln_b, w_out, ln1_g, ln1_b,
           w_router, router_bias, w_e_gate, w_e_up, w_e_down, w_s_gate, w_s_up, w_s_down, ln2_g, ln2_b):
    b, s, d = x.shape
    t = b * s
    tm = min(512, s)
    qt, vt, gt, kvc, kn, conv_in = _in_proj(x, mod, w_in, tm)
    cmp_n, cmp_t = _compress(kvc, pe, w_cmp1, w_cmp2)
    gates = gt[:, :3 * N_HEADS].reshape(b, N_KV, 3 * HPG, s)
    gates = jnp.pad(gates, ((0, 0), (0, 0), (0, GATE_GROUP_ROWS - 3 * HPG), (0, 0)))
    attn = _attention(qt, gates, cmp_n, cmp_t, kn, vt)
    conv = _conv(conv_in, w_dw, b_dw, conv_ln_g, conv_ln_b, tm)

    tr = min(256, s)
    x1, h2p, idx, wts, counts = _mix_route(attn.reshape(t, -1), conv.reshape(t, -1), x.reshape(t, d), mod,
                                           w_out, ln1_g, ln1_b, w_router, router_bias, s, tr)
    counts = counts[0].astype(i32)
    padded = (counts + ROW_BLOCK - 1) // ROW_BLOCK * ROW_BLOCK
    seg_end = jnp.cumsum(padded)
    seg_start = seg_end - padded
    n_blk = -(-(t * TOP_K + N_EXPERTS * (ROW_BLOCK - 1)) // ROW_BLOCK)
    blk_row0 = jnp.arange(n_blk, dtype=i32) * ROW_BLOCK
    blk_e = jnp.minimum(jnp.searchsorted(seg_end, blk_row0, side='right'), N_EXPERTS - 1).astype(i32)
    blk_valid = jnp.clip(seg_start[blk_e] + counts[blk_e] - blk_row0, 0, ROW_BLOCK).astype(i32)
    n_used = (seg_end[-1:] // ROW_BLOCK).astype(i32)

    dest = _positions(idx, seg_start.astype(f32).reshape(1, N_EXPERTS), tr)
    dest_flat = dest[:, :TOP_K].reshape(-1)
    xs = _dispatch(dest_flat, h2p, n_blk * ROW_BLOCK, d // 2 // LANES, min(512, s))
    ys = _experts(blk_e, blk_valid, n_used, xs, w_e_gate, w_e_up, w_e_down)
    out = _combine(dest_flat, ys, wts, h2p, x1, mod, w_s_gate, w_s_up, w_s_down, ln2_g, ln2_b, s, min(256, s))
    return out.reshape(b, s, d)


def kernel(x, c, w_ada, b_ada, w_in, pe_k, pe_v, w_cmp_k1, w_cmp_k2, w_cmp_v1, w_cmp_v2, w_dw, b_dw,
           conv_ln_g, conv_ln_b, w_out, ln1_g, ln1_b, w_router, router_bias, w_e_gate, w_e_up, w_e_down,
           w_s_gate, w_s_up, w_s_down, ln2_g, ln2_b):
    assert w_ada.shape[0] == DEPTH
    layer = lambda a: a.reshape(a.shape[1:])
    mod = _ada(c, layer(w_ada), layer(b_ada))
    return _layer(x, mod, layer(w_in), jnp.concatenate([pe_k, pe_v]),
                  jnp.concatenate([w_cmp_k1, w_cmp_v1]), jnp.concatenate([w_cmp_k2, w_cmp_v2]),
                  *[layer(a) for a in (w_dw, b_dw, conv_ln_g, conv_ln_b, w_out, ln1_g, ln1_b, w_router,
                                       router_bias, w_e_gate, w_e_up, w_e_down, w_s_gate, w_s_up, w_s_down,
                                       ln2_g, ln2_b)])
```

```python
import functools

import jax
import jax.numpy as jnp
from jax import lax
from jax.experimental import pallas as pl
from jax.experimental.pallas import tpu as pltpu

N_HEADS = 8
N_KV = 2
HPG = N_HEADS // N_KV
HEAD_DIM = 64
D_ATTN = N_HEADS * HEAD_DIM
D_KV = N_KV * HEAD_DIM
CONV_WIDTH = 31
CMP_LEN = 32
CMP_STRIDE = 16
CMP_HID = 256
SEL_LEN = 64
SEL_TOPN = 16
WINDOW = 512
Q_BLOCK = 128
N_EXPERTS = 256
TOP_K = 8
N_GROUPS = 8
TOPK_GROUPS = 4
ROUTED_SCALE = 2.5
LN_EPS = 1e-5
DEPTH = 1
DEEPNORM_ALPHA = (2 * DEPTH) ** 0.25

LANES = 128
SUBLANES = 8
ROW_BLOCK = 256
NEG = -1e30
HIGHEST = lax.Precision.HIGHEST
VMEM_LIMIT = 48 * 1024 * 1024

f32 = jnp.float32
bf16 = jnp.bfloat16
i32 = jnp.int32
u32 = jnp.uint32


def _params(sem, vmem=VMEM_LIMIT):
    return pltpu.CompilerParams(dimension_semantics=sem, vmem_limit_bytes=vmem)


def _sigmoid(v):
    return 1.0 / (1.0 + jnp.exp(-v))


def _silu(v):
    return v * _sigmoid(v)


def _layer_norm(v, g, b):
    mu = jnp.mean(v, axis=-1, keepdims=True)
    var = jnp.mean(jnp.square(v - mu), axis=-1, keepdims=True)
    return (v - mu) * lax.rsqrt(var + LN_EPS) * g + b


def _dot_nt(a, b):
    return lax.dot_general(a, b, (((1,), (1,)), ((), ())), preferred_element_type=f32)


def _store_packed_rows(ref, v):
    n, d = v.shape
    half = d // 2
    bits = lax.bitcast_convert_type(v.astype(bf16).astype(f32), u32)
    words = bits[:, half:] | (bits[:, :half] >> 16)
    r = half // LANES
    for c in range(r):
        ref[pl.ds(c, n, stride=r), :] = words[:, c * LANES:(c + 1) * LANES]


def _unpack_words(w):
    return (lax.bitcast_convert_type(w << 16, f32),
            lax.bitcast_convert_type(w & jnp.uint32(0xFFFF0000), f32))


def _ada_kernel(c_ref, w_ref, b_ref, o_ref):
    c = c_ref[...]
    o_ref[...] = jnp.dot(_silu(c), w_ref[...], precision=HIGHEST,
                         preferred_element_type=f32) + b_ref[...]


def _ada(c, w_ada, b_ada):
    b, d = c.shape
    n = w_ada.shape[1]
    rows = 8
    c_pad = jnp.zeros((rows, d), f32).at[:b].set(c)
    tn = 1024
    out = pl.pallas_call(
        _ada_kernel,
        grid=(n // tn,),
        in_specs=[pl.BlockSpec((rows, d), lambda j: (0, 0)),
                  pl.BlockSpec((d, tn), lambda j: (0, j)),
                  pl.BlockSpec((1, tn), lambda j: (0, j))],
        out_specs=pl.BlockSpec((rows, tn), lambda j: (0, j)),
        out_shape=jax.ShapeDtypeStruct((rows, n), f32),
        compiler_params=_params(("arbitrary",)),
        name="ada",
    )(c_pad, w_ada, b_ada.reshape(1, n))
    return out[:b].reshape(b, 6, d)


GATE_ROWS = 32
KEY_TILE = 512
TILE_BLOCKS = KEY_TILE // SEL_LEN
CMP_PER_SEL = SEL_LEN // CMP_STRIDE
CMP_BACK = CMP_LEN // CMP_STRIDE - 1
assert SEL_LEN % CMP_STRIDE == 0 and CMP_LEN % CMP_STRIDE == 0 and CMP_BACK < CMP_PER_SEL
V_ROWS = HEAD_DIM + 16
Q_SCALE = HEAD_DIM ** -0.5 * 1.4426950408889634


def _in_proj_kernel(x_ref, mod_ref, wt_ref, wc_ref, wk_ref, wv_ref,
                    qt_ref, vt_ref, gt_ref, kvc_ref, kn_ref, cv_ref):
    m = mod_ref[0]
    h = (x_ref[0] * (1.0 + m[1:2]) + m[0:1]).astype(bf16)
    res_t = _dot_nt(wt_ref[...], h)
    qt_ref[0] = (res_t[0:D_ATTN] * Q_SCALE).astype(bf16)
    ones = jnp.ones((V_ROWS - HEAD_DIM, res_t.shape[1]), bf16)
    for j in range(2):
        for g in range(N_KV):
            off = D_ATTN + (j * N_KV + g) * HEAD_DIM
            vt_ref[0, j, g, 0:HEAD_DIM, :] = res_t[off:off + HEAD_DIM].astype(bf16)
            vt_ref[0, j, g, HEAD_DIM:V_ROWS, :] = ones
    gt_ref[0] = res_t[D_ATTN + 2 * D_KV:]
    kvc_ref[0] = jnp.dot(h, wc_ref[...], preferred_element_type=f32)
    kn = jnp.dot(h, wk_ref[...], preferred_element_type=f32).astype(bf16)
    tm = kn.shape[0]
    pos = pl.program_id(1) * tm + lax.broadcasted_iota(i32, (tm, LANES - HEAD_DIM), 0)
    lane = lax.broadcasted_iota(i32, (tm, LANES - HEAD_DIM), 1)
    onehot = jnp.where(lane == ((pos >> 6) & (TILE_BLOCKS - 1)), 1.0, 0.0).astype(bf16)
    for j in range(2):
        for g in range(N_KV):
            off = (j * N_KV + g) * HEAD_DIM
            kn_ref[0, j, g] = jnp.concatenate([kn[:, off:off + HEAD_DIM], onehot], axis=1)
    cv_ref[0] = jnp.dot(h, wv_ref[...], preferred_element_type=f32)


def _in_proj(x, mod, w_in, tm):
    b, s, d = x.shape
    o = 0
    wq = w_in[:, o:o + D_ATTN]; o += D_ATTN
    wkc = w_in[:, o:o + 2 * D_KV]; o += 2 * D_KV
    wk_s = w_in[:, o:o + D_KV]; o += D_KV
    wv_s = w_in[:, o:o + D_KV]; o += D_KV
    wk_w = w_in[:, o:o + D_KV]; o += D_KV
    wv_w = w_in[:, o:o + D_KV]; o += D_KV
    wg = w_in[:, o:o + 3 * N_HEADS]; o += 3 * N_HEADS
    wcv = w_in[:, o:]
    d_conv2 = wcv.shape[1]
    wg = jnp.zeros((d, GATE_ROWS), f32).at[:, :3 * N_HEADS].set(wg)
    wt = jnp.concatenate([wq, wv_s, wv_w, wg], axis=1).T
    ws = [w.astype(bf16) for w in (wt, wkc, jnp.concatenate([wk_s, wk_w], axis=1), wcv)]
    full = lambda a: pl.BlockSpec(a.shape, lambda bi, i: (0, 0))
    return pl.pallas_call(
        _in_proj_kernel,
        grid=(b, s // tm),
        in_specs=[pl.BlockSpec((1, tm, d), lambda bi, i: (bi, i, 0)),
                  pl.BlockSpec((1, 6, d), lambda bi, i: (bi, 0, 0))] + [full(w) for w in ws],
        out_specs=[pl.BlockSpec((1, D_ATTN, tm), lambda bi, i: (bi, 0, i)),
                   pl.BlockSpec((1, 2, N_KV, V_ROWS, tm), lambda bi, i: (bi, 0, 0, 0, i)),
                   pl.BlockSpec((1, GATE_ROWS, tm), lambda bi, i: (bi, 0, i)),
                   pl.BlockSpec((1, tm, 2 * D_KV), lambda bi, i: (bi, i, 0)),
                   pl.BlockSpec((1, 2, N_KV, tm, LANES), lambda bi, i: (bi, 0, 0, i, 0)),
                   pl.BlockSpec((1, tm, d_conv2), lambda bi, i: (bi, i, 0))],
        out_shape=[jax.ShapeDtypeStruct((b, D_ATTN, s), bf16),
                   jax.ShapeDtypeStruct((b, 2, N_KV, V_ROWS, s), bf16),
                   jax.ShapeDtypeStruct((b, GATE_ROWS, s), f32),
                   jax.ShapeDtypeStruct((b, s, 2 * D_KV), f32),
                   jax.ShapeDtypeStruct((b, 2, N_KV, s, LANES), bf16),
                   jax.ShapeDtypeStruct((b, s, d_conv2), f32)],
        compiler_params=_params(("parallel", "parallel")),
        name="in_proj",
    )(x, mod, *ws)


def _compress_kernel(c_ref, pe_ref, w1_ref, w2_ref, w2t_ref, o_ref, ot_ref):
    c = c_ref[0, 0, 0]
    n_chunk = c.shape[0]
    a = jnp.dot((c + pe_ref[0, 0]).astype(bf16), w1_ref[0, 0], preferred_element_type=f32)
    bm = jnp.dot((c + pe_ref[0, 1]).astype(bf16), w1_ref[0, 1], preferred_element_type=f32)
    hid = a + pltpu.roll(bm, n_chunk - 1, 0)
    act = 0.5 * hid * (1.0 + jnp.tanh(0.7978845608028654 * (hid + 0.044715 * (hid * hid * hid))))
    act = act.astype(bf16)
    o_ref[0, 0, 0] = jnp.dot(act, w2_ref[0], preferred_element_type=f32).astype(bf16)
    ot_ref[0, 0, 0] = _dot_nt(w2t_ref[0], act).astype(bf16)


def _compress(kvc, pe, w1, w2):
    b, s, _ = kvc.shape
    n_chunk = s // CMP_STRIDE
    half = CMP_STRIDE * HEAD_DIM
    c = kvc.reshape(b, n_chunk, CMP_STRIDE, 2, N_KV, HEAD_DIM).transpose(0, 3, 4, 1, 2, 5)
    c = c.reshape(b, 2, N_KV, n_chunk, half)
    pe2 = pe.reshape(2, 2, 1, half)
    w1h = w1.reshape(2, 2, half, CMP_HID).astype(bf16)
    w2b = w2.astype(bf16)
    w2t = w2b.transpose(0, 2, 1)
    return pl.pallas_call(
        _compress_kernel,
        grid=(b, 2, N_KV),
        in_specs=[pl.BlockSpec((1, 1, 1, n_chunk, half), lambda bi, j, g: (bi, j, g, 0, 0)),
                  pl.BlockSpec((1, 2, 1, half), lambda bi, j, g: (j, 0, 0, 0)),
                  pl.BlockSpec((1, 2, half, CMP_HID), lambda bi, j, g: (j, 0, 0, 0)),
                  pl.BlockSpec((1, CMP_HID, HEAD_DIM), lambda bi, j, g: (j, 0, 0)),
                  pl.BlockSpec((1, HEAD_DIM, CMP_HID), lambda bi, j, g: (j, 0, 0))],
        out_specs=[pl.BlockSpec((1, 1, 1, n_chunk, HEAD_DIM), lambda bi, j, g: (bi, j, g, 0, 0)),
                   pl.BlockSpec((1, 1, 1, HEAD_DIM, n_chunk), lambda bi, j, g: (bi, j, g, 0, 0))],
        out_shape=[jax.ShapeDtypeStruct((b, 2, N_KV, n_chunk, HEAD_DIM), bf16),
                   jax.ShapeDtypeStruct((b, 2, N_KV, HEAD_DIM, n_chunk), bf16)],
        compiler_params=_params(("parallel", "parallel", "parallel")),
        name="compress",
    )(c, pe2, w1h, w2b, w2t)


def _attn_kernel(q_ref, g_ref, kc_ref, vc_ref, kn_ref, vt_ref, o_ref, psum_ref, selbias_ref, sa_ref, sb_ref,
                 *, seq, tk, top_n):
    i = pl.program_id(2)
    s0 = i * Q_BLOCK
    n_cmp_rows = kc_ref.shape[3]
    n_sel = seq // SEL_LEN

    q4 = q_ref[0]
    qt = jnp.concatenate([q4[p * HEAD_DIM:(p + 1) * HEAD_DIM, :] for p in range(HPG)], axis=1)
    t_row = s0 + lax.broadcasted_iota(i32, (1, Q_BLOCK), 1)

    s_c = jnp.dot(kc_ref[0, 0, 0], qt, preferred_element_type=f32)
    cmp_end = lax.broadcasted_iota(i32, (n_cmp_rows, 1), 0) * CMP_STRIDE + (CMP_LEN - 1)
    bias_c = jnp.where(cmp_end <= t_row, 0.0, NEG)
    any_c = t_row >= CMP_LEN - 1
    p_sum = jnp.zeros((n_cmp_rows, Q_BLOCK), f32)
    pcs = []
    for p in range(HPG):
        sp = s_c[:, p * Q_BLOCK:(p + 1) * Q_BLOCK] + bias_c
        e = jnp.exp2(sp - jnp.max(sp, axis=0, keepdims=True))
        pn = e * jnp.where(any_c, 1.0 / jnp.sum(e, axis=0, keepdims=True), 0.0)
        p_sum = p_sum + pn
        pcs.append(pn.astype(bf16))
    o_c = jnp.dot(vc_ref[0, 0, 0], jnp.concatenate(pcs, axis=1), preferred_element_type=f32)

    cols = HPG * Q_BLOCK
    pad_rows = LANES - HEAD_DIM
    q_pad = jnp.concatenate([qt, jnp.zeros((pad_rows, cols), bf16)], axis=0)
    span = WINDOW + Q_BLOCK
    w0 = pl.multiple_of(jnp.maximum(s0 - WINDOW, 0), Q_BLOCK)
    s_w = jnp.dot(kn_ref[0, 1, 0, pl.ds(w0, span), :], q_pad, preferred_element_type=f32)
    wpos = w0 + lax.broadcasted_iota(i32, (span, 1), 0)
    bias_w = jnp.where((wpos <= t_row) & (wpos > t_row - WINDOW), 0.0, NEG)
    pws = []
    for p in range(HPG):
        sp = s_w[:, p * Q_BLOCK:(p + 1) * Q_BLOCK] + bias_w
        pws.append(jnp.exp2(sp - jnp.max(sp, axis=0, keepdims=True)).astype(bf16))
    acc_w = jnp.dot(vt_ref[0, 1, 0, :, pl.ds(w0, span)], jnp.concatenate(pws, axis=1),
                    preferred_element_type=f32)
    o_w = acc_w[0:HEAD_DIM] * (1.0 / acc_w[HEAD_DIM:HEAD_DIM + 1])

    psum_ref[...] = p_sum
    blk = lax.broadcasted_iota(i32, (n_sel, Q_BLOCK), 0)
    imp = psum_ref[pl.ds(0, n_sel, stride=CMP_PER_SEL), :]
    for r in range(1, CMP_PER_SEL):
        imp = imp + psum_ref[pl.ds(r, n_sel, stride=CMP_PER_SEL), :]
    for back in range(1, CMP_BACK + 1):
        prev = pltpu.roll(psum_ref[pl.ds(CMP_PER_SEL - back, n_sel, stride=CMP_PER_SEL), :], 1, 0)
        imp = imp + jnp.where(blk >= 1, prev, 0.0)

    cur = t_row >> 6
    forced = (blk == 0) | (blk == cur) | (blk == cur - 1)
    vals = jnp.where(forced, jnp.inf, jnp.where(blk <= cur, imp, -jnp.inf))
    sel = jnp.zeros((n_sel, Q_BLOCK), f32)
    for _ in range(top_n):
        mx = jnp.max(vals, axis=0, keepdims=True)
        first = jnp.min(jnp.where(vals == mx, blk, n_sel), axis=0, keepdims=True)
        pick = blk == first
        sel = jnp.where(pick & (mx > -jnp.inf), 1.0, sel)
        vals = jnp.where(pick, -jnp.inf, vals)
    selbias_ref[0:n_sel] = jnp.where((sel > 0.5) & (blk < (s0 >> 6)), 0.0, NEG)
    selbias_ref[n_sel:n_sel + TILE_BLOCKS] = jnp.full((TILE_BLOCKS, Q_BLOCK), NEG, f32)
    last_tile = seq // tk - 1

    def scores(kt, s_ref):
        k0 = pl.multiple_of(jnp.minimum(kt, last_tile) * tk, tk)
        sb = selbias_ref[pl.ds(pl.multiple_of(kt * TILE_BLOCKS, TILE_BLOCKS), TILE_BLOCKS), :]
        rows = jnp.concatenate([jnp.concatenate([sb] * HPG, axis=1),
                                jnp.zeros((pad_rows - TILE_BLOCKS, cols), f32)], axis=0).astype(bf16)
        s_ref[...] = jnp.dot(kn_ref[0, 0, 0, pl.ds(k0, tk), :], jnp.concatenate([qt, rows], axis=0),
                             preferred_element_type=f32)

    def fold(scores_of_head, v_t, carry):
        m_i, acc = carry
        es, ms, alphas = [], [], []
        for p in range(HPG):
            c = slice(p * Q_BLOCK, (p + 1) * Q_BLOCK)
            sp = scores_of_head(c)
            m_new = jnp.maximum(m_i[:, c], jnp.max(sp, axis=0, keepdims=True))
            es.append(jnp.exp2(sp - m_new).astype(bf16))
            alphas.append(jnp.exp2(m_i[:, c] - m_new))
            ms.append(m_new)
        pv = jnp.dot(v_t, jnp.concatenate(es, axis=1), preferred_element_type=f32)
        return jnp.concatenate(ms, axis=1), jnp.concatenate(alphas, axis=1) * acc + pv

    def sel_tile(kt, s_ref, carry):
        k0 = pl.multiple_of(jnp.minimum(kt, last_tile) * tk, tk)
        return fold(lambda c: s_ref[:, c], vt_ref[0, 0, 0, :, pl.ds(k0, tk)], carry)

    def sel_pair(i, carry):
        scores(2 * i + 1, sb_ref)
        carry = sel_tile(2 * i, sa_ref, carry)
        scores(2 * i + 2, sa_ref)
        return sel_tile(2 * i + 1, sb_ref, carry)

    n_sweep = (s0 + tk - 1) // tk
    init = (jnp.full((1, cols), NEG, f32), jnp.zeros((V_ROWS, cols), f32))
    scores(0, sa_ref)
    carry = lax.fori_loop(0, (n_sweep + 1) // 2, sel_pair, init)
    d0 = pl.multiple_of(s0, Q_BLOCK)
    s_d = jnp.dot(kn_ref[0, 0, 0, pl.ds(d0, Q_BLOCK), :], q_pad, preferred_element_type=f32)
    bias_d = jnp.where(s0 + lax.broadcasted_iota(i32, (Q_BLOCK, 1), 0) <= t_row, 0.0, NEG)
    _, acc_s = fold(lambda c: s_d[:, c] + bias_d, vt_ref[0, 0, 0, :, pl.ds(d0, Q_BLOCK)], carry)
    o_s = acc_s[0:HEAD_DIM] * (1.0 / acc_s[HEAD_DIM:HEAD_DIM + 1])

    gate = _sigmoid(g_ref[0, 0])
    outs = []
    for p in range(HPG):
        c = slice(p * Q_BLOCK, (p + 1) * Q_BLOCK)
        outs.append(gate[3 * p:3 * p + 1] * o_c[:, c] + gate[3 * p + 1:3 * p + 2] * o_s[:, c]
                    + gate[3 * p + 2:3 * p + 3] * o_w[:, c])
    o_ref[0] = jnp.concatenate(outs, axis=0).T.astype(bf16)


GATE_GROUP_ROWS = 16


def _attention(qt, gates, cmp_n, cmp_t, kn, vt):
    b, _, s = qt.shape
    n_chunk = cmp_n.shape[3]
    n_sel = s // SEL_LEN
    top_n = min(SEL_TOPN, n_sel)
    tk = KEY_TILE
    assert s % tk == 0 and n_chunk == n_sel * CMP_PER_SEL
    gw = HPG * HEAD_DIM
    return pl.pallas_call(
        functools.partial(_attn_kernel, seq=s, tk=tk, top_n=top_n),
        grid=(b, N_KV, s // Q_BLOCK),
        in_specs=[pl.BlockSpec((1, gw, Q_BLOCK), lambda bi, g, i: (bi, g, i)),
                  pl.BlockSpec((1, 1, GATE_GROUP_ROWS, Q_BLOCK), lambda bi, g, i: (bi, g, 0, i)),
                  pl.BlockSpec((1, 1, 1, n_chunk, HEAD_DIM), lambda bi, g, i: (bi, 0, g, 0, 0)),
                  pl.BlockSpec((1, 1, 1, HEAD_DIM, n_chunk), lambda bi, g, i: (bi, 1, g, 0, 0)),
                  pl.BlockSpec((1, 2, 1, s, LANES), lambda bi, g, i: (bi, 0, g, 0, 0)),
                  pl.BlockSpec((1, 2, 1, V_ROWS, s), lambda bi, g, i: (bi, 0, g, 0, 0))],
        out_specs=pl.BlockSpec((1, Q_BLOCK, gw), lambda bi, g, i: (bi, i, g)),
        out_shape=jax.ShapeDtypeStruct((b, s, D_ATTN), bf16),
        scratch_shapes=[pltpu.VMEM((n_chunk, Q_BLOCK), f32), pltpu.VMEM((n_sel + TILE_BLOCKS, Q_BLOCK), f32),
                        pltpu.VMEM((tk, HPG * Q_BLOCK), f32), pltpu.VMEM((tk, HPG * Q_BLOCK), f32)],
        compiler_params=_params(("parallel", "parallel", "arbitrary")),
        name="attention",
    )(qt, gates, cmp_n, cmp_t, kn, vt)


CONV_HALO = 32
CONV_ROWS = 32


def _conv_kernel(cur_ref, prev_ref, w_ref, b_ref, g_ref, bb_ref, o_ref, glu_ref):
    i = pl.program_id(1)
    ts = cur_ref.shape[1]
    dc = o_ref.shape[2]
    cur = cur_ref[0]
    prev = prev_ref[0]
    glu_prev = prev[:, :dc] * _sigmoid(prev[:, dc:])
    glu_ref[0:CONV_HALO] = jnp.where(i == 0, 0.0, glu_prev)
    glu_ref[CONV_HALO:CONV_HALO + ts] = cur[:, :dc] * _sigmoid(cur[:, dc:])
    lead = CONV_HALO - (CONV_WIDTH - 1)

    def chunk(r, _):
        r0 = pl.multiple_of(r * CONV_ROWS, CONV_ROWS)
        win = glu_ref[pl.ds(r0, CONV_ROWS + CONV_HALO), :]
        acc = jnp.zeros((CONV_ROWS, dc), f32)
        for j in range(CONV_WIDTH):
            acc = acc + w_ref[j:j + 1, :] * win[lead + j:lead + j + CONV_ROWS]
        y = _layer_norm(acc + b_ref[...], g_ref[...], bb_ref[...])
        o_ref[0, pl.ds(r0, CONV_ROWS), :] = _silu(y).astype(bf16)
        return 0

    lax.fori_loop(0, ts // CONV_ROWS, chunk, 0)


def _conv(conv_in, w_dw, b_dw, ln_g, ln_b, ts):
    b, s, dc2 = conv_in.shape
    dc = dc2 // 2
    per = ts // CONV_HALO
    row = lambda a: a.reshape(1, dc)
    return pl.pallas_call(
        _conv_kernel,
        grid=(b, s // ts),
        in_specs=[pl.BlockSpec((1, ts, dc2), lambda bi, i: (bi, i, 0)),
                  pl.BlockSpec((1, CONV_HALO, dc2), lambda bi, i: (bi, jnp.maximum(i * per - 1, 0), 0)),
                  pl.BlockSpec((CONV_WIDTH, dc), lambda bi, i: (0, 0)),
                  pl.BlockSpec((1, dc), lambda bi, i: (0, 0)),
                  pl.BlockSpec((1, dc), lambda bi, i: (0, 0)),
                  pl.BlockSpec((1, dc), lambda bi, i: (0, 0))],
        out_specs=pl.BlockSpec((1, ts, dc), lambda bi, i: (bi, i, 0)),
        out_shape=jax.ShapeDtypeStruct((b, s, dc), bf16),
        scratch_shapes=[pltpu.VMEM((CONV_HALO + ts, dc), f32)],
        compiler_params=_params(("parallel", "parallel")),
        name="conv",
    )(conv_in, conv_in, w_dw.reshape(CONV_WIDTH, dc), row(b_dw), row(ln_g), row(ln_b))


def _mix_route_kernel(a_ref, cv_ref, x_ref, mod_ref, wo_ref, lg_ref, lb_ref, wr_ref, rb_ref,
                      x1_ref, h2p_ref, idx_ref, wt_ref, cnt_ref):
    step = pl.program_id(0)
    tm = x_ref.shape[0]
    m = mod_ref[0]
    da = a_ref.shape[1]
    mix = (jnp.dot(a_ref[...], wo_ref[0:da, :], preferred_element_type=f32)
           + jnp.dot(cv_ref[...], wo_ref[da:, :], preferred_element_type=f32))
    x1 = _layer_norm(DEEPNORM_ALPHA * x_ref[...] + m[2:3] * mix, lg_ref[...], lb_ref[...])
    x1_ref[...] = x1
    h2 = x1 * (1.0 + m[4:5]) + m[3:4]
    _store_packed_rows(h2p_ref, h2)

    score = _sigmoid(jnp.dot(h2, wr_ref[...], precision=HIGHEST, preferred_element_type=f32))
    sel = score + rb_ref[...]
    lane = lax.broadcasted_iota(i32, (tm, N_EXPERTS), 1)
    grp = lane >> 5
    gs = []
    for g in range(N_GROUPS):
        v = jnp.where(grp == g, sel, -jnp.inf)
        m1 = jnp.max(v, axis=-1, keepdims=True)
        i1 = jnp.min(jnp.where(v == m1, lane, N_EXPERTS), axis=-1, keepdims=True)
        m2 = jnp.max(jnp.where(lane == i1, -jnp.inf, v), axis=-1, keepdims=True)
        gs.append(m1 + m2)
    gmask = jnp.zeros((tm, N_EXPERTS), jnp.bool_)
    for g in range(N_GROUPS):
        rank = jnp.zeros((tm, 1), i32)
        for o in range(N_GROUPS):
            if o == g:
                continue
            beats = (gs[o] > gs[g]) | (gs[o] == gs[g]) if o < g else gs[o] > gs[g]
            rank = rank + beats.astype(i32)
        gmask = gmask | ((grp == g) & (rank < TOPK_GROUPS))
    cand = jnp.where(gmask, sel, -jnp.inf)
    lane_o = lax.broadcasted_iota(i32, (tm, LANES), 1)
    idx_out = jnp.zeros((tm, LANES), i32)
    wt_out = jnp.zeros((tm, LANES), f32)
    picked = jnp.zeros((tm, N_EXPERTS), f32)
    w_sum = jnp.zeros((tm, 1), f32)
    for k in range(TOP_K):
        mx = jnp.max(cand, axis=-1, keepdims=True)
        ik = jnp.min(jnp.where(cand == mx, lane, N_EXPERTS), axis=-1, keepdims=True)
        pick = lane == ik
        wk = jnp.sum(jnp.where(pick, score, 0.0), axis=-1, keepdims=True)
        cand = jnp.where(pick, -jnp.inf, cand)
        picked = jnp.where(pick, 1.0, picked)
        idx_out = jnp.where(lane_o == k, ik, idx_out)
        wt_out = jnp.where(lane_o == k, wk, wt_out)
        w_sum = w_sum + wk
    idx_ref[...] = idx_out
    wt_ref[...] = wt_out / w_sum * ROUTED_SCALE

    @pl.when(step == 0)
    def _():
        cnt_ref[...] = jnp.zeros_like(cnt_ref)

    cnt_ref[...] += jnp.sum(picked, axis=0, keepdims=True)


def _mix_route(attn, conv, x2, mod, w_out, ln_g, ln_b, w_router, router_bias, s, tm):
    t, d = x2.shape
    per = s // tm
    da = attn.shape[1]
    pack_rows = d // 2 // LANES
    row = lambda a: a.reshape(1, -1)
    tile = lambda w: pl.BlockSpec((tm, w), lambda i: (i, 0))
    full = lambda a: pl.BlockSpec(a.shape, lambda i: (0,) * a.ndim)
    args = (attn, conv, x2, mod, w_out.astype(bf16), row(ln_g), row(ln_b), w_router, row(router_bias))
    return pl.pallas_call(
        _mix_route_kernel,
        grid=(t // tm,),
        in_specs=[tile(da), tile(conv.shape[1]), tile(d),
                  pl.BlockSpec((1, 6, d), lambda i: (i // per, 0, 0))] + [full(a) for a in args[4:]],
        out_specs=[tile(d), pl.BlockSpec((tm * pack_rows, LANES), lambda i: (i, 0)),
                   tile(LANES), tile(LANES), pl.BlockSpec((1, N_EXPERTS), lambda i: (0, 0))],
        out_shape=[jax.ShapeDtypeStruct((t, d), f32),
                   jax.ShapeDtypeStruct((t * pack_rows, LANES), u32),
                   jax.ShapeDtypeStruct((t, LANES), i32), jax.ShapeDtypeStruct((t, LANES), f32),
                   jax.ShapeDtypeStruct((1, N_EXPERTS), f32)],
        compiler_params=_params(("arbitrary",)),
        name="mix_route",
    )(*args)


def _positions_kernel(idx_ref, start_ref, o_ref, run_ref):
    step = pl.program_id(0)
    tm = idx_ref.shape[0]

    @pl.when(step == 0)
    def _():
        run_ref[...] = jnp.zeros_like(run_ref)

    idx = idx_ref[...]
    lane = lax.broadcasted_iota(i32, (tm, N_EXPERTS), 1)
    onehot = jnp.zeros((tm, N_EXPERTS), f32)
    for k in range(TOP_K):
        onehot = jnp.where(lane == idx[:, k:k + 1], 1.0, onehot)
    r = lax.broadcasted_iota(i32, (tm, tm), 0)
    c = lax.broadcasted_iota(i32, (tm, tm), 1)
    below = jnp.where(r > c, 1.0, 0.0).astype(bf16)
    prior = jnp.dot(below, onehot.astype(bf16), preferred_element_type=f32)
    pos = prior + run_ref[...] + start_ref[...]
    lane_o = lax.broadcasted_iota(i32, (tm, LANES), 1)
    out = jnp.zeros((tm, LANES), i32)
    for k in range(TOP_K):
        dk = jnp.sum(jnp.where(lane == idx[:, k:k + 1], pos, 0.0), axis=-1, keepdims=True)
        out = jnp.where(lane_o == k, dk.astype(i32), out)
    o_ref[...] = out
    run_ref[...] += jnp.sum(onehot, axis=0, keepdims=True)


def _positions(idx, seg_start, tm):
    t = idx.shape[0]
    return pl.pallas_call(
        _positions_kernel,
        grid=(t // tm,),
        in_specs=[pl.BlockSpec((tm, LANES), lambda i: (i, 0)),
                  pl.BlockSpec((1, N_EXPERTS), lambda i: (0, 0))],
        out_specs=pl.BlockSpec((tm, LANES), lambda i: (i, 0)),
        out_shape=jax.ShapeDtypeStruct((t, LANES), i32),
        scratch_shapes=[pltpu.VMEM((1, N_EXPERTS), f32)],
        compiler_params=_params(("arbitrary",)),
        name="positions",
    )(idx, seg_start)


def _dispatch_kernel(dest_hbm, h_ref, o_hbm, dest_smem, sem_i, sem, *, r):
    step = pl.program_id(0)
    n = dest_smem.shape[0]
    load = pltpu.make_async_copy(dest_hbm.at[pl.ds(pl.multiple_of(step * n, n), n)], dest_smem, sem_i)
    load.start()
    load.wait()

    def issue(tok, _):
        src = pl.multiple_of(tok * r, r)
        for k in range(TOP_K):
            dst = pl.multiple_of(dest_smem[tok * TOP_K + k] * r, r)
            pltpu.make_async_copy(h_ref.at[pl.ds(src, r)], o_hbm.at[pl.ds(dst, r)], sem).start(priority=k % 2)
        return 0

    lax.fori_loop(0, n // TOP_K, issue, 0)
    everything = o_hbm.at[pl.ds(0, n * r)]
    pltpu.make_async_copy(everything, everything, sem).wait()


def _dispatch(dest_flat, h2p, n_buf, r, tm):
    t = h2p.shape[0] // r
    return pl.pallas_call(
        functools.partial(_dispatch_kernel, r=r),
        grid=(t // tm,),
        in_specs=[pl.BlockSpec(memory_space=pl.ANY),
                  pl.BlockSpec((tm * r, LANES), lambda i: (i, 0))],
        out_specs=pl.BlockSpec(memory_space=pl.ANY),
        out_shape=jax.ShapeDtypeStruct((n_buf * r, LANES), u32),
        scratch_shapes=[pltpu.SMEM((tm * TOP_K,), i32), pltpu.SemaphoreType.DMA, pltpu.SemaphoreType.DMA],
        compiler_params=_params(("arbitrary",)),
        name="dispatch",
    )(dest_flat, h2p)


def _experts_kernel(be_ref, bv_ref, nu_ref, x_ref, wg_ref, wu_ref, wd_ref, o_ref, wg_s, wu_s, wd_s):
    j = pl.program_id(0)
    prev = be_ref[jnp.maximum(j - 1, 0)]
    used = j < nu_ref[0]
    d = wg_s.shape[0]

    @pl.when(used & ((j == 0) | (be_ref[j] != prev)))
    def _():
        wg_s[...] = wg_ref[0].astype(bf16)
        wu_s[...] = wu_ref[0].astype(bf16)
        wd_s[...] = wd_ref[0].astype(bf16)

    @pl.when(used)
    def _():
        live = lax.broadcasted_iota(i32, (ROW_BLOCK, 1), 0) < bv_ref[j]
        half = d // 2
        r = half // LANES
        hg = jnp.zeros((ROW_BLOCK, wg_s.shape[1]), f32)
        hu = jnp.zeros((ROW_BLOCK, wg_s.shape[1]), f32)
        for c in range(r):
            parts = _unpack_words(x_ref[pl.ds(c, ROW_BLOCK, stride=r), :])
            for off, part in zip((0, half), parts):
                xc = jnp.where(live, part, 0.0).astype(bf16)
                rows = slice(off + c * LANES, off + (c + 1) * LANES)
                hg = hg + jnp.dot(xc, wg_s[rows, :], preferred_element_type=f32)
                hu = hu + jnp.dot(xc, wu_s[rows, :], preferred_element_type=f32)
        hid = (_silu(hg) * hu).astype(bf16)
        _store_packed_rows(o_ref, jnp.dot(hid, wd_s[...], preferred_element_type=f32))


def _experts(blk_e, blk_valid, n_used, xs, w_gate, w_up, w_down):
    d, f = w_gate.shape[1], w_gate.shape[2]
    r = d // 2 // LANES
    n_blk = xs.shape[0] // r // ROW_BLOCK
    rows = lambda j, be, bv, nu: (jnp.minimum(j, nu[0] - 1), 0)
    wsel = lambda j, be, bv, nu: (be[j], 0, 0)
    return pl.pallas_call(
        _experts_kernel,
        grid_spec=pltpu.PrefetchScalarGridSpec(
            num_scalar_prefetch=3,
            grid=(n_blk,),
            in_specs=[pl.BlockSpec((ROW_BLOCK * r, LANES), rows),
                      pl.BlockSpec((1, d, f), wsel),
                      pl.BlockSpec((1, d, f), wsel),
                      pl.BlockSpec((1, f, d), wsel)],
            out_specs=pl.BlockSpec((ROW_BLOCK * r, LANES), rows),
            scratch_shapes=[pltpu.VMEM((d, f), bf16), pltpu.VMEM((d, f), bf16), pltpu.VMEM((f, d), bf16)]),
        out_shape=jax.ShapeDtypeStruct(xs.shape, u32),
        compiler_params=_params(("arbitrary",)),
        name="experts",
    )(blk_e, blk_valid, n_used, xs, w_gate, w_up, w_down)


def _combine_kernel(dest_hbm, ys_hbm, wt_ref, h_ref, x1_ref, mod_ref, wsg_ref, wsu_ref, wsd_ref,
                    lg_ref, lb_ref, o_ref, dest_smem, rows_ref, sem_i, sem):
    step = pl.program_id(0)
    n = dest_smem.shape[0]
    load = pltpu.make_async_copy(dest_hbm.at[pl.ds(pl.multiple_of(step * n, n), n)], dest_smem, sem_i)
    load.start()
    load.wait()

    tm, d = x1_ref.shape
    half = d // 2
    r = half // LANES

    def issue(tok, _):
        dst = pl.multiple_of(tok * r, r)
        for k in range(TOP_K):
            src = pl.multiple_of(dest_smem[tok * TOP_K + k] * r, r)
            pltpu.make_async_copy(ys_hbm.at[pl.ds(src, r)], rows_ref.at[k, pl.ds(dst, r)],
                                  sem).start(priority=k % 2)
        return 0

    lax.fori_loop(0, tm, issue, 0)

    hg = jnp.zeros((tm, wsg_ref.shape[1]), f32)
    hu = jnp.zeros((tm, wsg_ref.shape[1]), f32)
    for c in range(r):
        parts = _unpack_words(h_ref[pl.ds(c, tm, stride=r), :])
        for off, part in zip((0, half), parts):
            rows = slice(off + c * LANES, off + (c + 1) * LANES)
            hg = hg + jnp.dot(part.astype(bf16), wsg_ref[rows, :], preferred_element_type=f32)
            hu = hu + jnp.dot(part.astype(bf16), wsu_ref[rows, :], preferred_element_type=f32)
    y = jnp.dot((_silu(hg) * hu).astype(bf16), wsd_ref[...], preferred_element_type=f32)

    pltpu.make_async_copy(rows_ref, rows_ref, sem).wait()
    wt = wt_ref[...]
    lo = [jnp.zeros((tm, LANES), f32) for _ in range(r)]
    hi = [jnp.zeros((tm, LANES), f32) for _ in range(r)]
    for k in range(TOP_K):
        wk = wt[:, k:k + 1]
        for c in range(r):
            pl_, ph_ = _unpack_words(rows_ref[k, pl.ds(c, tm, stride=r), :])
            lo[c] = lo[c] + wk * pl_
            hi[c] = hi[c] + wk * ph_
    y = y + jnp.concatenate(lo + hi, axis=1)
    m = mod_ref[0]
    o_ref[...] = _layer_norm(DEEPNORM_ALPHA * x1_ref[...] + m[5:6] * y, lg_ref[...], lb_ref[...])


def _combine(dest_flat, ys, wts, h2p, x1, mod, w_s_gate, w_s_up, w_s_down, ln_g, ln_b, s, tm):
    t, d = x1.shape
    per = s // tm
    r = d // 2 // LANES
    row = lambda a: a.reshape(1, -1)
    tile = lambda w: pl.BlockSpec((tm, w), lambda i: (i, 0))
    full = lambda a: pl.BlockSpec(a.shape, lambda i: (0,) * a.ndim)
    tail = (w_s_gate.astype(bf16), w_s_up.astype(bf16), w_s_down.astype(bf16), row(ln_g), row(ln_b))
    return pl.pallas_call(
        _combine_kernel,
        grid=(t // tm,),
        in_specs=[pl.BlockSpec(memory_space=pl.ANY), pl.BlockSpec(memory_space=pl.ANY),
                  tile(LANES), pl.BlockSpec((tm * r, LANES), lambda i: (i, 0)), tile(d),
                  pl.BlockSpec((1, 6, d), lambda i: (i // per, 0, 0))] + [full(a) for a in tail],
        out_specs=tile(d),
        out_shape=jax.ShapeDtypeStruct((t, d), f32),
        scratch_shapes=[pltpu.SMEM((tm * TOP_K,), i32), pltpu.VMEM((TOP_K, tm * r, LANES), u32),
                        pltpu.SemaphoreType.DMA, pltpu.SemaphoreType.DMA],
        compiler_params=_params(("arbitrary",)),
        name="combine",
    )(dest_flat, ys, wts, h2p, x1, mod, *tail)


def _layer(x, mod, w_in, pe, w_cmp1, w_cmp2, w_dw, b_dw, conv_ln_g, conv_ln_b, w_out, ln1_g, ln1_b,
           w_router, router_bias, w_e_gate, w_e_up, w_e_down, w_s_gate, w_s_up, w_s_down, ln2_g, ln2_b):
    b, s, d = x.shape
    t = b * s
    tm = min(512, s)
    qt, vt, gt, kvc, kn, conv_in = _in_proj(x, mod, w_in, tm)
    cmp_n, cmp_t = _compress(kvc, pe, w_cmp1, w_cmp2)
    gates = gt[:, :3 * N_HEADS].reshape(b, N_KV, 3 * HPG, s)
    gates = jnp.pad(gates, ((0, 0), (0, 0), (0, GATE_GROUP_ROWS - 3 * HPG), (0, 0)))
    attn = _attention(qt, gates, cmp_n, cmp_t, kn, vt)
    conv = _conv(conv_in, w_dw, b_dw, conv_ln_g, conv_ln_b, tm)

    tr = min(256, s)
    x1, h2p, idx, wts, counts = _mix_route(attn.reshape(t, -1), conv.reshape(t, -1), x.reshape(t, d), mod,
                                           w_out, ln1_g, ln1_b, w_router, router_bias, s, tr)
    counts = counts[0].astype(i32)
    padded = (counts + ROW_BLOCK - 1) // ROW_BLOCK * ROW_BLOCK
    seg_end = jnp.cumsum(padded)
    seg_start = seg_end - padded
    n_blk = -(-(t * TOP_K + N_EXPERTS * (ROW_BLOCK - 1)) // ROW_BLOCK)
    blk_row0 = jnp.arange(n_blk, dtype=i32) * ROW_BLOCK
    blk_e = jnp.minimum(jnp.searchsorted(seg_end, blk_row0, side='right'), N_EXPERTS - 1).astype(i32)
    blk_valid = jnp.clip(seg_start[blk_e] + counts[blk_e] - blk_row0, 0, ROW_BLOCK).astype(i32)
    n_used = (seg_end[-1:] // ROW_BLOCK).astype(i32)

    dest = _positions(idx, seg_start.astype(f32).reshape(1, N_EXPERTS), tr)
    dest_flat = dest[:, :TOP_K].reshape(-1)
    xs = _dispatch(dest_flat, h2p, n_blk * ROW_BLOCK, d // 2 // LANES, min(512, s))
    ys = _experts(blk_e, blk_valid, n_used, xs, w_e_gate, w_e_up, w_e_down)
    out = _combine(dest_flat, ys, wts, h2p, x1, mod, w_s_gate, w_s_up, w_s_down, ln2_g, ln2_b, s, min(256, s))
    return out.reshape(b, s, d)


def kernel(x, c, w_ada, b_ada, w_in, pe_k, pe_v, w_cmp_k1, w_cmp_k2, w_cmp_v1, w_cmp_v2, w_dw, b_dw,
           conv_ln_g, conv_ln_b, w_out, ln1_g, ln1_b, w_router, router_bias, w_e_gate, w_e_up, w_e_down,
           w_s_gate, w_s_up, w_s_down, ln2_g, ln2_b):
    assert w_ada.shape[0] == DEPTH
    layer = lambda a: a.reshape(a.shape[1:])
    mod = _ada(c, layer(w_ada), layer(b_ada))
    return _layer(x, mod, layer(w_in), jnp.concatenate([pe_k, pe_v]),
                  jnp.concatenate([w_cmp_k1, w_cmp_v1]), jnp.concatenate([w_cmp_k2, w_cmp_v2]),
                  *[layer(a) for a in (w_dw, b_dw, conv_ln_g, conv_ln_b, w_out, ln1_g, ln1_b, w_router,
                                       router_bias, w_e_gate, w_e_up, w_e_down, w_s_gate, w_s_up, w_s_down,
                                       ln2_g, ln2_b)])
```

```python
import functools

import jax
import jax.numpy as jnp
from jax import lax
from jax.experimental import pallas as pl
from jax.experimental.pallas import tpu as pltpu

N_HEADS = 8
N_KV = 2
HPG = N_HEADS // N_KV
HEAD_DIM = 64
D_ATTN = N_HEADS * HEAD_DIM
D_KV = N_KV * HEAD_DIM
CONV_WIDTH = 31
CMP_LEN = 32
CMP_STRIDE = 16
CMP_HID = 256
SEL_LEN = 64
SEL_TOPN = 16
WINDOW = 512
Q_BLOCK = 128
N_EXPERTS = 256
TOP_K = 8
N_GROUPS = 8
TOPK_GROUPS = 4
ROUTED_SCALE = 2.5
LN_EPS = 1e-5
DEPTH = 1
DEEPNORM_ALPHA = (2 * DEPTH) ** 0.25

LANES = 128
SUBLANES = 8
ROW_BLOCK = 256
NEG = -1e30
HIGHEST = lax.Precision.HIGHEST
VMEM_LIMIT = 48 * 1024 * 1024

f32 = jnp.float32
bf16 = jnp.bfloat16
i32 = jnp.int32
u32 = jnp.uint32


def _params(sem, vmem=VMEM_LIMIT):
    return pltpu.CompilerParams(dimension_semantics=sem, vmem_limit_bytes=vmem)


def _sigmoid(v):
    return 1.0 / (1.0 + jnp.exp(-v))


def _silu(v):
    return v * _sigmoid(v)


def _layer_norm(v, g, b):
    mu = jnp.mean(v, axis=-1, keepdims=True)
    var = jnp.mean(jnp.square(v - mu), axis=-1, keepdims=True)
    return (v - mu) * lax.rsqrt(var + LN_EPS) * g + b


def _dot_nt(a, b):
    return lax.dot_general(a, b, (((1,), (1,)), ((), ())), preferred_element_type=f32)


def _store_packed_rows(ref, v, first=0):
    n, d = v.shape
    half = d // 2
    bits = lax.bitcast_convert_type(v.astype(bf16).astype(f32), u32)
    words = bits[:, half:] | (bits[:, :half] >> 16)
    r = half // LANES
    for c in range(r):
        ref[pl.ds(first * r + c, n, stride=r), :] = words[:, c * LANES:(c + 1) * LANES]


def _unpack_words(w):
    return (lax.bitcast_convert_type(w << 16, f32),
            lax.bitcast_convert_type(w & jnp.uint32(0xFFFF0000), f32))


def _load_packed_rows(ref, n, d):
    r = d // 2 // LANES
    parts = [_unpack_words(ref[pl.ds(c, n, stride=r), :]) for c in range(r)]
    return jnp.concatenate([p[0] for p in parts] + [p[1] for p in parts], axis=1)


def _ada_kernel(c_ref, w_ref, b_ref, o_ref):
    c = c_ref[...]
    o_ref[...] = jnp.dot(_silu(c), w_ref[...], precision=HIGHEST,
                         preferred_element_type=f32) + b_ref[...]


def _ada(c, w_ada, b_ada):
    b, d = c.shape
    n = w_ada.shape[1]
    rows = 8
    c_pad = jnp.zeros((rows, d), f32).at[:b].set(c)
    tn = 1024
    out = pl.pallas_call(
        _ada_kernel,
        grid=(n // tn,),
        in_specs=[pl.BlockSpec((rows, d), lambda j: (0, 0)),
                  pl.BlockSpec((d, tn), lambda j: (0, j)),
                  pl.BlockSpec((1, tn), lambda j: (0, j))],
        out_specs=pl.BlockSpec((rows, tn), lambda j: (0, j)),
        out_shape=jax.ShapeDtypeStruct((rows, n), f32),
        compiler_params=_params(("arbitrary",)),
        name="ada",
    )(c_pad, w_ada, b_ada.reshape(1, n))
    return out[:b].reshape(b, 6, d)


GATE_ROWS = 32
KEY_TILE = 512
TILE_BLOCKS = KEY_TILE // SEL_LEN
CMP_PER_SEL = SEL_LEN // CMP_STRIDE
CMP_BACK = CMP_LEN // CMP_STRIDE - 1
assert SEL_LEN % CMP_STRIDE == 0 and CMP_LEN % CMP_STRIDE == 0 and CMP_BACK < CMP_PER_SEL
V_ROWS = HEAD_DIM + 16
Q_SCALE = HEAD_DIM ** -0.5 * 1.4426950408889634


def _in_proj_kernel(x_ref, mod_ref, wt_ref, wc_ref, wk_ref, wv_ref,
                    qt_ref, vt_ref, gt_ref, kvc_ref, kn_ref, cv_ref):
    m = mod_ref[0]
    h = (x_ref[0] * (1.0 + m[1:2]) + m[0:1]).astype(bf16)
    res_t = _dot_nt(wt_ref[...], h)
    qt_ref[0] = (res_t[0:D_ATTN] * Q_SCALE).astype(bf16)
    ones = jnp.ones((V_ROWS - HEAD_DIM, res_t.shape[1]), bf16)
    for j in range(2):
        for g in range(N_KV):
            off = D_ATTN + (j * N_KV + g) * HEAD_DIM
            vt_ref[0, j, g, 0:HEAD_DIM, :] = res_t[off:off + HEAD_DIM].astype(bf16)
            vt_ref[0, j, g, HEAD_DIM:V_ROWS, :] = ones
    gt_ref[0] = res_t[D_ATTN + 2 * D_KV:]
    kvc_ref[0] = jnp.dot(h, wc_ref[...], preferred_element_type=f32)
    kn = jnp.dot(h, wk_ref[...], preferred_element_type=f32).astype(bf16)
    tm = kn.shape[0]
    pos = pl.program_id(1) * tm + lax.broadcasted_iota(i32, (tm, LANES - HEAD_DIM), 0)
    lane = lax.broadcasted_iota(i32, (tm, LANES - HEAD_DIM), 1)
    onehot = jnp.where(lane == ((pos >> 6) & (TILE_BLOCKS - 1)), 1.0, 0.0).astype(bf16)
    for j in range(2):
        for g in range(N_KV):
            off = (j * N_KV + g) * HEAD_DIM
            kn_ref[0, j, g] = jnp.concatenate([kn[:, off:off + HEAD_DIM], onehot], axis=1)
    cv_ref[0] = jnp.dot(h, wv_ref[...], preferred_element_type=f32)


def _in_proj(x, mod, w_in, tm):
    b, s, d = x.shape
    o = 0
    wq = w_in[:, o:o + D_ATTN]; o += D_ATTN
    wkc = w_in[:, o:o + 2 * D_KV]; o += 2 * D_KV
    wk_s = w_in[:, o:o + D_KV]; o += D_KV
    wv_s = w_in[:, o:o + D_KV]; o += D_KV
    wk_w = w_in[:, o:o + D_KV]; o += D_KV
    wv_w = w_in[:, o:o + D_KV]; o += D_KV
    wg = w_in[:, o:o + 3 * N_HEADS]; o += 3 * N_HEADS
    wcv = w_in[:, o:]
    d_conv2 = wcv.shape[1]
    wg = jnp.zeros((d, GATE_ROWS), f32).at[:, :3 * N_HEADS].set(wg)
    wt = jnp.concatenate([wq, wv_s, wv_w, wg], axis=1).T
    ws = [w.astype(bf16) for w in (wt, wkc, jnp.concatenate([wk_s, wk_w], axis=1), wcv)]
    full = lambda a: pl.BlockSpec(a.shape, lambda bi, i: (0, 0))
    return pl.pallas_call(
        _in_proj_kernel,
        grid=(b, s // tm),
        in_specs=[pl.BlockSpec((1, tm, d), lambda bi, i: (bi, i, 0)),
                  pl.BlockSpec((1, 6, d), lambda bi, i: (bi, 0, 0))] + [full(w) for w in ws],
        out_specs=[pl.BlockSpec((1, D_ATTN, tm), lambda bi, i: (bi, 0, i)),
                   pl.BlockSpec((1, 2, N_KV, V_ROWS, tm), lambda bi, i: (bi, 0, 0, 0, i)),
                   pl.BlockSpec((1, GATE_ROWS, tm), lambda bi, i: (bi, 0, i)),
                   pl.BlockSpec((1, tm, 2 * D_KV), lambda bi, i: (bi, i, 0)),
                   pl.BlockSpec((1, 2, N_KV, tm, LANES), lambda bi, i: (bi, 0, 0, i, 0)),
                   pl.BlockSpec((1, tm, d_conv2), lambda bi, i: (bi, i, 0))],
        out_shape=[jax.ShapeDtypeStruct((b, D_ATTN, s), bf16),
                   jax.ShapeDtypeStruct((b, 2, N_KV, V_ROWS, s), bf16),
                   jax.ShapeDtypeStruct((b, GATE_ROWS, s), f32),
                   jax.ShapeDtypeStruct((b, s, 2 * D_KV), f32),
                   jax.ShapeDtypeStruct((b, 2, N_KV, s, LANES), bf16),
                   jax.ShapeDtypeStruct((b, s, d_conv2), f32)],
        compiler_params=_params(("parallel", "parallel")),
        name="in_proj",
    )(x, mod, *ws)


def _compress_kernel(c_ref, pe_ref, w1_ref, w2_ref, w2t_ref, o_ref, ot_ref):
    c = c_ref[0, 0, 0]
    n_chunk = c.shape[0]
    a = jnp.dot((c + pe_ref[0, 0]).astype(bf16), w1_ref[0, 0], preferred_element_type=f32)
    bm = jnp.dot((c + pe_ref[0, 1]).astype(bf16), w1_ref[0, 1], preferred_element_type=f32)
    hid = a + pltpu.roll(bm, n_chunk - 1, 0)
    act = 0.5 * hid * (1.0 + jnp.tanh(0.7978845608028654 * (hid + 0.044715 * (hid * hid * hid))))
    act = act.astype(bf16)
    o_ref[0, 0, 0] = jnp.dot(act, w2_ref[0], preferred_element_type=f32).astype(bf16)
    ot_ref[0, 0, 0] = _dot_nt(w2t_ref[0], act).astype(bf16)


def _compress(kvc, pe, w1, w2):
    b, s, _ = kvc.shape
    n_chunk = s // CMP_STRIDE
    half = CMP_STRIDE * HEAD_DIM
    c = kvc.reshape(b, n_chunk, CMP_STRIDE, 2, N_KV, HEAD_DIM).transpose(0, 3, 4, 1, 2, 5)
    c = c.reshape(b, 2, N_KV, n_chunk, half)
    pe2 = pe.reshape(2, 2, 1, half)
    w1h = w1.reshape(2, 2, half, CMP_HID).astype(bf16)
    w2b = w2.astype(bf16)
    w2t = w2b.transpose(0, 2, 1)
    return pl.pallas_call(
        _compress_kernel,
        grid=(b, 2, N_KV),
        in_specs=[pl.BlockSpec((1, 1, 1, n_chunk, half), lambda bi, j, g: (bi, j, g, 0, 0)),
                  pl.BlockSpec((1, 2, 1, half), lambda bi, j, g: (j, 0, 0, 0)),
                  pl.BlockSpec((1, 2, half, CMP_HID), lambda bi, j, g: (j, 0, 0, 0)),
                  pl.BlockSpec((1, CMP_HID, HEAD_DIM), lambda bi, j, g: (j, 0, 0)),
                  pl.BlockSpec((1, HEAD_DIM, CMP_HID), lambda bi, j, g: (j, 0, 0))],
        out_specs=[pl.BlockSpec((1, 1, 1, n_chunk, HEAD_DIM), lambda bi, j, g: (bi, j, g, 0, 0)),
                   pl.BlockSpec((1, 1, 1, HEAD_DIM, n_chunk), lambda bi, j, g: (bi, j, g, 0, 0))],
        out_shape=[jax.ShapeDtypeStruct((b, 2, N_KV, n_chunk, HEAD_DIM), bf16),
                   jax.ShapeDtypeStruct((b, 2, N_KV, HEAD_DIM, n_chunk), bf16)],
        compiler_params=_params(("parallel", "parallel", "parallel")),
        name="compress",
    )(c, pe2, w1h, w2b, w2t)


def _attn_kernel(q_ref, g_ref, kc_ref, vc_ref, kn_ref, vt_ref, o_ref, psum_ref, selbias_ref, sa_ref, sb_ref,
                 *, seq, tk, top_n):
    i = pl.program_id(2)
    s0 = i * Q_BLOCK
    n_cmp_rows = kc_ref.shape[3]
    n_sel = seq // SEL_LEN

    q4 = q_ref[0]
    qt = jnp.concatenate([q4[p * HEAD_DIM:(p + 1) * HEAD_DIM, :] for p in range(HPG)], axis=1)
    t_row = s0 + lax.broadcasted_iota(i32, (1, Q_BLOCK), 1)

    s_c = jnp.dot(kc_ref[0, 0, 0], qt, preferred_element_type=f32)
    cmp_end = lax.broadcasted_iota(i32, (n_cmp_rows, 1), 0) * CMP_STRIDE + (CMP_LEN - 1)
    bias_c = jnp.where(cmp_end <= t_row, 0.0, NEG)
    any_c = t_row >= CMP_LEN - 1
    p_sum = jnp.zeros((n_cmp_rows, Q_BLOCK), f32)
    pcs = []
    for p in range(HPG):
        sp = s_c[:, p * Q_BLOCK:(p + 1) * Q_BLOCK] + bias_c
        e = jnp.exp2(sp - jnp.max(sp, axis=0, keepdims=True))
        pn = e * jnp.where(any_c, 1.0 / jnp.sum(e, axis=0, keepdims=True), 0.0)
        p_sum = p_sum + pn
        pcs.append(pn.astype(bf16))
    o_c = jnp.dot(vc_ref[0, 0, 0], jnp.concatenate(pcs, axis=1), preferred_element_type=f32)

    cols = HPG * Q_BLOCK
    pad_rows = LANES - HEAD_DIM
    q_pad = jnp.concatenate([qt, jnp.zeros((pad_rows, cols), bf16)], axis=0)
    span = WINDOW + Q_BLOCK
    w0 = pl.multiple_of(jnp.maximum(s0 - WINDOW, 0), Q_BLOCK)
    s_w = jnp.dot(kn_ref[0, 1, 0, pl.ds(w0, span), :], q_pad, preferred_element_type=f32)
    wpos = w0 + lax.broadcasted_iota(i32, (span, 1), 0)
    bias_w = jnp.where((wpos <= t_row) & (wpos > t_row - WINDOW), 0.0, NEG)
    pws = []
    for p in range(HPG):
        sp = s_w[:, p * Q_BLOCK:(p + 1) * Q_BLOCK] + bias_w
        pws.append(jnp.exp2(sp - jnp.max(sp, axis=0, keepdims=True)).astype(bf16))
    acc_w = jnp.dot(vt_ref[0, 1, 0, :, pl.ds(w0, span)], jnp.concatenate(pws, axis=1),
                    preferred_element_type=f32)
    o_w = acc_w[0:HEAD_DIM] * (1.0 / acc_w[HEAD_DIM:HEAD_DIM + 1])

    psum_ref[...] = p_sum
    blk = lax.broadcasted_iota(i32, (n_sel, Q_BLOCK), 0)
    imp = psum_ref[pl.ds(0, n_sel, stride=CMP_PER_SEL), :]
    for r in range(1, CMP_PER_SEL):
        imp = imp + psum_ref[pl.ds(r, n_sel, stride=CMP_PER_SEL), :]
    for back in range(1, CMP_BACK + 1):
        prev = pltpu.roll(psum_ref[pl.ds(CMP_PER_SEL - back, n_sel, stride=CMP_PER_SEL), :], 1, 0)
        imp = imp + jnp.where(blk >= 1, prev, 0.0)

    cur = t_row >> 6
    forced = (blk == 0) | (blk == cur) | (blk == cur - 1)
    vals = jnp.where(forced, jnp.inf, jnp.where(blk <= cur, imp, -jnp.inf))
    sel = jnp.zeros((n_sel, Q_BLOCK), f32)
    for _ in range(top_n):
        mx = jnp.max(vals, axis=0, keepdims=True)
        first = jnp.min(jnp.where(vals == mx, blk, n_sel), axis=0, keepdims=True)
        pick = blk == first
        sel = jnp.where(pick & (mx > -jnp.inf), 1.0, sel)
        vals = jnp.where(pick, -jnp.inf, vals)
    selbias_ref[0:n_sel] = jnp.where((sel > 0.5) & (blk < (s0 >> 6)), 0.0, NEG)
    selbias_ref[n_sel:n_sel + TILE_BLOCKS] = jnp.full((TILE_BLOCKS, Q_BLOCK), NEG, f32)
    last_tile = seq // tk - 1

    def scores(kt, s_ref):
        k0 = pl.multiple_of(jnp.minimum(kt, last_tile) * tk, tk)
        sb = selbias_ref[pl.ds(pl.multiple_of(kt * TILE_BLOCKS, TILE_BLOCKS), TILE_BLOCKS), :]
        rows = jnp.concatenate([jnp.concatenate([sb] * HPG, axis=1),
                                jnp.zeros((pad_rows - TILE_BLOCKS, cols), f32)], axis=0).astype(bf16)
        s_ref[...] = jnp.dot(kn_ref[0, 0, 0, pl.ds(k0, tk), :], jnp.concatenate([qt, rows], axis=0),
                             preferred_element_type=f32)

    def fold(scores_of_head, v_t, carry):
        m_i, acc = carry
        es, ms, alphas = [], [], []
        for p in range(HPG):
            c = slice(p * Q_BLOCK, (p + 1) * Q_BLOCK)
            sp = scores_of_head(c)
            m_new = jnp.maximum(m_i[:, c], jnp.max(sp, axis=0, keepdims=True))
            es.append(jnp.exp2(sp - m_new).astype(bf16))
            alphas.append(jnp.exp2(m_i[:, c] - m_new))
            ms.append(m_new)
        pv = jnp.dot(v_t, jnp.concatenate(es, axis=1), preferred_element_type=f32)
        return jnp.concatenate(ms, axis=1), jnp.concatenate(alphas, axis=1) * acc + pv

    def sel_tile(kt, s_ref, carry):
        k0 = pl.multiple_of(jnp.minimum(kt, last_tile) * tk, tk)
        return fold(lambda c: s_ref[:, c], vt_ref[0, 0, 0, :, pl.ds(k0, tk)], carry)

    def sel_pair(i, carry):
        scores(2 * i + 1, sb_ref)
        carry = sel_tile(2 * i, sa_ref, carry)
        scores(2 * i + 2, sa_ref)
        return sel_tile(2 * i + 1, sb_ref, carry)

    n_sweep = (s0 + tk - 1) // tk
    init = (jnp.full((1, cols), NEG, f32), jnp.zeros((V_ROWS, cols), f32))
    scores(0, sa_ref)
    carry = lax.fori_loop(0, (n_sweep + 1) // 2, sel_pair, init)
    d0 = pl.multiple_of(s0, Q_BLOCK)
    s_d = jnp.dot(kn_ref[0, 0, 0, pl.ds(d0, Q_BLOCK), :], q_pad, preferred_element_type=f32)
    bias_d = jnp.where(s0 + lax.broadcasted_iota(i32, (Q_BLOCK, 1), 0) <= t_row, 0.0, NEG)
    _, acc_s = fold(lambda c: s_d[:, c] + bias_d, vt_ref[0, 0, 0, :, pl.ds(d0, Q_BLOCK)], carry)
    o_s = acc_s[0:HEAD_DIM] * (1.0 / acc_s[HEAD_DIM:HEAD_DIM + 1])

    gate = _sigmoid(g_ref[0, 0])
    outs = []
    for p in range(HPG):
        c = slice(p * Q_BLOCK, (p + 1) * Q_BLOCK)
        outs.append(gate[3 * p:3 * p + 1] * o_c[:, c] + gate[3 * p + 1:3 * p + 2] * o_s[:, c]
                    + gate[3 * p + 2:3 * p + 3] * o_w[:, c])
    o_ref[0] = jnp.concatenate(outs, axis=0).T.astype(bf16)


GATE_GROUP_ROWS = 16


def _attention(qt, gates, cmp_n, cmp_t, kn, vt):
    b, _, s = qt.shape
    n_chunk = cmp_n.shape[3]
    n_sel = s // SEL_LEN
    top_n = min(SEL_TOPN, n_sel)
    tk = KEY_TILE
    assert s % tk == 0 and n_chunk == n_sel * CMP_PER_SEL
    gw = HPG * HEAD_DIM
    return pl.pallas_call(
        functools.partial(_attn_kernel, seq=s, tk=tk, top_n=top_n),
        grid=(b, N_KV, s // Q_BLOCK),
        in_specs=[pl.BlockSpec((1, gw, Q_BLOCK), lambda bi, g, i: (bi, g, i)),
                  pl.BlockSpec((1, 1, GATE_GROUP_ROWS, Q_BLOCK), lambda bi, g, i: (bi, g, 0, i)),
                  pl.BlockSpec((1, 1, 1, n_chunk, HEAD_DIM), lambda bi, g, i: (bi, 0, g, 0, 0)),
                  pl.BlockSpec((1, 1, 1, HEAD_DIM, n_chunk), lambda bi, g, i: (bi, 1, g, 0, 0)),
                  pl.BlockSpec((1, 2, 1, s, LANES), lambda bi, g, i: (bi, 0, g, 0, 0)),
                  pl.BlockSpec((1, 2, 1, V_ROWS, s), lambda bi, g, i: (bi, 0, g, 0, 0))],
        out_specs=pl.BlockSpec((1, Q_BLOCK, gw), lambda bi, g, i: (bi, i, g)),
        out_shape=jax.ShapeDtypeStruct((b, s, D_ATTN), bf16),
        scratch_shapes=[pltpu.VMEM((n_chunk, Q_BLOCK), f32), pltpu.VMEM((n_sel + TILE_BLOCKS, Q_BLOCK), f32),
                        pltpu.VMEM((tk, HPG * Q_BLOCK), f32), pltpu.VMEM((tk, HPG * Q_BLOCK), f32)],
        compiler_params=_params(("parallel", "parallel", "arbitrary")),
        name="attention",
    )(qt, gates, cmp_n, cmp_t, kn, vt)


CONV_HALO = 32
CONV_ROWS = 32


def _conv_kernel(cur_ref, prev_ref, w_ref, b_ref, g_ref, bb_ref, o_ref, glu_ref):
    i = pl.program_id(1)
    ts = cur_ref.shape[1]
    dc = o_ref.shape[2]
    cur = cur_ref[0]
    prev = prev_ref[0]
    glu_prev = prev[:, :dc] * _sigmoid(prev[:, dc:])
    glu_ref[0:CONV_HALO] = jnp.where(i == 0, 0.0, glu_prev)
    glu_ref[CONV_HALO:CONV_HALO + ts] = cur[:, :dc] * _sigmoid(cur[:, dc:])
    lead = CONV_HALO - (CONV_WIDTH - 1)

    def chunk(r, _):
        r0 = pl.multiple_of(r * CONV_ROWS, CONV_ROWS)
        win = glu_ref[pl.ds(r0, CONV_ROWS + CONV_HALO), :]
        acc = jnp.zeros((CONV_ROWS, dc), f32)
        for j in range(CONV_WIDTH):
            acc = acc + w_ref[j:j + 1, :] * win[lead + j:lead + j + CONV_ROWS]
        y = _layer_norm(acc + b_ref[...], g_ref[...], bb_ref[...])
        o_ref[0, pl.ds(r0, CONV_ROWS), :] = _silu(y).astype(bf16)
        return 0

    lax.fori_loop(0, ts // CONV_ROWS, chunk, 0)


def _conv(conv_in, w_dw, b_dw, ln_g, ln_b, ts):
    b, s, dc2 = conv_in.shape
    dc = dc2 // 2
    per = ts // CONV_HALO
    row = lambda a: a.reshape(1, dc)
    return pl.pallas_call(
        _conv_kernel,
        grid=(b, s // ts),
        in_specs=[pl.BlockSpec((1, ts, dc2), lambda bi, i: (bi, i, 0)),
                  pl.BlockSpec((1, CONV_HALO, dc2), lambda bi, i: (bi, jnp.maximum(i * per - 1, 0), 0)),
                  pl.BlockSpec((CONV_WIDTH, dc), lambda bi, i: (0, 0)),
                  pl.BlockSpec((1, dc), lambda bi, i: (0, 0)),
                  pl.BlockSpec((1, dc), lambda bi, i: (0, 0)),
                  pl.BlockSpec((1, dc), lambda bi, i: (0, 0))],
        out_specs=pl.BlockSpec((1, ts, dc), lambda bi, i: (bi, i, 0)),
        out_shape=jax.ShapeDtypeStruct((b, s, dc), bf16),
        scratch_shapes=[pltpu.VMEM((CONV_HALO + ts, dc), f32)],
        compiler_params=_params(("parallel", "parallel")),
        name="conv",
    )(conv_in, conv_in, w_dw.reshape(CONV_WIDTH, dc), row(b_dw), row(ln_g), row(ln_b))


ROUTE_HALF = 128


def _mix_route_kernel(a_ref, cv_ref, x_ref, mod_ref, wo_ref, lg_ref, lb_ref, wrh_ref, wrl_ref, rb_ref,
                      x1_ref, h2p_ref, idx_ref, wt_ref, cnt_ref):
    step = pl.program_id(0)

    @pl.when(step == 0)
    def _():
        cnt_ref[...] = jnp.zeros_like(cnt_ref)

    m = mod_ref[0]
    da = a_ref.shape[1]
    counts = jnp.zeros(cnt_ref.shape, f32)
    for first in range(0, x_ref.shape[0], ROUTE_HALF):
        rows = slice(first, first + ROUTE_HALF)
        mix = (jnp.dot(a_ref[rows, :], wo_ref[0:da, :], preferred_element_type=f32)
               + jnp.dot(cv_ref[rows, :], wo_ref[da:, :], preferred_element_type=f32))
        x1 = _layer_norm(DEEPNORM_ALPHA * x_ref[rows, :] + m[2:3] * mix, lg_ref[...], lb_ref[...])
        x1_ref[rows, :] = x1
        h2 = x1 * (1.0 + m[4:5]) + m[3:4]
        _store_packed_rows(h2p_ref, h2, first)
        h_hi = h2.astype(bf16)
        h_lo = (h2 - h_hi.astype(f32)).astype(bf16)
        logits = _dot_nt(wrh_ref[...], h_hi) + (_dot_nt(wrh_ref[...], h_lo) + _dot_nt(wrl_ref[...], h_hi))
        idx, wt, cnt = _route(_sigmoid(logits), rb_ref[...])
        idx_ref[:, rows] = idx
        wt_ref[:, rows] = wt
        counts = counts + cnt
    cnt_ref[...] += counts


def _route(score, bias):
    tm = score.shape[1]
    sel = score + bias
    eid = lax.broadcasted_iota(i32, (N_EXPERTS, tm), 0)
    per_group = N_EXPERTS // N_GROUPS
    gs = []
    for g in range(N_GROUPS):
        rows = slice(g * per_group, (g + 1) * per_group)
        v = sel[rows]
        e = g * per_group + lax.broadcasted_iota(i32, (per_group, tm), 0)
        m1 = jnp.max(v, axis=0, keepdims=True)
        i1 = jnp.min(jnp.where(v == m1, e, N_EXPERTS), axis=0, keepdims=True)
        m2 = jnp.max(jnp.where(e == i1, -jnp.inf, v), axis=0, keepdims=True)
        gs.append(m1 + m2)
    cands = []
    for g in range(N_GROUPS):
        rank = jnp.zeros((1, tm), i32)
        for o in range(N_GROUPS):
            if o == g:
                continue
            beats = (gs[o] > gs[g]) | (gs[o] == gs[g]) if o < g else gs[o] > gs[g]
            rank = rank + beats.astype(i32)
        drop = jnp.where(rank < TOPK_GROUPS, 0.0, -jnp.inf)
        cands.append(sel[g * per_group:(g + 1) * per_group] + drop)
    cand = jnp.concatenate(cands, axis=0)
    row_o = lax.broadcasted_iota(i32, (TOP_K, tm), 0)
    idx_out = jnp.zeros((TOP_K, tm), i32)
    wt_out = jnp.zeros((TOP_K, tm), f32)
    picked = jnp.zeros((N_EXPERTS, tm), f32)
    w_sum = jnp.zeros((1, tm), f32)
    for k in range(TOP_K):
        mx = jnp.max(cand, axis=0, keepdims=True)
        ik = jnp.min(jnp.where(cand == mx, eid, N_EXPERTS), axis=0, keepdims=True)
        pick = eid == ik
        wk = jnp.sum(jnp.where(pick, score, 0.0), axis=0, keepdims=True)
        cand = jnp.where(pick, -jnp.inf, cand)
        picked = jnp.where(pick, 1.0, picked)
        idx_out = jnp.where(row_o == k, ik, idx_out)
        wt_out = jnp.where(row_o == k, wk, wt_out)
        w_sum = w_sum + wk
    return idx_out, wt_out / w_sum * ROUTED_SCALE, jnp.sum(picked, axis=1, keepdims=True)


def _mix_route(attn, conv, x2, mod, w_out, ln_g, ln_b, w_router, router_bias, s, tm):
    t, d = x2.shape
    per = s // tm
    da = attn.shape[1]
    pack_rows = d // 2 // LANES
    row = lambda a: a.reshape(1, -1)
    tile = lambda w: pl.BlockSpec((tm, w), lambda i: (i, 0))
    full = lambda a: pl.BlockSpec(a.shape, lambda i: (0,) * a.ndim)
    assert tm % ROUTE_HALF == 0
    wr_hi = w_router.T.astype(bf16)
    wr_lo = (w_router.T - wr_hi.astype(f32)).astype(bf16)
    args = (attn, conv, x2, mod, w_out.astype(bf16), row(ln_g), row(ln_b), wr_hi, wr_lo,
            router_bias.reshape(N_EXPERTS, 1))
    per_token = lambda rows: pl.BlockSpec((rows, tm), lambda i: (0, i))
    return pl.pallas_call(
        _mix_route_kernel,
        grid=(t // tm,),
        in_specs=[tile(da), tile(conv.shape[1]), tile(d),
                  pl.BlockSpec((1, 6, d), lambda i: (i // per, 0, 0))] + [full(a) for a in args[4:]],
        out_specs=[tile(d), pl.BlockSpec((tm * pack_rows, LANES), lambda i: (i, 0)),
                   per_token(TOP_K), per_token(TOP_K), pl.BlockSpec((N_EXPERTS, 1), lambda i: (0, 0))],
        out_shape=[jax.ShapeDtypeStruct((t, d), f32),
                   jax.ShapeDtypeStruct((t * pack_rows, LANES), u32),
                   jax.ShapeDtypeStruct((TOP_K, t), i32), jax.ShapeDtypeStruct((TOP_K, t), f32),
                   jax.ShapeDtypeStruct((N_EXPERTS, 1), f32)],
        compiler_params=_params(("arbitrary",)),
        name="mix_route",
    )(*args)


def _positions_kernel(idx_ref, start_ref, o_ref, run_ref):
    step = pl.program_id(0)
    tm = idx_ref.shape[1]

    @pl.when(step == 0)
    def _():
        run_ref[...] = jnp.zeros_like(run_ref)

    idx = idx_ref[...]
    eid = lax.broadcasted_iota(i32, (N_EXPERTS, tm), 0)
    onehot = jnp.zeros((N_EXPERTS, tm), f32)
    for k in range(TOP_K):
        onehot = jnp.where(eid == idx[k:k + 1], 1.0, onehot)
    r = lax.broadcasted_iota(i32, (tm, tm), 0)
    c = lax.broadcasted_iota(i32, (tm, tm), 1)
    earlier = jnp.where(r < c, 1.0, 0.0).astype(bf16)
    prior = jnp.dot(onehot.astype(bf16), earlier, preferred_element_type=f32)
    pos = prior + run_ref[...] + start_ref[...]
    row_o = lax.broadcasted_iota(i32, (TOP_K, tm), 0)
    out = jnp.zeros((TOP_K, tm), i32)
    for k in range(TOP_K):
        dk = jnp.sum(jnp.where(eid == idx[k:k + 1], pos, 0.0), axis=0, keepdims=True)
        out = jnp.where(row_o == k, dk.astype(i32), out)
    o_ref[...] = out
    run_ref[...] += jnp.sum(onehot, axis=1, keepdims=True)


def _positions(idx, seg_start, tm):
    t = idx.shape[1]
    return pl.pallas_call(
        _positions_kernel,
        grid=(t // tm,),
        in_specs=[pl.BlockSpec((TOP_K, tm), lambda i: (0, i)),
                  pl.BlockSpec((N_EXPERTS, 1), lambda i: (0, 0))],
        out_specs=pl.BlockSpec((TOP_K, tm), lambda i: (0, i)),
        out_shape=jax.ShapeDtypeStruct((TOP_K, t), i32),
        scratch_shapes=[pltpu.VMEM((N_EXPERTS, 1), f32)],
        compiler_params=_params(("arbitrary",)),
        name="positions",
    )(idx, seg_start)


def _dispatch_kernel(dest_hbm, h_ref, o_hbm, dest_smem, sem_i, sem, *, r):
    step = pl.program_id(0)
    n = dest_smem.shape[0]
    load = pltpu.make_async_copy(dest_hbm.at[pl.ds(pl.multiple_of(step * n, n), n)], dest_smem, sem_i)
    load.start()
    load.wait()

    def issue(tok, _):
        src = pl.multiple_of(tok * r, r)
        for k in range(TOP_K):
            dst = pl.multiple_of(dest_smem[tok * TOP_K + k] * r, r)
            pltpu.make_async_copy(h_ref.at[pl.ds(src, r)], o_hbm.at[pl.ds(dst, r)], sem).start(priority=k % 2)
        return 0

    lax.fori_loop(0, n // TOP_K, issue, 0)
    everything = o_hbm.at[pl.ds(0, n * r)]
    pltpu.make_async_copy(everything, everything, sem).wait()


def _dispatch(dest_flat, h2p, n_buf, r, tm):
    t = h2p.shape[0] // r
    return pl.pallas_call(
        functools.partial(_dispatch_kernel, r=r),
        grid=(t // tm,),
        in_specs=[pl.BlockSpec(memory_space=pl.ANY),
                  pl.BlockSpec((tm * r, LANES), lambda i: (i, 0))],
        out_specs=pl.BlockSpec(memory_space=pl.ANY),
        out_shape=jax.ShapeDtypeStruct((n_buf * r, LANES), u32),
        scratch_shapes=[pltpu.SMEM((tm * TOP_K,), i32), pltpu.SemaphoreType.DMA, pltpu.SemaphoreType.DMA],
        compiler_params=_params(("arbitrary",)),
        name="dispatch",
    )(dest_flat, h2p)


def _experts_kernel(be_ref, bv_ref, nu_ref, x_ref, wg_ref, wu_ref, wd_ref, o_ref, wg_s, wu_s, wd_s):
    j = pl.program_id(0)
    prev = be_ref[jnp.maximum(j - 1, 0)]
    used = j < nu_ref[0]
    d = wg_s.shape[0]

    @pl.when(used & ((j == 0) | (be_ref[j] != prev)))
    def _():
        wg_s[...] = wg_ref[0].astype(bf16)
        wu_s[...] = wu_ref[0].astype(bf16)
        wd_s[...] = wd_ref[0].astype(bf16)

    @pl.when(used)
    def _():
        live = lax.broadcasted_iota(i32, (ROW_BLOCK, 1), 0) < bv_ref[j]
        x = jnp.where(live, _load_packed_rows(x_ref, ROW_BLOCK, d), 0.0).astype(bf16)
        hg = jnp.dot(x, wg_s[...], preferred_element_type=f32)
        hu = jnp.dot(x, wu_s[...], preferred_element_type=f32)
        hid = (_silu(hg) * hu).astype(bf16)
        _store_packed_rows(o_ref, jnp.dot(hid, wd_s[...], preferred_element_type=f32))


def _experts(blk_e, blk_valid, n_used, xs, w_gate, w_up, w_down):
    d, f = w_gate.shape[1], w_gate.shape[2]
    r = d // 2 // LANES
    n_blk = xs.shape[0] // r // ROW_BLOCK
    rows = lambda j, be, bv, nu: (jnp.minimum(j, nu[0] - 1), 0)
    wsel = lambda j, be, bv, nu: (be[j], 0, 0)
    return pl.pallas_call(
        _experts_kernel,
        grid_spec=pltpu.PrefetchScalarGridSpec(
            num_scalar_prefetch=3,
            grid=(n_blk,),
            in_specs=[pl.BlockSpec((ROW_BLOCK * r, LANES), rows),
                      pl.BlockSpec((1, d, f), wsel),
                      pl.BlockSpec((1, d, f), wsel),
                      pl.BlockSpec((1, f, d), wsel)],
            out_specs=pl.BlockSpec((ROW_BLOCK * r, LANES), rows),
            scratch_shapes=[pltpu.VMEM((d, f), bf16), pltpu.VMEM((d, f), bf16), pltpu.VMEM((f, d), bf16)]),
        out_shape=jax.ShapeDtypeStruct(xs.shape, u32),
        compiler_params=_params(("arbitrary",)),
        name="experts",
    )(blk_e, blk_valid, n_used, xs, w_gate, w_up, w_down)


def _combine_kernel(dest_hbm, ys_hbm, wt_ref, h_ref, x1_ref, mod_ref, wsg_ref, wsu_ref, wsd_ref,
                    lg_ref, lb_ref, o_ref, dest_smem, rows_ref, sem_i, sem):
    step = pl.program_id(0)
    n = dest_smem.shape[0]
    load = pltpu.make_async_copy(dest_hbm.at[pl.ds(pl.multiple_of(step * n, n), n)], dest_smem, sem_i)
    load.start()
    load.wait()

    tm, d = x1_ref.shape
    r = d // 2 // LANES

    def issue(tok, _):
        dst = pl.multiple_of(tok * r, r)
        for k in range(TOP_K):
            src = pl.multiple_of(dest_smem[tok * TOP_K + k] * r, r)
            pltpu.make_async_copy(ys_hbm.at[pl.ds(src, r)], rows_ref.at[k, pl.ds(dst, r)],
                                  sem).start(priority=k % 2)
        return 0

    lax.fori_loop(0, tm, issue, 0)

    h = _load_packed_rows(h_ref, tm, d).astype(bf16)
    hg = jnp.dot(h, wsg_ref[...], preferred_element_type=f32)
    hu = jnp.dot(h, wsu_ref[...], preferred_element_type=f32)
    y = jnp.dot((_silu(hg) * hu).astype(bf16), wsd_ref[...], preferred_element_type=f32)

    pltpu.make_async_copy(rows_ref, rows_ref, sem).wait()
    wt = wt_ref[...]
    lo = [jnp.zeros((tm, LANES), f32) for _ in range(r)]
    hi = [jnp.zeros((tm, LANES), f32) for _ in range(r)]
    for k in range(TOP_K):
        wk = wt[:, k:k + 1]
        for c in range(r):
            pl_, ph_ = _unpack_words(rows_ref[k, pl.ds(c, tm, stride=r), :])
            lo[c] = lo[c] + wk * pl_
            hi[c] = hi[c] + wk * ph_
    y = y + jnp.concatenate(lo + hi, axis=1)
    m = mod_ref[0]
    o_ref[...] = _layer_norm(DEEPNORM_ALPHA * x1_ref[...] + m[5:6] * y, lg_ref[...], lb_ref[...])


def _combine(dest_flat, ys, wts, h2p, x1, mod, w_s_gate, w_s_up, w_s_down, ln_g, ln_b, s, tm):
    t, d = x1.shape
    per = s // tm
    r = d // 2 // LANES
    row = lambda a: a.reshape(1, -1)
    tile = lambda w: pl.BlockSpec((tm, w), lambda i: (i, 0))
    full = lambda a: pl.BlockSpec(a.shape, lambda i: (0,) * a.ndim)
    tail = (w_s_gate.astype(bf16), w_s_up.astype(bf16), w_s_down.astype(bf16), row(ln_g), row(ln_b))
    return pl.pallas_call(
        _combine_kernel,
        grid=(t // tm,),
        in_specs=[pl.BlockSpec(memory_space=pl.ANY), pl.BlockSpec(memory_space=pl.ANY),
                  tile(LANES), pl.BlockSpec((tm * r, LANES), lambda i: (i, 0)), tile(d),
                  pl.BlockSpec((1, 6, d), lambda i: (i // per, 0, 0))] + [full(a) for a in tail],
        out_specs=tile(d),
        out_shape=jax.ShapeDtypeStruct((t, d), f32),
        scratch_shapes=[pltpu.SMEM((tm * TOP_K,), i32), pltpu.VMEM((TOP_K, tm * r, LANES), u32),
                        pltpu.SemaphoreType.DMA, pltpu.SemaphoreType.DMA],
        compiler_params=_params(("arbitrary",)),
        name="combine",
    )(dest_flat, ys, wts, h2p, x1, mod, *tail)


def _layer(x, mod, w_in, pe, w_cmp1, w_cmp2, w_dw, b_dw, conv_ln_g, conv_ln_b, w_out, ln1_g, ln1_b,
           w_router, router_bias, w_e_gate, w_e_up, w_e_down, w_s_gate, w_s_up, w_s_down, ln2_g, ln2_b):
    b, s, d = x.shape
    t = b * s
    tm = min(512, s)
    qt, vt, gt, kvc, kn, conv_in = _in_proj(x, mod, w_in, tm)
    cmp_n, cmp_t = _compress(kvc, pe, w_cmp1, w_cmp2)
    gates = gt[:, :3 * N_HEADS].reshape(b, N_KV, 3 * HPG, s)
    gates = jnp.pad(gates, ((0, 0), (0, 0), (0, GATE_GROUP_ROWS - 3 * HPG), (0, 0)))
    attn = _attention(qt, gates, cmp_n, cmp_t, kn, vt)
    conv = _conv(conv_in, w_dw, b_dw, conv_ln_g, conv_ln_b, tm)

    tr = min(256, s)
    x1, h2p, idx, wts, counts = _mix_route(attn.reshape(t, -1), conv.reshape(t, -1), x.reshape(t, d), mod,
                                           w_out, ln1_g, ln1_b, w_router, router_bias, s, tr)
    counts = counts[:, 0].astype(i32)
    padded = (counts + ROW_BLOCK - 1) // ROW_BLOCK * ROW_BLOCK
    seg_end = jnp.cumsum(padded)
    seg_start = seg_end - padded
    n_blk = -(-(t * TOP_K + N_EXPERTS * (ROW_BLOCK - 1)) // ROW_BLOCK)
    blk_row0 = jnp.arange(n_blk, dtype=i32) * ROW_BLOCK
    blk_e = jnp.minimum(jnp.searchsorted(seg_end, blk_row0, side='right'), N_EXPERTS - 1).astype(i32)
    blk_valid = jnp.clip(seg_start[blk_e] + counts[blk_e] - blk_row0, 0, ROW_BLOCK).astype(i32)
    n_used = (seg_end[-1:] // ROW_BLOCK).astype(i32)

    dest = _positions(idx, seg_start.astype(f32).reshape(N_EXPERTS, 1), tr)
    dest_flat = dest.T.reshape(-1)
    wts = jnp.zeros((t, LANES), f32).at[:, :TOP_K].set(wts.T)
    xs = _dispatch(dest_flat, h2p, n_blk * ROW_BLOCK, d // 2 // LANES, min(512, s))
    ys = _experts(blk_e, blk_valid, n_used, xs, w_e_gate, w_e_up, w_e_down)
    out = _combine(dest_flat, ys, wts, h2p, x1, mod, w_s_gate, w_s_up, w_s_down, ln2_g, ln2_b, s, min(256, s))
    return out.reshape(b, s, d)


def kernel(x, c, w_ada, b_ada, w_in, pe_k, pe_v, w_cmp_k1, w_cmp_k2, w_cmp_v1, w_cmp_v2, w_dw, b_dw,
           conv_ln_g, conv_ln_b, w_out, ln1_g, ln1_b, w_router, router_bias, w_e_gate, w_e_up, w_e_down,
           w_s_gate, w_s_up, w_s_down, ln2_g, ln2_b):
    assert w_ada.shape[0] == DEPTH
    layer = lambda a: a.reshape(a.shape[1:])
    mod = _ada(c, layer(w_ada), layer(b_ada))
    return _layer(x, mod, layer(w_in), jnp.concatenate([pe_k, pe_v]),
                  jnp.concatenate([w_cmp_k1, w_cmp_v1]), jnp.concatenate([w_cmp_k2, w_cmp_v2]),
                  *[layer(a) for a in (w_dw, b_dw, conv_ln_g, conv_ln_b, w_out, ln1_g, ln1_b, w_router,
                                       router_bias, w_e_gate, w_e_up, w_e_down, w_s_gate, w_s_up, w_s_down,
                                       ln2_g, ln2_b)])
```

```python
import functools

import jax
import jax.numpy as jnp
from jax import lax
from jax.experimental import pallas as pl
from jax.experimental.pallas import tpu as pltpu

N_HEADS = 8
N_KV = 2
HPG = N_HEADS // N_KV
HEAD_DIM = 64
D_ATTN = N_HEADS * HEAD_DIM
D_KV = N_KV * HEAD_DIM
CONV_WIDTH = 31
CMP_LEN = 32
CMP_STRIDE = 16
CMP_HID = 256
SEL_LEN = 64
SEL_TOPN = 16
WINDOW = 512
Q_BLOCK = 128
N_EXPERTS = 256
TOP_K = 8
N_GROUPS = 8
TOPK_GROUPS = 4
ROUTED_SCALE = 2.5
LN_EPS = 1e-5
DEPTH = 1
DEEPNORM_ALPHA = (2 * DEPTH) ** 0.25

LANES = 128
SUBLANES = 8
ROW_BLOCK = 256
NEG = -1e30
HIGHEST = lax.Precision.HIGHEST
VMEM_LIMIT = 48 * 1024 * 1024

f32 = jnp.float32
bf16 = jnp.bfloat16
i32 = jnp.int32
u32 = jnp.uint32


def _params(sem, vmem=VMEM_LIMIT):
    return pltpu.CompilerParams(dimension_semantics=sem, vmem_limit_bytes=vmem)


def _sigmoid(v):
    return 1.0 / (1.0 + jnp.exp(-v))


def _silu(v):
    return v * _sigmoid(v)


def _layer_norm(v, g, b):
    mu = jnp.mean(v, axis=-1, keepdims=True)
    var = jnp.mean(jnp.square(v - mu), axis=-1, keepdims=True)
    return (v - mu) * lax.rsqrt(var + LN_EPS) * g + b


def _dot_nt(a, b):
    return lax.dot_general(a, b, (((1,), (1,)), ((), ())), preferred_element_type=f32)


def _store_packed_rows(ref, v, first=0):
    n, d = v.shape
    half = d // 2
    bits = lax.bitcast_convert_type(v.astype(bf16).astype(f32), u32)
    words = bits[:, half:] | (bits[:, :half] >> 16)
    r = half // LANES
    for c in range(r):
        ref[pl.ds(first * r + c, n, stride=r), :] = words[:, c * LANES:(c + 1) * LANES]


def _unpack_words(w):
    return (lax.bitcast_convert_type(w << 16, f32),
            lax.bitcast_convert_type(w & jnp.uint32(0xFFFF0000), f32))


def _load_packed_rows(ref, n, d):
    r = d // 2 // LANES
    parts = [_unpack_words(ref[pl.ds(c, n, stride=r), :]) for c in range(r)]
    return jnp.concatenate([p[0] for p in parts] + [p[1] for p in parts], axis=1)


def _ada_kernel(c_ref, w_ref, b_ref, o_ref):
    c = c_ref[...]
    o_ref[...] = jnp.dot(_silu(c), w_ref[...], precision=HIGHEST,
                         preferred_element_type=f32) + b_ref[...]


def _ada(c, w_ada, b_ada):
    b, d = c.shape
    n = w_ada.shape[1]
    rows = 8
    c_pad = jnp.zeros((rows, d), f32).at[:b].set(c)
    tn = 1024
    out = pl.pallas_call(
        _ada_kernel,
        grid=(n // tn,),
        in_specs=[pl.BlockSpec((rows, d), lambda j: (0, 0)),
                  pl.BlockSpec((d, tn), lambda j: (0, j)),
                  pl.BlockSpec((1, tn), lambda j: (0, j))],
        out_specs=pl.BlockSpec((rows, tn), lambda j: (0, j)),
        out_shape=jax.ShapeDtypeStruct((rows, n), f32),
        compiler_params=_params(("arbitrary",)),
        name="ada",
    )(c_pad, w_ada, b_ada.reshape(1, n))
    return out[:b].reshape(b, 6, d)


GATE_ROWS = 32
KEY_TILE = 512
TILE_BLOCKS = KEY_TILE // SEL_LEN
CMP_PER_SEL = SEL_LEN // CMP_STRIDE
CMP_BACK = CMP_LEN // CMP_STRIDE - 1
assert SEL_LEN % CMP_STRIDE == 0 and CMP_LEN % CMP_STRIDE == 0 and CMP_BACK < CMP_PER_SEL
V_ROWS = HEAD_DIM + 16
Q_SCALE = HEAD_DIM ** -0.5 * 1.4426950408889634


def _in_proj_kernel(x_ref, mod_ref, wt_ref, wc_ref, wk_ref, wv_ref,
                    qt_ref, vt_ref, gt_ref, kvc_ref, kn_ref, cv_ref):
    m = mod_ref[0]
    h = (x_ref[0] * (1.0 + m[1:2]) + m[0:1]).astype(bf16)
    res_t = _dot_nt(wt_ref[...], h)
    qt_ref[0] = (res_t[0:D_ATTN] * Q_SCALE).astype(bf16)
    ones = jnp.ones((V_ROWS - HEAD_DIM, res_t.shape[1]), bf16)
    for j in range(2):
        for g in range(N_KV):
            off = D_ATTN + (j * N_KV + g) * HEAD_DIM
            vt_ref[0, j, g, 0:HEAD_DIM, :] = res_t[off:off + HEAD_DIM].astype(bf16)
            vt_ref[0, j, g, HEAD_DIM:V_ROWS, :] = ones
    gt_ref[0] = res_t[D_ATTN + 2 * D_KV:]
    kvc = jnp.dot(h, wc_ref[...], preferred_element_type=f32)
    kvc_ref[0, 0] = kvc[:, :D_KV]
    kvc_ref[0, 1] = kvc[:, D_KV:]
    kn = jnp.dot(h, wk_ref[...], preferred_element_type=f32).astype(bf16)
    tm = kn.shape[0]
    pos = pl.program_id(1) * tm + lax.broadcasted_iota(i32, (tm, LANES - HEAD_DIM), 0)
    lane = lax.broadcasted_iota(i32, (tm, LANES - HEAD_DIM), 1)
    onehot = jnp.where(lane == ((pos >> 6) & (TILE_BLOCKS - 1)), 1.0, 0.0).astype(bf16)
    for j in range(2):
        for g in range(N_KV):
            off = (j * N_KV + g) * HEAD_DIM
            kn_ref[0, j, g] = jnp.concatenate([kn[:, off:off + HEAD_DIM], onehot], axis=1)
    cv_ref[0] = jnp.dot(h, wv_ref[...], preferred_element_type=f32)


def _in_proj(x, mod, w_in, tm):
    b, s, d = x.shape
    o = 0
    wq = w_in[:, o:o + D_ATTN]; o += D_ATTN
    wkc = w_in[:, o:o + 2 * D_KV]; o += 2 * D_KV
    wk_s = w_in[:, o:o + D_KV]; o += D_KV
    wv_s = w_in[:, o:o + D_KV]; o += D_KV
    wk_w = w_in[:, o:o + D_KV]; o += D_KV
    wv_w = w_in[:, o:o + D_KV]; o += D_KV
    wg = w_in[:, o:o + 3 * N_HEADS]; o += 3 * N_HEADS
    wcv = w_in[:, o:]
    d_conv2 = wcv.shape[1]
    wg = jnp.zeros((d, GATE_ROWS), f32).at[:, :3 * N_HEADS].set(wg)
    wt = jnp.concatenate([wq, wv_s, wv_w, wg], axis=1).T
    ws = [w.astype(bf16) for w in (wt, wkc, jnp.concatenate([wk_s, wk_w], axis=1), wcv)]
    full = lambda a: pl.BlockSpec(a.shape, lambda bi, i: (0, 0))
    return pl.pallas_call(
        _in_proj_kernel,
        grid=(b, s // tm),
        in_specs=[pl.BlockSpec((1, tm, d), lambda bi, i: (bi, i, 0)),
                  pl.BlockSpec((1, 6, d), lambda bi, i: (bi, 0, 0))] + [full(w) for w in ws],
        out_specs=[pl.BlockSpec((1, D_ATTN, tm), lambda bi, i: (bi, 0, i)),
                   pl.BlockSpec((1, 2, N_KV, V_ROWS, tm), lambda bi, i: (bi, 0, 0, 0, i)),
                   pl.BlockSpec((1, GATE_ROWS, tm), lambda bi, i: (bi, 0, i)),
                   pl.BlockSpec((1, 2, tm, D_KV), lambda bi, i: (bi, 0, i, 0)),
                   pl.BlockSpec((1, 2, N_KV, tm, LANES), lambda bi, i: (bi, 0, 0, i, 0)),
                   pl.BlockSpec((1, tm, d_conv2), lambda bi, i: (bi, i, 0))],
        out_shape=[jax.ShapeDtypeStruct((b, D_ATTN, s), bf16),
                   jax.ShapeDtypeStruct((b, 2, N_KV, V_ROWS, s), bf16),
                   jax.ShapeDtypeStruct((b, GATE_ROWS, s), f32),
                   jax.ShapeDtypeStruct((b, 2, s, D_KV), f32),
                   jax.ShapeDtypeStruct((b, 2, N_KV, s, LANES), bf16),
                   jax.ShapeDtypeStruct((b, s, d_conv2), f32)],
        compiler_params=_params(("parallel", "parallel")),
        name="in_proj",
    )(x, mod, *ws)


def _compress_kernel(x_ref, pe_ref, w1_ref, w2_ref, w2t_ref, o_ref, ot_ref):
    n_chunk = o_ref.shape[3]
    for j in range(2):
        for g in range(N_KV):
            cols = slice(g * HEAD_DIM, (g + 1) * HEAD_DIM)
            a = jnp.zeros((n_chunk, CMP_HID), f32)
            bm = jnp.zeros((n_chunk, CMP_HID), f32)
            for l in range(CMP_STRIDE):
                xl = x_ref[0, j, pl.ds(l, n_chunk, stride=CMP_STRIDE), :][:, cols]
                a = a + jnp.dot((xl + pe_ref[j, l:l + 1, :]).astype(bf16), w1_ref[j, l],
                                preferred_element_type=f32)
                bm = bm + jnp.dot((xl + pe_ref[j, CMP_STRIDE + l:CMP_STRIDE + l + 1, :]).astype(bf16),
                                  w1_ref[j, CMP_STRIDE + l], preferred_element_type=f32)
            hid = a + pltpu.roll(bm, n_chunk - 1, 0)
            act = 0.5 * hid * (1.0 + jnp.tanh(0.7978845608028654 * (hid + 0.044715 * (hid * hid * hid))))
            act = act.astype(bf16)
            o_ref[0, j, g] = jnp.dot(act, w2_ref[j], preferred_element_type=f32).astype(bf16)
            ot_ref[0, j, g] = _dot_nt(w2t_ref[j], act).astype(bf16)


def _compress(kvc, pe, w1, w2):
    assert CMP_LEN == 2 * CMP_STRIDE
    b, _, s, width = kvc.shape
    n_chunk = s // CMP_STRIDE
    w1b = w1.reshape(2, CMP_LEN, HEAD_DIM, CMP_HID).astype(bf16)
    w2b = w2.astype(bf16)
    w2t = w2b.transpose(0, 2, 1)
    full = lambda a: pl.BlockSpec(a.shape, lambda bi: (0,) * a.ndim)
    return pl.pallas_call(
        _compress_kernel,
        grid=(b,),
        in_specs=[pl.BlockSpec((1, 2, s, width), lambda bi: (bi, 0, 0, 0)),
                  full(pe), full(w1b), full(w2b), full(w2t)],
        out_specs=[pl.BlockSpec((1, 2, N_KV, n_chunk, HEAD_DIM), lambda bi: (bi, 0, 0, 0, 0)),
                   pl.BlockSpec((1, 2, N_KV, HEAD_DIM, n_chunk), lambda bi: (bi, 0, 0, 0, 0))],
        out_shape=[jax.ShapeDtypeStruct((b, 2, N_KV, n_chunk, HEAD_DIM), bf16),
                   jax.ShapeDtypeStruct((b, 2, N_KV, HEAD_DIM, n_chunk), bf16)],
        compiler_params=_params(("parallel",)),
        name="compress",
    )(kvc, pe, w1b, w2b, w2t)


def _attn_kernel(q_ref, g_ref, kc_ref, vc_ref, kn_ref, vt_ref, o_ref, psum_ref, selbias_ref, sa_ref, sb_ref,
                 *, seq, tk, top_n):
    i = pl.program_id(2)
    s0 = i * Q_BLOCK
    n_cmp_rows = kc_ref.shape[3]
    n_sel = seq // SEL_LEN

    q4 = q_ref[0]
    qt = jnp.concatenate([q4[p * HEAD_DIM:(p + 1) * HEAD_DIM, :] for p in range(HPG)], axis=1)
    t_row = s0 + lax.broadcasted_iota(i32, (1, Q_BLOCK), 1)

    s_c = jnp.dot(kc_ref[0, 0, 0], qt, preferred_element_type=f32)
    cmp_end = lax.broadcasted_iota(i32, (n_cmp_rows, 1), 0) * CMP_STRIDE + (CMP_LEN - 1)
    bias_c = jnp.where(cmp_end <= t_row, 0.0, NEG)
    any_c = t_row >= CMP_LEN - 1
    p_sum = jnp.zeros((n_cmp_rows, Q_BLOCK), f32)
    pcs = []
    for p in range(HPG):
        sp = s_c[:, p * Q_BLOCK:(p + 1) * Q_BLOCK] + bias_c
        e = jnp.exp2(sp - jnp.max(sp, axis=0, keepdims=True))
        pn = e * jnp.where(any_c, 1.0 / jnp.sum(e, axis=0, keepdims=True), 0.0)
        p_sum = p_sum + pn
        pcs.append(pn.astype(bf16))
    o_c = jnp.dot(vc_ref[0, 0, 0], jnp.concatenate(pcs, axis=1), preferred_element_type=f32)

    cols = HPG * Q_BLOCK
    pad_rows = LANES - HEAD_DIM
    q_pad = jnp.concatenate([qt, jnp.zeros((pad_rows, cols), bf16)], axis=0)
    span = WINDOW + Q_BLOCK
    w0 = pl.multiple_of(jnp.maximum(s0 - WINDOW, 0), Q_BLOCK)
    s_w = jnp.dot(kn_ref[0, 1, 0, pl.ds(w0, span), :], q_pad, preferred_element_type=f32)
    wpos = w0 + lax.broadcasted_iota(i32, (span, 1), 0)
    bias_w = jnp.where((wpos <= t_row) & (wpos > t_row - WINDOW), 0.0, NEG)
    pws = []
    for p in range(HPG):
        sp = s_w[:, p * Q_BLOCK:(p + 1) * Q_BLOCK] + bias_w
        pws.append(jnp.exp2(sp - jnp.max(sp, axis=0, keepdims=True)).astype(bf16))
    acc_w = jnp.dot(vt_ref[0, 1, 0, :, pl.ds(w0, span)], jnp.concatenate(pws, axis=1),
                    preferred_element_type=f32)
    o_w = acc_w[0:HEAD_DIM] * (1.0 / acc_w[HEAD_DIM:HEAD_DIM + 1])

    psum_ref[...] = p_sum
    blk = lax.broadcasted_iota(i32, (n_sel, Q_BLOCK), 0)
    imp = psum_ref[pl.ds(0, n_sel, stride=CMP_PER_SEL), :]
    for r in range(1, CMP_PER_SEL):
        imp = imp + psum_ref[pl.ds(r, n_sel, stride=CMP_PER_SEL), :]
    for back in range(1, CMP_BACK + 1):
        prev = pltpu.roll(psum_ref[pl.ds(CMP_PER_SEL - back, n_sel, stride=CMP_PER_SEL), :], 1, 0)
        imp = imp + jnp.where(blk >= 1, prev, 0.0)

    cur = t_row >> 6
    forced = (blk == 0) | (blk == cur) | (blk == cur - 1)
    vals = jnp.where(forced, jnp.inf, jnp.where(blk <= cur, imp, -jnp.inf))
    sel = jnp.zeros((n_sel, Q_BLOCK), f32)
    for _ in range(top_n):
        mx = jnp.max(vals, axis=0, keepdims=True)
        first = jnp.min(jnp.where(vals == mx, blk, n_sel), axis=0, keepdims=True)
        pick = blk == first
        sel = jnp.where(pick & (mx > -jnp.inf), 1.0, sel)
        vals = jnp.where(pick, -jnp.inf, vals)
    selbias_ref[0:n_sel] = jnp.where((sel > 0.5) & (blk < (s0 >> 6)), 0.0, NEG)
    selbias_ref[n_sel:n_sel + TILE_BLOCKS] = jnp.full((TILE_BLOCKS, Q_BLOCK), NEG, f32)
    last_tile = seq // tk - 1

    def scores(kt, s_ref):
        k0 = pl.multiple_of(jnp.minimum(kt, last_tile) * tk, tk)
        sb = selbias_ref[pl.ds(pl.multiple_of(kt * TILE_BLOCKS, TILE_BLOCKS), TILE_BLOCKS), :]
        rows = jnp.concatenate([jnp.concatenate([sb] * HPG, axis=1),
                                jnp.zeros((pad_rows - TILE_BLOCKS, cols), f32)], axis=0).astype(bf16)
        s_ref[...] = jnp.dot(kn_ref[0, 0, 0, pl.ds(k0, tk), :], jnp.concatenate([qt, rows], axis=0),
                             preferred_element_type=f32)

    def fold(scores_of_head, v_t, carry):
        m_i, acc = carry
        es, ms, alphas = [], [], []
        for p in range(HPG):
            c = slice(p * Q_BLOCK, (p + 1) * Q_BLOCK)
            sp = scores_of_head(c)
            m_new = jnp.maximum(m_i[:, c], jnp.max(sp, axis=0, keepdims=True))
            es.append(jnp.exp2(sp - m_new).astype(bf16))
            alphas.append(jnp.exp2(m_i[:, c] - m_new))
            ms.append(m_new)
        pv = jnp.dot(v_t, jnp.concatenate(es, axis=1), preferred_element_type=f32)
        return jnp.concatenate(ms, axis=1), jnp.concatenate(alphas, axis=1) * acc + pv

    def sel_tile(kt, s_ref, carry):
        k0 = pl.multiple_of(jnp.minimum(kt, last_tile) * tk, tk)
        return fold(lambda c: s_ref[:, c], vt_ref[0, 0, 0, :, pl.ds(k0, tk)], carry)

    def sel_pair(i, carry):
        scores(2 * i + 1, sb_ref)
        carry = sel_tile(2 * i, sa_ref, carry)
        scores(2 * i + 2, sa_ref)
        return sel_tile(2 * i + 1, sb_ref, carry)

    n_sweep = (s0 + tk - 1) // tk
    init = (jnp.full((1, cols), NEG, f32), jnp.zeros((V_ROWS, cols), f32))
    scores(0, sa_ref)
    carry = lax.fori_loop(0, (n_sweep + 1) // 2, sel_pair, init)
    d0 = pl.multiple_of(s0, Q_BLOCK)
    s_d = jnp.dot(kn_ref[0, 0, 0, pl.ds(d0, Q_BLOCK), :], q_pad, preferred_element_type=f32)
    bias_d = jnp.where(s0 + lax.broadcasted_iota(i32, (Q_BLOCK, 1), 0) <= t_row, 0.0, NEG)
    _, acc_s = fold(lambda c: s_d[:, c] + bias_d, vt_ref[0, 0, 0, :, pl.ds(d0, Q_BLOCK)], carry)
    o_s = acc_s[0:HEAD_DIM] * (1.0 / acc_s[HEAD_DIM:HEAD_DIM + 1])

    gate = _sigmoid(g_ref[0, 0])
    outs = []
    for p in range(HPG):
        c = slice(p * Q_BLOCK, (p + 1) * Q_BLOCK)
        outs.append(gate[3 * p:3 * p + 1] * o_c[:, c] + gate[3 * p + 1:3 * p + 2] * o_s[:, c]
                    + gate[3 * p + 2:3 * p + 3] * o_w[:, c])
    o_ref[0] = jnp.concatenate(outs, axis=0).T.astype(bf16)


GATE_GROUP_ROWS = 16


def _attention(qt, gates, cmp_n, cmp_t, kn, vt):
    b, _, s = qt.shape
    n_chunk = cmp_n.shape[3]
    n_sel = s // SEL_LEN
    top_n = min(SEL_TOPN, n_sel)
    tk = KEY_TILE
    assert s % tk == 0 and n_chunk == n_sel * CMP_PER_SEL
    gw = HPG * HEAD_DIM
    return pl.pallas_call(
        functools.partial(_attn_kernel, seq=s, tk=tk, top_n=top_n),
        grid=(b, N_KV, s // Q_BLOCK),
        in_specs=[pl.BlockSpec((1, gw, Q_BLOCK), lambda bi, g, i: (bi, g, i)),
                  pl.BlockSpec((1, 1, GATE_GROUP_ROWS, Q_BLOCK), lambda bi, g, i: (bi, g, 0, i)),
                  pl.BlockSpec((1, 1, 1, n_chunk, HEAD_DIM), lambda bi, g, i: (bi, 0, g, 0, 0)),
                  pl.BlockSpec((1, 1, 1, HEAD_DIM, n_chunk), lambda bi, g, i: (bi, 1, g, 0, 0)),
                  pl.BlockSpec((1, 2, 1, s, LANES), lambda bi, g, i: (bi, 0, g, 0, 0)),
                  pl.BlockSpec((1, 2, 1, V_ROWS, s), lambda bi, g, i: (bi, 0, g, 0, 0))],
        out_specs=pl.BlockSpec((1, Q_BLOCK, gw), lambda bi, g, i: (bi, i, g)),
        out_shape=jax.ShapeDtypeStruct((b, s, D_ATTN), bf16),
        scratch_shapes=[pltpu.VMEM((n_chunk, Q_BLOCK), f32), pltpu.VMEM((n_sel + TILE_BLOCKS, Q_BLOCK), f32),
                        pltpu.VMEM((tk, HPG * Q_BLOCK), f32), pltpu.VMEM((tk, HPG * Q_BLOCK), f32)],
        compiler_params=_params(("parallel", "parallel", "arbitrary")),
        name="attention",
    )(qt, gates, cmp_n, cmp_t, kn, vt)


CONV_HALO = 32
CONV_ROWS = 32


def _conv_kernel(cur_ref, prev_ref, w_ref, b_ref, g_ref, bb_ref, o_ref, glu_ref):
    i = pl.program_id(1)
    ts = cur_ref.shape[1]
    dc = o_ref.shape[2]
    cur = cur_ref[0]
    prev = prev_ref[0]
    glu_prev = prev[:, :dc] * _sigmoid(prev[:, dc:])
    glu_ref[0:CONV_HALO] = jnp.where(i == 0, 0.0, glu_prev)
    glu_ref[CONV_HALO:CONV_HALO + ts] = cur[:, :dc] * _sigmoid(cur[:, dc:])
    lead = CONV_HALO - (CONV_WIDTH - 1)

    def chunk(r, _):
        r0 = pl.multiple_of(r * CONV_ROWS, CONV_ROWS)
        win = glu_ref[pl.ds(r0, CONV_ROWS + CONV_HALO), :]
        span = CONV_ROWS + CONV_HALO
        acc = jnp.zeros((CONV_ROWS, dc), f32)
        for ph in range(SUBLANES):
            turned = win if ph == 0 else pltpu.roll(win, span - ph, 0)
            for j in range(CONV_WIDTH):
                if (lead + j) % SUBLANES == ph:
                    at = lead + j - ph
                    acc = acc + w_ref[j:j + 1, :] * turned[at:at + CONV_ROWS]
        y = _layer_norm(acc + b_ref[...], g_ref[...], bb_ref[...])
        o_ref[0, pl.ds(r0, CONV_ROWS), :] = _silu(y).astype(bf16)
        return 0

    lax.fori_loop(0, ts // CONV_ROWS, chunk, 0)


def _conv(conv_in, w_dw, b_dw, ln_g, ln_b, ts):
    b, s, dc2 = conv_in.shape
    dc = dc2 // 2
    per = ts // CONV_HALO
    row = lambda a: a.reshape(1, dc)
    return pl.pallas_call(
        _conv_kernel,
        grid=(b, s // ts),
        in_specs=[pl.BlockSpec((1, ts, dc2), lambda bi, i: (bi, i, 0)),
                  pl.BlockSpec((1, CONV_HALO, dc2), lambda bi, i: (bi, jnp.maximum(i * per - 1, 0), 0)),
                  pl.BlockSpec((CONV_WIDTH, dc), lambda bi, i: (0, 0)),
                  pl.BlockSpec((1, dc), lambda bi, i: (0, 0)),
                  pl.BlockSpec((1, dc), lambda bi, i: (0, 0)),
                  pl.BlockSpec((1, dc), lambda bi, i: (0, 0))],
        out_specs=pl.BlockSpec((1, ts, dc), lambda bi, i: (bi, i, 0)),
        out_shape=jax.ShapeDtypeStruct((b, s, dc), bf16),
        scratch_shapes=[pltpu.VMEM((CONV_HALO + ts, dc), f32)],
        compiler_params=_params(("parallel", "parallel")),
        name="conv",
    )(conv_in, conv_in, w_dw.reshape(CONV_WIDTH, dc), row(b_dw), row(ln_g), row(ln_b))


ROUTE_HALF = 128


def _mix_route_kernel(a_ref, cv_ref, x_ref, mod_ref, wo_ref, lg_ref, lb_ref, wrh_ref, wrl_ref, rb_ref,
                      x1_ref, h2p_ref, idx_ref, wt_ref, cnt_ref):
    step = pl.program_id(0)

    @pl.when(step == 0)
    def _():
        cnt_ref[...] = jnp.zeros_like(cnt_ref)

    m = mod_ref[0]
    da = a_ref.shape[1]
    counts = jnp.zeros(cnt_ref.shape, f32)
    for first in range(0, x_ref.shape[0], ROUTE_HALF):
        rows = slice(first, first + ROUTE_HALF)
        mix = (jnp.dot(a_ref[rows, :], wo_ref[0:da, :], preferred_element_type=f32)
               + jnp.dot(cv_ref[rows, :], wo_ref[da:, :], preferred_element_type=f32))
        x1 = _layer_norm(DEEPNORM_ALPHA * x_ref[rows, :] + m[2:3] * mix, lg_ref[...], lb_ref[...])
        x1_ref[rows, :] = x1
        h2 = x1 * (1.0 + m[4:5]) + m[3:4]
        _store_packed_rows(h2p_ref, h2, first)
        h_hi = h2.astype(bf16)
        h_lo = (h2 - h_hi.astype(f32)).astype(bf16)
        logits = _dot_nt(wrh_ref[...], h_hi) + (_dot_nt(wrh_ref[...], h_lo) + _dot_nt(wrl_ref[...], h_hi))
        idx, wt, cnt = _route(_sigmoid(logits), rb_ref[...])
        idx_ref[:, rows] = idx
        wt_ref[:, rows] = wt
        counts = counts + cnt
    cnt_ref[...] += counts


def _route(score, bias):
    tm = score.shape[1]
    sel = score + bias
    eid = lax.broadcasted_iota(i32, (N_EXPERTS, tm), 0)
    per_group = N_EXPERTS // N_GROUPS
    gs = []
    for g in range(N_GROUPS):
        rows = slice(g * per_group, (g + 1) * per_group)
        v = sel[rows]
        e = g * per_group + lax.broadcasted_iota(i32, (per_group, tm), 0)
        m1 = jnp.max(v, axis=0, keepdims=True)
        i1 = jnp.min(jnp.where(v == m1, e, N_EXPERTS), axis=0, keepdims=True)
        m2 = jnp.max(jnp.where(e == i1, -jnp.inf, v), axis=0, keepdims=True)
        gs.append(m1 + m2)
    cands = []
    for g in range(N_GROUPS):
        rank = jnp.zeros((1, tm), i32)
        for o in range(N_GROUPS):
            if o == g:
                continue
            beats = (gs[o] > gs[g]) | (gs[o] == gs[g]) if o < g else gs[o] > gs[g]
            rank = rank + beats.astype(i32)
        drop = jnp.where(rank < TOPK_GROUPS, 0.0, -jnp.inf)
        cands.append(sel[g * per_group:(g + 1) * per_group] + drop)
    cand = jnp.concatenate(cands, axis=0)
    row_o = lax.broadcasted_iota(i32, (TOP_K, tm), 0)
    idx_out = jnp.zeros((TOP_K, tm), i32)
    wt_out = jnp.zeros((TOP_K, tm), f32)
    picked = jnp.zeros((N_EXPERTS, tm), f32)
    w_sum = jnp.zeros((1, tm), f32)
    for k in range(TOP_K):
        mx = jnp.max(cand, axis=0, keepdims=True)
        ik = jnp.min(jnp.where(cand == mx, eid, N_EXPERTS), axis=0, keepdims=True)
        pick = eid == ik
        wk = jnp.sum(jnp.where(pick, score, 0.0), axis=0, keepdims=True)
        cand = jnp.where(pick, -jnp.inf, cand)
        picked = jnp.where(pick, 1.0, picked)
        idx_out = jnp.where(row_o == k, ik, idx_out)
        wt_out = jnp.where(row_o == k, wk, wt_out)
        w_sum = w_sum + wk
    return idx_out, wt_out / w_sum * ROUTED_SCALE, jnp.sum(picked, axis=1, keepdims=True)


def _mix_route(attn, conv, x2, mod, w_out, ln_g, ln_b, w_router, router_bias, s, tm):
    t, d = x2.shape
    per = s // tm
    da = attn.shape[1]
    pack_rows = d // 2 // LANES
    row = lambda a: a.reshape(1, -1)
    tile = lambda w: pl.BlockSpec((tm, w), lambda i: (i, 0))
    full = lambda a: pl.BlockSpec(a.shape, lambda i: (0,) * a.ndim)
    assert tm % ROUTE_HALF == 0
    wr_hi = w_router.T.astype(bf16)
    wr_lo = (w_router.T - wr_hi.astype(f32)).astype(bf16)
    args = (attn, conv, x2, mod, w_out.astype(bf16), row(ln_g), row(ln_b), wr_hi, wr_lo,
            router_bias.reshape(N_EXPERTS, 1))
    per_token = lambda rows: pl.BlockSpec((rows, tm), lambda i: (0, i))
    return pl.pallas_call(
        _mix_route_kernel,
        grid=(t // tm,),
        in_specs=[tile(da), tile(conv.shape[1]), tile(d),
                  pl.BlockSpec((1, 6, d), lambda i: (i // per, 0, 0))] + [full(a) for a in args[4:]],
        out_specs=[tile(d), pl.BlockSpec((tm * pack_rows, LANES), lambda i: (i, 0)),
                   per_token(TOP_K), per_token(TOP_K), pl.BlockSpec((N_EXPERTS, 1), lambda i: (0, 0))],
        out_shape=[jax.ShapeDtypeStruct((t, d), f32),
                   jax.ShapeDtypeStruct((t * pack_rows, LANES), u32),
                   jax.ShapeDtypeStruct((TOP_K, t), i32), jax.ShapeDtypeStruct((TOP_K, t), f32),
                   jax.ShapeDtypeStruct((N_EXPERTS, 1), f32)],
        compiler_params=_params(("arbitrary",)),
        name="mix_route",
    )(*args)


def _positions_kernel(idx_ref, start_ref, o_ref, run_ref):
    step = pl.program_id(0)
    tm = idx_ref.shape[1]

    @pl.when(step == 0)
    def _():
        run_ref[...] = jnp.zeros_like(run_ref)

    idx = idx_ref[...]
    eid = lax.broadcasted_iota(i32, (N_EXPERTS, tm), 0)
    onehot = jnp.zeros((N_EXPERTS, tm), f32)
    for k in range(TOP_K):
        onehot = jnp.where(eid == idx[k:k + 1], 1.0, onehot)
    r = lax.broadcasted_iota(i32, (tm, tm), 0)
    c = lax.broadcasted_iota(i32, (tm, tm), 1)
    earlier = jnp.where(r < c, 1.0, 0.0).astype(bf16)
    prior = jnp.dot(onehot.astype(bf16), earlier, preferred_element_type=f32)
    pos = prior + run_ref[...] + start_ref[...]
    row_o = lax.broadcasted_iota(i32, (TOP_K, tm), 0)
    out = jnp.zeros((TOP_K, tm), i32)
    for k in range(TOP_K):
        dk = jnp.sum(jnp.where(eid == idx[k:k + 1], pos, 0.0), axis=0, keepdims=True)
        out = jnp.where(row_o == k, dk.astype(i32), out)
    o_ref[...] = out
    run_ref[...] += jnp.sum(onehot, axis=1, keepdims=True)


def _positions(idx, seg_start, tm):
    t = idx.shape[1]
    return pl.pallas_call(
        _positions_kernel,
        grid=(t // tm,),
        in_specs=[pl.BlockSpec((TOP_K, tm), lambda i: (0, i)),
                  pl.BlockSpec((N_EXPERTS, 1), lambda i: (0, 0))],
        out_specs=pl.BlockSpec((TOP_K, tm), lambda i: (0, i)),
        out_shape=jax.ShapeDtypeStruct((TOP_K, t), i32),
        scratch_shapes=[pltpu.VMEM((N_EXPERTS, 1), f32)],
        compiler_params=_params(("arbitrary",)),
        name="positions",
    )(idx, seg_start)


def _dispatch_kernel(dest_hbm, h_ref, o_hbm, dest_smem, sem_i, sem, *, r):
    step = pl.program_id(0)
    n = dest_smem.shape[0]
    load = pltpu.make_async_copy(dest_hbm.at[pl.ds(pl.multiple_of(step * n, n), n)], dest_smem, sem_i)
    load.start()
    load.wait()

    def issue(tok, _):
        src = pl.multiple_of(tok * r, r)
        for k in range(TOP_K):
            dst = pl.multiple_of(dest_smem[tok * TOP_K + k] * r, r)
            pltpu.make_async_copy(h_ref.at[pl.ds(src, r)], o_hbm.at[pl.ds(dst, r)], sem).start(priority=k % 2)
        return 0

    lax.fori_loop(0, n // TOP_K, issue, 0)
    everything = o_hbm.at[pl.ds(0, n * r)]
    pltpu.make_async_copy(everything, everything, sem).wait()


def _dispatch(dest_flat, h2p, n_buf, r, tm):
    t = h2p.shape[0] // r
    return pl.pallas_call(
        functools.partial(_dispatch_kernel, r=r),
        grid=(t // tm,),
        in_specs=[pl.BlockSpec(memory_space=pl.ANY),
                  pl.BlockSpec((tm * r, LANES), lambda i: (i, 0))],
        out_specs=pl.BlockSpec(memory_space=pl.ANY),
        out_shape=jax.ShapeDtypeStruct((n_buf * r, LANES), u32),
        scratch_shapes=[pltpu.SMEM((tm * TOP_K,), i32), pltpu.SemaphoreType.DMA, pltpu.SemaphoreType.DMA],
        compiler_params=_params(("arbitrary",)),
        name="dispatch",
    )(dest_flat, h2p)


def _experts_kernel(be_ref, bv_ref, nu_ref, x_ref, wg_ref, wu_ref, wd_ref, o_ref, wg_s, wu_s, wd_s):
    j = pl.program_id(0)
    prev = be_ref[jnp.maximum(j - 1, 0)]
    used = j < nu_ref[0]
    d = wg_s.shape[0]

    @pl.when(used & ((j == 0) | (be_ref[j] != prev)))
    def _():
        wg_s[...] = wg_ref[0].astype(bf16)
        wu_s[...] = wu_ref[0].astype(bf16)
        wd_s[...] = wd_ref[0].astype(bf16)

    @pl.when(used)
    def _():
        live = lax.broadcasted_iota(i32, (ROW_BLOCK, 1), 0) < bv_ref[j]
        x = jnp.where(live, _load_packed_rows(x_ref, ROW_BLOCK, d), 0.0).astype(bf16)
        hg = jnp.dot(x, wg_s[...], preferred_element_type=f32)
        hu = jnp.dot(x, wu_s[...], preferred_element_type=f32)
        hid = (_silu(hg) * hu).astype(bf16)
        _store_packed_rows(o_ref, jnp.dot(hid, wd_s[...], preferred_element_type=f32))


def _experts(blk_e, blk_valid, n_used, xs, w_gate, w_up, w_down):
    d, f = w_gate.shape[1], w_gate.shape[2]
    r = d // 2 // LANES
    n_blk = xs.shape[0] // r // ROW_BLOCK
    rows = lambda j, be, bv, nu: (jnp.minimum(j, nu[0] - 1), 0)
    wsel = lambda j, be, bv, nu: (be[j], 0, 0)
    return pl.pallas_call(
        _experts_kernel,
        grid_spec=pltpu.PrefetchScalarGridSpec(
            num_scalar_prefetch=3,
            grid=(n_blk,),
            in_specs=[pl.BlockSpec((ROW_BLOCK * r, LANES), rows),
                      pl.BlockSpec((1, d, f), wsel),
                      pl.BlockSpec((1, d, f), wsel),
                      pl.BlockSpec((1, f, d), wsel)],
            out_specs=pl.BlockSpec((ROW_BLOCK * r, LANES), rows),
            scratch_shapes=[pltpu.VMEM((d, f), bf16), pltpu.VMEM((d, f), bf16), pltpu.VMEM((f, d), bf16)]),
        out_shape=jax.ShapeDtypeStruct(xs.shape, u32),
        compiler_params=_params(("arbitrary",)),
        name="experts",
    )(blk_e, blk_valid, n_used, xs, w_gate, w_up, w_down)


def _combine_kernel(dest_hbm, ys_hbm, wt_ref, h_ref, x1_ref, mod_ref, wsg_ref, wsu_ref, wsd_ref,
                    lg_ref, lb_ref, o_ref, dest_smem, rows_ref, sem_i, sem):
    step = pl.program_id(0)
    n = dest_smem.shape[0]
    load = pltpu.make_async_copy(dest_hbm.at[pl.ds(pl.multiple_of(step * n, n), n)], dest_smem, sem_i)
    load.start()
    load.wait()

    tm, d = x1_ref.shape
    r = d // 2 // LANES

    def issue(tok, _):
        dst = pl.multiple_of(tok * r, r)
        for k in range(TOP_K):
            src = pl.multiple_of(dest_smem[tok * TOP_K + k] * r, r)
            pltpu.make_async_copy(ys_hbm.at[pl.ds(src, r)], rows_ref.at[k, pl.ds(dst, r)],
                                  sem).start(priority=k % 2)
        return 0

    lax.fori_loop(0, tm, issue, 0)

    h = _load_packed_rows(h_ref, tm, d).astype(bf16)
    hg = jnp.dot(h, wsg_ref[...], preferred_element_type=f32)
    hu = jnp.dot(h, wsu_ref[...], preferred_element_type=f32)
    y = jnp.dot((_silu(hg) * hu).astype(bf16), wsd_ref[...], preferred_element_type=f32)

    pltpu.make_async_copy(rows_ref, rows_ref, sem).wait()
    wt = wt_ref[...]
    lo = [jnp.zeros((tm, LANES), f32) for _ in range(r)]
    hi = [jnp.zeros((tm, LANES), f32) for _ in range(r)]
    for k in range(TOP_K):
        wk = wt[:, k:k + 1]
        for c in range(r):
            pl_, ph_ = _unpack_words(rows_ref[k, pl.ds(c, tm, stride=r), :])
            lo[c] = lo[c] + wk * pl_
            hi[c] = hi[c] + wk * ph_
    y = y + jnp.concatenate(lo + hi, axis=1)
    m = mod_ref[0]
    o_ref[...] = _layer_norm(DEEPNORM_ALPHA * x1_ref[...] + m[5:6] * y, lg_ref[...], lb_ref[...])


def _combine(dest_flat, ys, wts, h2p, x1, mod, w_s_gate, w_s_up, w_s_down, ln_g, ln_b, s, tm):
    t, d = x1.shape
    per = s // tm
    r = d // 2 // LANES
    row = lambda a: a.reshape(1, -1)
    tile = lambda w: pl.BlockSpec((tm, w), lambda i: (i, 0))
    full = lambda a: pl.BlockSpec(a.shape, lambda i: (0,) * a.ndim)
    tail = (w_s_gate.astype(bf16), w_s_up.astype(bf16), w_s_down.astype(bf16), row(ln_g), row(ln_b))
    return pl.pallas_call(
        _combine_kernel,
        grid=(t // tm,),
        in_specs=[pl.BlockSpec(memory_space=pl.ANY), pl.BlockSpec(memory_space=pl.ANY),
                  tile(LANES), pl.BlockSpec((tm * r, LANES), lambda i: (i, 0)), tile(d),
                  pl.BlockSpec((1, 6, d), lambda i: (i // per, 0, 0))] + [full(a) for a in tail],
        out_specs=tile(d),
        out_shape=jax.ShapeDtypeStruct((t, d), f32),
        scratch_shapes=[pltpu.SMEM((tm * TOP_K,), i32), pltpu.VMEM((TOP_K, tm * r, LANES), u32),
                        pltpu.SemaphoreType.DMA, pltpu.SemaphoreType.DMA],
        compiler_params=_params(("arbitrary",)),
        name="combine",
    )(dest_flat, ys, wts, h2p, x1, mod, *tail)


def _layer(x, mod, w_in, pe, w_cmp1, w_cmp2, w_dw, b_dw, conv_ln_g, conv_ln_b, w_out, ln1_g, ln1_b,
           w_router, router_bias, w_e_gate, w_e_up, w_e_down, w_s_gate, w_s_up, w_s_down, ln2_g, ln2_b):
    b, s, d = x.shape
    t = b * s
    tm = min(512, s)
    qt, vt, gt, kvc, kn, conv_in = _in_proj(x, mod, w_in, tm)
    cmp_n, cmp_t = _compress(kvc, pe, w_cmp1, w_cmp2)
    gates = gt[:, :3 * N_HEADS].reshape(b, N_KV, 3 * HPG, s)
    gates = jnp.pad(gates, ((0, 0), (0, 0), (0, GATE_GROUP_ROWS - 3 * HPG), (0, 0)))
    attn = _attention(qt, gates, cmp_n, cmp_t, kn, vt)
    conv = _conv(conv_in, w_dw, b_dw, conv_ln_g, conv_ln_b, tm)

    tr = min(256, s)
    x1, h2p, idx, wts, counts = _mix_route(attn.reshape(t, -1), conv.reshape(t, -1), x.reshape(t, d), mod,
                                           w_out, ln1_g, ln1_b, w_router, router_bias, s, tr)
    counts = counts[:, 0].astype(i32)
    padded = (counts + ROW_BLOCK - 1) // ROW_BLOCK * ROW_BLOCK
    seg_end = jnp.cumsum(padded)
    seg_start = seg_end - padded
    n_blk = -(-(t * TOP_K + N_EXPERTS * (ROW_BLOCK - 1)) // ROW_BLOCK)
    blk_row0 = jnp.arange(n_blk, dtype=i32) * ROW_BLOCK
    owns = (blk_row0[:, None] >= seg_start[None, :]) & (blk_row0[:, None] < seg_end[None, :])
    blk_e = jnp.where(blk_row0 < seg_end[-1], jnp.argmax(owns, axis=1), N_EXPERTS - 1).astype(i32)
    live_end = jnp.sum(jnp.where(owns, (seg_start + counts)[None, :], 0), axis=1)
    blk_valid = jnp.clip(live_end - blk_row0, 0, ROW_BLOCK).astype(i32)
    n_used = (seg_end[-1:] // ROW_BLOCK).astype(i32)

    dest = _positions(idx, seg_start.astype(f32).reshape(N_EXPERTS, 1), tr)
    dest_flat = dest.T.reshape(-1)
    wts = jnp.zeros((t, LANES), f32).at[:, :TOP_K].set(wts.T)
    xs = _dispatch(dest_flat, h2p, n_blk * ROW_BLOCK, d // 2 // LANES, min(512, s))
    ys = _experts(blk_e, blk_valid, n_used, xs, w_e_gate, w_e_up, w_e_down)
    out = _combine(dest_flat, ys, wts, h2p, x1, mod, w_s_gate, w_s_up, w_s_down, ln2_g, ln2_b, s, min(256, s))
    return out.reshape(b, s, d)


def kernel(x, c, w_ada, b_ada, w_in, pe_k, pe_v, w_cmp_k1, w_cmp_k2, w_cmp_v1, w_cmp_v2, w_dw, b_dw,
           conv_ln_g, conv_ln_b, w_out, ln1_g, ln1_b, w_router, router_bias, w_e_gate, w_e_up, w_e_down,
           w_s_gate, w_s_up, w_s_down, ln2_g, ln2_b):
    assert w_ada.shape[0] == DEPTH
    layer = lambda a: a.reshape(a.shape[1:])
    mod = _ada(c, layer(w_ada), layer(b_ada))
    return _layer(x, mod, layer(w_in), jnp.concatenate([pe_k, pe_v]),
                  jnp.concatenate([w_cmp_k1, w_cmp_v1]), jnp.concatenate([w_cmp_k2, w_cmp_v2]),
                  *[layer(a) for a in (w_dw, b_dw, conv_ln_g, conv_ln_b, w_out, ln1_g, ln1_b, w_router,
                                       router_bias, w_e_gate, w_e_up, w_e_down, w_s_gate, w_s_up, w_s_down,
                                       ln2_g, ln2_b)])
```

```python
import functools

import jax
import jax.numpy as jnp
from jax import lax
from jax.experimental import pallas as pl
from jax.experimental.pallas import tpu as pltpu

N_HEADS = 8
N_KV = 2
HPG = N_HEADS // N_KV
HEAD_DIM = 64
D_ATTN = N_HEADS * HEAD_DIM
D_KV = N_KV * HEAD_DIM
CONV_WIDTH = 31
CMP_LEN = 32
CMP_STRIDE = 16
CMP_HID = 256
SEL_LEN = 64
SEL_TOPN = 16
WINDOW = 512
Q_BLOCK = 128
N_EXPERTS = 256
TOP_K = 8
N_GROUPS = 8
TOPK_GROUPS = 4
ROUTED_SCALE = 2.5
LN_EPS = 1e-5
DEPTH = 1
DEEPNORM_ALPHA = (2 * DEPTH) ** 0.25

LANES = 128
SUBLANES = 8
ROW_BLOCK = 256
NEG = -1e30
HIGHEST = lax.Precision.HIGHEST
VMEM_LIMIT = 48 * 1024 * 1024

f32 = jnp.float32
bf16 = jnp.bfloat16
i32 = jnp.int32
u32 = jnp.uint32


def _params(sem, vmem=VMEM_LIMIT):
    return pltpu.CompilerParams(dimension_semantics=sem, vmem_limit_bytes=vmem)


def _sigmoid(v):
    return 1.0 / (1.0 + jnp.exp(-v))


def _silu(v):
    return v * _sigmoid(v)


def _layer_norm(v, g, b):
    mu = jnp.mean(v, axis=-1, keepdims=True)
    var = jnp.mean(jnp.square(v - mu), axis=-1, keepdims=True)
    return (v - mu) * lax.rsqrt(var + LN_EPS) * g + b


def _dot_nt(a, b):
    return lax.dot_general(a, b, (((1,), (1,)), ((), ())), preferred_element_type=f32)


def _store_packed_rows(ref, v, first=0):
    n, d = v.shape
    half = d // 2
    bits = lax.bitcast_convert_type(v.astype(bf16).astype(f32), u32)
    words = bits[:, half:] | (bits[:, :half] >> 16)
    r = half // LANES
    for c in range(r):
        ref[pl.ds(first * r + c, n, stride=r), :] = words[:, c * LANES:(c + 1) * LANES]


def _unpack_words(w):
    return (lax.bitcast_convert_type(w << 16, f32),
            lax.bitcast_convert_type(w & jnp.uint32(0xFFFF0000), f32))


def _load_packed_rows(ref, n, d):
    r = d // 2 // LANES
    parts = [_unpack_words(ref[pl.ds(c, n, stride=r), :]) for c in range(r)]
    return jnp.concatenate([p[0] for p in parts] + [p[1] for p in parts], axis=1)


def _ada_kernel(c_ref, w_ref, b_ref, o_ref):
    c = c_ref[...]
    o_ref[...] = jnp.dot(_silu(c), w_ref[...], precision=HIGHEST,
                         preferred_element_type=f32) + b_ref[...]


def _ada(c, w_ada, b_ada):
    b, d = c.shape
    n = w_ada.shape[1]
    rows = 8
    c_pad = jnp.zeros((rows, d), f32).at[:b].set(c)
    tn = 1024
    out = pl.pallas_call(
        _ada_kernel,
        grid=(n // tn,),
        in_specs=[pl.BlockSpec((rows, d), lambda j: (0, 0)),
                  pl.BlockSpec((d, tn), lambda j: (0, j)),
                  pl.BlockSpec((1, tn), lambda j: (0, j))],
        out_specs=pl.BlockSpec((rows, tn), lambda j: (0, j)),
        out_shape=jax.ShapeDtypeStruct((rows, n), f32),
        compiler_params=_params(("arbitrary",)),
        name="ada",
    )(c_pad, w_ada, b_ada.reshape(1, n))
    return out[:b].reshape(b, 6, d)


GATE_ROWS = 32
KEY_TILE = 512
TILE_BLOCKS = KEY_TILE // SEL_LEN
CMP_PER_SEL = SEL_LEN // CMP_STRIDE
CMP_BACK = CMP_LEN // CMP_STRIDE - 1
assert SEL_LEN % CMP_STRIDE == 0 and CMP_LEN % CMP_STRIDE == 0 and CMP_BACK < CMP_PER_SEL
V_ROWS = HEAD_DIM + 16
Q_SCALE = HEAD_DIM ** -0.5 * 1.4426950408889634


def _in_proj_kernel(x_ref, mod_ref, wt_ref, wc_ref, wk_ref, wv_ref,
                    qt_ref, vt_ref, gt_ref, kvc_ref, kn_ref, cv_ref):
    m = mod_ref[0]
    h = (x_ref[0] * (1.0 + m[1:2]) + m[0:1]).astype(bf16)
    res_t = _dot_nt(wt_ref[...], h)
    qt_ref[0] = (res_t[0:D_ATTN] * Q_SCALE).astype(bf16)
    ones = jnp.ones((V_ROWS - HEAD_DIM, res_t.shape[1]), bf16)
    for j in range(2):
        for g in range(N_KV):
            off = D_ATTN + (j * N_KV + g) * HEAD_DIM
            vt_ref[0, j, g, 0:HEAD_DIM, :] = res_t[off:off + HEAD_DIM].astype(bf16)
            vt_ref[0, j, g, HEAD_DIM:V_ROWS, :] = ones
    gt_ref[0] = res_t[D_ATTN + 2 * D_KV:]
    kvc = jnp.dot(h, wc_ref[...], preferred_element_type=f32)
    kvc_ref[0, 0] = kvc[:, :D_KV]
    kvc_ref[0, 1] = kvc[:, D_KV:]
    kn = jnp.dot(h, wk_ref[...], preferred_element_type=f32).astype(bf16)
    tm = kn.shape[0]
    pos = pl.program_id(1) * tm + lax.broadcasted_iota(i32, (tm, LANES - HEAD_DIM), 0)
    lane = lax.broadcasted_iota(i32, (tm, LANES - HEAD_DIM), 1)
    onehot = jnp.where(lane == ((pos >> 6) & (TILE_BLOCKS - 1)), 1.0, 0.0).astype(bf16)
    for j in range(2):
        for g in range(N_KV):
            off = (j * N_KV + g) * HEAD_DIM
            kn_ref[0, j, g] = jnp.concatenate([kn[:, off:off + HEAD_DIM], onehot], axis=1)
    cv_ref[0] = jnp.dot(h, wv_ref[...], preferred_element_type=f32)


def _in_proj(x, mod, w_in, tm):
    b, s, d = x.shape
    o = 0
    wq = w_in[:, o:o + D_ATTN]; o += D_ATTN
    wkc = w_in[:, o:o + 2 * D_KV]; o += 2 * D_KV
    wk_s = w_in[:, o:o + D_KV]; o += D_KV
    wv_s = w_in[:, o:o + D_KV]; o += D_KV
    wk_w = w_in[:, o:o + D_KV]; o += D_KV
    wv_w = w_in[:, o:o + D_KV]; o += D_KV
    wg = w_in[:, o:o + 3 * N_HEADS]; o += 3 * N_HEADS
    wcv = w_in[:, o:]
    d_conv2 = wcv.shape[1]
    wg = jnp.zeros((d, GATE_ROWS), f32).at[:, :3 * N_HEADS].set(wg)
    wt = jnp.concatenate([wq, wv_s, wv_w, wg], axis=1).T
    ws = [w.astype(bf16) for w in (wt, wkc, jnp.concatenate([wk_s, wk_w], axis=1), wcv)]
    full = lambda a: pl.BlockSpec(a.shape, lambda bi, i: (0, 0))
    return pl.pallas_call(
        _in_proj_kernel,
        grid=(b, s // tm),
        in_specs=[pl.BlockSpec((1, tm, d), lambda bi, i: (bi, i, 0)),
                  pl.BlockSpec((1, 6, d), lambda bi, i: (bi, 0, 0))] + [full(w) for w in ws],
        out_specs=[pl.BlockSpec((1, D_ATTN, tm), lambda bi, i: (bi, 0, i)),
                   pl.BlockSpec((1, 2, N_KV, V_ROWS, tm), lambda bi, i: (bi, 0, 0, 0, i)),
                   pl.BlockSpec((1, GATE_ROWS, tm), lambda bi, i: (bi, 0, i)),
                   pl.BlockSpec((1, 2, tm, D_KV), lambda bi, i: (bi, 0, i, 0)),
                   pl.BlockSpec((1, 2, N_KV, tm, LANES), lambda bi, i: (bi, 0, 0, i, 0)),
                   pl.BlockSpec((1, tm, d_conv2), lambda bi, i: (bi, i, 0))],
        out_shape=[jax.ShapeDtypeStruct((b, D_ATTN, s), bf16),
                   jax.ShapeDtypeStruct((b, 2, N_KV, V_ROWS, s), bf16),
                   jax.ShapeDtypeStruct((b, GATE_ROWS, s), f32),
                   jax.ShapeDtypeStruct((b, 2, s, D_KV), f32),
                   jax.ShapeDtypeStruct((b, 2, N_KV, s, LANES), bf16),
                   jax.ShapeDtypeStruct((b, s, d_conv2), f32)],
        compiler_params=_params(("parallel", "parallel")),
        name="in_proj",
    )(x, mod, *ws)


def _compress_kernel(x_ref, pe_ref, w1_ref, w2_ref, w2t_ref, o_ref, ot_ref):
    n_chunk = o_ref.shape[3]
    for j in range(2):
        for g in range(N_KV):
            cols = slice(g * HEAD_DIM, (g + 1) * HEAD_DIM)
            a = jnp.zeros((n_chunk, CMP_HID), f32)
            bm = jnp.zeros((n_chunk, CMP_HID), f32)
            for l in range(CMP_STRIDE):
                xl = x_ref[0, j, pl.ds(l, n_chunk, stride=CMP_STRIDE), :][:, cols]
                a = a + jnp.dot((xl + pe_ref[j, l:l + 1, :]).astype(bf16), w1_ref[j, l],
                                preferred_element_type=f32)
                bm = bm + jnp.dot((xl + pe_ref[j, CMP_STRIDE + l:CMP_STRIDE + l + 1, :]).astype(bf16),
                                  w1_ref[j, CMP_STRIDE + l], preferred_element_type=f32)
            hid = a + pltpu.roll(bm, n_chunk - 1, 0)
            act = 0.5 * hid * (1.0 + jnp.tanh(0.7978845608028654 * (hid + 0.044715 * (hid * hid * hid))))
            act = act.astype(bf16)
            o_ref[0, j, g] = jnp.dot(act, w2_ref[j], preferred_element_type=f32).astype(bf16)
            ot_ref[0, j, g] = _dot_nt(w2t_ref[j], act).astype(bf16)


def _compress(kvc, pe, w1, w2):
    assert CMP_LEN == 2 * CMP_STRIDE
    b, _, s, width = kvc.shape
    n_chunk = s // CMP_STRIDE
    w1b = w1.reshape(2, CMP_LEN, HEAD_DIM, CMP_HID).astype(bf16)
    w2b = w2.astype(bf16)
    w2t = w2b.transpose(0, 2, 1)
    full = lambda a: pl.BlockSpec(a.shape, lambda bi: (0,) * a.ndim)
    return pl.pallas_call(
        _compress_kernel,
        grid=(b,),
        in_specs=[pl.BlockSpec((1, 2, s, width), lambda bi: (bi, 0, 0, 0)),
                  full(pe), full(w1b), full(w2b), full(w2t)],
        out_specs=[pl.BlockSpec((1, 2, N_KV, n_chunk, HEAD_DIM), lambda bi: (bi, 0, 0, 0, 0)),
                   pl.BlockSpec((1, 2, N_KV, HEAD_DIM, n_chunk), lambda bi: (bi, 0, 0, 0, 0))],
        out_shape=[jax.ShapeDtypeStruct((b, 2, N_KV, n_chunk, HEAD_DIM), bf16),
                   jax.ShapeDtypeStruct((b, 2, N_KV, HEAD_DIM, n_chunk), bf16)],
        compiler_params=_params(("parallel",)),
        name="compress",
    )(kvc, pe, w1b, w2b, w2t)


def _attn_kernel(q_ref, g_ref, kc_ref, vc_ref, kn_ref, vt_ref, o_ref, psum_ref, selbias_ref, sa_ref, sb_ref,
                 *, seq, tk, top_n):
    i = pl.program_id(2)
    s0 = i * Q_BLOCK
    n_cmp_rows = kc_ref.shape[3]
    n_sel = seq // SEL_LEN

    q4 = q_ref[0]
    qt = jnp.concatenate([q4[p * HEAD_DIM:(p + 1) * HEAD_DIM, :] for p in range(HPG)], axis=1)
    t_row = s0 + lax.broadcasted_iota(i32, (1, Q_BLOCK), 1)

    s_c = jnp.dot(kc_ref[0, 0, 0], qt, preferred_element_type=f32)
    cmp_end = lax.broadcasted_iota(i32, (n_cmp_rows, 1), 0) * CMP_STRIDE + (CMP_LEN - 1)
    bias_c = jnp.where(cmp_end <= t_row, 0.0, NEG)
    any_c = t_row >= CMP_LEN - 1
    p_sum = jnp.zeros((n_cmp_rows, Q_BLOCK), f32)
    pcs = []
    for p in range(HPG):
        sp = s_c[:, p * Q_BLOCK:(p + 1) * Q_BLOCK] + bias_c
        e = jnp.exp2(sp - jnp.max(sp, axis=0, keepdims=True))
        pn = e * jnp.where(any_c, 1.0 / jnp.sum(e, axis=0, keepdims=True), 0.0)
        p_sum = p_sum + pn
        pcs.append(pn.astype(bf16))
    o_c = jnp.dot(vc_ref[0, 0, 0], jnp.concatenate(pcs, axis=1), preferred_element_type=f32)

    cols = HPG * Q_BLOCK
    pad_rows = LANES - HEAD_DIM
    q_pad = jnp.concatenate([qt, jnp.zeros((pad_rows, cols), bf16)], axis=0)
    span = WINDOW + Q_BLOCK
    w0 = pl.multiple_of(jnp.maximum(s0 - WINDOW, 0), Q_BLOCK)
    s_w = jnp.dot(kn_ref[0, 1, 0, pl.ds(w0, span), :], q_pad, preferred_element_type=f32)
    wpos = w0 + lax.broadcasted_iota(i32, (span, 1), 0)
    bias_w = jnp.where((wpos <= t_row) & (wpos > t_row - WINDOW), 0.0, NEG)
    pws = []
    for p in range(HPG):
        sp = s_w[:, p * Q_BLOCK:(p + 1) * Q_BLOCK] + bias_w
        pws.append(jnp.exp2(sp - jnp.max(sp, axis=0, keepdims=True)).astype(bf16))
    acc_w = jnp.dot(vt_ref[0, 1, 0, :, pl.ds(w0, span)], jnp.concatenate(pws, axis=1),
                    preferred_element_type=f32)
    o_w = acc_w[0:HEAD_DIM] * (1.0 / acc_w[HEAD_DIM:HEAD_DIM + 1])

    psum_ref[...] = p_sum
    blk = lax.broadcasted_iota(i32, (n_sel, Q_BLOCK), 0)
    imp = psum_ref[pl.ds(0, n_sel, stride=CMP_PER_SEL), :]
    for r in range(1, CMP_PER_SEL):
        imp = imp + psum_ref[pl.ds(r, n_sel, stride=CMP_PER_SEL), :]
    for back in range(1, CMP_BACK + 1):
        prev = pltpu.roll(psum_ref[pl.ds(CMP_PER_SEL - back, n_sel, stride=CMP_PER_SEL), :], 1, 0)
        imp = imp + jnp.where(blk >= 1, prev, 0.0)

    cur = t_row >> 6
    forced = (blk == 0) | (blk == cur) | (blk == cur - 1)
    vals = jnp.where(forced, jnp.inf, jnp.where(blk <= cur, imp, -jnp.inf))
    sel = jnp.zeros((n_sel, Q_BLOCK), f32)
    for _ in range(top_n):
        mx = jnp.max(vals, axis=0, keepdims=True)
        first = jnp.min(jnp.where(vals == mx, blk, n_sel), axis=0, keepdims=True)
        pick = blk == first
        sel = jnp.where(pick & (mx > -jnp.inf), 1.0, sel)
        vals = jnp.where(pick, -jnp.inf, vals)
    selbias_ref[0:n_sel] = jnp.where((sel > 0.5) & (blk < (s0 >> 6)), 0.0, NEG)
    selbias_ref[n_sel:n_sel + TILE_BLOCKS] = jnp.full((TILE_BLOCKS, Q_BLOCK), NEG, f32)
    last_tile = seq // tk - 1

    def scores(kt, s_ref):
        k0 = pl.multiple_of(jnp.minimum(kt, last_tile) * tk, tk)
        sb = selbias_ref[pl.ds(pl.multiple_of(kt * TILE_BLOCKS, TILE_BLOCKS), TILE_BLOCKS), :]
        rows = jnp.concatenate([jnp.concatenate([sb] * HPG, axis=1),
                                jnp.zeros((pad_rows - TILE_BLOCKS, cols), f32)], axis=0).astype(bf16)
        s_ref[...] = jnp.dot(kn_ref[0, 0, 0, pl.ds(k0, tk), :], jnp.concatenate([qt, rows], axis=0),
                             preferred_element_type=f32)

    def fold(scores_of_head, v_t, carry):
        m_i, acc = carry
        es, ms, alphas = [], [], []
        for p in range(HPG):
            c = slice(p * Q_BLOCK, (p + 1) * Q_BLOCK)
            sp = scores_of_head(c)
            m_new = jnp.maximum(m_i[:, c], jnp.max(sp, axis=0, keepdims=True))
            es.append(jnp.exp2(sp - m_new).astype(bf16))
            alphas.append(jnp.exp2(m_i[:, c] - m_new))
            ms.append(m_new)
        pv = jnp.dot(v_t, jnp.concatenate(es, axis=1), preferred_element_type=f32)
        return jnp.concatenate(ms, axis=1), jnp.concatenate(alphas, axis=1) * acc + pv

    def sel_tile(kt, s_ref, carry):
        k0 = pl.multiple_of(jnp.minimum(kt, last_tile) * tk, tk)
        return fold(lambda c: s_ref[:, c], vt_ref[0, 0, 0, :, pl.ds(k0, tk)], carry)

    def sel_pair(i, carry):
        scores(2 * i + 1, sb_ref)
        carry = sel_tile(2 * i, sa_ref, carry)
        scores(2 * i + 2, sa_ref)
        return sel_tile(2 * i + 1, sb_ref, carry)

    n_sweep = (s0 + tk - 1) // tk
    init = (jnp.full((1, cols), NEG, f32), jnp.zeros((V_ROWS, cols), f32))
    scores(0, sa_ref)
    carry = lax.fori_loop(0, (n_sweep + 1) // 2, sel_pair, init)
    d0 = pl.multiple_of(s0, Q_BLOCK)
    s_d = jnp.dot(kn_ref[0, 0, 0, pl.ds(d0, Q_BLOCK), :], q_pad, preferred_element_type=f32)
    bias_d = jnp.where(s0 + lax.broadcasted_iota(i32, (Q_BLOCK, 1), 0) <= t_row, 0.0, NEG)
    _, acc_s = fold(lambda c: s_d[:, c] + bias_d, vt_ref[0, 0, 0, :, pl.ds(d0, Q_BLOCK)], carry)
    o_s = acc_s[0:HEAD_DIM] * (1.0 / acc_s[HEAD_DIM:HEAD_DIM + 1])

    gate = _sigmoid(g_ref[0, 0])
    outs = []
    for p in range(HPG):
        c = slice(p * Q_BLOCK, (p + 1) * Q_BLOCK)
        outs.append(gate[3 * p:3 * p + 1] * o_c[:, c] + gate[3 * p + 1:3 * p + 2] * o_s[:, c]
                    + gate[3 * p + 2:3 * p + 3] * o_w[:, c])
    o_ref[0] = jnp.concatenate(outs, axis=0).T.astype(bf16)


GATE_GROUP_ROWS = 16


def _attention(qt, gates, cmp_n, cmp_t, kn, vt):
    b, _, s = qt.shape
    n_chunk = cmp_n.shape[3]
    n_sel = s // SEL_LEN
    top_n = min(SEL_TOPN, n_sel)
    tk = KEY_TILE
    assert s % tk == 0 and n_chunk == n_sel * CMP_PER_SEL
    gw = HPG * HEAD_DIM
    return pl.pallas_call(
        functools.partial(_attn_kernel, seq=s, tk=tk, top_n=top_n),
        grid=(b, N_KV, s // Q_BLOCK),
        in_specs=[pl.BlockSpec((1, gw, Q_BLOCK), lambda bi, g, i: (bi, g, i)),
                  pl.BlockSpec((1, 1, GATE_GROUP_ROWS, Q_BLOCK), lambda bi, g, i: (bi, g, 0, i)),
                  pl.BlockSpec((1, 1, 1, n_chunk, HEAD_DIM), lambda bi, g, i: (bi, 0, g, 0, 0)),
                  pl.BlockSpec((1, 1, 1, HEAD_DIM, n_chunk), lambda bi, g, i: (bi, 1, g, 0, 0)),
                  pl.BlockSpec((1, 2, 1, s, LANES), lambda bi, g, i: (bi, 0, g, 0, 0)),
                  pl.BlockSpec((1, 2, 1, V_ROWS, s), lambda bi, g, i: (bi, 0, g, 0, 0))],
        out_specs=pl.BlockSpec((1, Q_BLOCK, gw), lambda bi, g, i: (bi, i, g)),
        out_shape=jax.ShapeDtypeStruct((b, s, D_ATTN), bf16),
        scratch_shapes=[pltpu.VMEM((n_chunk, Q_BLOCK), f32), pltpu.VMEM((n_sel + TILE_BLOCKS, Q_BLOCK), f32),
                        pltpu.VMEM((tk, HPG * Q_BLOCK), f32), pltpu.VMEM((tk, HPG * Q_BLOCK), f32)],
        compiler_params=_params(("parallel", "parallel", "arbitrary")),
        name="attention",
    )(qt, gates, cmp_n, cmp_t, kn, vt)


CONV_HALO = 32
CONV_ROWS = 32


def _conv_kernel(cur_ref, prev_ref, w_ref, b_ref, g_ref, bb_ref, o_ref, glu_ref):
    i = pl.program_id(1)
    ts = cur_ref.shape[1]
    dc = o_ref.shape[2]
    cur = cur_ref[0]
    prev = prev_ref[0]
    glu_prev = prev[:, :dc] * _sigmoid(prev[:, dc:])
    glu_ref[0:CONV_HALO] = jnp.where(i == 0, 0.0, glu_prev)
    glu_ref[CONV_HALO:CONV_HALO + ts] = cur[:, :dc] * _sigmoid(cur[:, dc:])
    lead = CONV_HALO - (CONV_WIDTH - 1)

    def chunk(r, _):
        r0 = pl.multiple_of(r * CONV_ROWS, CONV_ROWS)
        win = glu_ref[pl.ds(r0, CONV_ROWS + CONV_HALO), :]
        span = CONV_ROWS + CONV_HALO
        acc = jnp.zeros((CONV_ROWS, dc), f32)
        for ph in range(SUBLANES):
            turned = win if ph == 0 else pltpu.roll(win, span - ph, 0)
            for j in range(CONV_WIDTH):
                if (lead + j) % SUBLANES == ph:
                    at = lead + j - ph
                    acc = acc + w_ref[j:j + 1, :] * turned[at:at + CONV_ROWS]
        y = _layer_norm(acc + b_ref[...], g_ref[...], bb_ref[...])
        o_ref[0, pl.ds(r0, CONV_ROWS), :] = _silu(y).astype(bf16)
        return 0

    lax.fori_loop(0, ts // CONV_ROWS, chunk, 0)


def _conv(conv_in, w_dw, b_dw, ln_g, ln_b, ts):
    b, s, dc2 = conv_in.shape
    dc = dc2 // 2
    per = ts // CONV_HALO
    row = lambda a: a.reshape(1, dc)
    return pl.pallas_call(
        _conv_kernel,
        grid=(b, s // ts),
        in_specs=[pl.BlockSpec((1, ts, dc2), lambda bi, i: (bi, i, 0)),
                  pl.BlockSpec((1, CONV_HALO, dc2), lambda bi, i: (bi, jnp.maximum(i * per - 1, 0), 0)),
                  pl.BlockSpec((CONV_WIDTH, dc), lambda bi, i: (0, 0)),
                  pl.BlockSpec((1, dc), lambda bi, i: (0, 0)),
                  pl.BlockSpec((1, dc), lambda bi, i: (0, 0)),
                  pl.BlockSpec((1, dc), lambda bi, i: (0, 0))],
        out_specs=pl.BlockSpec((1, ts, dc), lambda bi, i: (bi, i, 0)),
        out_shape=jax.ShapeDtypeStruct((b, s, dc), bf16),
        scratch_shapes=[pltpu.VMEM((CONV_HALO + ts, dc), f32)],
        compiler_params=_params(("parallel", "parallel")),
        name="conv",
    )(conv_in, conv_in, w_dw.reshape(CONV_WIDTH, dc), row(b_dw), row(ln_g), row(ln_b))


ROUTE_HALF = 128


def _mix_route_kernel(a_ref, cv_ref, x_ref, mod_ref, wo_ref, lg_ref, lb_ref, wrh_ref, wrl_ref, rb_ref,
                      x1_ref, h2p_ref, idx_ref, wt_ref, cnt_ref):
    step = pl.program_id(0)

    @pl.when(step == 0)
    def _():
        cnt_ref[...] = jnp.zeros_like(cnt_ref)

    m = mod_ref[0]
    da = a_ref.shape[1]
    counts = jnp.zeros(cnt_ref.shape, f32)
    for first in range(0, x_ref.shape[0], ROUTE_HALF):
        rows = slice(first, first + ROUTE_HALF)
        mix = (jnp.dot(a_ref[rows, :], wo_ref[0:da, :], preferred_element_type=f32)
               + jnp.dot(cv_ref[rows, :], wo_ref[da:, :], preferred_element_type=f32))
        x1 = _layer_norm(DEEPNORM_ALPHA * x_ref[rows, :] + m[2:3] * mix, lg_ref[...], lb_ref[...])
        x1_ref[rows, :] = x1
        h2 = x1 * (1.0 + m[4:5]) + m[3:4]
        _store_packed_rows(h2p_ref, h2, first)
        h_hi = h2.astype(bf16)
        h_lo = (h2 - h_hi.astype(f32)).astype(bf16)
        logits = _dot_nt(wrh_ref[...], h_hi) + (_dot_nt(wrh_ref[...], h_lo) + _dot_nt(wrl_ref[...], h_hi))
        idx, wt, cnt = _route(_sigmoid(logits), rb_ref[...])
        idx_ref[:, rows] = idx
        wt_ref[:, rows] = wt
        counts = counts + cnt
    cnt_ref[...] += counts


def _route(score, bias):
    tm = score.shape[1]
    sel = score + bias
    eid = lax.broadcasted_iota(i32, (N_EXPERTS, tm), 0)
    per_group = N_EXPERTS // N_GROUPS
    gs = []
    for g in range(N_GROUPS):
        rows = slice(g * per_group, (g + 1) * per_group)
        v = sel[rows]
        e = g * per_group + lax.broadcasted_iota(i32, (per_group, tm), 0)
        m1 = jnp.max(v, axis=0, keepdims=True)
        i1 = jnp.min(jnp.where(v == m1, e, N_EXPERTS), axis=0, keepdims=True)
        m2 = jnp.max(jnp.where(e == i1, -jnp.inf, v), axis=0, keepdims=True)
        gs.append(m1 + m2)
    cands = []
    for g in range(N_GROUPS):
        rank = jnp.zeros((1, tm), i32)
        for o in range(N_GROUPS):
            if o == g:
                continue
            beats = (gs[o] > gs[g]) | (gs[o] == gs[g]) if o < g else gs[o] > gs[g]
            rank = rank + beats.astype(i32)
        drop = jnp.where(rank < TOPK_GROUPS, 0.0, -jnp.inf)
        cands.append(sel[g * per_group:(g + 1) * per_group] + drop)
    cand = jnp.concatenate(cands, axis=0)
    row_o = lax.broadcasted_iota(i32, (TOP_K, tm), 0)
    idx_out = jnp.zeros((TOP_K, tm), i32)
    wt_out = jnp.zeros((TOP_K, tm), f32)
    picked = jnp.zeros((N_EXPERTS, tm), f32)
    w_sum = jnp.zeros((1, tm), f32)
    for k in range(TOP_K):
        mx = jnp.max(cand, axis=0, keepdims=True)
        ik = jnp.min(jnp.where(cand == mx, eid, N_EXPERTS), axis=0, keepdims=True)
        pick = eid == ik
        wk = jnp.sum(jnp.where(pick, score, 0.0), axis=0, keepdims=True)
        cand = jnp.where(pick, -jnp.inf, cand)
        picked = jnp.where(pick, 1.0, picked)
        idx_out = jnp.where(row_o == k, ik, idx_out)
        wt_out = jnp.where(row_o == k, wk, wt_out)
        w_sum = w_sum + wk
    return idx_out, wt_out / w_sum * ROUTED_SCALE, jnp.sum(picked, axis=1, keepdims=True)


def _mix_route(attn, conv, x2, mod, w_out, ln_g, ln_b, w_router, router_bias, s, tm):
    t, d = x2.shape
    per = s // tm
    da = attn.shape[1]
    pack_rows = d // 2 // LANES
    row = lambda a: a.reshape(1, -1)
    tile = lambda w: pl.BlockSpec((tm, w), lambda i: (i, 0))
    full = lambda a: pl.BlockSpec(a.shape, lambda i: (0,) * a.ndim)
    assert tm % ROUTE_HALF == 0
    wr_hi = w_router.T.astype(bf16)
    wr_lo = (w_router.T - wr_hi.astype(f32)).astype(bf16)
    args = (attn, conv, x2, mod, w_out.astype(bf16), row(ln_g), row(ln_b), wr_hi, wr_lo,
            router_bias.reshape(N_EXPERTS, 1))
    per_token = lambda rows: pl.BlockSpec((rows, tm), lambda i: (0, i))
    return pl.pallas_call(
        _mix_route_kernel,
        grid=(t // tm,),
        in_specs=[tile(da), tile(conv.shape[1]), tile(d),
                  pl.BlockSpec((1, 6, d), lambda i: (i // per, 0, 0))] + [full(a) for a in args[4:]],
        out_specs=[tile(d), pl.BlockSpec((tm * pack_rows, LANES), lambda i: (i, 0)),
                   per_token(TOP_K), per_token(TOP_K), pl.BlockSpec((N_EXPERTS, 1), lambda i: (0, 0))],
        out_shape=[jax.ShapeDtypeStruct((t, d), f32),
                   jax.ShapeDtypeStruct((t * pack_rows, LANES), u32),
                   jax.ShapeDtypeStruct((TOP_K, t), i32), jax.ShapeDtypeStruct((TOP_K, t), f32),
                   jax.ShapeDtypeStruct((N_EXPERTS, 1), f32)],
        compiler_params=_params(("arbitrary",)),
        name="mix_route",
    )(*args)


def _positions_kernel(idx_ref, start_ref, o_ref, run_ref):
    step = pl.program_id(0)
    tm = idx_ref.shape[1]

    @pl.when(step == 0)
    def _():
        run_ref[...] = jnp.zeros_like(run_ref)

    idx = idx_ref[...]
    eid = lax.broadcasted_iota(i32, (N_EXPERTS, tm), 0)
    onehot = jnp.zeros((N_EXPERTS, tm), f32)
    for k in range(TOP_K):
        onehot = jnp.where(eid == idx[k:k + 1], 1.0, onehot)
    r = lax.broadcasted_iota(i32, (tm, tm), 0)
    c = lax.broadcasted_iota(i32, (tm, tm), 1)
    earlier = jnp.where(r < c, 1.0, 0.0).astype(bf16)
    prior = jnp.dot(onehot.astype(bf16), earlier, preferred_element_type=f32)
    pos = prior + run_ref[...] + start_ref[...]
    row_o = lax.broadcasted_iota(i32, (TOP_K, tm), 0)
    out = jnp.zeros((TOP_K, tm), i32)
    for k in range(TOP_K):
        dk = jnp.sum(jnp.where(eid == idx[k:k + 1], pos, 0.0), axis=0, keepdims=True)
        out = jnp.where(row_o == k, dk.astype(i32), out)
    o_ref[...] = out
    run_ref[...] += jnp.sum(onehot, axis=1, keepdims=True)


def _positions(idx, seg_start, tm):
    t = idx.shape[1]
    return pl.pallas_call(
        _positions_kernel,
        grid=(t // tm,),
        in_specs=[pl.BlockSpec((TOP_K, tm), lambda i: (0, i)),
                  pl.BlockSpec((N_EXPERTS, 1), lambda i: (0, 0))],
        out_specs=pl.BlockSpec((TOP_K, tm), lambda i: (0, i)),
        out_shape=jax.ShapeDtypeStruct((TOP_K, t), i32),
        scratch_shapes=[pltpu.VMEM((N_EXPERTS, 1), f32)],
        compiler_params=_params(("arbitrary",)),
        name="positions",
    )(idx, seg_start)


def _dispatch_kernel(dest_hbm, h_ref, o_hbm, dest_smem, sem_i, sem, *, r):
    step = pl.program_id(0)
    n = dest_smem.shape[0]
    load = pltpu.make_async_copy(dest_hbm.at[pl.ds(pl.multiple_of(step * n, n), n)], dest_smem, sem_i)
    load.start()
    load.wait()

    def issue(tok, _):
        src = pl.multiple_of(tok * r, r)
        for k in range(TOP_K):
            dst = pl.multiple_of(dest_smem[tok * TOP_K + k] * r, r)
            pltpu.make_async_copy(h_ref.at[pl.ds(src, r)], o_hbm.at[pl.ds(dst, r)], sem).start(priority=k % 2)
        return 0

    lax.fori_loop(0, n // TOP_K, issue, 0)
    everything = o_hbm.at[pl.ds(0, n * r)]
    pltpu.make_async_copy(everything, everything, sem).wait()


def _dispatch(dest_flat, h2p, n_buf, r, tm):
    t = h2p.shape[0] // r
    return pl.pallas_call(
        functools.partial(_dispatch_kernel, r=r),
        grid=(t // tm,),
        in_specs=[pl.BlockSpec(memory_space=pl.ANY),
                  pl.BlockSpec((tm * r, LANES), lambda i: (i, 0))],
        out_specs=pl.BlockSpec(memory_space=pl.ANY),
        out_shape=jax.ShapeDtypeStruct((n_buf * r, LANES), u32),
        scratch_shapes=[pltpu.SMEM((tm * TOP_K,), i32), pltpu.SemaphoreType.DMA, pltpu.SemaphoreType.DMA],
        compiler_params=_params(("arbitrary",)),
        name="dispatch",
    )(dest_flat, h2p)


def _experts_kernel(be_ref, bv_ref, nu_ref, x_ref, wg_ref, wu_ref, wd_ref, o_ref, wg_s, wu_s, wd_s):
    j = pl.program_id(0)
    prev = be_ref[jnp.maximum(j - 1, 0)]
    used = j < nu_ref[0]
    d = wg_s.shape[0]

    @pl.when(used & ((j == 0) | (be_ref[j] != prev)))
    def _():
        wg_s[...] = wg_ref[0].astype(bf16)
        wu_s[...] = wu_ref[0].astype(bf16)
        wd_s[...] = wd_ref[0].astype(bf16)

    @pl.when(used)
    def _():
        live = lax.broadcasted_iota(i32, (ROW_BLOCK, 1), 0) < bv_ref[j]
        x = jnp.where(live, _load_packed_rows(x_ref, ROW_BLOCK, d), 0.0).astype(bf16)
        hg = jnp.dot(x, wg_s[...], preferred_element_type=f32)
        hu = jnp.dot(x, wu_s[...], preferred_element_type=f32)
        hid = (_silu(hg) * hu).astype(bf16)
        _store_packed_rows(o_ref, jnp.dot(hid, wd_s[...], preferred_element_type=f32))


def _experts(blk_e, blk_valid, n_used, xs, w_gate, w_up, w_down):
    d, f = w_gate.shape[1], w_gate.shape[2]
    r = d // 2 // LANES
    n_blk = xs.shape[0] // r // ROW_BLOCK
    rows = lambda j, be, bv, nu: (jnp.minimum(j, nu[0] - 1), 0)
    wsel = lambda j, be, bv, nu: (be[j], 0, 0)
    return pl.pallas_call(
        _experts_kernel,
        grid_spec=pltpu.PrefetchScalarGridSpec(
            num_scalar_prefetch=3,
            grid=(n_blk,),
            in_specs=[pl.BlockSpec((ROW_BLOCK * r, LANES), rows),
                      pl.BlockSpec((1, d, f), wsel),
                      pl.BlockSpec((1, d, f), wsel),
                      pl.BlockSpec((1, f, d), wsel)],
            out_specs=pl.BlockSpec((ROW_BLOCK * r, LANES), rows),
            scratch_shapes=[pltpu.VMEM((d, f), bf16), pltpu.VMEM((d, f), bf16), pltpu.VMEM((f, d), bf16)]),
        out_shape=jax.ShapeDtypeStruct(xs.shape, u32),
        compiler_params=_params(("arbitrary",)),
        name="experts",
    )(blk_e, blk_valid, n_used, xs, w_gate, w_up, w_down)


def _combine_kernel(dest_hbm, ys_hbm, wt_ref, h_ref, x1_ref, mod_ref, wsg_ref, wsu_ref, wsd_ref,
                    lg_ref, lb_ref, o_ref, dest_smem, rows_ref, sem_d, sem_g):
    step = pl.program_id(0)
    n_steps = pl.num_programs(0)
    tm, d = x1_ref.shape
    r = d // 2 // LANES
    n = tm * TOP_K
    cur = step % 2
    nxt = 1 - cur

    def table_copy(tile, slot):
        return pltpu.make_async_copy(dest_hbm.at[pl.ds(pl.multiple_of(tile * n, n), n)],
                                     dest_smem.at[slot], sem_d.at[slot])

    def start_gathers(slot):
        def issue(tok, _):
            dst = pl.multiple_of(tok * r, r)
            for k in range(TOP_K):
                src = pl.multiple_of(dest_smem[slot, tok * TOP_K + k] * r, r)
                pltpu.make_async_copy(ys_hbm.at[pl.ds(src, r)], rows_ref.at[slot, k, pl.ds(dst, r)],
                                      sem_g.at[slot]).start(priority=k % 2)
            return 0

        lax.fori_loop(0, tm, issue, 0)

    @pl.when(step == 0)
    def _():
        table_copy(0, 0).start()
        table_copy(0, 0).wait()
        start_gathers(0)

        @pl.when(n_steps > 1)
        def _():
            table_copy(1, 1).start()

    @pl.when(step + 1 < n_steps)
    def _():
        table_copy(step + 1, nxt).wait()
        start_gathers(nxt)

        @pl.when(step + 2 < n_steps)
        def _():
            table_copy(step + 2, cur).start()

    h = _load_packed_rows(h_ref, tm, d).astype(bf16)
    hg = jnp.dot(h, wsg_ref[...], preferred_element_type=f32)
    hu = jnp.dot(h, wsu_ref[...], preferred_element_type=f32)
    y = jnp.dot((_silu(hg) * hu).astype(bf16), wsd_ref[...], preferred_element_type=f32)

    pltpu.make_async_copy(rows_ref.at[cur], rows_ref.at[cur], sem_g.at[cur]).wait()
    wt = wt_ref[...]
    lo = [jnp.zeros((tm, LANES), f32) for _ in range(r)]
    hi = [jnp.zeros((tm, LANES), f32) for _ in range(r)]
    for k in range(TOP_K):
        wk = wt[:, k:k + 1]
        for c in range(r):
            pl_, ph_ = _unpack_words(rows_ref[cur, k, pl.ds(c, tm, stride=r), :])
            lo[c] = lo[c] + wk * pl_
            hi[c] = hi[c] + wk * ph_
    y = y + jnp.concatenate(lo + hi, axis=1)
    m = mod_ref[0]
    o_ref[...] = _layer_norm(DEEPNORM_ALPHA * x1_ref[...] + m[5:6] * y, lg_ref[...], lb_ref[...])


def _combine(dest_flat, ys, wts, h2p, x1, mod, w_s_gate, w_s_up, w_s_down, ln_g, ln_b, s, tm):
    t, d = x1.shape
    per = s // tm
    r = d // 2 // LANES
    row = lambda a: a.reshape(1, -1)
    tile = lambda w: pl.BlockSpec((tm, w), lambda i: (i, 0))
    full = lambda a: pl.BlockSpec(a.shape, lambda i: (0,) * a.ndim)
    tail = (w_s_gate.astype(bf16), w_s_up.astype(bf16), w_s_down.astype(bf16), row(ln_g), row(ln_b))
    return pl.pallas_call(
        _combine_kernel,
        grid=(t // tm,),
        in_specs=[pl.BlockSpec(memory_space=pl.ANY), pl.BlockSpec(memory_space=pl.ANY),
                  tile(LANES), pl.BlockSpec((tm * r, LANES), lambda i: (i, 0)), tile(d),
                  pl.BlockSpec((1, 6, d), lambda i: (i // per, 0, 0))] + [full(a) for a in tail],
        out_specs=tile(d),
        out_shape=jax.ShapeDtypeStruct((t, d), f32),
        scratch_shapes=[pltpu.SMEM((2, tm * TOP_K), i32), pltpu.VMEM((2, TOP_K, tm * r, LANES), u32),
                        pltpu.SemaphoreType.DMA((2,)), pltpu.SemaphoreType.DMA((2,))],
        compiler_params=_params(("arbitrary",)),
        name="combine",
    )(dest_flat, ys, wts, h2p, x1, mod, *tail)


def _layer(x, mod, w_in, pe, w_cmp1, w_cmp2, w_dw, b_dw, conv_ln_g, conv_ln_b, w_out, ln1_g, ln1_b,
           w_router, router_bias, w_e_gate, w_e_up, w_e_down, w_s_gate, w_s_up, w_s_down, ln2_g, ln2_b):
    b, s, d = x.shape
    t = b * s
    tm = min(512, s)
    qt, vt, gt, kvc, kn, conv_in = _in_proj(x, mod, w_in, tm)
    cmp_n, cmp_t = _compress(kvc, pe, w_cmp1, w_cmp2)
    gates = gt[:, :3 * N_HEADS].reshape(b, N_KV, 3 * HPG, s)
    gates = jnp.pad(gates, ((0, 0), (0, 0), (0, GATE_GROUP_ROWS - 3 * HPG), (0, 0)))
    attn = _attention(qt, gates, cmp_n, cmp_t, kn, vt)
    conv = _conv(conv_in, w_dw, b_dw, conv_ln_g, conv_ln_b, tm)

    tr = min(256, s)
    x1, h2p, idx, wts, counts = _mix_route(attn.reshape(t, -1), conv.reshape(t, -1), x.reshape(t, d), mod,
                                           w_out, ln1_g, ln1_b, w_router, router_bias, s, tr)
    counts = counts[:, 0].astype(i32)
    padded = (counts + ROW_BLOCK - 1) // ROW_BLOCK * ROW_BLOCK
    seg_end = jnp.cumsum(padded)
    seg_start = seg_end - padded
    n_blk = -(-(t * TOP_K + N_EXPERTS * (ROW_BLOCK - 1)) // ROW_BLOCK)
    blk_row0 = jnp.arange(n_blk, dtype=i32) * ROW_BLOCK
    owns = (blk_row0[:, None] >= seg_start[None, :]) & (blk_row0[:, None] < seg_end[None, :])
    blk_e = jnp.where(blk_row0 < seg_end[-1], jnp.argmax(owns, axis=1), N_EXPERTS - 1).astype(i32)
    live_end = jnp.sum(jnp.where(owns, (seg_start + counts)[None, :], 0), axis=1)
    blk_valid = jnp.clip(live_end - blk_row0, 0, ROW_BLOCK).astype(i32)
    n_used = (seg_end[-1:] // ROW_BLOCK).astype(i32)

    dest = _positions(idx, seg_start.astype(f32).reshape(N_EXPERTS, 1), tr)
    dest_flat = dest.T.reshape(-1)
    wts = jnp.zeros((t, LANES), f32).at[:, :TOP_K].set(wts.T)
    xs = _dispatch(dest_flat, h2p, n_blk * ROW_BLOCK, d // 2 // LANES, min(512, s))
    ys = _experts(blk_e, blk_valid, n_used, xs, w_e_gate, w_e_up, w_e_down)
    out = _combine(dest_flat, ys, wts, h2p, x1, mod, w_s_gate, w_s_up, w_s_down, ln2_g, ln2_b, s, min(256, s))
    return out.reshape(b, s, d)


def kernel(x, c, w_ada, b_ada, w_in, pe_k, pe_v, w_cmp_k1, w_cmp_k2, w_cmp_v1, w_cmp_v2, w_dw, b_dw,
           conv_ln_g, conv_ln_b, w_out, ln1_g, ln1_b, w_router, router_bias, w_e_gate, w_e_up, w_e_down,
           w_s_gate, w_s_up, w_s_down, ln2_g, ln2_b):
    assert w_ada.shape[0] == DEPTH
    layer = lambda a: a.reshape(a.shape[1:])
    mod = _ada(c, layer(w_ada), layer(b_ada))
    return _layer(x, mod, layer(w_in), jnp.concatenate([pe_k, pe_v]),
                  jnp.concatenate([w_cmp_k1, w_cmp_v1]), jnp.concatenate([w_cmp_k2, w_cmp_v2]),
                  *[layer(a) for a in (w_dw, b_dw, conv_ln_g, conv_ln_b, w_out, ln1_g, ln1_b, w_router,
                                       router_bias, w_e_gate, w_e_up, w_e_down, w_s_gate, w_s_up, w_s_down,
                                       ln2_g, ln2_b)])
```

```python
import functools

import jax
import jax.numpy as jnp
from jax import lax
from jax.experimental import pallas as pl
from jax.experimental.pallas import tpu as pltpu

N_HEADS = 8
N_KV = 2
HPG = N_HEADS // N_KV
HEAD_DIM = 64
D_ATTN = N_HEADS * HEAD_DIM
D_KV = N_KV * HEAD_DIM
CONV_WIDTH = 31
CMP_LEN = 32
CMP_STRIDE = 16
CMP_HID = 256
SEL_LEN = 64
SEL_TOPN = 16
WINDOW = 512
Q_BLOCK = 128
N_EXPERTS = 256
TOP_K = 8
N_GROUPS = 8
TOPK_GROUPS = 4
ROUTED_SCALE = 2.5
LN_EPS = 1e-5
DEPTH = 1
DEEPNORM_ALPHA = (2 * DEPTH) ** 0.25

LANES = 128
SUBLANES = 8
ROW_BLOCK = 256
NEG = -1e30
HIGHEST = lax.Precision.HIGHEST
VMEM_LIMIT = 48 * 1024 * 1024

f32 = jnp.float32
bf16 = jnp.bfloat16
i32 = jnp.int32
u32 = jnp.uint32


def _params(sem, vmem=VMEM_LIMIT):
    return pltpu.CompilerParams(dimension_semantics=sem, vmem_limit_bytes=vmem)


def _sigmoid(v):
    return 1.0 / (1.0 + jnp.exp(-v))


def _silu(v):
    return v * _sigmoid(v)


def _layer_norm(v, g, b):
    mu = jnp.mean(v, axis=-1, keepdims=True)
    var = jnp.mean(jnp.square(v - mu), axis=-1, keepdims=True)
    return (v - mu) * lax.rsqrt(var + LN_EPS) * g + b


def _dot_nt(a, b):
    return lax.dot_general(a, b, (((1,), (1,)), ((), ())), preferred_element_type=f32)


def _store_packed_rows(ref, v, first=0):
    n, d = v.shape
    half = d // 2
    bits = lax.bitcast_convert_type(v.astype(bf16).astype(f32), u32)
    words = bits[:, half:] | (bits[:, :half] >> 16)
    r = half // LANES
    for c in range(r):
        ref[pl.ds(first * r + c, n, stride=r), :] = words[:, c * LANES:(c + 1) * LANES]


def _unpack_words(w):
    return (lax.bitcast_convert_type(w << 16, f32),
            lax.bitcast_convert_type(w & jnp.uint32(0xFFFF0000), f32))


def _load_packed_rows(ref, n, d):
    r = d // 2 // LANES
    parts = [_unpack_words(ref[pl.ds(c, n, stride=r), :]) for c in range(r)]
    return jnp.concatenate([p[0] for p in parts] + [p[1] for p in parts], axis=1)


def _ada_kernel(c_ref, w_ref, b_ref, o_ref):
    c = c_ref[...]
    o_ref[...] = jnp.dot(_silu(c), w_ref[...], precision=HIGHEST,
                         preferred_element_type=f32) + b_ref[...]


def _ada(c, w_ada, b_ada):
    b, d = c.shape
    n = w_ada.shape[1]
    rows = 8
    c_pad = jnp.zeros((rows, d), f32).at[:b].set(c)
    tn = 1024
    out = pl.pallas_call(
        _ada_kernel,
        grid=(n // tn,),
        in_specs=[pl.BlockSpec((rows, d), lambda j: (0, 0)),
                  pl.BlockSpec((d, tn), lambda j: (0, j)),
                  pl.BlockSpec((1, tn), lambda j: (0, j))],
        out_specs=pl.BlockSpec((rows, tn), lambda j: (0, j)),
        out_shape=jax.ShapeDtypeStruct((rows, n), f32),
        compiler_params=_params(("arbitrary",)),
        name="ada",
    )(c_pad, w_ada, b_ada.reshape(1, n))
    return out[:b].reshape(b, 6, d)


GATE_ROWS = 32
KEY_TILE = 512
TILE_BLOCKS = KEY_TILE // SEL_LEN
CMP_PER_SEL = SEL_LEN // CMP_STRIDE
CMP_BACK = CMP_LEN // CMP_STRIDE - 1
assert SEL_LEN % CMP_STRIDE == 0 and CMP_LEN % CMP_STRIDE == 0 and CMP_BACK < CMP_PER_SEL
V_ROWS = HEAD_DIM + 16
Q_SCALE = HEAD_DIM ** -0.5 * 1.4426950408889634


def _in_proj_kernel(x_ref, mod_ref, wt_ref, wc_ref, wk_ref, wv_ref,
                    qt_ref, vt_ref, gt_ref, kvc_ref, kn_ref, cv_ref):
    m = mod_ref[0]
    h = (x_ref[0] * (1.0 + m[1:2]) + m[0:1]).astype(bf16)
    res_t = _dot_nt(wt_ref[...], h)
    qt_ref[0] = (res_t[0:D_ATTN] * Q_SCALE).astype(bf16)
    ones = jnp.ones((V_ROWS - HEAD_DIM, res_t.shape[1]), bf16)
    for j in range(2):
        for g in range(N_KV):
            off = D_ATTN + (j * N_KV + g) * HEAD_DIM
            vt_ref[0, j, g, 0:HEAD_DIM, :] = res_t[off:off + HEAD_DIM].astype(bf16)
            vt_ref[0, j, g, HEAD_DIM:V_ROWS, :] = ones
    gt_ref[0] = res_t[D_ATTN + 2 * D_KV:]
    kvc = jnp.dot(h, wc_ref[...], preferred_element_type=f32)
    kvc_ref[0, 0] = kvc[:, :D_KV]
    kvc_ref[0, 1] = kvc[:, D_KV:]
    kn = jnp.dot(h, wk_ref[...], preferred_element_type=f32).astype(bf16)
    tm = kn.shape[0]
    pos = pl.program_id(1) * tm + lax.broadcasted_iota(i32, (tm, LANES - HEAD_DIM), 0)
    lane = lax.broadcasted_iota(i32, (tm, LANES - HEAD_DIM), 1)
    onehot = jnp.where(lane == ((pos >> 6) & (TILE_BLOCKS - 1)), 1.0, 0.0).astype(bf16)
    for j in range(2):
        for g in range(N_KV):
            off = (j * N_KV + g) * HEAD_DIM
            kn_ref[0, j, g] = jnp.concatenate([kn[:, off:off + HEAD_DIM], onehot], axis=1)
    cv_ref[0] = jnp.dot(h, wv_ref[...], preferred_element_type=f32)


def _in_proj(x, mod, w_in, tm):
    b, s, d = x.shape
    o = 0
    wq = w_in[:, o:o + D_ATTN]; o += D_ATTN
    wkc = w_in[:, o:o + 2 * D_KV]; o += 2 * D_KV
    wk_s = w_in[:, o:o + D_KV]; o += D_KV
    wv_s = w_in[:, o:o + D_KV]; o += D_KV
    wk_w = w_in[:, o:o + D_KV]; o += D_KV
    wv_w = w_in[:, o:o + D_KV]; o += D_KV
    wg = w_in[:, o:o + 3 * N_HEADS]; o += 3 * N_HEADS
    wcv = w_in[:, o:]
    d_conv2 = wcv.shape[1]
    wg = jnp.zeros((d, GATE_ROWS), f32).at[:, :3 * N_HEADS].set(wg)
    wt = jnp.concatenate([wq, wv_s, wv_w, wg], axis=1).T
    ws = [w.astype(bf16) for w in (wt, wkc, jnp.concatenate([wk_s, wk_w], axis=1), wcv)]
    full = lambda a: pl.BlockSpec(a.shape, lambda bi, i: (0, 0))
    return pl.pallas_call(
        _in_proj_kernel,
        grid=(b, s // tm),
        in_specs=[pl.BlockSpec((1, tm, d), lambda bi, i: (bi, i, 0)),
                  pl.BlockSpec((1, 6, d), lambda bi, i: (bi, 0, 0))] + [full(w) for w in ws],
        out_specs=[pl.BlockSpec((1, D_ATTN, tm), lambda bi, i: (bi, 0, i)),
                   pl.BlockSpec((1, 2, N_KV, V_ROWS, tm), lambda bi, i: (bi, 0, 0, 0, i)),
                   pl.BlockSpec((1, GATE_ROWS, tm), lambda bi, i: (bi, 0, i)),
                   pl.BlockSpec((1, 2, tm, D_KV), lambda bi, i: (bi, 0, i, 0)),
                   pl.BlockSpec((1, 2, N_KV, tm, LANES), lambda bi, i: (bi, 0, 0, i, 0)),
                   pl.BlockSpec((1, tm, d_conv2), lambda bi, i: (bi, i, 0))],
        out_shape=[jax.ShapeDtypeStruct((b, D_ATTN, s), bf16),
                   jax.ShapeDtypeStruct((b, 2, N_KV, V_ROWS, s), bf16),
                   jax.ShapeDtypeStruct((b, GATE_ROWS, s), f32),
                   jax.ShapeDtypeStruct((b, 2, s, D_KV), f32),
                   jax.ShapeDtypeStruct((b, 2, N_KV, s, LANES), bf16),
                   jax.ShapeDtypeStruct((b, s, d_conv2), f32)],
        compiler_params=_params(("parallel", "parallel")),
        name="in_proj",
    )(x, mod, *ws)


def _compress_kernel(x_ref, pe_ref, w1_ref, w2_ref, w2t_ref, o_ref, ot_ref):
    n_chunk = o_ref.shape[3]
    for j in range(2):
        for g in range(N_KV):
            cols = slice(g * HEAD_DIM, (g + 1) * HEAD_DIM)
            a = jnp.zeros((n_chunk, CMP_HID), f32)
            bm = jnp.zeros((n_chunk, CMP_HID), f32)
            for l in range(CMP_STRIDE):
                xl = x_ref[0, j, pl.ds(l, n_chunk, stride=CMP_STRIDE), :][:, cols]
                a = a + jnp.dot((xl + pe_ref[j, l:l + 1, :]).astype(bf16), w1_ref[j, l],
                                preferred_element_type=f32)
                bm = bm + jnp.dot((xl + pe_ref[j, CMP_STRIDE + l:CMP_STRIDE + l + 1, :]).astype(bf16),
                                  w1_ref[j, CMP_STRIDE + l], preferred_element_type=f32)
            hid = a + pltpu.roll(bm, n_chunk - 1, 0)
            act = 0.5 * hid * (1.0 + jnp.tanh(0.7978845608028654 * (hid + 0.044715 * (hid * hid * hid))))
            act = act.astype(bf16)
            o_ref[0, j, g] = jnp.dot(act, w2_ref[j], preferred_element_type=f32).astype(bf16)
            ot_ref[0, j, g] = _dot_nt(w2t_ref[j], act).astype(bf16)


def _compress(kvc, pe, w1, w2):
    assert CMP_LEN == 2 * CMP_STRIDE
    b, _, s, width = kvc.shape
    n_chunk = s // CMP_STRIDE
    w1b = w1.reshape(2, CMP_LEN, HEAD_DIM, CMP_HID).astype(bf16)
    w2b = w2.astype(bf16)
    w2t = w2b.transpose(0, 2, 1)
    full = lambda a: pl.BlockSpec(a.shape, lambda bi: (0,) * a.ndim)
    return pl.pallas_call(
        _compress_kernel,
        grid=(b,),
        in_specs=[pl.BlockSpec((1, 2, s, width), lambda bi: (bi, 0, 0, 0)),
                  full(pe), full(w1b), full(w2b), full(w2t)],
        out_specs=[pl.BlockSpec((1, 2, N_KV, n_chunk, HEAD_DIM), lambda bi: (bi, 0, 0, 0, 0)),
                   pl.BlockSpec((1, 2, N_KV, HEAD_DIM, n_chunk), lambda bi: (bi, 0, 0, 0, 0))],
        out_shape=[jax.ShapeDtypeStruct((b, 2, N_KV, n_chunk, HEAD_DIM), bf16),
                   jax.ShapeDtypeStruct((b, 2, N_KV, HEAD_DIM, n_chunk), bf16)],
        compiler_params=_params(("parallel",)),
        name="compress",
    )(kvc, pe, w1b, w2b, w2t)


def _attn_kernel(q_ref, g_ref, kc_ref, vc_ref, kn_ref, vt_ref, o_ref, psum_ref, selbias_ref, sa_ref, sb_ref,
                 *, seq, tk, top_n):
    i = pl.program_id(2)
    s0 = i * Q_BLOCK
    n_cmp_rows = kc_ref.shape[3]
    n_sel = seq // SEL_LEN

    q4 = q_ref[0]
    qt = jnp.concatenate([q4[p * HEAD_DIM:(p + 1) * HEAD_DIM, :] for p in range(HPG)], axis=1)
    t_row = s0 + lax.broadcasted_iota(i32, (1, Q_BLOCK), 1)

    s_c = jnp.dot(kc_ref[0, 0, 0], qt, preferred_element_type=f32)
    cmp_end = lax.broadcasted_iota(i32, (n_cmp_rows, 1), 0) * CMP_STRIDE + (CMP_LEN - 1)
    bias_c = jnp.where(cmp_end <= t_row, 0.0, NEG)
    any_c = t_row >= CMP_LEN - 1
    p_sum = jnp.zeros((n_cmp_rows, Q_BLOCK), f32)
    pcs = []
    for p in range(HPG):
        sp = s_c[:, p * Q_BLOCK:(p + 1) * Q_BLOCK] + bias_c
        e = jnp.exp2(sp - jnp.max(sp, axis=0, keepdims=True))
        pn = e * jnp.where(any_c, 1.0 / jnp.sum(e, axis=0, keepdims=True), 0.0)
        p_sum = p_sum + pn
        pcs.append(pn.astype(bf16))
    o_c = jnp.dot(vc_ref[0, 0, 0], jnp.concatenate(pcs, axis=1), preferred_element_type=f32)

    cols = HPG * Q_BLOCK
    pad_rows = LANES - HEAD_DIM
    q_pad = jnp.concatenate([qt, jnp.zeros((pad_rows, cols), bf16)], axis=0)
    span = WINDOW + Q_BLOCK
    w0 = pl.multiple_of(jnp.maximum(s0 - WINDOW, 0), Q_BLOCK)
    s_w = jnp.dot(kn_ref[0, 1, 0, pl.ds(w0, span), :], q_pad, preferred_element_type=f32)
    wpos = w0 + lax.broadcasted_iota(i32, (span, 1), 0)
    bias_w = jnp.where((wpos <= t_row) & (wpos > t_row - WINDOW), 0.0, NEG)
    pws = []
    for p in range(HPG):
        sp = s_w[:, p * Q_BLOCK:(p + 1) * Q_BLOCK] + bias_w
        pws.append(jnp.exp2(sp - jnp.max(sp, axis=0, keepdims=True)).astype(bf16))
    acc_w = jnp.dot(vt_ref[0, 1, 0, :, pl.ds(w0, span)], jnp.concatenate(pws, axis=1),
                    preferred_element_type=f32)
    o_w = acc_w[0:HEAD_DIM] * (1.0 / acc_w[HEAD_DIM:HEAD_DIM + 1])

    psum_ref[...] = p_sum
    blk = lax.broadcasted_iota(i32, (n_sel, Q_BLOCK), 0)
    imp = psum_ref[pl.ds(0, n_sel, stride=CMP_PER_SEL), :]
    for r in range(1, CMP_PER_SEL):
        imp = imp + psum_ref[pl.ds(r, n_sel, stride=CMP_PER_SEL), :]
    for back in range(1, CMP_BACK + 1):
        prev = pltpu.roll(psum_ref[pl.ds(CMP_PER_SEL - back, n_sel, stride=CMP_PER_SEL), :], 1, 0)
        imp = imp + jnp.where(blk >= 1, prev, 0.0)

    cur = t_row >> 6
    forced = (blk == 0) | (blk == cur) | (blk == cur - 1)
    vals = jnp.where(forced, jnp.inf, jnp.where(blk <= cur, imp, -jnp.inf))
    sel = jnp.zeros((n_sel, Q_BLOCK), f32)
    for _ in range(top_n):
        mx = jnp.max(vals, axis=0, keepdims=True)
        first = jnp.min(jnp.where(vals == mx, blk, n_sel), axis=0, keepdims=True)
        pick = blk == first
        sel = jnp.where(pick & (mx > -jnp.inf), 1.0, sel)
        vals = jnp.where(pick, -jnp.inf, vals)
    selbias_ref[0:n_sel] = jnp.where((sel > 0.5) & (blk < (s0 >> 6)), 0.0, NEG)
    selbias_ref[n_sel:n_sel + TILE_BLOCKS] = jnp.full((TILE_BLOCKS, Q_BLOCK), NEG, f32)
    last_tile = seq // tk - 1

    def scores(kt, s_ref):
        k0 = pl.multiple_of(jnp.minimum(kt, last_tile) * tk, tk)
        sb = selbias_ref[pl.ds(pl.multiple_of(kt * TILE_BLOCKS, TILE_BLOCKS), TILE_BLOCKS), :]
        rows = jnp.concatenate([jnp.concatenate([sb] * HPG, axis=1),
                                jnp.zeros((pad_rows - TILE_BLOCKS, cols), f32)], axis=0).astype(bf16)
        s_ref[...] = jnp.dot(kn_ref[0, 0, 0, pl.ds(k0, tk), :], jnp.concatenate([qt, rows], axis=0),
                             preferred_element_type=f32)

    def fold(scores_of_head, v_t, carry):
        m_i, acc = carry
        es, ms, alphas = [], [], []
        for p in range(HPG):
            c = slice(p * Q_BLOCK, (p + 1) * Q_BLOCK)
            sp = scores_of_head(c)
            m_new = jnp.maximum(m_i[:, c], jnp.max(sp, axis=0, keepdims=True))
            es.append(jnp.exp2(sp - m_new).astype(bf16))
            alphas.append(jnp.exp2(m_i[:, c] - m_new))
            ms.append(m_new)
        pv = jnp.dot(v_t, jnp.concatenate(es, axis=1), preferred_element_type=f32)
        return jnp.concatenate(ms, axis=1), jnp.concatenate(alphas, axis=1) * acc + pv

    def sel_tile(kt, s_ref, carry):
        k0 = pl.multiple_of(jnp.minimum(kt, last_tile) * tk, tk)
        return fold(lambda c: s_ref[:, c], vt_ref[0, 0, 0, :, pl.ds(k0, tk)], carry)

    def sel_pair(i, carry):
        scores(2 * i + 1, sb_ref)
        carry = sel_tile(2 * i, sa_ref, carry)
        scores(2 * i + 2, sa_ref)
        return sel_tile(2 * i + 1, sb_ref, carry)

    n_sweep = (s0 + tk - 1) // tk
    init = (jnp.full((1, cols), NEG, f32), jnp.zeros((V_ROWS, cols), f32))
    scores(0, sa_ref)
    carry = lax.fori_loop(0, (n_sweep + 1) // 2, sel_pair, init)
    d0 = pl.multiple_of(s0, Q_BLOCK)
    s_d = jnp.dot(kn_ref[0, 0, 0, pl.ds(d0, Q_BLOCK), :], q_pad, preferred_element_type=f32)
    bias_d = jnp.where(s0 + lax.broadcasted_iota(i32, (Q_BLOCK, 1), 0) <= t_row, 0.0, NEG)
    _, acc_s = fold(lambda c: s_d[:, c] + bias_d, vt_ref[0, 0, 0, :, pl.ds(d0, Q_BLOCK)], carry)
    o_s = acc_s[0:HEAD_DIM] * (1.0 / acc_s[HEAD_DIM:HEAD_DIM + 1])

    gate = _sigmoid(g_ref[0, 0])
    outs = []
    for p in range(HPG):
        c = slice(p * Q_BLOCK, (p + 1) * Q_BLOCK)
        outs.append(gate[3 * p:3 * p + 1] * o_c[:, c] + gate[3 * p + 1:3 * p + 2] * o_s[:, c]
                    + gate[3 * p + 2:3 * p + 3] * o_w[:, c])
    o_ref[0] = jnp.concatenate(outs, axis=0).T.astype(bf16)


GATE_GROUP_ROWS = 16


def _attention(qt, gates, cmp_n, cmp_t, kn, vt):
    b, _, s = qt.shape
    n_chunk = cmp_n.shape[3]
    n_sel = s // SEL_LEN
    top_n = min(SEL_TOPN, n_sel)
    tk = KEY_TILE
    assert s % tk == 0 and n_chunk == n_sel * CMP_PER_SEL
    gw = HPG * HEAD_DIM
    return pl.pallas_call(
        functools.partial(_attn_kernel, seq=s, tk=tk, top_n=top_n),
        grid=(b, N_KV, s // Q_BLOCK),
        in_specs=[pl.BlockSpec((1, gw, Q_BLOCK), lambda bi, g, i: (bi, g, i)),
                  pl.BlockSpec((1, 1, GATE_GROUP_ROWS, Q_BLOCK), lambda bi, g, i: (bi, g, 0, i)),
                  pl.BlockSpec((1, 1, 1, n_chunk, HEAD_DIM), lambda bi, g, i: (bi, 0, g, 0, 0)),
                  pl.BlockSpec((1, 1, 1, HEAD_DIM, n_chunk), lambda bi, g, i: (bi, 1, g, 0, 0)),
                  pl.BlockSpec((1, 2, 1, s, LANES), lambda bi, g, i: (bi, 0, g, 0, 0)),
                  pl.BlockSpec((1, 2, 1, V_ROWS, s), lambda bi, g, i: (bi, 0, g, 0, 0))],
        out_specs=pl.BlockSpec((1, Q_BLOCK, gw), lambda bi, g, i: (bi, i, g)),
        out_shape=jax.ShapeDtypeStruct((b, s, D_ATTN), bf16),
        scratch_shapes=[pltpu.VMEM((n_chunk, Q_BLOCK), f32), pltpu.VMEM((n_sel + TILE_BLOCKS, Q_BLOCK), f32),
                        pltpu.VMEM((tk, HPG * Q_BLOCK), f32), pltpu.VMEM((tk, HPG * Q_BLOCK), f32)],
        compiler_params=_params(("parallel", "parallel", "arbitrary")),
        name="attention",
    )(qt, gates, cmp_n, cmp_t, kn, vt)


CONV_HALO = 32
CONV_ROWS = 32


def _conv_kernel(cur_ref, prev_ref, w_ref, b_ref, g_ref, bb_ref, o_ref, glu_ref):
    i = pl.program_id(1)
    ts = cur_ref.shape[1]
    dc = o_ref.shape[2]
    cur = cur_ref[0]
    prev = prev_ref[0]
    glu_prev = prev[:, :dc] * _sigmoid(prev[:, dc:])
    glu_ref[0:CONV_HALO] = jnp.where(i == 0, 0.0, glu_prev)
    glu_ref[CONV_HALO:CONV_HALO + ts] = cur[:, :dc] * _sigmoid(cur[:, dc:])
    lead = CONV_HALO - (CONV_WIDTH - 1)

    def chunk(r, _):
        r0 = pl.multiple_of(r * CONV_ROWS, CONV_ROWS)
        win = glu_ref[pl.ds(r0, CONV_ROWS + CONV_HALO), :]
        span = CONV_ROWS + CONV_HALO
        acc = jnp.zeros((CONV_ROWS, dc), f32)
        for ph in range(SUBLANES):
            turned = win if ph == 0 else pltpu.roll(win, span - ph, 0)
            for j in range(CONV_WIDTH):
                if (lead + j) % SUBLANES == ph:
                    at = lead + j - ph
                    acc = acc + w_ref[j:j + 1, :] * turned[at:at + CONV_ROWS]
        y = _layer_norm(acc + b_ref[...], g_ref[...], bb_ref[...])
        o_ref[0, pl.ds(r0, CONV_ROWS), :] = _silu(y).astype(bf16)
        return 0

    lax.fori_loop(0, ts // CONV_ROWS, chunk, 0)


def _conv(conv_in, w_dw, b_dw, ln_g, ln_b, ts):
    b, s, dc2 = conv_in.shape
    dc = dc2 // 2
    per = ts // CONV_HALO
    row = lambda a: a.reshape(1, dc)
    return pl.pallas_call(
        _conv_kernel,
        grid=(b, s // ts),
        in_specs=[pl.BlockSpec((1, ts, dc2), lambda bi, i: (bi, i, 0)),
                  pl.BlockSpec((1, CONV_HALO, dc2), lambda bi, i: (bi, jnp.maximum(i * per - 1, 0), 0)),
                  pl.BlockSpec((CONV_WIDTH, dc), lambda bi, i: (0, 0)),
                  pl.BlockSpec((1, dc), lambda bi, i: (0, 0)),
                  pl.BlockSpec((1, dc), lambda bi, i: (0, 0)),
                  pl.BlockSpec((1, dc), lambda bi, i: (0, 0))],
        out_specs=pl.BlockSpec((1, ts, dc), lambda bi, i: (bi, i, 0)),
        out_shape=jax.ShapeDtypeStruct((b, s, dc), bf16),
        scratch_shapes=[pltpu.VMEM((CONV_HALO + ts, dc), f32)],
        compiler_params=_params(("parallel", "parallel")),
        name="conv",
    )(conv_in, conv_in, w_dw.reshape(CONV_WIDTH, dc), row(b_dw), row(ln_g), row(ln_b))


ROUTE_HALF = 128


def _mix_route_kernel(a_ref, cv_ref, x_ref, mod_ref, wo_ref, lg_ref, lb_ref, wrh_ref, wrl_ref, rb_ref,
                      x1_ref, h2p_ref, idx_ref, wt_ref, cnt_ref):
    step = pl.program_id(0)

    @pl.when(step == 0)
    def _():
        cnt_ref[...] = jnp.zeros_like(cnt_ref)

    m = mod_ref[0]
    da = a_ref.shape[1]
    counts = jnp.zeros(cnt_ref.shape, f32)
    for first in range(0, x_ref.shape[0], ROUTE_HALF):
        rows = slice(first, first + ROUTE_HALF)
        mix = (jnp.dot(a_ref[rows, :], wo_ref[0:da, :], preferred_element_type=f32)
               + jnp.dot(cv_ref[rows, :], wo_ref[da:, :], preferred_element_type=f32))
        x1 = _layer_norm(DEEPNORM_ALPHA * x_ref[rows, :] + m[2:3] * mix, lg_ref[...], lb_ref[...])
        x1_ref[rows, :] = x1
        h2 = x1 * (1.0 + m[4:5]) + m[3:4]
        _store_packed_rows(h2p_ref, h2, first)
        h_hi = h2.astype(bf16)
        h_lo = (h2 - h_hi.astype(f32)).astype(bf16)
        logits = _dot_nt(wrh_ref[...], h_hi) + (_dot_nt(wrh_ref[...], h_lo) + _dot_nt(wrl_ref[...], h_hi))
        idx, wt, cnt = _route(_sigmoid(logits), rb_ref[...])
        idx_ref[:, rows] = idx
        wt_ref[:, rows] = wt
        counts = counts + cnt
    cnt_ref[...] += counts


def _route(score, bias):
    tm = score.shape[1]
    sel = score + bias
    eid = lax.broadcasted_iota(i32, (N_EXPERTS, tm), 0)
    per_group = N_EXPERTS // N_GROUPS
    gs = []
    for g in range(N_GROUPS):
        rows = slice(g * per_group, (g + 1) * per_group)
        v = sel[rows]
        e = g * per_group + lax.broadcasted_iota(i32, (per_group, tm), 0)
        m1 = jnp.max(v, axis=0, keepdims=True)
        i1 = jnp.min(jnp.where(v == m1, e, N_EXPERTS), axis=0, keepdims=True)
        m2 = jnp.max(jnp.where(e == i1, -jnp.inf, v), axis=0, keepdims=True)
        gs.append(m1 + m2)
    cands = []
    for g in range(N_GROUPS):
        rank = jnp.zeros((1, tm), i32)
        for o in range(N_GROUPS):
            if o == g:
                continue
            beats = (gs[o] > gs[g]) | (gs[o] == gs[g]) if o < g else gs[o] > gs[g]
            rank = rank + beats.astype(i32)
        drop = jnp.where(rank < TOPK_GROUPS, 0.0, -jnp.inf)
        cands.append(sel[g * per_group:(g + 1) * per_group] + drop)
    cand = jnp.concatenate(cands, axis=0)
    row_o = lax.broadcasted_iota(i32, (TOP_K, tm), 0)
    idx_out = jnp.zeros((TOP_K, tm), i32)
    wt_out = jnp.zeros((TOP_K, tm), f32)
    picked = jnp.zeros((N_EXPERTS, tm), f32)
    w_sum = jnp.zeros((1, tm), f32)
    for k in range(TOP_K):
        mx = jnp.max(cand, axis=0, keepdims=True)
        ik = jnp.min(jnp.where(cand == mx, eid, N_EXPERTS), axis=0, keepdims=True)
        pick = eid == ik
        wk = jnp.sum(jnp.where(pick, score, 0.0), axis=0, keepdims=True)
        cand = jnp.where(pick, -jnp.inf, cand)
        picked = jnp.where(pick, 1.0, picked)
        idx_out = jnp.where(row_o == k, ik, idx_out)
        wt_out = jnp.where(row_o == k, wk, wt_out)
        w_sum = w_sum + wk
    return idx_out, wt_out / w_sum * ROUTED_SCALE, jnp.sum(picked, axis=1, keepdims=True)


def _mix_route(attn, conv, x2, mod, w_out, ln_g, ln_b, w_router, router_bias, s, tm):
    t, d = x2.shape
    per = s // tm
    da = attn.shape[1]
    pack_rows = d // 2 // LANES
    row = lambda a: a.reshape(1, -1)
    tile = lambda w: pl.BlockSpec((tm, w), lambda i: (i, 0))
    full = lambda a: pl.BlockSpec(a.shape, lambda i: (0,) * a.ndim)
    assert tm % ROUTE_HALF == 0
    wr_hi = w_router.T.astype(bf16)
    wr_lo = (w_router.T - wr_hi.astype(f32)).astype(bf16)
    args = (attn, conv, x2, mod, w_out.astype(bf16), row(ln_g), row(ln_b), wr_hi, wr_lo,
            router_bias.reshape(N_EXPERTS, 1))
    per_token = lambda rows: pl.BlockSpec((rows, tm), lambda i: (0, i))
    return pl.pallas_call(
        _mix_route_kernel,
        grid=(t // tm,),
        in_specs=[tile(da), tile(conv.shape[1]), tile(d),
                  pl.BlockSpec((1, 6, d), lambda i: (i // per, 0, 0))] + [full(a) for a in args[4:]],
        out_specs=[tile(d), pl.BlockSpec((tm * pack_rows, LANES), lambda i: (i, 0)),
                   per_token(TOP_K), per_token(TOP_K), pl.BlockSpec((N_EXPERTS, 1), lambda i: (0, 0))],
        out_shape=[jax.ShapeDtypeStruct((t, d), f32),
                   jax.ShapeDtypeStruct((t * pack_rows, LANES), u32),
                   jax.ShapeDtypeStruct((TOP_K, t), i32), jax.ShapeDtypeStruct((TOP_K, t), f32),
                   jax.ShapeDtypeStruct((N_EXPERTS, 1), f32)],
        compiler_params=_params(("arbitrary",)),
        name="mix_route",
    )(*args)


def _positions_kernel(idx_ref, start_ref, o_ref, run_ref):
    step = pl.program_id(0)
    tm = idx_ref.shape[1]

    @pl.when(step == 0)
    def _():
        run_ref[...] = jnp.zeros_like(run_ref)

    idx = idx_ref[...]
    eid = lax.broadcasted_iota(i32, (N_EXPERTS, tm), 0)
    onehot = jnp.zeros((N_EXPERTS, tm), f32)
    for k in range(TOP_K):
        onehot = jnp.where(eid == idx[k:k + 1], 1.0, onehot)
    r = lax.broadcasted_iota(i32, (tm, tm), 0)
    c = lax.broadcasted_iota(i32, (tm, tm), 1)
    earlier = jnp.where(r < c, 1.0, 0.0).astype(bf16)
    prior = jnp.dot(onehot.astype(bf16), earlier, preferred_element_type=f32)
    pos = prior + run_ref[...] + start_ref[...]
    row_o = lax.broadcasted_iota(i32, (TOP_K, tm), 0)
    out = jnp.zeros((TOP_K, tm), i32)
    for k in range(TOP_K):
        dk = jnp.sum(jnp.where(eid == idx[k:k + 1], pos, 0.0), axis=0, keepdims=True)
        out = jnp.where(row_o == k, dk.astype(i32), out)
    o_ref[...] = out
    run_ref[...] += jnp.sum(onehot, axis=1, keepdims=True)


def _positions(idx, seg_start, tm):
    t = idx.shape[1]
    return pl.pallas_call(
        _positions_kernel,
        grid=(t // tm,),
        in_specs=[pl.BlockSpec((TOP_K, tm), lambda i: (0, i)),
                  pl.BlockSpec((N_EXPERTS, 1), lambda i: (0, 0))],
        out_specs=pl.BlockSpec((TOP_K, tm), lambda i: (0, i)),
        out_shape=jax.ShapeDtypeStruct((TOP_K, t), i32),
        scratch_shapes=[pltpu.VMEM((N_EXPERTS, 1), f32)],
        compiler_params=_params(("arbitrary",)),
        name="positions",
    )(idx, seg_start)


def _dispatch_kernel(dest_hbm, h_ref, o_hbm, dest_smem, sem_i, sem, *, r):
    step = pl.program_id(0)
    n = dest_smem.shape[0]
    load = pltpu.make_async_copy(dest_hbm.at[pl.ds(pl.multiple_of(step * n, n), n)], dest_smem, sem_i)
    load.start()
    load.wait()

    def issue(tok, _):
        src = pl.multiple_of(tok * r, r)
        for k in range(TOP_K):
            dst = pl.multiple_of(dest_smem[tok * TOP_K + k] * r, r)
            pltpu.make_async_copy(h_ref.at[pl.ds(src, r)], o_hbm.at[pl.ds(dst, r)], sem).start(priority=k % 2)
        return 0

    lax.fori_loop(0, n // TOP_K, issue, 0)
    everything = o_hbm.at[pl.ds(0, n * r)]
    pltpu.make_async_copy(everything, everything, sem).wait()


def _dispatch(dest_flat, h2p, n_buf, r, tm):
    t = h2p.shape[0] // r
    return pl.pallas_call(
        functools.partial(_dispatch_kernel, r=r),
        grid=(t // tm,),
        in_specs=[pl.BlockSpec(memory_space=pl.ANY),
                  pl.BlockSpec((tm * r, LANES), lambda i: (i, 0))],
        out_specs=pl.BlockSpec(memory_space=pl.ANY),
        out_shape=jax.ShapeDtypeStruct((n_buf * r, LANES), u32),
        scratch_shapes=[pltpu.SMEM((tm * TOP_K,), i32), pltpu.SemaphoreType.DMA, pltpu.SemaphoreType.DMA],
        compiler_params=_params(("arbitrary",)),
        name="dispatch",
    )(dest_flat, h2p)


def _experts_kernel(be_ref, bv_ref, nu_ref, x_ref, wg_ref, wu_ref, wd_ref, o_ref, wg_s, wu_s, wd_s):
    j = pl.program_id(0)
    prev = be_ref[jnp.maximum(j - 1, 0)]
    used = j < nu_ref[0]
    d = wg_s.shape[0]

    @pl.when(used & ((j == 0) | (be_ref[j] != prev)))
    def _():
        wg_s[...] = wg_ref[0].astype(bf16)
        wu_s[...] = wu_ref[0].astype(bf16)
        wd_s[...] = wd_ref[0].astype(bf16)

    @pl.when(used)
    def _():
        live = lax.broadcasted_iota(i32, (ROW_BLOCK, 1), 0) < bv_ref[j]
        x = jnp.where(live, _load_packed_rows(x_ref, ROW_BLOCK, d), 0.0).astype(bf16)
        hg = jnp.dot(x, wg_s[...], preferred_element_type=f32)
        hu = jnp.dot(x, wu_s[...], preferred_element_type=f32)
        hid = (_silu(hg) * hu).astype(bf16)
        _store_packed_rows(o_ref, jnp.dot(hid, wd_s[...], preferred_element_type=f32))


def _experts(blk_e, blk_valid, n_used, xs, w_gate, w_up, w_down):
    d, f = w_gate.shape[1], w_gate.shape[2]
    r = d // 2 // LANES
    n_blk = xs.shape[0] // r // ROW_BLOCK
    rows = lambda j, be, bv, nu: (jnp.minimum(j, nu[0] - 1), 0)
    wsel = lambda j, be, bv, nu: (be[j], 0, 0)
    return pl.pallas_call(
        _experts_kernel,
        grid_spec=pltpu.PrefetchScalarGridSpec(
            num_scalar_prefetch=3,
            grid=(n_blk,),
            in_specs=[pl.BlockSpec((ROW_BLOCK * r, LANES), rows),
                      pl.BlockSpec((1, d, f), wsel),
                      pl.BlockSpec((1, d, f), wsel),
                      pl.BlockSpec((1, f, d), wsel)],
            out_specs=pl.BlockSpec((ROW_BLOCK * r, LANES), rows),
            scratch_shapes=[pltpu.VMEM((d, f), bf16), pltpu.VMEM((d, f), bf16), pltpu.VMEM((f, d), bf16)]),
        out_shape=jax.ShapeDtypeStruct(xs.shape, u32),
        compiler_params=_params(("arbitrary",)),
        name="experts",
    )(blk_e, blk_valid, n_used, xs, w_gate, w_up, w_down)


def _combine_kernel(dest_hbm, ys_hbm, wt_ref, h_ref, x1_ref, mod_ref, wsg_ref, wsu_ref, wsd_ref,
                    lg_ref, lb_ref, o_ref, dest_a, dest_b, rows_a, rows_b, sem_d, sem_g):
    step = pl.program_id(0)
    n_steps = pl.num_programs(0)
    tm, d = x1_ref.shape
    r = d // 2 // LANES
    n = tm * TOP_K
    tables, rows = (dest_a, dest_b), (rows_a, rows_b)

    def table_copy(tile, slot):
        return pltpu.make_async_copy(dest_hbm.at[pl.ds(pl.multiple_of(tile * n, n), n)],
                                     tables[slot], sem_d.at[slot])

    def start_gathers(slot):
        def issue(tok, _):
            dst = pl.multiple_of(tok * r, r)
            for k in range(TOP_K):
                src = pl.multiple_of(tables[slot][tok * TOP_K + k] * r, r)
                pltpu.make_async_copy(ys_hbm.at[pl.ds(src, r)], rows[slot].at[k, pl.ds(dst, r)],
                                      sem_g.at[slot]).start(priority=k % 2)
            return 0

        lax.fori_loop(0, tm, issue, 0)

    @pl.when(step == 0)
    def _():
        table_copy(0, 0).start()
        table_copy(0, 0).wait()
        start_gathers(0)

        @pl.when(n_steps > 1)
        def _():
            table_copy(1, 1).start()

    def tile_body(slot):
        @pl.when(step + 1 < n_steps)
        def _():
            table_copy(step + 1, 1 - slot).wait()
            start_gathers(1 - slot)

            @pl.when(step + 2 < n_steps)
            def _():
                table_copy(step + 2, slot).start()

        h = _load_packed_rows(h_ref, tm, d).astype(bf16)
        hg = jnp.dot(h, wsg_ref[...], preferred_element_type=f32)
        hu = jnp.dot(h, wsu_ref[...], preferred_element_type=f32)
        y = jnp.dot((_silu(hg) * hu).astype(bf16), wsd_ref[...], preferred_element_type=f32)

        pltpu.make_async_copy(rows[slot], rows[slot], sem_g.at[slot]).wait()
        wt = wt_ref[...]
        lo = [jnp.zeros((tm, LANES), f32) for _ in range(r)]
        hi = [jnp.zeros((tm, LANES), f32) for _ in range(r)]
        for k in range(TOP_K):
            wk = wt[:, k:k + 1]
            for c in range(r):
                pl_, ph_ = _unpack_words(rows[slot][k, pl.ds(c, tm, stride=r), :])
                lo[c] = lo[c] + wk * pl_
                hi[c] = hi[c] + wk * ph_
        y = y + jnp.concatenate(lo + hi, axis=1)
        m = mod_ref[0]
        o_ref[...] = _layer_norm(DEEPNORM_ALPHA * x1_ref[...] + m[5:6] * y, lg_ref[...], lb_ref[...])

    for slot in range(2):
        pl.when(step % 2 == slot)(functools.partial(tile_body, slot))


def _combine(dest_flat, ys, wts, h2p, x1, mod, w_s_gate, w_s_up, w_s_down, ln_g, ln_b, s, tm):
    t, d = x1.shape
    per = s // tm
    r = d // 2 // LANES
    row = lambda a: a.reshape(1, -1)
    tile = lambda w: pl.BlockSpec((tm, w), lambda i: (i, 0))
    full = lambda a: pl.BlockSpec(a.shape, lambda i: (0,) * a.ndim)
    tail = (w_s_gate.astype(bf16), w_s_up.astype(bf16), w_s_down.astype(bf16), row(ln_g), row(ln_b))
    return pl.pallas_call(
        _combine_kernel,
        grid=(t // tm,),
        in_specs=[pl.BlockSpec(memory_space=pl.ANY), pl.BlockSpec(memory_space=pl.ANY),
                  tile(LANES), pl.BlockSpec((tm * r, LANES), lambda i: (i, 0)), tile(d),
                  pl.BlockSpec((1, 6, d), lambda i: (i // per, 0, 0))] + [full(a) for a in tail],
        out_specs=tile(d),
        out_shape=jax.ShapeDtypeStruct((t, d), f32),
        scratch_shapes=[pltpu.SMEM((tm * TOP_K,), i32), pltpu.SMEM((tm * TOP_K,), i32),
                        pltpu.VMEM((TOP_K, tm * r, LANES), u32), pltpu.VMEM((TOP_K, tm * r, LANES), u32),
                        pltpu.SemaphoreType.DMA((2,)), pltpu.SemaphoreType.DMA((2,))],
        compiler_params=_params(("arbitrary",)),
        name="combine",
    )(dest_flat, ys, wts, h2p, x1, mod, *tail)


def _layer(x, mod, w_in, pe, w_cmp1, w_cmp2, w_dw, b_dw, conv_ln_g, conv_ln_b, w_out, ln1_g, ln1_b,
           w_router, router_bias, w_e_gate, w_e_up, w_e_down, w_s_gate, w_s_up, w_s_down, ln2_g, ln2_b):
    b, s, d = x.shape
    t = b * s
    tm = min(512, s)
    qt, vt, gt, kvc, kn, conv_in = _in_proj(x, mod, w_in, tm)
    cmp_n, cmp_t = _compress(kvc, pe, w_cmp1, w_cmp2)
    gates = gt[:, :3 * N_HEADS].reshape(b, N_KV, 3 * HPG, s)
    gates = jnp.pad(gates, ((0, 0), (0, 0), (0, GATE_GROUP_ROWS - 3 * HPG), (0, 0)))
    attn = _attention(qt, gates, cmp_n, cmp_t, kn, vt)
    conv = _conv(conv_in, w_dw, b_dw, conv_ln_g, conv_ln_b, tm)

    tr = min(256, s)
    x1, h2p, idx, wts, counts = _mix_route(attn.reshape(t, -1), conv.reshape(t, -1), x.reshape(t, d), mod,
                                           w_out, ln1_g, ln1_b, w_router, router_bias, s, tr)
    counts = counts[:, 0].astype(i32)
    padded = (counts + ROW_BLOCK - 1) // ROW_BLOCK * ROW_BLOCK
    seg_end = jnp.cumsum(padded)
    seg_start = seg_end - padded
    n_blk = -(-(t * TOP_K + N_EXPERTS * (ROW_BLOCK - 1)) // ROW_BLOCK)
    blk_row0 = jnp.arange(n_blk, dtype=i32) * ROW_BLOCK
    owns = (blk_row0[:, None] >= seg_start[None, :]) & (blk_row0[:, None] < seg_end[None, :])
    blk_e = jnp.where(blk_row0 < seg_end[-1], jnp.argmax(owns, axis=1), N_EXPERTS - 1).astype(i32)
    live_end = jnp.sum(jnp.where(owns, (seg_start + counts)[None, :], 0), axis=1)
    blk_valid = jnp.clip(live_end - blk_row0, 0, ROW_BLOCK).astype(i32)
    n_used = (seg_end[-1:] // ROW_BLOCK).astype(i32)

    dest = _positions(idx, seg_start.astype(f32).reshape(N_EXPERTS, 1), tr)
    dest_flat = dest.T.reshape(-1)
    wts = jnp.zeros((t, LANES), f32).at[:, :TOP_K].set(wts.T)
    xs = _dispatch(dest_flat, h2p, n_blk * ROW_BLOCK, d // 2 // LANES, min(512, s))
    ys = _experts(blk_e, blk_valid, n_used, xs, w_e_gate, w_e_up, w_e_down)
    out = _combine(dest_flat, ys, wts, h2p, x1, mod, w_s_gate, w_s_up, w_s_down, ln2_g, ln2_b, s, min(256, s))
    return out.reshape(b, s, d)


def kernel(x, c, w_ada, b_ada, w_in, pe_k, pe_v, w_cmp_k1, w_cmp_k2, w_cmp_v1, w_cmp_v2, w_dw, b_dw,
           conv_ln_g, conv_ln_b, w_out, ln1_g, ln1_b, w_router, router_bias, w_e_gate, w_e_up, w_e_down,
           w_s_gate, w_s_up, w_s_down, ln2_g, ln2_b):
    assert w_ada.shape[0] == DEPTH
    layer = lambda a: a.reshape(a.shape[1:])
    mod = _ada(c, layer(w_ada), layer(b_ada))
    return _layer(x, mod, layer(w_in), jnp.concatenate([pe_k, pe_v]),
                  jnp.concatenate([w_cmp_k1, w_cmp_v1]), jnp.concatenate([w_cmp_k2, w_cmp_v2]),
                  *[layer(a) for a in (w_dw, b_dw, conv_ln_g, conv_ln_b, w_out, ln1_g, ln1_b, w_router,
                                       router_bias, w_e_gate, w_e_up, w_e_down, w_s_gate, w_s_up, w_s_down,
                                       ln2_g, ln2_b)])
```

```python
import functools

import jax
import jax.numpy as jnp
from jax import lax
from jax.experimental import pallas as pl
from jax.experimental.pallas import tpu as pltpu

N_HEADS = 8
N_KV = 2
HPG = N_HEADS // N_KV
HEAD_DIM = 64
D_ATTN = N_HEADS * HEAD_DIM
D_KV = N_KV * HEAD_DIM
CONV_WIDTH = 31
CMP_LEN = 32
CMP_STRIDE = 16
CMP_HID = 256
SEL_LEN = 64
SEL_TOPN = 16
WINDOW = 512
Q_BLOCK = 256
N_EXPERTS = 256
TOP_K = 8
N_GROUPS = 8
TOPK_GROUPS = 4
ROUTED_SCALE = 2.5
LN_EPS = 1e-5
DEPTH = 1
DEEPNORM_ALPHA = (2 * DEPTH) ** 0.25

LANES = 128
SUBLANES = 8
ROW_BLOCK = 256
NEG = -1e30
HIGHEST = lax.Precision.HIGHEST
VMEM_LIMIT = 48 * 1024 * 1024

f32 = jnp.float32
bf16 = jnp.bfloat16
i32 = jnp.int32
u32 = jnp.uint32


def _params(sem, vmem=VMEM_LIMIT):
    return pltpu.CompilerParams(dimension_semantics=sem, vmem_limit_bytes=vmem)


def _sigmoid(v):
    return 1.0 / (1.0 + jnp.exp(-v))


def _silu(v):
    return v * _sigmoid(v)


def _layer_norm(v, g, b):
    mu = jnp.mean(v, axis=-1, keepdims=True)
    var = jnp.mean(jnp.square(v - mu), axis=-1, keepdims=True)
    return (v - mu) * lax.rsqrt(var + LN_EPS) * g + b


def _dot_nt(a, b):
    return lax.dot_general(a, b, (((1,), (1,)), ((), ())), preferred_element_type=f32)


def _store_packed_rows(ref, v, first=0):
    n, d = v.shape
    half = d // 2
    bits = lax.bitcast_convert_type(v.astype(bf16).astype(f32), u32)
    words = bits[:, half:] | (bits[:, :half] >> 16)
    r = half // LANES
    for c in range(r):
        ref[pl.ds(first * r + c, n, stride=r), :] = words[:, c * LANES:(c + 1) * LANES]


def _unpack_words(w):
    return (lax.bitcast_convert_type(w << 16, f32),
            lax.bitcast_convert_type(w & jnp.uint32(0xFFFF0000), f32))


def _load_packed_rows(ref, n, d):
    r = d // 2 // LANES
    parts = [_unpack_words(ref[pl.ds(c, n, stride=r), :]) for c in range(r)]
    return jnp.concatenate([p[0] for p in parts] + [p[1] for p in parts], axis=1)


def _ada_kernel(c_ref, w_ref, b_ref, o_ref):
    c = c_ref[...]
    o_ref[...] = jnp.dot(_silu(c), w_ref[...], precision=HIGHEST,
                         preferred_element_type=f32) + b_ref[...]


def _ada(c, w_ada, b_ada):
    b, d = c.shape
    n = w_ada.shape[1]
    rows = 8
    c_pad = jnp.zeros((rows, d), f32).at[:b].set(c)
    tn = 1024
    out = pl.pallas_call(
        _ada_kernel,
        grid=(n // tn,),
        in_specs=[pl.BlockSpec((rows, d), lambda j: (0, 0)),
                  pl.BlockSpec((d, tn), lambda j: (0, j)),
                  pl.BlockSpec((1, tn), lambda j: (0, j))],
        out_specs=pl.BlockSpec((rows, tn), lambda j: (0, j)),
        out_shape=jax.ShapeDtypeStruct((rows, n), f32),
        compiler_params=_params(("arbitrary",)),
        name="ada",
    )(c_pad, w_ada, b_ada.reshape(1, n))
    return out[:b].reshape(b, 6, d)


GATE_ROWS = 32
KEY_TILE = 512
TILE_BLOCKS = KEY_TILE // SEL_LEN
PATCH_BLOCKS = Q_BLOCK // SEL_LEN
PAIR_UNROLL = 2
CMP_PER_SEL = SEL_LEN // CMP_STRIDE
CMP_BACK = CMP_LEN // CMP_STRIDE - 1
assert SEL_LEN % CMP_STRIDE == 0 and CMP_LEN % CMP_STRIDE == 0 and CMP_BACK < CMP_PER_SEL
V_ROWS = HEAD_DIM + 16
Q_SCALE = HEAD_DIM ** -0.5 * 1.4426950408889634


def _in_proj_kernel(x_ref, mod_ref, wt_ref, wc_ref, wk_ref, wv_ref,
                    qt_ref, vt_ref, gt_ref, kvc_ref, kn_ref, cv_ref):
    m = mod_ref[0]
    h = (x_ref[0] * (1.0 + m[1:2]) + m[0:1]).astype(bf16)
    res_t = _dot_nt(wt_ref[...], h)
    qt_ref[0] = (res_t[0:D_ATTN] * Q_SCALE).astype(bf16)
    ones = jnp.ones((V_ROWS - HEAD_DIM, res_t.shape[1]), bf16)
    for j in range(2):
        for g in range(N_KV):
            off = D_ATTN + (j * N_KV + g) * HEAD_DIM
            vt_ref[0, j, g, 0:HEAD_DIM, :] = res_t[off:off + HEAD_DIM].astype(bf16)
            vt_ref[0, j, g, HEAD_DIM:V_ROWS, :] = ones
    gt_ref[0] = res_t[D_ATTN + 2 * D_KV:]
    kvc = jnp.dot(h, wc_ref[...], preferred_element_type=f32)
    kvc_ref[0, 0] = kvc[:, :D_KV]
    kvc_ref[0, 1] = kvc[:, D_KV:]
    kn = jnp.dot(h, wk_ref[...], preferred_element_type=f32).astype(bf16)
    tm = kn.shape[0]
    pos = pl.program_id(1) * tm + lax.broadcasted_iota(i32, (tm, LANES - HEAD_DIM), 0)
    lane = lax.broadcasted_iota(i32, (tm, LANES - HEAD_DIM), 1)
    onehot = jnp.where(lane == ((pos >> 6) & (TILE_BLOCKS - 1)), 1.0, 0.0).astype(bf16)
    for j in range(2):
        for g in range(N_KV):
            off = (j * N_KV + g) * HEAD_DIM
            kn_ref[0, j, g] = jnp.concatenate([kn[:, off:off + HEAD_DIM], onehot], axis=1)
    cv_ref[0] = jnp.dot(h, wv_ref[...], preferred_element_type=f32)


def _in_proj(x, mod, w_in, tm):
    b, s, d = x.shape
    o = 0
    wq = w_in[:, o:o + D_ATTN]; o += D_ATTN
    wkc = w_in[:, o:o + 2 * D_KV]; o += 2 * D_KV
    wk_s = w_in[:, o:o + D_KV]; o += D_KV
    wv_s = w_in[:, o:o + D_KV]; o += D_KV
    wk_w = w_in[:, o:o + D_KV]; o += D_KV
    wv_w = w_in[:, o:o + D_KV]; o += D_KV
    wg = w_in[:, o:o + 3 * N_HEADS]; o += 3 * N_HEADS
    wcv = w_in[:, o:]
    d_conv2 = wcv.shape[1]
    wg = jnp.zeros((d, GATE_ROWS), f32).at[:, :3 * N_HEADS].set(wg)
    wt = jnp.concatenate([wq, wv_s, wv_w, wg], axis=1).T
    ws = [w.astype(bf16) for w in (wt, wkc, jnp.concatenate([wk_s, wk_w], axis=1), wcv)]
    full = lambda a: pl.BlockSpec(a.shape, lambda bi, i: (0, 0))
    return pl.pallas_call(
        _in_proj_kernel,
        grid=(b, s // tm),
        in_specs=[pl.BlockSpec((1, tm, d), lambda bi, i: (bi, i, 0)),
                  pl.BlockSpec((1, 6, d), lambda bi, i: (bi, 0, 0))] + [full(w) for w in ws],
        out_specs=[pl.BlockSpec((1, D_ATTN, tm), lambda bi, i: (bi, 0, i)),
                   pl.BlockSpec((1, 2, N_KV, V_ROWS, tm), lambda bi, i: (bi, 0, 0, 0, i)),
                   pl.BlockSpec((1, GATE_ROWS, tm), lambda bi, i: (bi, 0, i)),
                   pl.BlockSpec((1, 2, tm, D_KV), lambda bi, i: (bi, 0, i, 0)),
                   pl.BlockSpec((1, 2, N_KV, tm, LANES), lambda bi, i: (bi, 0, 0, i, 0)),
                   pl.BlockSpec((1, tm, d_conv2), lambda bi, i: (bi, i, 0))],
        out_shape=[jax.ShapeDtypeStruct((b, D_ATTN, s), bf16),
                   jax.ShapeDtypeStruct((b, 2, N_KV, V_ROWS, s), bf16),
                   jax.ShapeDtypeStruct((b, GATE_ROWS, s), f32),
                   jax.ShapeDtypeStruct((b, 2, s, D_KV), f32),
                   jax.ShapeDtypeStruct((b, 2, N_KV, s, LANES), bf16),
                   jax.ShapeDtypeStruct((b, s, d_conv2), f32)],
        compiler_params=_params(("parallel", "parallel")),
        name="in_proj",
    )(x, mod, *ws)


def _compress_kernel(x_ref, pe_ref, w1_ref, w2_ref, w2t_ref, o_ref, ot_ref):
    n_chunk = o_ref.shape[3]
    for j in range(2):
        for g in range(N_KV):
            cols = slice(g * HEAD_DIM, (g + 1) * HEAD_DIM)
            a = jnp.zeros((n_chunk, CMP_HID), f32)
            bm = jnp.zeros((n_chunk, CMP_HID), f32)
            for l in range(CMP_STRIDE):
                xl = x_ref[0, j, pl.ds(l, n_chunk, stride=CMP_STRIDE), :][:, cols]
                a = a + jnp.dot((xl + pe_ref[j, l:l + 1, :]).astype(bf16), w1_ref[j, l],
                                preferred_element_type=f32)
                bm = bm + jnp.dot((xl + pe_ref[j, CMP_STRIDE + l:CMP_STRIDE + l + 1, :]).astype(bf16),
                                  w1_ref[j, CMP_STRIDE + l], preferred_element_type=f32)
            hid = a + pltpu.roll(bm, n_chunk - 1, 0)
            act = 0.5 * hid * (1.0 + jnp.tanh(0.7978845608028654 * (hid + 0.044715 * (hid * hid * hid))))
            act = act.astype(bf16)
            o_ref[0, j, g] = jnp.dot(act, w2_ref[j], preferred_element_type=f32).astype(bf16)
            ot_ref[0, j, g] = _dot_nt(w2t_ref[j], act).astype(bf16)


def _compress(kvc, pe, w1, w2):
    assert CMP_LEN == 2 * CMP_STRIDE
    b, _, s, width = kvc.shape
    n_chunk = s // CMP_STRIDE
    w1b = w1.reshape(2, CMP_LEN, HEAD_DIM, CMP_HID).astype(bf16)
    w2b = w2.astype(bf16)
    w2t = w2b.transpose(0, 2, 1)
    full = lambda a: pl.BlockSpec(a.shape, lambda bi: (0,) * a.ndim)
    return pl.pallas_call(
        _compress_kernel,
        grid=(b,),
        in_specs=[pl.BlockSpec((1, 2, s, width), lambda bi: (bi, 0, 0, 0)),
                  full(pe), full(w1b), full(w2b), full(w2t)],
        out_specs=[pl.BlockSpec((1, 2, N_KV, n_chunk, HEAD_DIM), lambda bi: (bi, 0, 0, 0, 0)),
                   pl.BlockSpec((1, 2, N_KV, HEAD_DIM, n_chunk), lambda bi: (bi, 0, 0, 0, 0))],
        out_shape=[jax.ShapeDtypeStruct((b, 2, N_KV, n_chunk, HEAD_DIM), bf16),
                   jax.ShapeDtypeStruct((b, 2, N_KV, HEAD_DIM, n_chunk), bf16)],
        compiler_params=_params(("parallel",)),
        name="compress",
    )(kvc, pe, w1b, w2b, w2t)


def _attn_kernel(q_ref, g_ref, kc_ref, vc_ref, kn_ref, vt_ref, o_ref, psum_ref, selbias_ref, sa_ref, sb_ref,
                 *, seq, tk, top_n):
    i = pl.program_id(2)
    s0 = i * Q_BLOCK
    n_cmp_rows = kc_ref.shape[3]
    n_sel = seq // SEL_LEN

    q4 = q_ref[0]
    qt = jnp.concatenate([q4[p * HEAD_DIM:(p + 1) * HEAD_DIM, :] for p in range(HPG)], axis=1)
    t_row = s0 + lax.broadcasted_iota(i32, (1, Q_BLOCK), 1)

    s_c = jnp.dot(kc_ref[0, 0, 0], qt, preferred_element_type=f32)
    cmp_end = lax.broadcasted_iota(i32, (n_cmp_rows, 1), 0) * CMP_STRIDE + (CMP_LEN - 1)
    bias_c = jnp.where(cmp_end <= t_row, 0.0, NEG)
    any_c = t_row >= CMP_LEN - 1
    p_sum = jnp.zeros((n_cmp_rows, Q_BLOCK), f32)
    pcs = []
    for p in range(HPG):
        sp = s_c[:, p * Q_BLOCK:(p + 1) * Q_BLOCK] + bias_c
        e = jnp.exp2(sp - jnp.max(sp, axis=0, keepdims=True))
        pn = e * jnp.where(any_c, 1.0 / jnp.sum(e, axis=0, keepdims=True), 0.0)
        p_sum = p_sum + pn
        pcs.append(pn.astype(bf16))
    o_c = jnp.dot(vc_ref[0, 0, 0], jnp.concatenate(pcs, axis=1), preferred_element_type=f32)

    cols = HPG * Q_BLOCK
    pad_rows = LANES - HEAD_DIM
    q_pad = jnp.concatenate([qt, jnp.zeros((pad_rows, cols), bf16)], axis=0)
    span = WINDOW + Q_BLOCK
    w0 = pl.multiple_of(jnp.maximum(s0 - WINDOW, 0), Q_BLOCK)
    s_w = jnp.dot(kn_ref[0, 1, 0, pl.ds(w0, span), :], q_pad, preferred_element_type=f32)
    wpos = w0 + lax.broadcasted_iota(i32, (span, 1), 0)
    bias_w = jnp.where((wpos <= t_row) & (wpos > t_row - WINDOW), 0.0, NEG)
    pws = []
    for p in range(HPG):
        sp = s_w[:, p * Q_BLOCK:(p + 1) * Q_BLOCK] + bias_w
        pws.append(jnp.exp2(sp - jnp.max(sp, axis=0, keepdims=True)).astype(bf16))
    acc_w = jnp.dot(vt_ref[0, 1, 0, :, pl.ds(w0, span)], jnp.concatenate(pws, axis=1),
                    preferred_element_type=f32)
    o_w = acc_w[0:HEAD_DIM] * (1.0 / acc_w[HEAD_DIM:HEAD_DIM + 1])

    blk = lax.broadcasted_iota(i32, (n_sel, Q_BLOCK), 0)
    for lt in range(Q_BLOCK // LANES):
        psum_ref[lt] = p_sum[:, lt * LANES:(lt + 1) * LANES]

    def every(first):
        return jnp.concatenate([psum_ref[lt, pl.ds(first, n_sel, stride=CMP_PER_SEL), :]
                                for lt in range(Q_BLOCK // LANES)], axis=1)

    imp = every(0)
    for r in range(1, CMP_PER_SEL):
        imp = imp + every(r)
    for back in range(1, CMP_BACK + 1):
        imp = imp + jnp.where(blk >= 1, pltpu.roll(every(CMP_PER_SEL - back), 1, 0), 0.0)

    cur = t_row >> 6
    forced = (blk == 0) | (blk == cur) | (blk == cur - 1)
    vals = jnp.where(forced, jnp.inf, jnp.where(blk <= cur, imp, -jnp.inf))
    sel = jnp.zeros((n_sel, Q_BLOCK), f32)
    for _ in range(top_n):
        mx = jnp.max(vals, axis=0, keepdims=True)
        first = jnp.min(jnp.where(vals == mx, blk, n_sel), axis=0, keepdims=True)
        pick = blk == first
        sel = jnp.where(pick & (mx > -jnp.inf), 1.0, sel)
        vals = jnp.where(pick, -jnp.inf, vals)
    first_blk = s0 >> 6
    full_at = n_sel + TILE_BLOCKS
    selbias_ref[0:n_sel] = jnp.where((sel > 0.5) & (blk < first_blk), 0.0, NEG)
    selbias_ref[n_sel:full_at] = jnp.full((TILE_BLOCKS, Q_BLOCK), NEG, f32)
    selbias_ref[full_at:full_at + n_sel] = jnp.where(sel > 0.5, 0.0, NEG)
    last_tile = seq // tk - 1

    def scores(kt, s_ref):
        k0 = pl.multiple_of(jnp.minimum(kt, last_tile) * tk, tk)
        sb = selbias_ref[pl.ds(pl.multiple_of(kt * TILE_BLOCKS, TILE_BLOCKS), TILE_BLOCKS), :]
        rows = jnp.concatenate([jnp.concatenate([sb] * HPG, axis=1),
                                jnp.zeros((pad_rows - TILE_BLOCKS, cols), f32)], axis=0).astype(bf16)
        s_ref[...] = jnp.dot(kn_ref[0, 0, 0, pl.ds(k0, tk), :], jnp.concatenate([qt, rows], axis=0),
                             preferred_element_type=f32)

    def fold(scores_of_head, v_t, carry):
        m_i, acc = carry
        es, ms, alphas = [], [], []
        for p in range(HPG):
            c = slice(p * Q_BLOCK, (p + 1) * Q_BLOCK)
            sp = scores_of_head(c)
            m_new = jnp.maximum(m_i[:, c], jnp.max(sp, axis=0, keepdims=True))
            es.append(jnp.exp2(sp - m_new).astype(bf16))
            alphas.append(jnp.exp2(m_i[:, c] - m_new))
            ms.append(m_new)
        pv = jnp.dot(v_t, jnp.concatenate(es, axis=1), preferred_element_type=f32)
        return jnp.concatenate(ms, axis=1), jnp.concatenate(alphas, axis=1) * acc + pv

    def sel_tile(kt, s_ref, carry):
        k0 = pl.multiple_of(jnp.minimum(kt, last_tile) * tk, tk)
        return fold(lambda c: s_ref[:, c], vt_ref[0, 0, 0, :, pl.ds(k0, tk)], carry)

    def sel_pair(i, carry):
        scores(2 * i + 1, sb_ref)
        carry = sel_tile(2 * i, sa_ref, carry)
        scores(2 * i + 2, sa_ref)
        return sel_tile(2 * i + 1, sb_ref, carry)

    n_sweep = (s0 + tk - 1) // tk
    n_pairs = (n_sweep + 1) // 2
    init = (jnp.full((1, cols), NEG, f32), jnp.zeros((V_ROWS, cols), f32))
    scores(0, sa_ref)
    def sel_pairs(j, carry):
        for u in range(PAIR_UNROLL):
            carry = sel_pair(PAIR_UNROLL * j + u, carry)
        return carry

    carry = lax.fori_loop(0, n_pairs // PAIR_UNROLL, sel_pairs, init)
    carry = lax.fori_loop(n_pairs // PAIR_UNROLL * PAIR_UNROLL, n_pairs, sel_pair, carry)
    d0 = pl.multiple_of(s0, Q_BLOCK)
    s_d = jnp.dot(kn_ref[0, 0, 0, pl.ds(d0, Q_BLOCK), :], q_pad, preferred_element_type=f32)
    group = selbias_ref[pl.ds(full_at + pl.multiple_of((first_blk >> 3) << 3, SUBLANES), SUBLANES), :]
    own = group[0:PATCH_BLOCKS]
    for at in range(PATCH_BLOCKS, SUBLANES, PATCH_BLOCKS):
        own = jnp.where((first_blk & (SUBLANES - 1)) == at, group[at:at + PATCH_BLOCKS], own)
    bias_d = jnp.concatenate([jnp.broadcast_to(own[j:j + 1], (SEL_LEN, Q_BLOCK)) for j in range(PATCH_BLOCKS)],
                             axis=0)
    bias_d = jnp.where(s0 + lax.broadcasted_iota(i32, (Q_BLOCK, 1), 0) <= t_row, bias_d, NEG)
    _, acc_s = fold(lambda c: s_d[:, c] + bias_d, vt_ref[0, 0, 0, :, pl.ds(d0, Q_BLOCK)], carry)
    o_s = acc_s[0:HEAD_DIM] * (1.0 / acc_s[HEAD_DIM:HEAD_DIM + 1])

    gate = _sigmoid(g_ref[0, 0])
    outs = []
    for p in range(HPG):
        c = slice(p * Q_BLOCK, (p + 1) * Q_BLOCK)
        outs.append(gate[3 * p:3 * p + 1] * o_c[:, c] + gate[3 * p + 1:3 * p + 2] * o_s[:, c]
                    + gate[3 * p + 2:3 * p + 3] * o_w[:, c])
    o_ref[0] = jnp.concatenate(outs, axis=0).T.astype(bf16)


GATE_GROUP_ROWS = 16


def _attention(qt, gates, cmp_n, cmp_t, kn, vt):
    b, _, s = qt.shape
    n_chunk = cmp_n.shape[3]
    n_sel = s // SEL_LEN
    top_n = min(SEL_TOPN, n_sel)
    tk = KEY_TILE
    assert s % tk == 0 and n_chunk == n_sel * CMP_PER_SEL
    gw = HPG * HEAD_DIM
    return pl.pallas_call(
        functools.partial(_attn_kernel, seq=s, tk=tk, top_n=top_n),
        grid=(b, N_KV, s // Q_BLOCK),
        in_specs=[pl.BlockSpec((1, gw, Q_BLOCK), lambda bi, g, i: (bi, g, i)),
                  pl.BlockSpec((1, 1, GATE_GROUP_ROWS, Q_BLOCK), lambda bi, g, i: (bi, g, 0, i)),
                  pl.BlockSpec((1, 1, 1, n_chunk, HEAD_DIM), lambda bi, g, i: (bi, 0, g, 0, 0)),
                  pl.BlockSpec((1, 1, 1, HEAD_DIM, n_chunk), lambda bi, g, i: (bi, 1, g, 0, 0)),
                  pl.BlockSpec((1, 2, 1, s, LANES), lambda bi, g, i: (bi, 0, g, 0, 0)),
                  pl.BlockSpec((1, 2, 1, V_ROWS, s), lambda bi, g, i: (bi, 0, g, 0, 0))],
        out_specs=pl.BlockSpec((1, Q_BLOCK, gw), lambda bi, g, i: (bi, i, g)),
        out_shape=jax.ShapeDtypeStruct((b, s, D_ATTN), bf16),
        scratch_shapes=[pltpu.VMEM((Q_BLOCK // LANES, n_chunk, LANES), f32),
                        pltpu.VMEM((2 * n_sel + TILE_BLOCKS, Q_BLOCK), f32),
                        pltpu.VMEM((tk, HPG * Q_BLOCK), f32), pltpu.VMEM((tk, HPG * Q_BLOCK), f32)],
        compiler_params=_params(("parallel", "parallel", "arbitrary")),
        name="attention",
    )(qt, gates, cmp_n, cmp_t, kn, vt)


CONV_HALO = 32
CONV_ROWS = 32


def _conv_kernel(cur_ref, prev_ref, w_ref, b_ref, g_ref, bb_ref, o_ref, glu_ref):
    i = pl.program_id(1)
    ts = cur_ref.shape[1]
    dc = o_ref.shape[2]
    cur = cur_ref[0]
    prev = prev_ref[0]
    glu_prev = prev[:, :dc] * _sigmoid(prev[:, dc:])
    glu_ref[0:CONV_HALO] = jnp.where(i == 0, 0.0, glu_prev)
    glu_ref[CONV_HALO:CONV_HALO + ts] = cur[:, :dc] * _sigmoid(cur[:, dc:])
    lead = CONV_HALO - (CONV_WIDTH - 1)

    def chunk(r, _):
        r0 = pl.multiple_of(r * CONV_ROWS, CONV_ROWS)
        win = glu_ref[pl.ds(r0, CONV_ROWS + CONV_HALO), :]
        span = CONV_ROWS + CONV_HALO
        acc = jnp.zeros((CONV_ROWS, dc), f32)
        for ph in range(SUBLANES):
            turned = win if ph == 0 else pltpu.roll(win, span - ph, 0)
            for j in range(CONV_WIDTH):
                if (lead + j) % SUBLANES == ph:
                    at = lead + j - ph
                    acc = acc + w_ref[j:j + 1, :] * turned[at:at + CONV_ROWS]
        y = _layer_norm(acc + b_ref[...], g_ref[...], bb_ref[...])
        o_ref[0, pl.ds(r0, CONV_ROWS), :] = _silu(y).astype(bf16)
        return 0

    lax.fori_loop(0, ts // CONV_ROWS, chunk, 0)


def _conv(conv_in, w_dw, b_dw, ln_g, ln_b, ts):
    b, s, dc2 = conv_in.shape
    dc = dc2 // 2
    per = ts // CONV_HALO
    row = lambda a: a.reshape(1, dc)
    return pl.pallas_call(
        _conv_kernel,
        grid=(b, s // ts),
        in_specs=[pl.BlockSpec((1, ts, dc2), lambda bi, i: (bi, i, 0)),
                  pl.BlockSpec((1, CONV_HALO, dc2), lambda bi, i: (bi, jnp.maximum(i * per - 1, 0), 0)),
                  pl.BlockSpec((CONV_WIDTH, dc), lambda bi, i: (0, 0)),
                  pl.BlockSpec((1, dc), lambda bi, i: (0, 0)),
                  pl.BlockSpec((1, dc), lambda bi, i: (0, 0)),
                  pl.BlockSpec((1, dc), lambda bi, i: (0, 0))],
        out_specs=pl.BlockSpec((1, ts, dc), lambda bi, i: (bi, i, 0)),
        out_shape=jax.ShapeDtypeStruct((b, s, dc), bf16),
        scratch_shapes=[pltpu.VMEM((CONV_HALO + ts, dc), f32)],
        compiler_params=_params(("parallel", "parallel")),
        name="conv",
    )(conv_in, conv_in, w_dw.reshape(CONV_WIDTH, dc), row(b_dw), row(ln_g), row(ln_b))


ROUTE_HALF = 128


def _mix_route_kernel(a_ref, cv_ref, x_ref, mod_ref, wo_ref, lg_ref, lb_ref, wrh_ref, wrl_ref, rb_ref,
                      x1_ref, h2p_ref, idx_ref, wt_ref, cnt_ref):
    step = pl.program_id(0)

    @pl.when(step == 0)
    def _():
        cnt_ref[...] = jnp.zeros_like(cnt_ref)

    m = mod_ref[0]
    da = a_ref.shape[1]
    counts = jnp.zeros(cnt_ref.shape, f32)
    for first in range(0, x_ref.shape[0], ROUTE_HALF):
        rows = slice(first, first + ROUTE_HALF)
        mix = (jnp.dot(a_ref[rows, :], wo_ref[0:da, :], preferred_element_type=f32)
               + jnp.dot(cv_ref[rows, :], wo_ref[da:, :], preferred_element_type=f32))
        x1 = _layer_norm(DEEPNORM_ALPHA * x_ref[rows, :] + m[2:3] * mix, lg_ref[...], lb_ref[...])
        x1_ref[rows, :] = x1
        h2 = x1 * (1.0 + m[4:5]) + m[3:4]
        _store_packed_rows(h2p_ref, h2, first)
        h_hi = h2.astype(bf16)
        h_lo = (h2 - h_hi.astype(f32)).astype(bf16)
        logits = _dot_nt(wrh_ref[...], h_hi) + (_dot_nt(wrh_ref[...], h_lo) + _dot_nt(wrl_ref[...], h_hi))
        idx, wt, cnt = _route(_sigmoid(logits), rb_ref[...])
        idx_ref[:, rows] = idx
        wt_ref[:, rows] = wt
        counts = counts + cnt
    cnt_ref[...] += counts


def _route(score, bias):
    tm = score.shape[1]
    sel = score + bias
    eid = lax.broadcasted_iota(i32, (N_EXPERTS, tm), 0)
    per_group = N_EXPERTS // N_GROUPS
    gs = []
    for g in range(N_GROUPS):
        rows = slice(g * per_group, (g + 1) * per_group)
        v = sel[rows]
        e = g * per_group + lax.broadcasted_iota(i32, (per_group, tm), 0)
        m1 = jnp.max(v, axis=0, keepdims=True)
        i1 = jnp.min(jnp.where(v == m1, e, N_EXPERTS), axis=0, keepdims=True)
        m2 = jnp.max(jnp.where(e == i1, -jnp.inf, v), axis=0, keepdims=True)
        gs.append(m1 + m2)
    cands = []
    for g in range(N_GROUPS):
        rank = jnp.zeros((1, tm), i32)
        for o in range(N_GROUPS):
            if o == g:
                continue
            beats = (gs[o] > gs[g]) | (gs[o] == gs[g]) if o < g else gs[o] > gs[g]
            rank = rank + beats.astype(i32)
        drop = jnp.where(rank < TOPK_GROUPS, 0.0, -jnp.inf)
        cands.append(sel[g * per_group:(g + 1) * per_group] + drop)
    cand = jnp.concatenate(cands, axis=0)
    row_o = lax.broadcasted_iota(i32, (TOP_K, tm), 0)
    idx_out = jnp.zeros((TOP_K, tm), i32)
    wt_out = jnp.zeros((TOP_K, tm), f32)
    picked = jnp.zeros((N_EXPERTS, tm), f32)
    w_sum = jnp.zeros((1, tm), f32)
    for k in range(TOP_K):
        mx = jnp.max(cand, axis=0, keepdims=True)
        ik = jnp.min(jnp.where(cand == mx, eid, N_EXPERTS), axis=0, keepdims=True)
        pick = eid == ik
        wk = jnp.sum(jnp.where(pick, score, 0.0), axis=0, keepdims=True)
        cand = jnp.where(pick, -jnp.inf, cand)
        picked = jnp.where(pick, 1.0, picked)
        idx_out = jnp.where(row_o == k, ik, idx_out)
        wt_out = jnp.where(row_o == k, wk, wt_out)
        w_sum = w_sum + wk
    return idx_out, wt_out / w_sum * ROUTED_SCALE, jnp.sum(picked, axis=1, keepdims=True)


def _mix_route(attn, conv, x2, mod, w_out, ln_g, ln_b, w_router, router_bias, s, tm):
    t, d = x2.shape
    per = s // tm
    da = attn.shape[1]
    pack_rows = d // 2 // LANES
    row = lambda a: a.reshape(1, -1)
    tile = lambda w: pl.BlockSpec((tm, w), lambda i: (i, 0))
    full = lambda a: pl.BlockSpec(a.shape, lambda i: (0,) * a.ndim)
    assert tm % ROUTE_HALF == 0
    wr_hi = w_router.T.astype(bf16)
    wr_lo = (w_router.T - wr_hi.astype(f32)).astype(bf16)
    args = (attn, conv, x2, mod, w_out.astype(bf16), row(ln_g), row(ln_b), wr_hi, wr_lo,
            router_bias.reshape(N_EXPERTS, 1))
    per_token = lambda rows: pl.BlockSpec((rows, tm), lambda i: (0, i))
    return pl.pallas_call(
        _mix_route_kernel,
        grid=(t // tm,),
        in_specs=[tile(da), tile(conv.shape[1]), tile(d),
                  pl.BlockSpec((1, 6, d), lambda i: (i // per, 0, 0))] + [full(a) for a in args[4:]],
        out_specs=[tile(d), pl.BlockSpec((tm * pack_rows, LANES), lambda i: (i, 0)),
                   per_token(TOP_K), per_token(TOP_K), pl.BlockSpec((N_EXPERTS, 1), lambda i: (0, 0))],
        out_shape=[jax.ShapeDtypeStruct((t, d), f32),
                   jax.ShapeDtypeStruct((t * pack_rows, LANES), u32),
                   jax.ShapeDtypeStruct((TOP_K, t), i32), jax.ShapeDtypeStruct((TOP_K, t), f32),
                   jax.ShapeDtypeStruct((N_EXPERTS, 1), f32)],
        compiler_params=_params(("arbitrary",)),
        name="mix_route",
    )(*args)


def _positions_kernel(idx_ref, start_ref, o_ref, run_ref):
    step = pl.program_id(0)
    tm = idx_ref.shape[1]

    @pl.when(step == 0)
    def _():
        run_ref[...] = jnp.zeros_like(run_ref)

    idx = idx_ref[...]
    eid = lax.broadcasted_iota(i32, (N_EXPERTS, tm), 0)
    onehot = jnp.zeros((N_EXPERTS, tm), f32)
    for k in range(TOP_K):
        onehot = jnp.where(eid == idx[k:k + 1], 1.0, onehot)
    r = lax.broadcasted_iota(i32, (tm, tm), 0)
    c = lax.broadcasted_iota(i32, (tm, tm), 1)
    earlier = jnp.where(r < c, 1.0, 0.0).astype(bf16)
    prior = jnp.dot(onehot.astype(bf16), earlier, preferred_element_type=f32)
    pos = prior + run_ref[...] + start_ref[...]
    row_o = lax.broadcasted_iota(i32, (TOP_K, tm), 0)
    out = jnp.zeros((TOP_K, tm), i32)
    for k in range(TOP_K):
        dk = jnp.sum(jnp.where(eid == idx[k:k + 1], pos, 0.0), axis=0, keepdims=True)
        out = jnp.where(row_o == k, dk.astype(i32), out)
    o_ref[...] = out
    run_ref[...] += jnp.sum(onehot, axis=1, keepdims=True)


def _positions(idx, seg_start, tm):
    t = idx.shape[1]
    return pl.pallas_call(
        _positions_kernel,
        grid=(t // tm,),
        in_specs=[pl.BlockSpec((TOP_K, tm), lambda i: (0, i)),
                  pl.BlockSpec((N_EXPERTS, 1), lambda i: (0, 0))],
        out_specs=pl.BlockSpec((TOP_K, tm), lambda i: (0, i)),
        out_shape=jax.ShapeDtypeStruct((TOP_K, t), i32),
        scratch_shapes=[pltpu.VMEM((N_EXPERTS, 1), f32)],
        compiler_params=_params(("arbitrary",)),
        name="positions",
    )(idx, seg_start)


def _dispatch_kernel(dest_hbm, h_ref, o_hbm, dest_smem, sem_i, sem, *, r):
    step = pl.program_id(0)
    n = dest_smem.shape[0]
    load = pltpu.make_async_copy(dest_hbm.at[pl.ds(pl.multiple_of(step * n, n), n)], dest_smem, sem_i)
    load.start()
    load.wait()

    def issue(tok, _):
        src = pl.multiple_of(tok * r, r)
        for k in range(TOP_K):
            dst = pl.multiple_of(dest_smem[tok * TOP_K + k] * r, r)
            pltpu.make_async_copy(h_ref.at[pl.ds(src, r)], o_hbm.at[pl.ds(dst, r)], sem).start(priority=k % 2)
        return 0

    lax.fori_loop(0, n // TOP_K, issue, 0)
    everything = o_hbm.at[pl.ds(0, n * r)]
    pltpu.make_async_copy(everything, everything, sem).wait()


def _dispatch(dest_flat, h2p, n_buf, r, tm):
    t = h2p.shape[0] // r
    return pl.pallas_call(
        functools.partial(_dispatch_kernel, r=r),
        grid=(t // tm,),
        in_specs=[pl.BlockSpec(memory_space=pl.ANY),
                  pl.BlockSpec((tm * r, LANES), lambda i: (i, 0))],
        out_specs=pl.BlockSpec(memory_space=pl.ANY),
        out_shape=jax.ShapeDtypeStruct((n_buf * r, LANES), u32),
        scratch_shapes=[pltpu.SMEM((tm * TOP_K,), i32), pltpu.SemaphoreType.DMA, pltpu.SemaphoreType.DMA],
        compiler_params=_params(("arbitrary",)),
        name="dispatch",
    )(dest_flat, h2p)


def _experts_kernel(be_ref, bv_ref, nu_ref, x_ref, wg_ref, wu_ref, wd_ref, o_ref, wg_s, wu_s, wd_s):
    j = pl.program_id(0)
    prev = be_ref[jnp.maximum(j - 1, 0)]
    used = j < nu_ref[0]
    d = wg_s.shape[0]

    @pl.when(used & ((j == 0) | (be_ref[j] != prev)))
    def _():
        wg_s[...] = wg_ref[0].astype(bf16)
        wu_s[...] = wu_ref[0].astype(bf16)
        wd_s[...] = wd_ref[0].astype(bf16)

    @pl.when(used)
    def _():
        live = lax.broadcasted_iota(i32, (ROW_BLOCK, 1), 0) < bv_ref[j]
        x = jnp.where(live, _load_packed_rows(x_ref, ROW_BLOCK, d), 0.0).astype(bf16)
        hg = jnp.dot(x, wg_s[...], preferred_element_type=f32)
        hu = jnp.dot(x, wu_s[...], preferred_element_type=f32)
        hid = (_silu(hg) * hu).astype(bf16)
        _store_packed_rows(o_ref, jnp.dot(hid, wd_s[...], preferred_element_type=f32))


def _experts(blk_e, blk_valid, n_used, xs, w_gate, w_up, w_down):
    d, f = w_gate.shape[1], w_gate.shape[2]
    r = d // 2 // LANES
    n_blk = xs.shape[0] // r // ROW_BLOCK
    rows = lambda j, be, bv, nu: (jnp.minimum(j, nu[0] - 1), 0)
    wsel = lambda j, be, bv, nu: (be[j], 0, 0)
    return pl.pallas_call(
        _experts_kernel,
        grid_spec=pltpu.PrefetchScalarGridSpec(
            num_scalar_prefetch=3,
            grid=(n_blk,),
            in_specs=[pl.BlockSpec((ROW_BLOCK * r, LANES), rows),
                      pl.BlockSpec((1, d, f), wsel),
                      pl.BlockSpec((1, d, f), wsel),
                      pl.BlockSpec((1, f, d), wsel)],
            out_specs=pl.BlockSpec((ROW_BLOCK * r, LANES), rows),
            scratch_shapes=[pltpu.VMEM((d, f), bf16), pltpu.VMEM((d, f), bf16), pltpu.VMEM((f, d), bf16)]),
        out_shape=jax.ShapeDtypeStruct(xs.shape, u32),
        compiler_params=_params(("arbitrary",)),
        name="experts",
    )(blk_e, blk_valid, n_used, xs, w_gate, w_up, w_down)


def _combine_kernel(dest_hbm, ys_hbm, wt_ref, h_ref, x1_ref, mod_ref, wsg_ref, wsu_ref, wsd_ref,
                    lg_ref, lb_ref, o_ref, dest_a, dest_b, rows_a, rows_b, sem_d, sem_g):
    step = pl.program_id(0)
    n_steps = pl.num_programs(0)
    tm, d = x1_ref.shape
    r = d // 2 // LANES
    n = tm * TOP_K
    tables, rows = (dest_a, dest_b), (rows_a, rows_b)

    def table_copy(tile, slot):
        return pltpu.make_async_copy(dest_hbm.at[pl.ds(pl.multiple_of(tile * n, n), n)],
                                     tables[slot], sem_d.at[slot])

    def start_gathers(slot):
        def issue(tok, _):
            dst = pl.multiple_of(tok * r, r)
            for k in range(TOP_K):
                src = pl.multiple_of(tables[slot][tok * TOP_K + k] * r, r)
                pltpu.make_async_copy(ys_hbm.at[pl.ds(src, r)], rows[slot].at[k, pl.ds(dst, r)],
                                      sem_g.at[slot]).start(priority=k % 2)
            return 0

        lax.fori_loop(0, tm, issue, 0)

    @pl.when(step == 0)
    def _():
        table_copy(0, 0).start()
        table_copy(0, 0).wait()
        start_gathers(0)

        @pl.when(n_steps > 1)
        def _():
            table_copy(1, 1).start()

    def tile_body(slot):
        @pl.when(step + 1 < n_steps)
        def _():
            table_copy(step + 1, 1 - slot).wait()
            start_gathers(1 - slot)

            @pl.when(step + 2 < n_steps)
            def _():
                table_copy(step + 2, slot).start()

        h = _load_packed_rows(h_ref, tm, d).astype(bf16)
        hg = jnp.dot(h, wsg_ref[...], preferred_element_type=f32)
        hu = jnp.dot(h, wsu_ref[...], preferred_element_type=f32)
        y = jnp.dot((_silu(hg) * hu).astype(bf16), wsd_ref[...], preferred_element_type=f32)

        pltpu.make_async_copy(rows[slot], rows[slot], sem_g.at[slot]).wait()
        wt = wt_ref[...]
        lo = [jnp.zeros((tm, LANES), f32) for _ in range(r)]
        hi = [jnp.zeros((tm, LANES), f32) for _ in range(r)]
        for k in range(TOP_K):
            wk = wt[:, k:k + 1]
            for c in range(r):
                pl_, ph_ = _unpack_words(rows[slot][k, pl.ds(c, tm, stride=r), :])
                lo[c] = lo[c] + wk * pl_
                hi[c] = hi[c] + wk * ph_
        y = y + jnp.concatenate(lo + hi, axis=1)
        m = mod_ref[0]
        o_ref[...] = _layer_norm(DEEPNORM_ALPHA * x1_ref[...] + m[5:6] * y, lg_ref[...], lb_ref[...])

    for slot in range(2):
        pl.when(step % 2 == slot)(functools.partial(tile_body, slot))


def _combine(dest_flat, ys, wts, h2p, x1, mod, w_s_gate, w_s_up, w_s_down, ln_g, ln_b, s, tm):
    t, d = x1.shape
    per = s // tm
    r = d // 2 // LANES
    row = lambda a: a.reshape(1, -1)
    tile = lambda w: pl.BlockSpec((tm, w), lambda i: (i, 0))
    full = lambda a: pl.BlockSpec(a.shape, lambda i: (0,) * a.ndim)
    tail = (w_s_gate.astype(bf16), w_s_up.astype(bf16), w_s_down.astype(bf16), row(ln_g), row(ln_b))
    return pl.pallas_call(
        _combine_kernel,
        grid=(t // tm,),
        in_specs=[pl.BlockSpec(memory_space=pl.ANY), pl.BlockSpec(memory_space=pl.ANY),
                  tile(LANES), pl.BlockSpec((tm * r, LANES), lambda i: (i, 0)), tile(d),
                  pl.BlockSpec((1, 6, d), lambda i: (i // per, 0, 0))] + [full(a) for a in tail],
        out_specs=tile(d),
        out_shape=jax.ShapeDtypeStruct((t, d), f32),
        scratch_shapes=[pltpu.SMEM((tm * TOP_K,), i32), pltpu.SMEM((tm * TOP_K,), i32),
                        pltpu.VMEM((TOP_K, tm * r, LANES), u32), pltpu.VMEM((TOP_K, tm * r, LANES), u32),
                        pltpu.SemaphoreType.DMA((2,)), pltpu.SemaphoreType.DMA((2,))],
        compiler_params=_params(("arbitrary",)),
        name="combine",
    )(dest_flat, ys, wts, h2p, x1, mod, *tail)


def _layer(x, mod, w_in, pe, w_cmp1, w_cmp2, w_dw, b_dw, conv_ln_g, conv_ln_b, w_out, ln1_g, ln1_b,
           w_router, router_bias, w_e_gate, w_e_up, w_e_down, w_s_gate, w_s_up, w_s_down, ln2_g, ln2_b):
    b, s, d = x.shape
    t = b * s
    tm = min(512, s)
    qt, vt, gt, kvc, kn, conv_in = _in_proj(x, mod, w_in, tm)
    cmp_n, cmp_t = _compress(kvc, pe, w_cmp1, w_cmp2)
    gates = gt[:, :3 * N_HEADS].reshape(b, N_KV, 3 * HPG, s)
    gates = jnp.pad(gates, ((0, 0), (0, 0), (0, GATE_GROUP_ROWS - 3 * HPG), (0, 0)))
    attn = _attention(qt, gates, cmp_n, cmp_t, kn, vt)
    conv = _conv(conv_in, w_dw, b_dw, conv_ln_g, conv_ln_b, tm)

    tr = min(256, s)
    x1, h2p, idx, wts, counts = _mix_route(attn.reshape(t, -1), conv.reshape(t, -1), x.reshape(t, d), mod,
                                           w_out, ln1_g, ln1_b, w_router, router_bias, s, tr)
    counts = counts[:, 0].astype(i32)
    padded = (counts + ROW_BLOCK - 1) // ROW_BLOCK * ROW_BLOCK
    seg_end = jnp.cumsum(padded)
    seg_start = seg_end - padded
    n_blk = -(-(t * TOP_K + N_EXPERTS * (ROW_BLOCK - 1)) // ROW_BLOCK)
    blk_row0 = jnp.arange(n_blk, dtype=i32) * ROW_BLOCK
    owns = (blk_row0[:, None] >= seg_start[None, :]) & (blk_row0[:, None] < seg_end[None, :])
    blk_e = jnp.where(blk_row0 < seg_end[-1], jnp.argmax(owns, axis=1), N_EXPERTS - 1).astype(i32)
    live_end = jnp.sum(jnp.where(owns, (seg_start + counts)[None, :], 0), axis=1)
    blk_valid = jnp.clip(live_end - blk_row0, 0, ROW_BLOCK).astype(i32)
    n_used = (seg_end[-1:] // ROW_BLOCK).astype(i32)

    dest = _positions(idx, seg_start.astype(f32).reshape(N_EXPERTS, 1), tr)
    dest_flat = dest.T.reshape(-1)
    wts = jnp.zeros((t, LANES), f32).at[:, :TOP_K].set(wts.T)
    xs = _dispatch(dest_flat, h2p, n_blk * ROW_BLOCK, d // 2 // LANES, min(512, s))
    ys = _experts(blk_e, blk_valid, n_used, xs, w_e_gate, w_e_up, w_e_down)
    out = _combine(dest_flat, ys, wts, h2p, x1, mod, w_s_gate, w_s_up, w_s_down, ln2_g, ln2_b, s, min(256, s))
    return out.reshape(b, s, d)


def kernel(x, c, w_ada, b_ada, w_in, pe_k, pe_v, w_cmp_k1, w_cmp_k2, w_cmp_v1, w_cmp_v2, w_dw, b_dw,
           conv_ln_g, conv_ln_b, w_out, ln1_g, ln1_b, w_router, router_bias, w_e_gate, w_e_up, w_e_down,
           w_s_gate, w_s_up, w_s_down, ln2_g, ln2_b):
    assert w_ada.shape[0] == DEPTH
    layer = lambda a: a.reshape(a.shape[1:])
    mod = _ada(c, layer(w_ada), layer(b_ada))
    return _layer(x, mod, layer(w_in), jnp.concatenate([pe_k, pe_v]),
                  jnp.concatenate([w_cmp_k1, w_cmp_v1]), jnp.concatenate([w_cmp_k2, w_cmp_v2]),
                  *[layer(a) for a in (w_dw, b_dw, conv_ln_g, conv_ln_b, w_out, ln1_g, ln1_b, w_router,
                                       router_bias, w_e_gate, w_e_up, w_e_down, w_s_gate, w_s_up, w_s_down,
                                       ln2_g, ln2_b)])
```

```python
import functools

import jax
import jax.numpy as jnp
from jax import lax
from jax.experimental import pallas as pl
from jax.experimental.pallas import tpu as pltpu

N_HEADS = 8
N_KV = 2
HPG = N_HEADS // N_KV
HEAD_DIM = 64
D_ATTN = N_HEADS * HEAD_DIM
D_KV = N_KV * HEAD_DIM
CONV_WIDTH = 31
CMP_LEN = 32
CMP_STRIDE = 16
CMP_HID = 256
SEL_LEN = 64
SEL_TOPN = 16
WINDOW = 512
Q_BLOCK = 256
N_EXPERTS = 256
TOP_K = 8
N_GROUPS = 8
TOPK_GROUPS = 4
ROUTED_SCALE = 2.5
LN_EPS = 1e-5
DEPTH = 1
DEEPNORM_ALPHA = (2 * DEPTH) ** 0.25

LANES = 128
SUBLANES = 8
ROW_BLOCK = 512
NEG = -1e30
HIGHEST = lax.Precision.HIGHEST
VMEM_LIMIT = 48 * 1024 * 1024

f32 = jnp.float32
bf16 = jnp.bfloat16
i32 = jnp.int32
u32 = jnp.uint32


def _params(sem, vmem=VMEM_LIMIT):
    return pltpu.CompilerParams(dimension_semantics=sem, vmem_limit_bytes=vmem)


def _sigmoid(v):
    return 1.0 / (1.0 + jnp.exp(-v))


def _silu(v):
    return v * _sigmoid(v)


def _layer_norm(v, g, b):
    mu = jnp.mean(v, axis=-1, keepdims=True)
    var = jnp.mean(jnp.square(v - mu), axis=-1, keepdims=True)
    return (v - mu) * lax.rsqrt(var + LN_EPS) * g + b


def _dot_nt(a, b):
    return lax.dot_general(a, b, (((1,), (1,)), ((), ())), preferred_element_type=f32)


def _store_packed_rows(ref, v, first=0):
    n, d = v.shape
    half = d // 2
    bits = lax.bitcast_convert_type(v.astype(bf16).astype(f32), u32)
    words = bits[:, half:] | (bits[:, :half] >> 16)
    r = half // LANES
    for c in range(r):
        ref[pl.ds(first * r + c, n, stride=r), :] = words[:, c * LANES:(c + 1) * LANES]


def _unpack_words(w):
    return (lax.bitcast_convert_type(w << 16, f32),
            lax.bitcast_convert_type(w & jnp.uint32(0xFFFF0000), f32))


def _load_packed_rows(ref, n, d):
    r = d // 2 // LANES
    parts = [_unpack_words(ref[pl.ds(c, n, stride=r), :]) for c in range(r)]
    return jnp.concatenate([p[0] for p in parts] + [p[1] for p in parts], axis=1)


def _ada_kernel(c_ref, w_ref, b_ref, o_ref):
    c = c_ref[...]
    o_ref[...] = jnp.dot(_silu(c), w_ref[...], precision=HIGHEST,
                         preferred_element_type=f32) + b_ref[...]


def _ada(c, w_ada, b_ada):
    b, d = c.shape
    n = w_ada.shape[1]
    rows = 8
    c_pad = jnp.zeros((rows, d), f32).at[:b].set(c)
    tn = 1024
    out = pl.pallas_call(
        _ada_kernel,
        grid=(n // tn,),
        in_specs=[pl.BlockSpec((rows, d), lambda j: (0, 0)),
                  pl.BlockSpec((d, tn), lambda j: (0, j)),
                  pl.BlockSpec((1, tn), lambda j: (0, j))],
        out_specs=pl.BlockSpec((rows, tn), lambda j: (0, j)),
        out_shape=jax.ShapeDtypeStruct((rows, n), f32),
        compiler_params=_params(("arbitrary",)),
        name="ada",
    )(c_pad, w_ada, b_ada.reshape(1, n))
    return out[:b].reshape(b, 6, d)


GATE_ROWS = 32
KEY_TILE = 512
TILE_BLOCKS = KEY_TILE // SEL_LEN
PATCH_BLOCKS = Q_BLOCK // SEL_LEN
PAIR_UNROLL = 2
CMP_PER_SEL = SEL_LEN // CMP_STRIDE
CMP_BACK = CMP_LEN // CMP_STRIDE - 1
assert SEL_LEN % CMP_STRIDE == 0 and CMP_LEN % CMP_STRIDE == 0 and CMP_BACK < CMP_PER_SEL
V_ROWS = HEAD_DIM + 16
Q_SCALE = HEAD_DIM ** -0.5 * 1.4426950408889634


def _in_proj_kernel(x_ref, mod_ref, wt_ref, wc_ref, wk_ref, wv_ref,
                    qt_ref, vt_ref, gt_ref, kvc_ref, kn_ref, cv_ref):
    m = mod_ref[0]
    h = (x_ref[0] * (1.0 + m[1:2]) + m[0:1]).astype(bf16)
    res_t = _dot_nt(wt_ref[...], h)
    qt_ref[0] = (res_t[0:D_ATTN] * Q_SCALE).astype(bf16)
    ones = jnp.ones((V_ROWS - HEAD_DIM, res_t.shape[1]), bf16)
    for j in range(2):
        for g in range(N_KV):
            off = D_ATTN + (j * N_KV + g) * HEAD_DIM
            vt_ref[0, j, g, 0:HEAD_DIM, :] = res_t[off:off + HEAD_DIM].astype(bf16)
            vt_ref[0, j, g, HEAD_DIM:V_ROWS, :] = ones
    gt_ref[0] = res_t[D_ATTN + 2 * D_KV:]
    kvc = jnp.dot(h, wc_ref[...], preferred_element_type=f32)
    kvc_ref[0, 0] = kvc[:, :D_KV]
    kvc_ref[0, 1] = kvc[:, D_KV:]
    kn = jnp.dot(h, wk_ref[...], preferred_element_type=f32).astype(bf16)
    tm = kn.shape[0]
    pos = pl.program_id(1) * tm + lax.broadcasted_iota(i32, (tm, LANES - HEAD_DIM), 0)
    lane = lax.broadcasted_iota(i32, (tm, LANES - HEAD_DIM), 1)
    onehot = jnp.where(lane == ((pos >> 6) & (TILE_BLOCKS - 1)), 1.0, 0.0).astype(bf16)
    for j in range(2):
        for g in range(N_KV):
            off = (j * N_KV + g) * HEAD_DIM
            kn_ref[0, j, g] = jnp.concatenate([kn[:, off:off + HEAD_DIM], onehot], axis=1)
    cv_ref[0] = jnp.dot(h, wv_ref[...], preferred_element_type=f32)


def _in_proj(x, mod, w_in, tm):
    b, s, d = x.shape
    o = 0
    wq = w_in[:, o:o + D_ATTN]; o += D_ATTN
    wkc = w_in[:, o:o + 2 * D_KV]; o += 2 * D_KV
    wk_s = w_in[:, o:o + D_KV]; o += D_KV
    wv_s = w_in[:, o:o + D_KV]; o += D_KV
    wk_w = w_in[:, o:o + D_KV]; o += D_KV
    wv_w = w_in[:, o:o + D_KV]; o += D_KV
    wg = w_in[:, o:o + 3 * N_HEADS]; o += 3 * N_HEADS
    wcv = w_in[:, o:]
    d_conv2 = wcv.shape[1]
    wg = jnp.zeros((d, GATE_ROWS), f32).at[:, :3 * N_HEADS].set(wg)
    wt = jnp.concatenate([wq, wv_s, wv_w, wg], axis=1).T
    ws = [w.astype(bf16) for w in (wt, wkc, jnp.concatenate([wk_s, wk_w], axis=1), wcv)]
    full = lambda a: pl.BlockSpec(a.shape, lambda bi, i: (0, 0))
    return pl.pallas_call(
        _in_proj_kernel,
        grid=(b, s // tm),
        in_specs=[pl.BlockSpec((1, tm, d), lambda bi, i: (bi, i, 0)),
                  pl.BlockSpec((1, 6, d), lambda bi, i: (bi, 0, 0))] + [full(w) for w in ws],
        out_specs=[pl.BlockSpec((1, D_ATTN, tm), lambda bi, i: (bi, 0, i)),
                   pl.BlockSpec((1, 2, N_KV, V_ROWS, tm), lambda bi, i: (bi, 0, 0, 0, i)),
                   pl.BlockSpec((1, GATE_ROWS, tm), lambda bi, i: (bi, 0, i)),
                   pl.BlockSpec((1, 2, tm, D_KV), lambda bi, i: (bi, 0, i, 0)),
                   pl.BlockSpec((1, 2, N_KV, tm, LANES), lambda bi, i: (bi, 0, 0, i, 0)),
                   pl.BlockSpec((1, tm, d_conv2), lambda bi, i: (bi, i, 0))],
        out_shape=[jax.ShapeDtypeStruct((b, D_ATTN, s), bf16),
                   jax.ShapeDtypeStruct((b, 2, N_KV, V_ROWS, s), bf16),
                   jax.ShapeDtypeStruct((b, GATE_ROWS, s), f32),
                   jax.ShapeDtypeStruct((b, 2, s, D_KV), f32),
                   jax.ShapeDtypeStruct((b, 2, N_KV, s, LANES), bf16),
                   jax.ShapeDtypeStruct((b, s, d_conv2), f32)],
        compiler_params=_params(("parallel", "parallel")),
        name="in_proj",
    )(x, mod, *ws)


def _compress_kernel(x_ref, pe_ref, w1_ref, w2_ref, w2t_ref, o_ref, ot_ref):
    n_chunk = o_ref.shape[3]
    for j in range(2):
        for g in range(N_KV):
            cols = slice(g * HEAD_DIM, (g + 1) * HEAD_DIM)
            a = jnp.zeros((n_chunk, CMP_HID), f32)
            bm = jnp.zeros((n_chunk, CMP_HID), f32)
            for l in range(CMP_STRIDE):
                xl = x_ref[0, j, pl.ds(l, n_chunk, stride=CMP_STRIDE), :][:, cols]
                a = a + jnp.dot((xl + pe_ref[j, l:l + 1, :]).astype(bf16), w1_ref[j, l],
                                preferred_element_type=f32)
                bm = bm + jnp.dot((xl + pe_ref[j, CMP_STRIDE + l:CMP_STRIDE + l + 1, :]).astype(bf16),
                                  w1_ref[j, CMP_STRIDE + l], preferred_element_type=f32)
            hid = a + pltpu.roll(bm, n_chunk - 1, 0)
            act = 0.5 * hid * (1.0 + jnp.tanh(0.7978845608028654 * (hid + 0.044715 * (hid * hid * hid))))
            act = act.astype(bf16)
            o_ref[0, j, g] = jnp.dot(act, w2_ref[j], preferred_element_type=f32).astype(bf16)
            ot_ref[0, j, g] = _dot_nt(w2t_ref[j], act).astype(bf16)


def _compress(kvc, pe, w1, w2):
    assert CMP_LEN == 2 * CMP_STRIDE
    b, _, s, width = kvc.shape
    n_chunk = s // CMP_STRIDE
    w1b = w1.reshape(2, CMP_LEN, HEAD_DIM, CMP_HID).astype(bf16)
    w2b = w2.astype(bf16)
    w2t = w2b.transpose(0, 2, 1)
    full = lambda a: pl.BlockSpec(a.shape, lambda bi: (0,) * a.ndim)
    return pl.pallas_call(
        _compress_kernel,
        grid=(b,),
        in_specs=[pl.BlockSpec((1, 2, s, width), lambda bi: (bi, 0, 0, 0)),
                  full(pe), full(w1b), full(w2b), full(w2t)],
        out_specs=[pl.BlockSpec((1, 2, N_KV, n_chunk, HEAD_DIM), lambda bi: (bi, 0, 0, 0, 0)),
                   pl.BlockSpec((1, 2, N_KV, HEAD_DIM, n_chunk), lambda bi: (bi, 0, 0, 0, 0))],
        out_shape=[jax.ShapeDtypeStruct((b, 2, N_KV, n_chunk, HEAD_DIM), bf16),
                   jax.ShapeDtypeStruct((b, 2, N_KV, HEAD_DIM, n_chunk), bf16)],
        compiler_params=_params(("parallel",)),
        name="compress",
    )(kvc, pe, w1b, w2b, w2t)


def _attn_kernel(q_ref, g_ref, kc_ref, vc_ref, kn_ref, vt_ref, o_ref, psum_ref, selbias_ref, sa_ref, sb_ref,
                 *, seq, tk, top_n):
    i = pl.program_id(2)
    s0 = i * Q_BLOCK
    n_cmp_rows = kc_ref.shape[3]
    n_sel = seq // SEL_LEN

    q4 = q_ref[0]
    qt = jnp.concatenate([q4[p * HEAD_DIM:(p + 1) * HEAD_DIM, :] for p in range(HPG)], axis=1)
    t_row = s0 + lax.broadcasted_iota(i32, (1, Q_BLOCK), 1)

    s_c = jnp.dot(kc_ref[0, 0, 0], qt, preferred_element_type=f32)
    cmp_end = lax.broadcasted_iota(i32, (n_cmp_rows, 1), 0) * CMP_STRIDE + (CMP_LEN - 1)
    bias_c = jnp.where(cmp_end <= t_row, 0.0, NEG)
    any_c = t_row >= CMP_LEN - 1
    p_sum = jnp.zeros((n_cmp_rows, Q_BLOCK), f32)
    pcs = []
    for p in range(HPG):
        sp = s_c[:, p * Q_BLOCK:(p + 1) * Q_BLOCK] + bias_c
        e = jnp.exp2(sp - jnp.max(sp, axis=0, keepdims=True))
        pn = e * jnp.where(any_c, 1.0 / jnp.sum(e, axis=0, keepdims=True), 0.0)
        p_sum = p_sum + pn
        pcs.append(pn.astype(bf16))
    o_c = jnp.dot(vc_ref[0, 0, 0], jnp.concatenate(pcs, axis=1), preferred_element_type=f32)

    cols = HPG * Q_BLOCK
    pad_rows = LANES - HEAD_DIM
    q_pad = jnp.concatenate([qt, jnp.zeros((pad_rows, cols), bf16)], axis=0)
    span = WINDOW + Q_BLOCK
    w0 = pl.multiple_of(jnp.maximum(s0 - WINDOW, 0), Q_BLOCK)
    s_w = jnp.dot(kn_ref[0, 1, 0, pl.ds(w0, span), :], q_pad, preferred_element_type=f32)
    wpos = w0 + lax.broadcasted_iota(i32, (span, 1), 0)
    bias_w = jnp.where((wpos <= t_row) & (wpos > t_row - WINDOW), 0.0, NEG)
    pws = []
    for p in range(HPG):
        sp = s_w[:, p * Q_BLOCK:(p + 1) * Q_BLOCK] + bias_w
        pws.append(jnp.exp2(sp - jnp.max(sp, axis=0, keepdims=True)).astype(bf16))
    acc_w = jnp.dot(vt_ref[0, 1, 0, :, pl.ds(w0, span)], jnp.concatenate(pws, axis=1),
                    preferred_element_type=f32)
    o_w = acc_w[0:HEAD_DIM] * (1.0 / acc_w[HEAD_DIM:HEAD_DIM + 1])

    blk = lax.broadcasted_iota(i32, (n_sel, Q_BLOCK), 0)
    for lt in range(Q_BLOCK // LANES):
        psum_ref[lt] = p_sum[:, lt * LANES:(lt + 1) * LANES]

    def every(first):
        return jnp.concatenate([psum_ref[lt, pl.ds(first, n_sel, stride=CMP_PER_SEL), :]
                                for lt in range(Q_BLOCK // LANES)], axis=1)

    imp = every(0)
    for r in range(1, CMP_PER_SEL):
        imp = imp + every(r)
    for back in range(1, CMP_BACK + 1):
        imp = imp + jnp.where(blk >= 1, pltpu.roll(every(CMP_PER_SEL - back), 1, 0), 0.0)

    cur = t_row >> 6
    forced = (blk == 0) | (blk == cur) | (blk == cur - 1)
    vals = jnp.where(forced, jnp.inf, jnp.where(blk <= cur, imp, -jnp.inf))
    sel = jnp.zeros((n_sel, Q_BLOCK), f32)
    for _ in range(top_n):
        mx = jnp.max(vals, axis=0, keepdims=True)
        first = jnp.min(jnp.where(vals == mx, blk, n_sel), axis=0, keepdims=True)
        pick = blk == first
        sel = jnp.where(pick & (mx > -jnp.inf), 1.0, sel)
        vals = jnp.where(pick, -jnp.inf, vals)
    first_blk = s0 >> 6
    full_at = n_sel + TILE_BLOCKS
    selbias_ref[0:n_sel] = jnp.where((sel > 0.5) & (blk < first_blk), 0.0, NEG)
    selbias_ref[n_sel:full_at] = jnp.full((TILE_BLOCKS, Q_BLOCK), NEG, f32)
    selbias_ref[full_at:full_at + n_sel] = jnp.where(sel > 0.5, 0.0, NEG)
    last_tile = seq // tk - 1

    def scores(kt, s_ref):
        k0 = pl.multiple_of(jnp.minimum(kt, last_tile) * tk, tk)
        sb = selbias_ref[pl.ds(pl.multiple_of(kt * TILE_BLOCKS, TILE_BLOCKS), TILE_BLOCKS), :]
        rows = jnp.concatenate([jnp.concatenate([sb] * HPG, axis=1),
                                jnp.zeros((pad_rows - TILE_BLOCKS, cols), f32)], axis=0).astype(bf16)
        s_ref[...] = jnp.dot(kn_ref[0, 0, 0, pl.ds(k0, tk), :], jnp.concatenate([qt, rows], axis=0),
                             preferred_element_type=f32)

    def fold(scores_of_head, v_t, carry):
        m_i, acc = carry
        es, ms, alphas = [], [], []
        for p in range(HPG):
            c = slice(p * Q_BLOCK, (p + 1) * Q_BLOCK)
            sp = scores_of_head(c)
            m_new = jnp.maximum(m_i[:, c], jnp.max(sp, axis=0, keepdims=True))
            es.append(jnp.exp2(sp - m_new).astype(bf16))
            alphas.append(jnp.exp2(m_i[:, c] - m_new))
            ms.append(m_new)
        pv = jnp.dot(v_t, jnp.concatenate(es, axis=1), preferred_element_type=f32)
        return jnp.concatenate(ms, axis=1), jnp.concatenate(alphas, axis=1) * acc + pv

    def sel_tile(kt, s_ref, carry):
        k0 = pl.multiple_of(jnp.minimum(kt, last_tile) * tk, tk)
        return fold(lambda c: s_ref[:, c], vt_ref[0, 0, 0, :, pl.ds(k0, tk)], carry)

    def sel_pair(i, carry):
        scores(2 * i + 1, sb_ref)
        carry = sel_tile(2 * i, sa_ref, carry)
        scores(2 * i + 2, sa_ref)
        return sel_tile(2 * i + 1, sb_ref, carry)

    n_sweep = (s0 + tk - 1) // tk
    n_pairs = (n_sweep + 1) // 2
    init = (jnp.full((1, cols), NEG, f32), jnp.zeros((V_ROWS, cols), f32))
    scores(0, sa_ref)
    def sel_pairs(j, carry):
        for u in range(PAIR_UNROLL):
            carry = sel_pair(PAIR_UNROLL * j + u, carry)
        return carry

    carry = lax.fori_loop(0, n_pairs // PAIR_UNROLL, sel_pairs, init)
    carry = lax.fori_loop(n_pairs // PAIR_UNROLL * PAIR_UNROLL, n_pairs, sel_pair, carry)
    d0 = pl.multiple_of(s0, Q_BLOCK)
    s_d = jnp.dot(kn_ref[0, 0, 0, pl.ds(d0, Q_BLOCK), :], q_pad, preferred_element_type=f32)
    group = selbias_ref[pl.ds(full_at + pl.multiple_of((first_blk >> 3) << 3, SUBLANES), SUBLANES), :]
    own = group[0:PATCH_BLOCKS]
    for at in range(PATCH_BLOCKS, SUBLANES, PATCH_BLOCKS):
        own = jnp.where((first_blk & (SUBLANES - 1)) == at, group[at:at + PATCH_BLOCKS], own)
    bias_d = jnp.concatenate([jnp.broadcast_to(own[j:j + 1], (SEL_LEN, Q_BLOCK)) for j in range(PATCH_BLOCKS)],
                             axis=0)
    bias_d = jnp.where(s0 + lax.broadcasted_iota(i32, (Q_BLOCK, 1), 0) <= t_row, bias_d, NEG)
    _, acc_s = fold(lambda c: s_d[:, c] + bias_d, vt_ref[0, 0, 0, :, pl.ds(d0, Q_BLOCK)], carry)
    o_s = acc_s[0:HEAD_DIM] * (1.0 / acc_s[HEAD_DIM:HEAD_DIM + 1])

    gate = _sigmoid(g_ref[0, 0])
    outs = []
    for p in range(HPG):
        c = slice(p * Q_BLOCK, (p + 1) * Q_BLOCK)
        outs.append(gate[3 * p:3 * p + 1] * o_c[:, c] + gate[3 * p + 1:3 * p + 2] * o_s[:, c]
                    + gate[3 * p + 2:3 * p + 3] * o_w[:, c])
    o_ref[0] = jnp.concatenate(outs, axis=0).T.astype(bf16)


GATE_GROUP_ROWS = 16


def _attention(qt, gates, cmp_n, cmp_t, kn, vt):
    b, _, s = qt.shape
    n_chunk = cmp_n.shape[3]
    n_sel = s // SEL_LEN
    top_n = min(SEL_TOPN, n_sel)
    tk = KEY_TILE
    assert s % tk == 0 and n_chunk == n_sel * CMP_PER_SEL
    gw = HPG * HEAD_DIM
    return pl.pallas_call(
        functools.partial(_attn_kernel, seq=s, tk=tk, top_n=top_n),
        grid=(b, N_KV, s // Q_BLOCK),
        in_specs=[pl.BlockSpec((1, gw, Q_BLOCK), lambda bi, g, i: (bi, g, i)),
                  pl.BlockSpec((1, 1, GATE_GROUP_ROWS, Q_BLOCK), lambda bi, g, i: (bi, g, 0, i)),
                  pl.BlockSpec((1, 1, 1, n_chunk, HEAD_DIM), lambda bi, g, i: (bi, 0, g, 0, 0)),
                  pl.BlockSpec((1, 1, 1, HEAD_DIM, n_chunk), lambda bi, g, i: (bi, 1, g, 0, 0)),
                  pl.BlockSpec((1, 2, 1, s, LANES), lambda bi, g, i: (bi, 0, g, 0, 0)),
                  pl.BlockSpec((1, 2, 1, V_ROWS, s), lambda bi, g, i: (bi, 0, g, 0, 0))],
        out_specs=pl.BlockSpec((1, Q_BLOCK, gw), lambda bi, g, i: (bi, i, g)),
        out_shape=jax.ShapeDtypeStruct((b, s, D_ATTN), bf16),
        scratch_shapes=[pltpu.VMEM((Q_BLOCK // LANES, n_chunk, LANES), f32),
                        pltpu.VMEM((2 * n_sel + TILE_BLOCKS, Q_BLOCK), f32),
                        pltpu.VMEM((tk, HPG * Q_BLOCK), f32), pltpu.VMEM((tk, HPG * Q_BLOCK), f32)],
        compiler_params=_params(("parallel", "parallel", "arbitrary")),
        name="attention",
    )(qt, gates, cmp_n, cmp_t, kn, vt)


CONV_HALO = 32
CONV_ROWS = 32


def _conv_kernel(cur_ref, prev_ref, w_ref, b_ref, g_ref, bb_ref, o_ref, glu_ref):
    i = pl.program_id(1)
    ts = cur_ref.shape[1]
    dc = o_ref.shape[2]
    cur = cur_ref[0]
    prev = prev_ref[0]
    glu_prev = prev[:, :dc] * _sigmoid(prev[:, dc:])
    glu_ref[0:CONV_HALO] = jnp.where(i == 0, 0.0, glu_prev)
    glu_ref[CONV_HALO:CONV_HALO + ts] = cur[:, :dc] * _sigmoid(cur[:, dc:])
    lead = CONV_HALO - (CONV_WIDTH - 1)

    def chunk(r, _):
        r0 = pl.multiple_of(r * CONV_ROWS, CONV_ROWS)
        win = glu_ref[pl.ds(r0, CONV_ROWS + CONV_HALO), :]
        span = CONV_ROWS + CONV_HALO
        acc = jnp.zeros((CONV_ROWS, dc), f32)
        for ph in range(SUBLANES):
            turned = win if ph == 0 else pltpu.roll(win, span - ph, 0)
            for j in range(CONV_WIDTH):
                if (lead + j) % SUBLANES == ph:
                    at = lead + j - ph
                    acc = acc + w_ref[j:j + 1, :] * turned[at:at + CONV_ROWS]
        y = _layer_norm(acc + b_ref[...], g_ref[...], bb_ref[...])
        o_ref[0, pl.ds(r0, CONV_ROWS), :] = _silu(y).astype(bf16)
        return 0

    lax.fori_loop(0, ts // CONV_ROWS, chunk, 0)


def _conv(conv_in, w_dw, b_dw, ln_g, ln_b, ts):
    b, s, dc2 = conv_in.shape
    dc = dc2 // 2
    per = ts // CONV_HALO
    row = lambda a: a.reshape(1, dc)
    return pl.pallas_call(
        _conv_kernel,
        grid=(b, s // ts),
        in_specs=[pl.BlockSpec((1, ts, dc2), lambda bi, i: (bi, i, 0)),
                  pl.BlockSpec((1, CONV_HALO, dc2), lambda bi, i: (bi, jnp.maximum(i * per - 1, 0), 0)),
                  pl.BlockSpec((CONV_WIDTH, dc), lambda bi, i: (0, 0)),
                  pl.BlockSpec((1, dc), lambda bi, i: (0, 0)),
                  pl.BlockSpec((1, dc), lambda bi, i: (0, 0)),
                  pl.BlockSpec((1, dc), lambda bi, i: (0, 0))],
        out_specs=pl.BlockSpec((1, ts, dc), lambda bi, i: (bi, i, 0)),
        out_shape=jax.ShapeDtypeStruct((b, s, dc), bf16),
        scratch_shapes=[pltpu.VMEM((CONV_HALO + ts, dc), f32)],
        compiler_params=_params(("parallel", "parallel")),
        name="conv",
    )(conv_in, conv_in, w_dw.reshape(CONV_WIDTH, dc), row(b_dw), row(ln_g), row(ln_b))


ROUTE_HALF = 128


def _mix_route_kernel(a_ref, cv_ref, x_ref, mod_ref, wo_ref, lg_ref, lb_ref, wrh_ref, wrl_ref, rb_ref,
                      x1_ref, h2p_ref, idx_ref, wt_ref, cnt_ref):
    step = pl.program_id(0)

    @pl.when(step == 0)
    def _():
        cnt_ref[...] = jnp.zeros_like(cnt_ref)

    m = mod_ref[0]
    da = a_ref.shape[1]
    counts = jnp.zeros(cnt_ref.shape, f32)
    for first in range(0, x_ref.shape[0], ROUTE_HALF):
        rows = slice(first, first + ROUTE_HALF)
        mix = (jnp.dot(a_ref[rows, :], wo_ref[0:da, :], preferred_element_type=f32)
               + jnp.dot(cv_ref[rows, :], wo_ref[da:, :], preferred_element_type=f32))
        x1 = _layer_norm(DEEPNORM_ALPHA * x_ref[rows, :] + m[2:3] * mix, lg_ref[...], lb_ref[...])
        x1_ref[rows, :] = x1
        h2 = x1 * (1.0 + m[4:5]) + m[3:4]
        _store_packed_rows(h2p_ref, h2, first)
        h_hi = h2.astype(bf16)
        h_lo = (h2 - h_hi.astype(f32)).astype(bf16)
        logits = _dot_nt(wrh_ref[...], h_hi) + (_dot_nt(wrh_ref[...], h_lo) + _dot_nt(wrl_ref[...], h_hi))
        idx, wt, cnt = _route(_sigmoid(logits), rb_ref[...])
        idx_ref[:, rows] = idx
        wt_ref[:, rows] = wt
        counts = counts + cnt
    cnt_ref[...] += counts


def _route(score, bias):
    tm = score.shape[1]
    sel = score + bias
    eid = lax.broadcasted_iota(i32, (N_EXPERTS, tm), 0)
    per_group = N_EXPERTS // N_GROUPS
    gs = []
    for g in range(N_GROUPS):
        rows = slice(g * per_group, (g + 1) * per_group)
        v = sel[rows]
        e = g * per_group + lax.broadcasted_iota(i32, (per_group, tm), 0)
        m1 = jnp.max(v, axis=0, keepdims=True)
        i1 = jnp.min(jnp.where(v == m1, e, N_EXPERTS), axis=0, keepdims=True)
        m2 = jnp.max(jnp.where(e == i1, -jnp.inf, v), axis=0, keepdims=True)
        gs.append(m1 + m2)
    cands = []
    for g in range(N_GROUPS):
        rank = jnp.zeros((1, tm), i32)
        for o in range(N_GROUPS):
            if o == g:
                continue
            beats = (gs[o] > gs[g]) | (gs[o] == gs[g]) if o < g else gs[o] > gs[g]
            rank = rank + beats.astype(i32)
        drop = jnp.where(rank < TOPK_GROUPS, 0.0, -jnp.inf)
        cands.append(sel[g * per_group:(g + 1) * per_group] + drop)
    cand = jnp.concatenate(cands, axis=0)
    row_o = lax.broadcasted_iota(i32, (TOP_K, tm), 0)
    idx_out = jnp.zeros((TOP_K, tm), i32)
    wt_out = jnp.zeros((TOP_K, tm), f32)
    picked = jnp.zeros((N_EXPERTS, tm), f32)
    w_sum = jnp.zeros((1, tm), f32)
    for k in range(TOP_K):
        mx = jnp.max(cand, axis=0, keepdims=True)
        ik = jnp.min(jnp.where(cand == mx, eid, N_EXPERTS), axis=0, keepdims=True)
        pick = eid == ik
        wk = jnp.sum(jnp.where(pick, score, 0.0), axis=0, keepdims=True)
        cand = jnp.where(pick, -jnp.inf, cand)
        picked = jnp.where(pick, 1.0, picked)
        idx_out = jnp.where(row_o == k, ik, idx_out)
        wt_out = jnp.where(row_o == k, wk, wt_out)
        w_sum = w_sum + wk
    return idx_out, wt_out / w_sum * ROUTED_SCALE, jnp.sum(picked, axis=1, keepdims=True)


def _mix_route(attn, conv, x2, mod, w_out, ln_g, ln_b, w_router, router_bias, s, tm):
    t, d = x2.shape
    per = s // tm
    da = attn.shape[1]
    pack_rows = d // 2 // LANES
    row = lambda a: a.reshape(1, -1)
    tile = lambda w: pl.BlockSpec((tm, w), lambda i: (i, 0))
    full = lambda a: pl.BlockSpec(a.shape, lambda i: (0,) * a.ndim)
    assert tm % ROUTE_HALF == 0
    wr_hi = w_router.T.astype(bf16)
    wr_lo = (w_router.T - wr_hi.astype(f32)).astype(bf16)
    args = (attn, conv, x2, mod, w_out.astype(bf16), row(ln_g), row(ln_b), wr_hi, wr_lo,
            router_bias.reshape(N_EXPERTS, 1))
    per_token = lambda rows: pl.BlockSpec((rows, tm), lambda i: (0, i))
    return pl.pallas_call(
        _mix_route_kernel,
        grid=(t // tm,),
        in_specs=[tile(da), tile(conv.shape[1]), tile(d),
                  pl.BlockSpec((1, 6, d), lambda i: (i // per, 0, 0))] + [full(a) for a in args[4:]],
        out_specs=[tile(d), pl.BlockSpec((tm * pack_rows, LANES), lambda i: (i, 0)),
                   per_token(TOP_K), per_token(TOP_K), pl.BlockSpec((N_EXPERTS, 1), lambda i: (0, 0))],
        out_shape=[jax.ShapeDtypeStruct((t, d), f32),
                   jax.ShapeDtypeStruct((t * pack_rows, LANES), u32),
                   jax.ShapeDtypeStruct((TOP_K, t), i32), jax.ShapeDtypeStruct((TOP_K, t), f32),
                   jax.ShapeDtypeStruct((N_EXPERTS, 1), f32)],
        compiler_params=_params(("arbitrary",)),
        name="mix_route",
    )(*args)


def _positions_kernel(idx_ref, start_ref, o_ref, run_ref):
    step = pl.program_id(0)
    tm = idx_ref.shape[1]

    @pl.when(step == 0)
    def _():
        run_ref[...] = jnp.zeros_like(run_ref)

    idx = idx_ref[...]
    eid = lax.broadcasted_iota(i32, (N_EXPERTS, tm), 0)
    onehot = jnp.zeros((N_EXPERTS, tm), f32)
    for k in range(TOP_K):
        onehot = jnp.where(eid == idx[k:k + 1], 1.0, onehot)
    r = lax.broadcasted_iota(i32, (tm, tm), 0)
    c = lax.broadcasted_iota(i32, (tm, tm), 1)
    earlier = jnp.where(r < c, 1.0, 0.0).astype(bf16)
    prior = jnp.dot(onehot.astype(bf16), earlier, preferred_element_type=f32)
    pos = prior + run_ref[...] + start_ref[...]
    row_o = lax.broadcasted_iota(i32, (TOP_K, tm), 0)
    out = jnp.zeros((TOP_K, tm), i32)
    for k in range(TOP_K):
        dk = jnp.sum(jnp.where(eid == idx[k:k + 1], pos, 0.0), axis=0, keepdims=True)
        out = jnp.where(row_o == k, dk.astype(i32), out)
    o_ref[...] = out
    run_ref[...] += jnp.sum(onehot, axis=1, keepdims=True)


def _positions(idx, seg_start, tm):
    t = idx.shape[1]
    return pl.pallas_call(
        _positions_kernel,
        grid=(t // tm,),
        in_specs=[pl.BlockSpec((TOP_K, tm), lambda i: (0, i)),
                  pl.BlockSpec((N_EXPERTS, 1), lambda i: (0, 0))],
        out_specs=pl.BlockSpec((TOP_K, tm), lambda i: (0, i)),
        out_shape=jax.ShapeDtypeStruct((TOP_K, t), i32),
        scratch_shapes=[pltpu.VMEM((N_EXPERTS, 1), f32)],
        compiler_params=_params(("arbitrary",)),
        name="positions",
    )(idx, seg_start)


def _dispatch_kernel(dest_hbm, h_ref, o_hbm, dest_smem, sem_i, sem, *, r):
    step = pl.program_id(0)
    n = dest_smem.shape[0]
    load = pltpu.make_async_copy(dest_hbm.at[pl.ds(pl.multiple_of(step * n, n), n)], dest_smem, sem_i)
    load.start()
    load.wait()

    def issue(tok, _):
        src = pl.multiple_of(tok * r, r)
        for k in range(TOP_K):
            dst = pl.multiple_of(dest_smem[tok * TOP_K + k] * r, r)
            pltpu.make_async_copy(h_ref.at[pl.ds(src, r)], o_hbm.at[pl.ds(dst, r)], sem).start(priority=k % 2)
        return 0

    lax.fori_loop(0, n // TOP_K, issue, 0)
    everything = o_hbm.at[pl.ds(0, n * r)]
    pltpu.make_async_copy(everything, everything, sem).wait()


def _dispatch(dest_flat, h2p, n_buf, r, tm):
    t = h2p.shape[0] // r
    return pl.pallas_call(
        functools.partial(_dispatch_kernel, r=r),
        grid=(t // tm,),
        in_specs=[pl.BlockSpec(memory_space=pl.ANY),
                  pl.BlockSpec((tm * r, LANES), lambda i: (i, 0))],
        out_specs=pl.BlockSpec(memory_space=pl.ANY),
        out_shape=jax.ShapeDtypeStruct((n_buf * r, LANES), u32),
        scratch_shapes=[pltpu.SMEM((tm * TOP_K,), i32), pltpu.SemaphoreType.DMA, pltpu.SemaphoreType.DMA],
        compiler_params=_params(("arbitrary",)),
        name="dispatch",
    )(dest_flat, h2p)


def _experts_kernel(be_ref, bv_ref, nu_ref, x_ref, wg_ref, wu_ref, wd_ref, o_ref, wg_s, wu_s, wd_s):
    j = pl.program_id(0)
    prev = be_ref[jnp.maximum(j - 1, 0)]
    used = j < nu_ref[0]
    d = wg_s.shape[0]

    @pl.when(used & ((j == 0) | (be_ref[j] != prev)))
    def _():
        wg_s[...] = wg_ref[0].astype(bf16)
        wu_s[...] = wu_ref[0].astype(bf16)
        wd_s[...] = wd_ref[0].astype(bf16)

    @pl.when(used)
    def _():
        live = lax.broadcasted_iota(i32, (ROW_BLOCK, 1), 0) < bv_ref[j]
        x = jnp.where(live, _load_packed_rows(x_ref, ROW_BLOCK, d), 0.0).astype(bf16)
        hg = jnp.dot(x, wg_s[...], preferred_element_type=f32)
        hu = jnp.dot(x, wu_s[...], preferred_element_type=f32)
        hid = (_silu(hg) * hu).astype(bf16)
        _store_packed_rows(o_ref, jnp.dot(hid, wd_s[...], preferred_element_type=f32))


def _experts(blk_e, blk_valid, n_used, xs, w_gate, w_up, w_down):
    d, f = w_gate.shape[1], w_gate.shape[2]
    r = d // 2 // LANES
    n_blk = xs.shape[0] // r // ROW_BLOCK
    rows = lambda j, be, bv, nu: (jnp.minimum(j, nu[0] - 1), 0)
    wsel = lambda j, be, bv, nu: (be[j], 0, 0)
    return pl.pallas_call(
        _experts_kernel,
        grid_spec=pltpu.PrefetchScalarGridSpec(
            num_scalar_prefetch=3,
            grid=(n_blk,),
            in_specs=[pl.BlockSpec((ROW_BLOCK * r, LANES), rows),
                      pl.BlockSpec((1, d, f), wsel),
                      pl.BlockSpec((1, d, f), wsel),
                      pl.BlockSpec((1, f, d), wsel)],
            out_specs=pl.BlockSpec((ROW_BLOCK * r, LANES), rows),
            scratch_shapes=[pltpu.VMEM((d, f), bf16), pltpu.VMEM((d, f), bf16), pltpu.VMEM((f, d), bf16)]),
        out_shape=jax.ShapeDtypeStruct(xs.shape, u32),
        compiler_params=_params(("arbitrary",)),
        name="experts",
    )(blk_e, blk_valid, n_used, xs, w_gate, w_up, w_down)


def _combine_kernel(dest_hbm, ys_hbm, wt_ref, h_ref, x1_ref, mod_ref, wsg_ref, wsu_ref, wsd_ref,
                    lg_ref, lb_ref, o_ref, dest_a, dest_b, rows_a, rows_b, sem_d, sem_g):
    step = pl.program_id(0)
    n_steps = pl.num_programs(0)
    tm, d = x1_ref.shape
    r = d // 2 // LANES
    n = tm * TOP_K
    tables, rows = (dest_a, dest_b), (rows_a, rows_b)

    def table_copy(tile, slot):
        return pltpu.make_async_copy(dest_hbm.at[pl.ds(pl.multiple_of(tile * n, n), n)],
                                     tables[slot], sem_d.at[slot])

    def start_gathers(slot):
        def issue(tok, _):
            dst = pl.multiple_of(tok * r, r)
            for k in range(TOP_K):
                src = pl.multiple_of(tables[slot][tok * TOP_K + k] * r, r)
                pltpu.make_async_copy(ys_hbm.at[pl.ds(src, r)], rows[slot].at[k, pl.ds(dst, r)],
                                      sem_g.at[slot]).start(priority=k % 2)
            return 0

        lax.fori_loop(0, tm, issue, 0)

    @pl.when(step == 0)
    def _():
        table_copy(0, 0).start()
        table_copy(0, 0).wait()
        start_gathers(0)

        @pl.when(n_steps > 1)
        def _():
            table_copy(1, 1).start()

    def tile_body(slot):
        @pl.when(step + 1 < n_steps)
        def _():
            table_copy(step + 1, 1 - slot).wait()
            start_gathers(1 - slot)

            @pl.when(step + 2 < n_steps)
            def _():
                table_copy(step + 2, slot).start()

        h = _load_packed_rows(h_ref, tm, d).astype(bf16)
        hg = jnp.dot(h, wsg_ref[...], preferred_element_type=f32)
        hu = jnp.dot(h, wsu_ref[...], preferred_element_type=f32)
        y = jnp.dot((_silu(hg) * hu).astype(bf16), wsd_ref[...], preferred_element_type=f32)

        pltpu.make_async_copy(rows[slot], rows[slot], sem_g.at[slot]).wait()
        wt = wt_ref[...]
        lo = [jnp.zeros((tm, LANES), f32) for _ in range(r)]
        hi = [jnp.zeros((tm, LANES), f32) for _ in range(r)]
        for k in range(TOP_K):
            wk = wt[:, k:k + 1]
            for c in range(r):
                pl_, ph_ = _unpack_words(rows[slot][k, pl.ds(c, tm, stride=r), :])
                lo[c] = lo[c] + wk * pl_
                hi[c] = hi[c] + wk * ph_
        y = y + jnp.concatenate(lo + hi, axis=1)
        m = mod_ref[0]
        o_ref[...] = _layer_norm(DEEPNORM_ALPHA * x1_ref[...] + m[5:6] * y, lg_ref[...], lb_ref[...])

    for slot in range(2):
        pl.when(step % 2 == slot)(functools.partial(tile_body, slot))


def _combine(dest_flat, ys, wts, h2p, x1, mod, w_s_gate, w_s_up, w_s_down, ln_g, ln_b, s, tm):
    t, d = x1.shape
    per = s // tm
    r = d // 2 // LANES
    row = lambda a: a.reshape(1, -1)
    tile = lambda w: pl.BlockSpec((tm, w), lambda i: (i, 0))
    full = lambda a: pl.BlockSpec(a.shape, lambda i: (0,) * a.ndim)
    tail = (w_s_gate.astype(bf16), w_s_up.astype(bf16), w_s_down.astype(bf16), row(ln_g), row(ln_b))
    return pl.pallas_call(
        _combine_kernel,
        grid=(t // tm,),
        in_specs=[pl.BlockSpec(memory_space=pl.ANY), pl.BlockSpec(memory_space=pl.ANY),
                  tile(LANES), pl.BlockSpec((tm * r, LANES), lambda i: (i, 0)), tile(d),
                  pl.BlockSpec((1, 6, d), lambda i: (i // per, 0, 0))] + [full(a) for a in tail],
        out_specs=tile(d),
        out_shape=jax.ShapeDtypeStruct((t, d), f32),
        scratch_shapes=[pltpu.SMEM((tm * TOP_K,), i32), pltpu.SMEM((tm * TOP_K,), i32),
                        pltpu.VMEM((TOP_K, tm * r, LANES), u32), pltpu.VMEM((TOP_K, tm * r, LANES), u32),
                        pltpu.SemaphoreType.DMA((2,)), pltpu.SemaphoreType.DMA((2,))],
        compiler_params=_params(("arbitrary",)),
        name="combine",
    )(dest_flat, ys, wts, h2p, x1, mod, *tail)


def _layer(x, mod, w_in, pe, w_cmp1, w_cmp2, w_dw, b_dw, conv_ln_g, conv_ln_b, w_out, ln1_g, ln1_b,
           w_router, router_bias, w_e_gate, w_e_up, w_e_down, w_s_gate, w_s_up, w_s_down, ln2_g, ln2_b):
    b, s, d = x.shape
    t = b * s
    tm = min(512, s)
    qt, vt, gt, kvc, kn, conv_in = _in_proj(x, mod, w_in, tm)
    cmp_n, cmp_t = _compress(kvc, pe, w_cmp1, w_cmp2)
    gates = gt[:, :3 * N_HEADS].reshape(b, N_KV, 3 * HPG, s)
    gates = jnp.pad(gates, ((0, 0), (0, 0), (0, GATE_GROUP_ROWS - 3 * HPG), (0, 0)))
    attn = _attention(qt, gates, cmp_n, cmp_t, kn, vt)
    conv = _conv(conv_in, w_dw, b_dw, conv_ln_g, conv_ln_b, tm)

    tr = min(256, s)
    x1, h2p, idx, wts, counts = _mix_route(attn.reshape(t, -1), conv.reshape(t, -1), x.reshape(t, d), mod,
                                           w_out, ln1_g, ln1_b, w_router, router_bias, s, tr)
    counts = counts[:, 0].astype(i32)
    padded = (counts + ROW_BLOCK - 1) // ROW_BLOCK * ROW_BLOCK
    seg_end = jnp.cumsum(padded)
    seg_start = seg_end - padded
    n_blk = -(-(t * TOP_K + N_EXPERTS * (ROW_BLOCK - 1)) // ROW_BLOCK)
    blk_row0 = jnp.arange(n_blk, dtype=i32) * ROW_BLOCK
    owns = (blk_row0[:, None] >= seg_start[None, :]) & (blk_row0[:, None] < seg_end[None, :])
    blk_e = jnp.where(blk_row0 < seg_end[-1], jnp.argmax(owns, axis=1), N_EXPERTS - 1).astype(i32)
    live_end = jnp.sum(jnp.where(owns, (seg_start + counts)[None, :], 0), axis=1)
    blk_valid = jnp.clip(live_end - blk_row0, 0, ROW_BLOCK).astype(i32)
    n_used = (seg_end[-1:] // ROW_BLOCK).astype(i32)

    dest = _positions(idx, seg_start.astype(f32).reshape(N_EXPERTS, 1), tr)
    dest_flat = dest.T.reshape(-1)
    wts = jnp.zeros((t, LANES), f32).at[:, :TOP_K].set(wts.T)
    xs = _dispatch(dest_flat, h2p, n_blk * ROW_BLOCK, d // 2 // LANES, min(512, s))
    ys = _experts(blk_e, blk_valid, n_used, xs, w_e_gate, w_e_up, w_e_down)
    out = _combine(dest_flat, ys, wts, h2p, x1, mod, w_s_gate, w_s_up, w_s_down, ln2_g, ln2_b, s, min(256, s))
    return out.reshape(b, s, d)


def kernel(x, c, w_ada, b_ada, w_in, pe_k, pe_v, w_cmp_k1, w_cmp_k2, w_cmp_v1, w_cmp_v2, w_dw, b_dw,
           conv_ln_g, conv_ln_b, w_out, ln1_g, ln1_b, w_router, router_bias, w_e_gate, w_e_up, w_e_down,
           w_s_gate, w_s_up, w_s_down, ln2_g, ln2_b):
    assert w_ada.shape[0] == DEPTH
    layer = lambda a: a.reshape(a.shape[1:])
    mod = _ada(c, layer(w_ada), layer(b_ada))
    return _layer(x, mod, layer(w_in), jnp.concatenate([pe_k, pe_v]),
                  jnp.concatenate([w_cmp_k1, w_cmp_v1]), jnp.concatenate([w_cmp_k2, w_cmp_v2]),
                  *[layer(a) for a in (w_dw, b_dw, conv_ln_g, conv_ln_b, w_out, ln1_g, ln1_b, w_router,
                                       router_bias, w_e_gate, w_e_up, w_e_down, w_s_gate, w_s_up, w_s_down,
                                       ln2_g, ln2_b)])
```

```python
import functools

import jax
import jax.numpy as jnp
from jax import lax
from jax.experimental import pallas as pl
from jax.experimental.pallas import tpu as pltpu

N_HEADS = 8
N_KV = 2
HPG = N_HEADS // N_KV
HEAD_DIM = 64
D_ATTN = N_HEADS * HEAD_DIM
D_KV = N_KV * HEAD_DIM
CONV_WIDTH = 31
CMP_LEN = 32
CMP_STRIDE = 16
CMP_HID = 256
SEL_LEN = 64
SEL_TOPN = 16
WINDOW = 512
Q_BLOCK = 256
N_EXPERTS = 256
TOP_K = 8
N_GROUPS = 8
TOPK_GROUPS = 4
ROUTED_SCALE = 2.5
LN_EPS = 1e-5
DEPTH = 1
DEEPNORM_ALPHA = (2 * DEPTH) ** 0.25

LANES = 128
SUBLANES = 8
ROW_BLOCK = 512
NEG = -1e30
HIGHEST = lax.Precision.HIGHEST
VMEM_LIMIT = 48 * 1024 * 1024

f32 = jnp.float32
bf16 = jnp.bfloat16
i32 = jnp.int32
u32 = jnp.uint32


def _params(sem, vmem=VMEM_LIMIT):
    return pltpu.CompilerParams(dimension_semantics=sem, vmem_limit_bytes=vmem)


def _sigmoid(v):
    return 1.0 / (1.0 + jnp.exp(-v))


def _silu(v):
    return v * _sigmoid(v)


def _layer_norm(v, g, b):
    mu = jnp.mean(v, axis=-1, keepdims=True)
    var = jnp.mean(jnp.square(v - mu), axis=-1, keepdims=True)
    return (v - mu) * lax.rsqrt(var + LN_EPS) * g + b


def _dot_nt(a, b):
    return lax.dot_general(a, b, (((1,), (1,)), ((), ())), preferred_element_type=f32)


def _store_packed_rows(ref, v, first=0):
    n, d = v.shape
    half = d // 2
    bits = lax.bitcast_convert_type(v.astype(bf16).astype(f32), u32)
    words = bits[:, half:] | (bits[:, :half] >> 16)
    r = half // LANES
    for c in range(r):
        ref[pl.ds(first * r + c, n, stride=r), :] = words[:, c * LANES:(c + 1) * LANES]


def _unpack_words(w):
    return (lax.bitcast_convert_type(w << 16, f32),
            lax.bitcast_convert_type(w & jnp.uint32(0xFFFF0000), f32))


def _load_packed_rows(ref, n, d):
    r = d // 2 // LANES
    parts = [_unpack_words(ref[pl.ds(c, n, stride=r), :]) for c in range(r)]
    return jnp.concatenate([p[0] for p in parts] + [p[1] for p in parts], axis=1)


def _ada_kernel(c_ref, w_ref, b_ref, o_ref):
    c = c_ref[...]
    o_ref[...] = jnp.dot(_silu(c), w_ref[...], precision=HIGHEST,
                         preferred_element_type=f32) + b_ref[...]


def _ada(c, w_ada, b_ada):
    b, d = c.shape
    n = w_ada.shape[1]
    rows = 8
    c_pad = jnp.zeros((rows, d), f32).at[:b].set(c)
    tn = 1024
    out = pl.pallas_call(
        _ada_kernel,
        grid=(n // tn,),
        in_specs=[pl.BlockSpec((rows, d), lambda j: (0, 0)),
                  pl.BlockSpec((d, tn), lambda j: (0, j)),
                  pl.BlockSpec((1, tn), lambda j: (0, j))],
        out_specs=pl.BlockSpec((rows, tn), lambda j: (0, j)),
        out_shape=jax.ShapeDtypeStruct((rows, n), f32),
        compiler_params=_params(("arbitrary",)),
        name="ada",
    )(c_pad, w_ada, b_ada.reshape(1, n))
    return out[:b].reshape(b, 6, d)


GATE_ROWS = 32
KEY_TILE = 512
TILE_BLOCKS = KEY_TILE // SEL_LEN
PATCH_BLOCKS = Q_BLOCK // SEL_LEN
PAIR_UNROLL = 2
CMP_PER_SEL = SEL_LEN // CMP_STRIDE
CMP_BACK = CMP_LEN // CMP_STRIDE - 1
assert SEL_LEN % CMP_STRIDE == 0 and CMP_LEN % CMP_STRIDE == 0 and CMP_BACK < CMP_PER_SEL
V_ROWS = HEAD_DIM + 16
Q_SCALE = HEAD_DIM ** -0.5 * 1.4426950408889634


def _in_proj_kernel(x_ref, mod_ref, wt_ref, wc_ref, wk_ref, wv_ref,
                    qt_ref, vt_ref, gt_ref, kvc_ref, kn_ref, cv_ref):
    m = mod_ref[0]
    h = (x_ref[0] * (1.0 + m[1:2]) + m[0:1]).astype(bf16)
    res_t = _dot_nt(wt_ref[...], h)
    qt_ref[0] = (res_t[0:D_ATTN] * Q_SCALE).astype(bf16)
    ones = jnp.ones((V_ROWS - HEAD_DIM, res_t.shape[1]), bf16)
    for j in range(2):
        for g in range(N_KV):
            off = D_ATTN + (j * N_KV + g) * HEAD_DIM
            vt_ref[0, j, g, 0:HEAD_DIM, :] = res_t[off:off + HEAD_DIM].astype(bf16)
            vt_ref[0, j, g, HEAD_DIM:V_ROWS, :] = ones
    gt_ref[0] = res_t[D_ATTN + 2 * D_KV:]
    kvc = jnp.dot(h, wc_ref[...], preferred_element_type=f32)
    kvc_ref[0, 0] = kvc[:, :D_KV]
    kvc_ref[0, 1] = kvc[:, D_KV:]
    kn = jnp.dot(h, wk_ref[...], preferred_element_type=f32).astype(bf16)
    tm = kn.shape[0]
    pos = pl.program_id(1) * tm + lax.broadcasted_iota(i32, (tm, LANES - HEAD_DIM), 0)
    lane = lax.broadcasted_iota(i32, (tm, LANES - HEAD_DIM), 1)
    onehot = jnp.where(lane == ((pos >> 6) & (TILE_BLOCKS - 1)), 1.0, 0.0).astype(bf16)
    for j in range(2):
        for g in range(N_KV):
            off = (j * N_KV + g) * HEAD_DIM
            kn_ref[0, j, g] = jnp.concatenate([kn[:, off:off + HEAD_DIM], onehot], axis=1)
    cv_ref[0] = jnp.dot(h, wv_ref[...], preferred_element_type=f32)


def _in_proj(x, mod, w_in, tm):
    b, s, d = x.shape
    o = 0
    wq = w_in[:, o:o + D_ATTN]; o += D_ATTN
    wkc = w_in[:, o:o + 2 * D_KV]; o += 2 * D_KV
    wk_s = w_in[:, o:o + D_KV]; o += D_KV
    wv_s = w_in[:, o:o + D_KV]; o += D_KV
    wk_w = w_in[:, o:o + D_KV]; o += D_KV
    wv_w = w_in[:, o:o + D_KV]; o += D_KV
    wg = w_in[:, o:o + 3 * N_HEADS]; o += 3 * N_HEADS
    wcv = w_in[:, o:]
    d_conv2 = wcv.shape[1]
    wg = jnp.zeros((d, GATE_ROWS), f32).at[:, :3 * N_HEADS].set(wg)
    wt = jnp.concatenate([wq, wv_s, wv_w, wg], axis=1).T
    ws = [w.astype(bf16) for w in (wt, wkc, jnp.concatenate([wk_s, wk_w], axis=1), wcv)]
    full = lambda a: pl.BlockSpec(a.shape, lambda bi, i: (0, 0))
    return pl.pallas_call(
        _in_proj_kernel,
        grid=(b, s // tm),
        in_specs=[pl.BlockSpec((1, tm, d), lambda bi, i: (bi, i, 0)),
                  pl.BlockSpec((1, 6, d), lambda bi, i: (bi, 0, 0))] + [full(w) for w in ws],
        out_specs=[pl.BlockSpec((1, D_ATTN, tm), lambda bi, i: (bi, 0, i)),
                   pl.BlockSpec((1, 2, N_KV, V_ROWS, tm), lambda bi, i: (bi, 0, 0, 0, i)),
                   pl.BlockSpec((1, GATE_ROWS, tm), lambda bi, i: (bi, 0, i)),
                   pl.BlockSpec((1, 2, tm, D_KV), lambda bi, i: (bi, 0, i, 0)),
                   pl.BlockSpec((1, 2, N_KV, tm, LANES), lambda bi, i: (bi, 0, 0, i, 0)),
                   pl.BlockSpec((1, tm, d_conv2), lambda bi, i: (bi, i, 0))],
        out_shape=[jax.ShapeDtypeStruct((b, D_ATTN, s), bf16),
                   jax.ShapeDtypeStruct((b, 2, N_KV, V_ROWS, s), bf16),
                   jax.ShapeDtypeStruct((b, GATE_ROWS, s), f32),
                   jax.ShapeDtypeStruct((b, 2, s, D_KV), f32),
                   jax.ShapeDtypeStruct((b, 2, N_KV, s, LANES), bf16),
                   jax.ShapeDtypeStruct((b, s, d_conv2), f32)],
        compiler_params=_params(("parallel", "parallel")),
        name="in_proj",
    )(x, mod, *ws)


def _compress_kernel(x_ref, pe_ref, w1_ref, w2_ref, w2t_ref, o_ref, ot_ref):
    n_chunk = o_ref.shape[3]
    for j in range(2):
        for g in range(N_KV):
            cols = slice(g * HEAD_DIM, (g + 1) * HEAD_DIM)
            a = jnp.zeros((n_chunk, CMP_HID), f32)
            bm = jnp.zeros((n_chunk, CMP_HID), f32)
            for l in range(CMP_STRIDE):
                xl = x_ref[0, j, pl.ds(l, n_chunk, stride=CMP_STRIDE), :][:, cols]
                a = a + jnp.dot((xl + pe_ref[j, l:l + 1, :]).astype(bf16), w1_ref[j, l],
                                preferred_element_type=f32)
                bm = bm + jnp.dot((xl + pe_ref[j, CMP_STRIDE + l:CMP_STRIDE + l + 1, :]).astype(bf16),
                                  w1_ref[j, CMP_STRIDE + l], preferred_element_type=f32)
            hid = a + pltpu.roll(bm, n_chunk - 1, 0)
            act = 0.5 * hid * (1.0 + jnp.tanh(0.7978845608028654 * (hid + 0.044715 * (hid * hid * hid))))
            act = act.astype(bf16)
            o_ref[0, j, g] = jnp.dot(act, w2_ref[j], preferred_element_type=f32).astype(bf16)
            ot_ref[0, j, g] = _dot_nt(w2t_ref[j], act).astype(bf16)


def _compress(kvc, pe, w1, w2):
    assert CMP_LEN == 2 * CMP_STRIDE
    b, _, s, width = kvc.shape
    n_chunk = s // CMP_STRIDE
    w1b = w1.reshape(2, CMP_LEN, HEAD_DIM, CMP_HID).astype(bf16)
    w2b = w2.astype(bf16)
    w2t = w2b.transpose(0, 2, 1)
    full = lambda a: pl.BlockSpec(a.shape, lambda bi: (0,) * a.ndim)
    return pl.pallas_call(
        _compress_kernel,
        grid=(b,),
        in_specs=[pl.BlockSpec((1, 2, s, width), lambda bi: (bi, 0, 0, 0)),
                  full(pe), full(w1b), full(w2b), full(w2t)],
        out_specs=[pl.BlockSpec((1, 2, N_KV, n_chunk, HEAD_DIM), lambda bi: (bi, 0, 0, 0, 0)),
                   pl.BlockSpec((1, 2, N_KV, HEAD_DIM, n_chunk), lambda bi: (bi, 0, 0, 0, 0))],
        out_shape=[jax.ShapeDtypeStruct((b, 2, N_KV, n_chunk, HEAD_DIM), bf16),
                   jax.ShapeDtypeStruct((b, 2, N_KV, HEAD_DIM, n_chunk), bf16)],
        compiler_params=_params(("parallel",)),
        name="compress",
    )(kvc, pe, w1b, w2b, w2t)


def _attn_kernel(q_ref, g_ref, kc_ref, vc_ref, kn_ref, vt_ref, o_ref, psum_ref, selbias_ref, sa_ref, sb_ref,
                 sd_ref, *, seq, tk, top_n):
    i = pl.program_id(2)
    s0 = i * Q_BLOCK
    n_cmp_rows = kc_ref.shape[3]
    n_sel = seq // SEL_LEN

    q4 = q_ref[0]
    qt = jnp.concatenate([q4[p * HEAD_DIM:(p + 1) * HEAD_DIM, :] for p in range(HPG)], axis=1)
    t_row = s0 + lax.broadcasted_iota(i32, (1, Q_BLOCK), 1)

    s_c = jnp.dot(kc_ref[0, 0, 0], qt, preferred_element_type=f32)
    cmp_end = lax.broadcasted_iota(i32, (n_cmp_rows, 1), 0) * CMP_STRIDE + (CMP_LEN - 1)
    bias_c = jnp.where(cmp_end <= t_row, 0.0, NEG)
    any_c = t_row >= CMP_LEN - 1
    p_sum = jnp.zeros((n_cmp_rows, Q_BLOCK), f32)
    pcs = []
    for p in range(HPG):
        sp = s_c[:, p * Q_BLOCK:(p + 1) * Q_BLOCK] + bias_c
        e = jnp.exp2(sp - jnp.max(sp, axis=0, keepdims=True))
        pn = e * jnp.where(any_c, 1.0 / jnp.sum(e, axis=0, keepdims=True), 0.0)
        p_sum = p_sum + pn
        pcs.append(pn.astype(bf16))
    o_c = jnp.dot(vc_ref[0, 0, 0], jnp.concatenate(pcs, axis=1), preferred_element_type=f32)

    cols = HPG * Q_BLOCK
    pad_rows = LANES - HEAD_DIM
    q_pad = jnp.concatenate([qt, jnp.zeros((pad_rows, cols), bf16)], axis=0)
    span = WINDOW + Q_BLOCK
    w0 = pl.multiple_of(jnp.maximum(s0 - WINDOW, 0), Q_BLOCK)
    s_w = jnp.dot(kn_ref[0, 1, 0, pl.ds(w0, span), :], q_pad, preferred_element_type=f32)
    wpos = w0 + lax.broadcasted_iota(i32, (span, 1), 0)
    bias_w = jnp.where((wpos <= t_row) & (wpos > t_row - WINDOW), 0.0, NEG)
    pws = []
    for p in range(HPG):
        sp = s_w[:, p * Q_BLOCK:(p + 1) * Q_BLOCK] + bias_w
        pws.append(jnp.exp2(sp - jnp.max(sp, axis=0, keepdims=True)).astype(bf16))
    acc_w = jnp.dot(vt_ref[0, 1, 0, :, pl.ds(w0, span)], jnp.concatenate(pws, axis=1),
                    preferred_element_type=f32)
    o_w = acc_w[0:HEAD_DIM] * (1.0 / acc_w[HEAD_DIM:HEAD_DIM + 1])
    d0 = pl.multiple_of(s0, Q_BLOCK)
    s_d = jnp.dot(kn_ref[0, 0, 0, pl.ds(d0, Q_BLOCK), :], q_pad, preferred_element_type=f32)

    blk = lax.broadcasted_iota(i32, (n_sel, Q_BLOCK), 0)
    for lt in range(Q_BLOCK // LANES):
        psum_ref[lt] = p_sum[:, lt * LANES:(lt + 1) * LANES]

    def every(first):
        return jnp.concatenate([psum_ref[lt, pl.ds(first, n_sel, stride=CMP_PER_SEL), :]
                                for lt in range(Q_BLOCK // LANES)], axis=1)

    imp = every(0)
    for r in range(1, CMP_PER_SEL):
        imp = imp + every(r)
    for back in range(1, CMP_BACK + 1):
        imp = imp + jnp.where(blk >= 1, pltpu.roll(every(CMP_PER_SEL - back), 1, 0), 0.0)

    cur = t_row >> 6
    forced = (blk == 0) | (blk == cur) | (blk == cur - 1)
    vals = jnp.where(forced, jnp.inf, jnp.where(blk <= cur, imp, -jnp.inf))
    sel = jnp.zeros((n_sel, Q_BLOCK), f32)
    for _ in range(top_n):
        mx = jnp.max(vals, axis=0, keepdims=True)
        first = jnp.min(jnp.where(vals == mx, blk, n_sel), axis=0, keepdims=True)
        pick = blk == first
        sel = jnp.where(pick & (mx > -jnp.inf), 1.0, sel)
        vals = jnp.where(pick, -jnp.inf, vals)
    first_blk = s0 >> 6
    full_at = n_sel + TILE_BLOCKS
    selbias_ref[0:n_sel] = jnp.where((sel > 0.5) & (blk < first_blk), 0.0, NEG)
    selbias_ref[n_sel:full_at] = jnp.full((TILE_BLOCKS, Q_BLOCK), NEG, f32)
    selbias_ref[full_at:full_at + n_sel] = jnp.where(sel > 0.5, 0.0, NEG)
    last_tile = seq // tk - 1

    def scores(kt, s_ref):
        k0 = pl.multiple_of(jnp.minimum(kt, last_tile) * tk, tk)
        sb = selbias_ref[pl.ds(pl.multiple_of(kt * TILE_BLOCKS, TILE_BLOCKS), TILE_BLOCKS), :]
        rows = jnp.concatenate([jnp.concatenate([sb] * HPG, axis=1),
                                jnp.zeros((pad_rows - TILE_BLOCKS, cols), f32)], axis=0).astype(bf16)
        s_ref[...] = jnp.dot(kn_ref[0, 0, 0, pl.ds(k0, tk), :], jnp.concatenate([qt, rows], axis=0),
                             preferred_element_type=f32)

    def fold(scores_of_head, v_t, carry):
        m_i, acc = carry
        es, ms, alphas = [], [], []
        for p in range(HPG):
            c = slice(p * Q_BLOCK, (p + 1) * Q_BLOCK)
            sp = scores_of_head(c)
            m_new = jnp.maximum(m_i[:, c], jnp.max(sp, axis=0, keepdims=True))
            es.append(jnp.exp2(sp - m_new).astype(bf16))
            alphas.append(jnp.exp2(m_i[:, c] - m_new))
            ms.append(m_new)
        pv = jnp.dot(v_t, jnp.concatenate(es, axis=1), preferred_element_type=f32)
        return jnp.concatenate(ms, axis=1), jnp.concatenate(alphas, axis=1) * acc + pv

    def sel_tile(kt, s_ref, carry):
        k0 = pl.multiple_of(jnp.minimum(kt, last_tile) * tk, tk)
        return fold(lambda c: s_ref[:, c], vt_ref[0, 0, 0, :, pl.ds(k0, tk)], carry)

    def sel_pair(i, carry):
        scores(2 * i + 1, sb_ref)
        carry = sel_tile(2 * i, sa_ref, carry)
        scores(2 * i + 2, sa_ref)
        return sel_tile(2 * i + 1, sb_ref, carry)

    n_sweep = (s0 + tk - 1) // tk
    n_pairs = (n_sweep + 1) // 2
    init = (jnp.full((1, cols), NEG, f32), jnp.zeros((V_ROWS, cols), f32))
    scores(0, sa_ref)
    def sel_pairs(j, carry):
        for u in range(PAIR_UNROLL):
            carry = sel_pair(PAIR_UNROLL * j + u, carry)
        return carry

    group = selbias_ref[pl.ds(full_at + pl.multiple_of((first_blk >> 3) << 3, SUBLANES), SUBLANES), :]
    own = group[0:PATCH_BLOCKS]
    for at in range(PATCH_BLOCKS, SUBLANES, PATCH_BLOCKS):
        own = jnp.where((first_blk & (SUBLANES - 1)) == at, group[at:at + PATCH_BLOCKS], own)
    bias_d = jnp.concatenate([jnp.broadcast_to(own[j:j + 1], (SEL_LEN, Q_BLOCK)) for j in range(PATCH_BLOCKS)],
                             axis=0)
    bias_d = jnp.where(s0 + lax.broadcasted_iota(i32, (Q_BLOCK, 1), 0) <= t_row, bias_d, NEG)
    for p in range(HPG):
        c = slice(p * Q_BLOCK, (p + 1) * Q_BLOCK)
        sd_ref[:, c] = s_d[:, c] + bias_d

    carry = lax.fori_loop(0, n_pairs // PAIR_UNROLL, sel_pairs, init)
    carry = lax.fori_loop(n_pairs // PAIR_UNROLL * PAIR_UNROLL, n_pairs, sel_pair, carry)
    _, acc_s = fold(lambda c: sd_ref[:, c], vt_ref[0, 0, 0, :, pl.ds(d0, Q_BLOCK)], carry)
    o_s = acc_s[0:HEAD_DIM] * (1.0 / acc_s[HEAD_DIM:HEAD_DIM + 1])

    gate = _sigmoid(g_ref[0, 0])
    outs = []
    for p in range(HPG):
        c = slice(p * Q_BLOCK, (p + 1) * Q_BLOCK)
        outs.append(gate[3 * p:3 * p + 1] * o_c[:, c] + gate[3 * p + 1:3 * p + 2] * o_s[:, c]
                    + gate[3 * p + 2:3 * p + 3] * o_w[:, c])
    o_ref[0] = jnp.concatenate(outs, axis=0).T.astype(bf16)


GATE_GROUP_ROWS = 16


def _attention(qt, gates, cmp_n, cmp_t, kn, vt):
    b, _, s = qt.shape
    n_chunk = cmp_n.shape[3]
    n_sel = s // SEL_LEN
    top_n = min(SEL_TOPN, n_sel)
    tk = KEY_TILE
    assert s % tk == 0 and n_chunk == n_sel * CMP_PER_SEL
    gw = HPG * HEAD_DIM
    return pl.pallas_call(
        functools.partial(_attn_kernel, seq=s, tk=tk, top_n=top_n),
        grid=(b, N_KV, s // Q_BLOCK),
        in_specs=[pl.BlockSpec((1, gw, Q_BLOCK), lambda bi, g, i: (bi, g, i)),
                  pl.BlockSpec((1, 1, GATE_GROUP_ROWS, Q_BLOCK), lambda bi, g, i: (bi, g, 0, i)),
                  pl.BlockSpec((1, 1, 1, n_chunk, HEAD_DIM), lambda bi, g, i: (bi, 0, g, 0, 0)),
                  pl.BlockSpec((1, 1, 1, HEAD_DIM, n_chunk), lambda bi, g, i: (bi, 1, g, 0, 0)),
                  pl.BlockSpec((1, 2, 1, s, LANES), lambda bi, g, i: (bi, 0, g, 0, 0)),
                  pl.BlockSpec((1, 2, 1, V_ROWS, s), lambda bi, g, i: (bi, 0, g, 0, 0))],
        out_specs=pl.BlockSpec((1, Q_BLOCK, gw), lambda bi, g, i: (bi, i, g)),
        out_shape=jax.ShapeDtypeStruct((b, s, D_ATTN), bf16),
        scratch_shapes=[pltpu.VMEM((Q_BLOCK // LANES, n_chunk, LANES), f32),
                        pltpu.VMEM((2 * n_sel + TILE_BLOCKS, Q_BLOCK), f32),
                        pltpu.VMEM((tk, HPG * Q_BLOCK), f32), pltpu.VMEM((tk, HPG * Q_BLOCK), f32),
                        pltpu.VMEM((Q_BLOCK, HPG * Q_BLOCK), f32)],
        compiler_params=_params(("parallel", "parallel", "arbitrary")),
        name="attention",
    )(qt, gates, cmp_n, cmp_t, kn, vt)


CONV_HALO = 32
CONV_ROWS = 32


def _conv_kernel(cur_ref, prev_ref, w_ref, b_ref, g_ref, bb_ref, o_ref, glu_ref):
    i = pl.program_id(1)
    ts = cur_ref.shape[1]
    dc = o_ref.shape[2]
    cur = cur_ref[0]
    prev = prev_ref[0]
    glu_prev = prev[:, :dc] * _sigmoid(prev[:, dc:])
    glu_ref[0:CONV_HALO] = jnp.where(i == 0, 0.0, glu_prev)
    glu_ref[CONV_HALO:CONV_HALO + ts] = cur[:, :dc] * _sigmoid(cur[:, dc:])
    lead = CONV_HALO - (CONV_WIDTH - 1)

    def chunk(r, _):
        r0 = pl.multiple_of(r * CONV_ROWS, CONV_ROWS)
        win = glu_ref[pl.ds(r0, CONV_ROWS + CONV_HALO), :]
        span = CONV_ROWS + CONV_HALO
        acc = jnp.zeros((CONV_ROWS, dc), f32)
        for ph in range(SUBLANES):
            turned = win if ph == 0 else pltpu.roll(win, span - ph, 0)
            for j in range(CONV_WIDTH):
                if (lead + j) % SUBLANES == ph:
                    at = lead + j - ph
                    acc = acc + w_ref[j:j + 1, :] * turned[at:at + CONV_ROWS]
        y = _layer_norm(acc + b_ref[...], g_ref[...], bb_ref[...])
        o_ref[0, pl.ds(r0, CONV_ROWS), :] = _silu(y).astype(bf16)
        return 0

    lax.fori_loop(0, ts // CONV_ROWS, chunk, 0)


def _conv(conv_in, w_dw, b_dw, ln_g, ln_b, ts):
    b, s, dc2 = conv_in.shape
    dc = dc2 // 2
    per = ts // CONV_HALO
    row = lambda a: a.reshape(1, dc)
    return pl.pallas_call(
        _conv_kernel,
        grid=(b, s // ts),
        in_specs=[pl.BlockSpec((1, ts, dc2), lambda bi, i: (bi, i, 0)),
                  pl.BlockSpec((1, CONV_HALO, dc2), lambda bi, i: (bi, jnp.maximum(i * per - 1, 0), 0)),
                  pl.BlockSpec((CONV_WIDTH, dc), lambda bi, i: (0, 0)),
                  pl.BlockSpec((1, dc), lambda bi, i: (0, 0)),
                  pl.BlockSpec((1, dc), lambda bi, i: (0, 0)),
                  pl.BlockSpec((1, dc), lambda bi, i: (0, 0))],
        out_specs=pl.BlockSpec((1, ts, dc), lambda bi, i: (bi, i, 0)),
        out_shape=jax.ShapeDtypeStruct((b, s, dc), bf16),
        scratch_shapes=[pltpu.VMEM((CONV_HALO + ts, dc), f32)],
        compiler_params=_params(("parallel", "parallel")),
        name="conv",
    )(conv_in, conv_in, w_dw.reshape(CONV_WIDTH, dc), row(b_dw), row(ln_g), row(ln_b))


ROUTE_HALF = 128


def _mix_route_kernel(a_ref, cv_ref, x_ref, mod_ref, wo_ref, lg_ref, lb_ref, wrh_ref, wrl_ref, rb_ref,
                      x1_ref, h2p_ref, idx_ref, wt_ref, cnt_ref):
    step = pl.program_id(0)

    @pl.when(step == 0)
    def _():
        cnt_ref[...] = jnp.zeros_like(cnt_ref)

    m = mod_ref[0]
    da = a_ref.shape[1]
    counts = jnp.zeros(cnt_ref.shape, f32)
    for first in range(0, x_ref.shape[0], ROUTE_HALF):
        rows = slice(first, first + ROUTE_HALF)
        mix = (jnp.dot(a_ref[rows, :], wo_ref[0:da, :], preferred_element_type=f32)
               + jnp.dot(cv_ref[rows, :], wo_ref[da:, :], preferred_element_type=f32))
        x1 = _layer_norm(DEEPNORM_ALPHA * x_ref[rows, :] + m[2:3] * mix, lg_ref[...], lb_ref[...])
        x1_ref[rows, :] = x1
        h2 = x1 * (1.0 + m[4:5]) + m[3:4]
        _store_packed_rows(h2p_ref, h2, first)
        h_hi = h2.astype(bf16)
        h_lo = (h2 - h_hi.astype(f32)).astype(bf16)
        logits = _dot_nt(wrh_ref[...], h_hi) + (_dot_nt(wrh_ref[...], h_lo) + _dot_nt(wrl_ref[...], h_hi))
        idx, wt, cnt = _route(_sigmoid(logits), rb_ref[...])
        idx_ref[:, rows] = idx
        wt_ref[:, rows] = wt
        counts = counts + cnt
    cnt_ref[...] += counts


def _route(score, bias):
    tm = score.shape[1]
    sel = score + bias
    eid = lax.broadcasted_iota(i32, (N_EXPERTS, tm), 0)
    per_group = N_EXPERTS // N_GROUPS
    gs = []
    for g in range(N_GROUPS):
        rows = slice(g * per_group, (g + 1) * per_group)
        v = sel[rows]
        e = g * per_group + lax.broadcasted_iota(i32, (per_group, tm), 0)
        m1 = jnp.max(v, axis=0, keepdims=True)
        i1 = jnp.min(jnp.where(v == m1, e, N_EXPERTS), axis=0, keepdims=True)
        m2 = jnp.max(jnp.where(e == i1, -jnp.inf, v), axis=0, keepdims=True)
        gs.append(m1 + m2)
    cands = []
    for g in range(N_GROUPS):
        rank = jnp.zeros((1, tm), i32)
        for o in range(N_GROUPS):
            if o == g:
                continue
            beats = (gs[o] > gs[g]) | (gs[o] == gs[g]) if o < g else gs[o] > gs[g]
            rank = rank + beats.astype(i32)
        drop = jnp.where(rank < TOPK_GROUPS, 0.0, -jnp.inf)
        cands.append(sel[g * per_group:(g + 1) * per_group] + drop)
    cand = jnp.concatenate(cands, axis=0)
    row_o = lax.broadcasted_iota(i32, (TOP_K, tm), 0)
    idx_out = jnp.zeros((TOP_K, tm), i32)
    wt_out = jnp.zeros((TOP_K, tm), f32)
    picked = jnp.zeros((N_EXPERTS, tm), f32)
    w_sum = jnp.zeros((1, tm), f32)
    for k in range(TOP_K):
        mx = jnp.max(cand, axis=0, keepdims=True)
        ik = jnp.min(jnp.where(cand == mx, eid, N_EXPERTS), axis=0, keepdims=True)
        pick = eid == ik
        wk = jnp.sum(jnp.where(pick, score, 0.0), axis=0, keepdims=True)
        cand = jnp.where(pick, -jnp.inf, cand)
        picked = jnp.where(pick, 1.0, picked)
        idx_out = jnp.where(row_o == k, ik, idx_out)
        wt_out = jnp.where(row_o == k, wk, wt_out)
        w_sum = w_sum + wk
    return idx_out, wt_out / w_sum * ROUTED_SCALE, jnp.sum(picked, axis=1, keepdims=True)


def _mix_route(attn, conv, x2, mod, w_out, ln_g, ln_b, w_router, router_bias, s, tm):
    t, d = x2.shape
    per = s // tm
    da = attn.shape[1]
    pack_rows = d // 2 // LANES
    row = lambda a: a.reshape(1, -1)
    tile = lambda w: pl.BlockSpec((tm, w), lambda i: (i, 0))
    full = lambda a: pl.BlockSpec(a.shape, lambda i: (0,) * a.ndim)
    assert tm % ROUTE_HALF == 0
    wr_hi = w_router.T.astype(bf16)
    wr_lo = (w_router.T - wr_hi.astype(f32)).astype(bf16)
    args = (attn, conv, x2, mod, w_out.astype(bf16), row(ln_g), row(ln_b), wr_hi, wr_lo,
            router_bias.reshape(N_EXPERTS, 1))
    per_token = lambda rows: pl.BlockSpec((rows, tm), lambda i: (0, i))
    return pl.pallas_call(
        _mix_route_kernel,
        grid=(t // tm,),
        in_specs=[tile(da), tile(conv.shape[1]), tile(d),
                  pl.BlockSpec((1, 6, d), lambda i: (i // per, 0, 0))] + [full(a) for a in args[4:]],
        out_specs=[tile(d), pl.BlockSpec((tm * pack_rows, LANES), lambda i: (i, 0)),
                   per_token(TOP_K), per_token(TOP_K), pl.BlockSpec((N_EXPERTS, 1), lambda i: (0, 0))],
        out_shape=[jax.ShapeDtypeStruct((t, d), f32),
                   jax.ShapeDtypeStruct((t * pack_rows, LANES), u32),
                   jax.ShapeDtypeStruct((TOP_K, t), i32), jax.ShapeDtypeStruct((TOP_K, t), f32),
                   jax.ShapeDtypeStruct((N_EXPERTS, 1), f32)],
        compiler_params=_params(("arbitrary",)),
        name="mix_route",
    )(*args)


def _positions_kernel(idx_ref, start_ref, o_ref, run_ref):
    step = pl.program_id(0)
    tm = idx_ref.shape[1]

    @pl.when(step == 0)
    def _():
        run_ref[...] = jnp.zeros_like(run_ref)

    idx = idx_ref[...]
    eid = lax.broadcasted_iota(i32, (N_EXPERTS, tm), 0)
    onehot = jnp.zeros((N_EXPERTS, tm), f32)
    for k in range(TOP_K):
        onehot = jnp.where(eid == idx[k:k + 1], 1.0, onehot)
    r = lax.broadcasted_iota(i32, (tm, tm), 0)
    c = lax.broadcasted_iota(i32, (tm, tm), 1)
    earlier = jnp.where(r < c, 1.0, 0.0).astype(bf16)
    prior = jnp.dot(onehot.astype(bf16), earlier, preferred_element_type=f32)
    pos = prior + run_ref[...] + start_ref[...]
    row_o = lax.broadcasted_iota(i32, (TOP_K, tm), 0)
    out = jnp.zeros((TOP_K, tm), i32)
    for k in range(TOP_K):
        dk = jnp.sum(jnp.where(eid == idx[k:k + 1], pos, 0.0), axis=0, keepdims=True)
        out = jnp.where(row_o == k, dk.astype(i32), out)
    o_ref[...] = out
    run_ref[...] += jnp.sum(onehot, axis=1, keepdims=True)


def _positions(idx, seg_start, tm):
    t = idx.shape[1]
    return pl.pallas_call(
        _positions_kernel,
        grid=(t // tm,),
        in_specs=[pl.BlockSpec((TOP_K, tm), lambda i: (0, i)),
                  pl.BlockSpec((N_EXPERTS, 1), lambda i: (0, 0))],
        out_specs=pl.BlockSpec((TOP_K, tm), lambda i: (0, i)),
        out_shape=jax.ShapeDtypeStruct((TOP_K, t), i32),
        scratch_shapes=[pltpu.VMEM((N_EXPERTS, 1), f32)],
        compiler_params=_params(("arbitrary",)),
        name="positions",
    )(idx, seg_start)


def _dispatch_kernel(dest_hbm, h_ref, o_hbm, dest_smem, sem_i, sem, *, r):
    step = pl.program_id(0)
    n = dest_smem.shape[0]
    load = pltpu.make_async_copy(dest_hbm.at[pl.ds(pl.multiple_of(step * n, n), n)], dest_smem, sem_i)
    load.start()
    load.wait()

    def issue(tok, _):
        src = pl.multiple_of(tok * r, r)
        for k in range(TOP_K):
            dst = pl.multiple_of(dest_smem[tok * TOP_K + k] * r, r)
            pltpu.make_async_copy(h_ref.at[pl.ds(src, r)], o_hbm.at[pl.ds(dst, r)], sem).start(priority=k % 2)
        return 0

    lax.fori_loop(0, n // TOP_K, issue, 0)
    everything = o_hbm.at[pl.ds(0, n * r)]
    pltpu.make_async_copy(everything, everything, sem).wait()


def _dispatch(dest_flat, h2p, n_buf, r, tm):
    t = h2p.shape[0] // r
    return pl.pallas_call(
        functools.partial(_dispatch_kernel, r=r),
        grid=(t // tm,),
        in_specs=[pl.BlockSpec(memory_space=pl.ANY),
                  pl.BlockSpec((tm * r, LANES), lambda i: (i, 0))],
        out_specs=pl.BlockSpec(memory_space=pl.ANY),
        out_shape=jax.ShapeDtypeStruct((n_buf * r, LANES), u32),
        scratch_shapes=[pltpu.SMEM((tm * TOP_K,), i32), pltpu.SemaphoreType.DMA, pltpu.SemaphoreType.DMA],
        compiler_params=_params(("arbitrary",)),
        name="dispatch",
    )(dest_flat, h2p)


def _experts_kernel(be_ref, bv_ref, nu_ref, x_ref, wg_ref, wu_ref, wd_ref, o_ref, wg_s, wu_s, wd_s):
    j = pl.program_id(0)
    prev = be_ref[jnp.maximum(j - 1, 0)]
    used = j < nu_ref[0]
    d = wg_s.shape[0]

    @pl.when(used & ((j == 0) | (be_ref[j] != prev)))
    def _():
        wg_s[...] = wg_ref[0].astype(bf16)
        wu_s[...] = wu_ref[0].astype(bf16)
        wd_s[...] = wd_ref[0].astype(bf16)

    @pl.when(used)
    def _():
        live = lax.broadcasted_iota(i32, (ROW_BLOCK, 1), 0) < bv_ref[j]
        x = jnp.where(live, _load_packed_rows(x_ref, ROW_BLOCK, d), 0.0).astype(bf16)
        hg = jnp.dot(x, wg_s[...], preferred_element_type=f32)
        hu = jnp.dot(x, wu_s[...], preferred_element_type=f32)
        hid = (_silu(hg) * hu).astype(bf16)
        _store_packed_rows(o_ref, jnp.dot(hid, wd_s[...], preferred_element_type=f32))


def _experts(blk_e, blk_valid, n_used, xs, w_gate, w_up, w_down):
    d, f = w_gate.shape[1], w_gate.shape[2]
    r = d // 2 // LANES
    n_blk = xs.shape[0] // r // ROW_BLOCK
    rows = lambda j, be, bv, nu: (jnp.minimum(j, nu[0] - 1), 0)
    wsel = lambda j, be, bv, nu: (be[j], 0, 0)
    return pl.pallas_call(
        _experts_kernel,
        grid_spec=pltpu.PrefetchScalarGridSpec(
            num_scalar_prefetch=3,
            grid=(n_blk,),
            in_specs=[pl.BlockSpec((ROW_BLOCK * r, LANES), rows),
                      pl.BlockSpec((1, d, f), wsel),
                      pl.BlockSpec((1, d, f), wsel),
                      pl.BlockSpec((1, f, d), wsel)],
            out_specs=pl.BlockSpec((ROW_BLOCK * r, LANES), rows),
            scratch_shapes=[pltpu.VMEM((d, f), bf16), pltpu.VMEM((d, f), bf16), pltpu.VMEM((f, d), bf16)]),
        out_shape=jax.ShapeDtypeStruct(xs.shape, u32),
        compiler_params=_params(("arbitrary",)),
        name="experts",
    )(blk_e, blk_valid, n_used, xs, w_gate, w_up, w_down)


def _combine_kernel(dest_hbm, ys_hbm, wt_ref, h_ref, x1_ref, mod_ref, wsg_ref, wsu_ref, wsd_ref,
                    lg_ref, lb_ref, o_ref, dest_a, dest_b, rows_a, rows_b, sem_d, sem_g):
    step = pl.program_id(0)
    n_steps = pl.num_programs(0)
    tm, d = x1_ref.shape
    r = d // 2 // LANES
    n = tm * TOP_K
    tables, rows = (dest_a, dest_b), (rows_a, rows_b)

    def table_copy(tile, slot):
        return pltpu.make_async_copy(dest_hbm.at[pl.ds(pl.multiple_of(tile * n, n), n)],
                                     tables[slot], sem_d.at[slot])

    def start_gathers(slot):
        def issue(tok, _):
            dst = pl.multiple_of(tok * r, r)
            for k in range(TOP_K):
                src = pl.multiple_of(tables[slot][tok * TOP_K + k] * r, r)
                pltpu.make_async_copy(ys_hbm.at[pl.ds(src, r)], rows[slot].at[k, pl.ds(dst, r)],
                                      sem_g.at[slot]).start(priority=k % 2)
            return 0

        lax.fori_loop(0, tm, issue, 0)

    @pl.when(step == 0)
    def _():
        table_copy(0, 0).start()
        table_copy(0, 0).wait()
        start_gathers(0)

        @pl.when(n_steps > 1)
        def _():
            table_copy(1, 1).start()

    def tile_body(slot):
        @pl.when(step + 1 < n_steps)
        def _():
            table_copy(step + 1, 1 - slot).wait()
            start_gathers(1 - slot)

            @pl.when(step + 2 < n_steps)
            def _():
                table_copy(step + 2, slot).start()

        h = _load_packed_rows(h_ref, tm, d).astype(bf16)
        hg = jnp.dot(h, wsg_ref[...], preferred_element_type=f32)
        hu = jnp.dot(h, wsu_ref[...], preferred_element_type=f32)
        y = jnp.dot((_silu(hg) * hu).astype(bf16), wsd_ref[...], preferred_element_type=f32)

        pltpu.make_async_copy(rows[slot], rows[slot], sem_g.at[slot]).wait()
        wt = wt_ref[...]
        lo = [jnp.zeros((tm, LANES), f32) for _ in range(r)]
        hi = [jnp.zeros((tm, LANES), f32) for _ in range(r)]
        for k in range(TOP_K):
            wk = wt[:, k:k + 1]
            for c in range(r):
                pl_, ph_ = _unpack_words(rows[slot][k, pl.ds(c, tm, stride=r), :])
                lo[c] = lo[c] + wk * pl_
                hi[c] = hi[c] + wk * ph_
        y = y + jnp.concatenate(lo + hi, axis=1)
        m = mod_ref[0]
        o_ref[...] = _layer_norm(DEEPNORM_ALPHA * x1_ref[...] + m[5:6] * y, lg_ref[...], lb_ref[...])

    for slot in range(2):
        pl.when(step % 2 == slot)(functools.partial(tile_body, slot))


def _combine(dest_flat, ys, wts, h2p, x1, mod, w_s_gate, w_s_up, w_s_down, ln_g, ln_b, s, tm):
    t, d = x1.shape
    per = s // tm
    r = d // 2 // LANES
    row = lambda a: a.reshape(1, -1)
    tile = lambda w: pl.BlockSpec((tm, w), lambda i: (i, 0))
    full = lambda a: pl.BlockSpec(a.shape, lambda i: (0,) * a.ndim)
    tail = (w_s_gate.astype(bf16), w_s_up.astype(bf16), w_s_down.astype(bf16), row(ln_g), row(ln_b))
    return pl.pallas_call(
        _combine_kernel,
        grid=(t // tm,),
        in_specs=[pl.BlockSpec(memory_space=pl.ANY), pl.BlockSpec(memory_space=pl.ANY),
                  tile(LANES), pl.BlockSpec((tm * r, LANES), lambda i: (i, 0)), tile(d),
                  pl.BlockSpec((1, 6, d), lambda i: (i // per, 0, 0))] + [full(a) for a in tail],
        out_specs=tile(d),
        out_shape=jax.ShapeDtypeStruct((t, d), f32),
        scratch_shapes=[pltpu.SMEM((tm * TOP_K,), i32), pltpu.SMEM((tm * TOP_K,), i32),
                        pltpu.VMEM((TOP_K, tm * r, LANES), u32), pltpu.VMEM((TOP_K, tm * r, LANES), u32),
                        pltpu.SemaphoreType.DMA((2,)), pltpu.SemaphoreType.DMA((2,))],
        compiler_params=_params(("arbitrary",)),
        name="combine",
    )(dest_flat, ys, wts, h2p, x1, mod, *tail)


def _layer(x, mod, w_in, pe, w_cmp1, w_cmp2, w_dw, b_dw, conv_ln_g, conv_ln_b, w_out, ln1_g, ln1_b,
           w_router, router_bias, w_e_gate, w_e_up, w_e_down, w_s_gate, w_s_up, w_s_down, ln2_g, ln2_b):
    b, s, d = x.shape
    t = b * s
    tm = min(512, s)
    qt, vt, gt, kvc, kn, conv_in = _in_proj(x, mod, w_in, tm)
    cmp_n, cmp_t = _compress(kvc, pe, w_cmp1, w_cmp2)
    gates = gt[:, :3 * N_HEADS].reshape(b, N_KV, 3 * HPG, s)
    gates = jnp.pad(gates, ((0, 0), (0, 0), (0, GATE_GROUP_ROWS - 3 * HPG), (0, 0)))
    attn = _attention(qt, gates, cmp_n, cmp_t, kn, vt)
    conv = _conv(conv_in, w_dw, b_dw, conv_ln_g, conv_ln_b, tm)

    tr = min(256, s)
    x1, h2p, idx, wts, counts = _mix_route(attn.reshape(t, -1), conv.reshape(t, -1), x.reshape(t, d), mod,
                                           w_out, ln1_g, ln1_b, w_router, router_bias, s, tr)
    counts = counts[:, 0].astype(i32)
    padded = (counts + ROW_BLOCK - 1) // ROW_BLOCK * ROW_BLOCK
    seg_end = jnp.cumsum(padded)
    seg_start = seg_end - padded
    n_blk = -(-(t * TOP_K + N_EXPERTS * (ROW_BLOCK - 1)) // ROW_BLOCK)
    blk_row0 = jnp.arange(n_blk, dtype=i32) * ROW_BLOCK
    owns = (blk_row0[:, None] >= seg_start[None, :]) & (blk_row0[:, None] < seg_end[None, :])
    blk_e = jnp.where(blk_row0 < seg_end[-1], jnp.argmax(owns, axis=1), N_EXPERTS - 1).astype(i32)
    live_end = jnp.sum(jnp.where(owns, (seg_start + counts)[None, :], 0), axis=1)
    blk_valid = jnp.clip(live_end - blk_row0, 0, ROW_BLOCK).astype(i32)
    n_used = (seg_end[-1:] // ROW_BLOCK).astype(i32)

    dest = _positions(idx, seg_start.astype(f32).reshape(N_EXPERTS, 1), tr)
    dest_flat = dest.T.reshape(-1)
    wts = jnp.zeros((t, LANES), f32).at[:, :TOP_K].set(wts.T)
    xs = _dispatch(dest_flat, h2p, n_blk * ROW_BLOCK, d // 2 // LANES, min(1024, s))
    ys = _experts(blk_e, blk_valid, n_used, xs, w_e_gate, w_e_up, w_e_down)
    out = _combine(dest_flat, ys, wts, h2p, x1, mod, w_s_gate, w_s_up, w_s_down, ln2_g, ln2_b, s, min(256, s))
    return out.reshape(b, s, d)


def kernel(x, c, w_ada, b_ada, w_in, pe_k, pe_v, w_cmp_k1, w_cmp_k2, w_cmp_v1, w_cmp_v2, w_dw, b_dw,
           conv_ln_g, conv_ln_b, w_out, ln1_g, ln1_b, w_router, router_bias, w_e_gate, w_e_up, w_e_down,
           w_s_gate, w_s_up, w_s_down, ln2_g, ln2_b):
    assert w_ada.shape[0] == DEPTH
    layer = lambda a: a.reshape(a.shape[1:])
    mod = _ada(c, layer(w_ada), layer(b_ada))
    return _layer(x, mod, layer(w_in), jnp.concatenate([pe_k, pe_v]),
                  jnp.concatenate([w_cmp_k1, w_cmp_v1]), jnp.concatenate([w_cmp_k2, w_cmp_v2]),
                  *[layer(a) for a in (w_dw, b_dw, conv_ln_g, conv_ln_b, w_out, ln1_g, ln1_b, w_router,
                                       router_bias, w_e_gate, w_e_up, w_e_down, w_s_gate, w_s_up, w_s_down,
                                       ln2_g, ln2_b)])
```

```python
import functools

import jax
import jax.numpy as jnp
from jax import lax
from jax.experimental import pallas as pl
from jax.experimental.pallas import tpu as pltpu

N_HEADS = 8
N_KV = 2
HPG = N_HEADS // N_KV
HEAD_DIM = 64
D_ATTN = N_HEADS * HEAD_DIM
D_KV = N_KV * HEAD_DIM
CONV_WIDTH = 31
CMP_LEN = 32
CMP_STRIDE = 16
CMP_HID = 256
SEL_LEN = 64
SEL_TOPN = 16
WINDOW = 512
Q_BLOCK = 256
N_EXPERTS = 256
TOP_K = 8
N_GROUPS = 8
TOPK_GROUPS = 4
ROUTED_SCALE = 2.5
LN_EPS = 1e-5
DEPTH = 1
DEEPNORM_ALPHA = (2 * DEPTH) ** 0.25

LANES = 128
SUBLANES = 8
ROW_BLOCK = 512
NEG = -1e30
HIGHEST = lax.Precision.HIGHEST
VMEM_LIMIT = 48 * 1024 * 1024

f32 = jnp.float32
bf16 = jnp.bfloat16
i32 = jnp.int32
u32 = jnp.uint32


def _params(sem, vmem=VMEM_LIMIT):
    return pltpu.CompilerParams(dimension_semantics=sem, vmem_limit_bytes=vmem)


def _sigmoid(v):
    return 1.0 / (1.0 + jnp.exp(-v))


def _silu(v):
    return v * _sigmoid(v)


def _layer_norm(v, g, b):
    mu = jnp.mean(v, axis=-1, keepdims=True)
    var = jnp.mean(jnp.square(v - mu), axis=-1, keepdims=True)
    return (v - mu) * lax.rsqrt(var + LN_EPS) * g + b


def _dot_nt(a, b):
    return lax.dot_general(a, b, (((1,), (1,)), ((), ())), preferred_element_type=f32)


def _store_packed_rows(ref, v, first=0):
    n, d = v.shape
    half = d // 2
    bits = lax.bitcast_convert_type(v.astype(bf16).astype(f32), u32)
    words = bits[:, half:] | (bits[:, :half] >> 16)
    r = half // LANES
    for c in range(r):
        ref[pl.ds(first * r + c, n, stride=r), :] = words[:, c * LANES:(c + 1) * LANES]


def _unpack_words(w):
    return (lax.bitcast_convert_type(w << 16, f32),
            lax.bitcast_convert_type(w & jnp.uint32(0xFFFF0000), f32))


def _load_packed_rows(ref, n, d):
    r = d // 2 // LANES
    parts = [_unpack_words(ref[pl.ds(c, n, stride=r), :]) for c in range(r)]
    return jnp.concatenate([p[0] for p in parts] + [p[1] for p in parts], axis=1)


def _ada_kernel(c_ref, w_ref, b_ref, o_ref):
    c = c_ref[...]
    o_ref[...] = jnp.dot(_silu(c), w_ref[...], precision=HIGHEST,
                         preferred_element_type=f32) + b_ref[...]


def _ada(c, w_ada, b_ada):
    b, d = c.shape
    n = w_ada.shape[1]
    rows = 8
    c_pad = jnp.zeros((rows, d), f32).at[:b].set(c)
    tn = 1024
    out = pl.pallas_call(
        _ada_kernel,
        grid=(n // tn,),
        in_specs=[pl.BlockSpec((rows, d), lambda j: (0, 0)),
                  pl.BlockSpec((d, tn), lambda j: (0, j)),
                  pl.BlockSpec((1, tn), lambda j: (0, j))],
        out_specs=pl.BlockSpec((rows, tn), lambda j: (0, j)),
        out_shape=jax.ShapeDtypeStruct((rows, n), f32),
        compiler_params=_params(("arbitrary",)),
        name="ada",
    )(c_pad, w_ada, b_ada.reshape(1, n))
    return out[:b].reshape(b, 6, d)


GATE_ROWS = 32
KEY_TILE = 512
TILE_BLOCKS = KEY_TILE // SEL_LEN
PATCH_BLOCKS = Q_BLOCK // SEL_LEN
PAIR_UNROLL = 2
CMP_PER_SEL = SEL_LEN // CMP_STRIDE
CMP_BACK = CMP_LEN // CMP_STRIDE - 1
assert SEL_LEN % CMP_STRIDE == 0 and CMP_LEN % CMP_STRIDE == 0 and CMP_BACK < CMP_PER_SEL
V_ROWS = HEAD_DIM + 16
Q_SCALE = HEAD_DIM ** -0.5 * 1.4426950408889634


def _in_proj_kernel(x_ref, mod_ref, wt_ref, wc_ref, wk_ref, wv_ref,
                    qt_ref, vt_ref, gt_ref, kvc_ref, kn_ref, cv_ref):
    m = mod_ref[0]
    h = (x_ref[0] * (1.0 + m[1:2]) + m[0:1]).astype(bf16)
    res_t = _dot_nt(wt_ref[...], h)
    qt_ref[0] = (res_t[0:D_ATTN] * Q_SCALE).astype(bf16)
    ones = jnp.ones((V_ROWS - HEAD_DIM, res_t.shape[1]), bf16)
    for j in range(2):
        for g in range(N_KV):
            off = D_ATTN + (j * N_KV + g) * HEAD_DIM
            vt_ref[0, j, g, 0:HEAD_DIM, :] = res_t[off:off + HEAD_DIM].astype(bf16)
            vt_ref[0, j, g, HEAD_DIM:V_ROWS, :] = ones
    gt_ref[0] = res_t[D_ATTN + 2 * D_KV:]
    kvc = jnp.dot(h, wc_ref[...], preferred_element_type=f32)
    kvc_ref[0, 0] = kvc[:, :D_KV]
    kvc_ref[0, 1] = kvc[:, D_KV:]
    kn = jnp.dot(h, wk_ref[...], preferred_element_type=f32).astype(bf16)
    tm = kn.shape[0]
    pos = pl.program_id(1) * tm + lax.broadcasted_iota(i32, (tm, LANES - HEAD_DIM), 0)
    lane = lax.broadcasted_iota(i32, (tm, LANES - HEAD_DIM), 1)
    onehot = jnp.where(lane == ((pos >> 6) & (TILE_BLOCKS - 1)), 1.0, 0.0).astype(bf16)
    for j in range(2):
        for g in range(N_KV):
            off = (j * N_KV + g) * HEAD_DIM
            kn_ref[0, j, g] = jnp.concatenate([kn[:, off:off + HEAD_DIM], onehot], axis=1)
    cv_ref[0] = jnp.dot(h, wv_ref[...], preferred_element_type=f32)


def _in_proj(x, mod, w_in, tm):
    b, s, d = x.shape
    o = 0
    wq = w_in[:, o:o + D_ATTN]; o += D_ATTN
    wkc = w_in[:, o:o + 2 * D_KV]; o += 2 * D_KV
    wk_s = w_in[:, o:o + D_KV]; o += D_KV
    wv_s = w_in[:, o:o + D_KV]; o += D_KV
    wk_w = w_in[:, o:o + D_KV]; o += D_KV
    wv_w = w_in[:, o:o + D_KV]; o += D_KV
    wg = w_in[:, o:o + 3 * N_HEADS]; o += 3 * N_HEADS
    wcv = w_in[:, o:]
    d_conv2 = wcv.shape[1]
    wg = jnp.zeros((d, GATE_ROWS), f32).at[:, :3 * N_HEADS].set(wg)
    wt = jnp.concatenate([wq, wv_s, wv_w, wg], axis=1).T
    ws = [w.astype(bf16) for w in (wt, wkc, jnp.concatenate([wk_s, wk_w], axis=1), wcv)]
    full = lambda a: pl.BlockSpec(a.shape, lambda bi, i: (0, 0))
    return pl.pallas_call(
        _in_proj_kernel,
        grid=(b, s // tm),
        in_specs=[pl.BlockSpec((1, tm, d), lambda bi, i: (bi, i, 0)),
                  pl.BlockSpec((1, 6, d), lambda bi, i: (bi, 0, 0))] + [full(w) for w in ws],
        out_specs=[pl.BlockSpec((1, D_ATTN, tm), lambda bi, i: (bi, 0, i)),
                   pl.BlockSpec((1, 2, N_KV, V_ROWS, tm), lambda bi, i: (bi, 0, 0, 0, i)),
                   pl.BlockSpec((1, GATE_ROWS, tm), lambda bi, i: (bi, 0, i)),
                   pl.BlockSpec((1, 2, tm, D_KV), lambda bi, i: (bi, 0, i, 0)),
                   pl.BlockSpec((1, 2, N_KV, tm, LANES), lambda bi, i: (bi, 0, 0, i, 0)),
                   pl.BlockSpec((1, tm, d_conv2), lambda bi, i: (bi, i, 0))],
        out_shape=[jax.ShapeDtypeStruct((b, D_ATTN, s), bf16),
                   jax.ShapeDtypeStruct((b, 2, N_KV, V_ROWS, s), bf16),
                   jax.ShapeDtypeStruct((b, GATE_ROWS, s), f32),
                   jax.ShapeDtypeStruct((b, 2, s, D_KV), f32),
                   jax.ShapeDtypeStruct((b, 2, N_KV, s, LANES), bf16),
                   jax.ShapeDtypeStruct((b, s, d_conv2), f32)],
        compiler_params=_params(("parallel", "parallel")),
        name="in_proj",
    )(x, mod, *ws)


def _compress_kernel(x_ref, pe_ref, w1_ref, w2_ref, w2t_ref, o_ref, ot_ref):
    n_chunk = o_ref.shape[3]
    for j in range(2):
        for g in range(N_KV):
            cols = slice(g * HEAD_DIM, (g + 1) * HEAD_DIM)
            a = jnp.zeros((n_chunk, CMP_HID), f32)
            bm = jnp.zeros((n_chunk, CMP_HID), f32)
            for l in range(CMP_STRIDE):
                xl = x_ref[0, j, pl.ds(l, n_chunk, stride=CMP_STRIDE), :][:, cols]
                a = a + jnp.dot((xl + pe_ref[j, l:l + 1, :]).astype(bf16), w1_ref[j, l],
                                preferred_element_type=f32)
                bm = bm + jnp.dot((xl + pe_ref[j, CMP_STRIDE + l:CMP_STRIDE + l + 1, :]).astype(bf16),
                                  w1_ref[j, CMP_STRIDE + l], preferred_element_type=f32)
            hid = a + pltpu.roll(bm, n_chunk - 1, 0)
            act = 0.5 * hid * (1.0 + jnp.tanh(0.7978845608028654 * (hid + 0.044715 * (hid * hid * hid))))
            act = act.astype(bf16)
            o_ref[0, j, g] = jnp.dot(act, w2_ref[j], preferred_element_type=f32).astype(bf16)
            ot_ref[0, j, g] = _dot_nt(w2t_ref[j], act).astype(bf16)


def _compress(kvc, pe, w1, w2):
    assert CMP_LEN == 2 * CMP_STRIDE
    b, _, s, width = kvc.shape
    n_chunk = s // CMP_STRIDE
    w1b = w1.reshape(2, CMP_LEN, HEAD_DIM, CMP_HID).astype(bf16)
    w2b = w2.astype(bf16)
    w2t = w2b.transpose(0, 2, 1)
    full = lambda a: pl.BlockSpec(a.shape, lambda bi: (0,) * a.ndim)
    return pl.pallas_call(
        _compress_kernel,
        grid=(b,),
        in_specs=[pl.BlockSpec((1, 2, s, width), lambda bi: (bi, 0, 0, 0)),
                  full(pe), full(w1b), full(w2b), full(w2t)],
        out_specs=[pl.BlockSpec((1, 2, N_KV, n_chunk, HEAD_DIM), lambda bi: (bi, 0, 0, 0, 0)),
                   pl.BlockSpec((1, 2, N_KV, HEAD_DIM, n_chunk), lambda bi: (bi, 0, 0, 0, 0))],
        out_shape=[jax.ShapeDtypeStruct((b, 2, N_KV, n_chunk, HEAD_DIM), bf16),
                   jax.ShapeDtypeStruct((b, 2, N_KV, HEAD_DIM, n_chunk), bf16)],
        compiler_params=_params(("parallel",)),
        name="compress",
    )(kvc, pe, w1b, w2b, w2t)


def _attn_kernel(q_ref, g_ref, kc_ref, vc_ref, kn_ref, vt_ref, o_ref, psum_ref, selbias_ref, sa_ref, sb_ref,
                 sd_ref, *, seq, tk, top_n):
    i = pl.program_id(2)
    s0 = i * Q_BLOCK
    n_cmp_rows = kc_ref.shape[3]
    n_sel = seq // SEL_LEN

    q4 = q_ref[0]
    qt = jnp.concatenate([q4[p * HEAD_DIM:(p + 1) * HEAD_DIM, :] for p in range(HPG)], axis=1)
    t_row = s0 + lax.broadcasted_iota(i32, (1, Q_BLOCK), 1)

    s_c = jnp.dot(kc_ref[0, 0, 0], qt, preferred_element_type=f32)
    cmp_end = lax.broadcasted_iota(i32, (n_cmp_rows, 1), 0) * CMP_STRIDE + (CMP_LEN - 1)
    bias_c = jnp.where(cmp_end <= t_row, 0.0, NEG)
    any_c = t_row >= CMP_LEN - 1
    p_sum = jnp.zeros((n_cmp_rows, Q_BLOCK), f32)
    pcs = []
    for p in range(HPG):
        sp = s_c[:, p * Q_BLOCK:(p + 1) * Q_BLOCK] + bias_c
        e = jnp.exp2(sp - jnp.max(sp, axis=0, keepdims=True))
        pn = e * jnp.where(any_c, 1.0 / jnp.sum(e, axis=0, keepdims=True), 0.0)
        p_sum = p_sum + pn
        pcs.append(pn.astype(bf16))
    o_c = jnp.dot(vc_ref[0, 0, 0], jnp.concatenate(pcs, axis=1), preferred_element_type=f32)

    cols = HPG * Q_BLOCK
    pad_rows = LANES - HEAD_DIM
    q_pad = jnp.concatenate([qt, jnp.zeros((pad_rows, cols), bf16)], axis=0)
    span = WINDOW + Q_BLOCK
    w0 = pl.multiple_of(jnp.maximum(s0 - WINDOW, 0), Q_BLOCK)
    s_w = jnp.dot(kn_ref[0, 1, 0, pl.ds(w0, span), :], q_pad, preferred_element_type=f32)
    wpos = w0 + lax.broadcasted_iota(i32, (span, 1), 0)
    bias_w = jnp.where((wpos <= t_row) & (wpos > t_row - WINDOW), 0.0, NEG)
    pws = []
    for p in range(HPG):
        sp = s_w[:, p * Q_BLOCK:(p + 1) * Q_BLOCK] + bias_w
        pws.append(jnp.exp2(sp - jnp.max(sp, axis=0, keepdims=True)).astype(bf16))
    acc_w = jnp.dot(vt_ref[0, 1, 0, :, pl.ds(w0, span)], jnp.concatenate(pws, axis=1),
                    preferred_element_type=f32)
    o_w = acc_w[0:HEAD_DIM] * (1.0 / acc_w[HEAD_DIM:HEAD_DIM + 1])
    d0 = pl.multiple_of(s0, Q_BLOCK)
    s_d = jnp.dot(kn_ref[0, 0, 0, pl.ds(d0, Q_BLOCK), :], q_pad, preferred_element_type=f32)

    blk = lax.broadcasted_iota(i32, (n_sel, Q_BLOCK), 0)
    for lt in range(Q_BLOCK // LANES):
        psum_ref[lt] = p_sum[:, lt * LANES:(lt + 1) * LANES]

    def every(first):
        return jnp.concatenate([psum_ref[lt, pl.ds(first, n_sel, stride=CMP_PER_SEL), :]
                                for lt in range(Q_BLOCK // LANES)], axis=1)

    imp = every(0)
    for r in range(1, CMP_PER_SEL):
        imp = imp + every(r)
    for back in range(1, CMP_BACK + 1):
        imp = imp + jnp.where(blk >= 1, pltpu.roll(every(CMP_PER_SEL - back), 1, 0), 0.0)

    cur = t_row >> 6
    forced = (blk == 0) | (blk == cur) | (blk == cur - 1)
    vals = jnp.where(forced, jnp.inf, jnp.where(blk <= cur, imp, -jnp.inf))
    sel = jnp.zeros((n_sel, Q_BLOCK), f32)
    for _ in range(top_n):
        mx = jnp.max(vals, axis=0, keepdims=True)
        first = jnp.min(jnp.where(vals == mx, blk, n_sel), axis=0, keepdims=True)
        pick = blk == first
        sel = jnp.where(pick & (mx > -jnp.inf), 1.0, sel)
        vals = jnp.where(pick, -jnp.inf, vals)
    first_blk = s0 >> 6
    full_at = n_sel + TILE_BLOCKS
    selbias_ref[0:n_sel] = jnp.where((sel > 0.5) & (blk < first_blk), 0.0, NEG)
    selbias_ref[n_sel:full_at] = jnp.full((TILE_BLOCKS, Q_BLOCK), NEG, f32)
    selbias_ref[full_at:full_at + n_sel] = jnp.where(sel > 0.5, 0.0, NEG)
    last_tile = seq // tk - 1

    def scores(kt, s_ref):
        k0 = pl.multiple_of(jnp.minimum(kt, last_tile) * tk, tk)
        sb = selbias_ref[pl.ds(pl.multiple_of(kt * TILE_BLOCKS, TILE_BLOCKS), TILE_BLOCKS), :]
        rows = jnp.concatenate([jnp.concatenate([sb] * HPG, axis=1),
                                jnp.zeros((pad_rows - TILE_BLOCKS, cols), f32)], axis=0).astype(bf16)
        s_ref[...] = jnp.dot(kn_ref[0, 0, 0, pl.ds(k0, tk), :], jnp.concatenate([qt, rows], axis=0),
                             preferred_element_type=f32)

    def fold(scores_of_head, v_t, carry):
        m_i, acc = carry
        es, ms, alphas = [], [], []
        for p in range(HPG):
            c = slice(p * Q_BLOCK, (p + 1) * Q_BLOCK)
            sp = scores_of_head(c)
            m_new = jnp.maximum(m_i[:, c], jnp.max(sp, axis=0, keepdims=True))
            es.append(jnp.exp2(sp - m_new).astype(bf16))
            alphas.append(jnp.exp2(m_i[:, c] - m_new))
            ms.append(m_new)
        pv = jnp.dot(v_t, jnp.concatenate(es, axis=1), preferred_element_type=f32)
        return jnp.concatenate(ms, axis=1), jnp.concatenate(alphas, axis=1) * acc + pv

    def sel_tile(kt, s_ref, carry):
        k0 = pl.multiple_of(jnp.minimum(kt, last_tile) * tk, tk)
        return fold(lambda c: s_ref[:, c], vt_ref[0, 0, 0, :, pl.ds(k0, tk)], carry)

    def sel_pair(i, carry):
        scores(2 * i + 1, sb_ref)
        carry = sel_tile(2 * i, sa_ref, carry)
        scores(2 * i + 2, sa_ref)
        return sel_tile(2 * i + 1, sb_ref, carry)

    n_sweep = (s0 + tk - 1) // tk
    n_pairs = (n_sweep + 1) // 2
    init = (jnp.full((1, cols), NEG, f32), jnp.zeros((V_ROWS, cols), f32))
    scores(0, sa_ref)
    def sel_pairs(j, carry):
        for u in range(PAIR_UNROLL):
            carry = sel_pair(PAIR_UNROLL * j + u, carry)
        return carry

    group = selbias_ref[pl.ds(full_at + pl.multiple_of((first_blk >> 3) << 3, SUBLANES), SUBLANES), :]
    own = group[0:PATCH_BLOCKS]
    for at in range(PATCH_BLOCKS, SUBLANES, PATCH_BLOCKS):
        own = jnp.where((first_blk & (SUBLANES - 1)) == at, group[at:at + PATCH_BLOCKS], own)
    bias_d = jnp.concatenate([jnp.broadcast_to(own[j:j + 1], (SEL_LEN, Q_BLOCK)) for j in range(PATCH_BLOCKS)],
                             axis=0)
    bias_d = jnp.where(s0 + lax.broadcasted_iota(i32, (Q_BLOCK, 1), 0) <= t_row, bias_d, NEG)
    for p in range(HPG):
        c = slice(p * Q_BLOCK, (p + 1) * Q_BLOCK)
        sd_ref[:, c] = s_d[:, c] + bias_d

    carry = lax.fori_loop(0, n_pairs // PAIR_UNROLL, sel_pairs, init)
    carry = lax.fori_loop(n_pairs // PAIR_UNROLL * PAIR_UNROLL, n_pairs, sel_pair, carry)
    _, acc_s = fold(lambda c: sd_ref[:, c], vt_ref[0, 0, 0, :, pl.ds(d0, Q_BLOCK)], carry)
    o_s = acc_s[0:HEAD_DIM] * (1.0 / acc_s[HEAD_DIM:HEAD_DIM + 1])

    gate = _sigmoid(g_ref[0, 0])
    outs = []
    for p in range(HPG):
        c = slice(p * Q_BLOCK, (p + 1) * Q_BLOCK)
        outs.append(gate[3 * p:3 * p + 1] * o_c[:, c] + gate[3 * p + 1:3 * p + 2] * o_s[:, c]
                    + gate[3 * p + 2:3 * p + 3] * o_w[:, c])
    o_ref[0] = jnp.concatenate(outs, axis=0).T.astype(bf16)


GATE_GROUP_ROWS = 16


def _attention(qt, gates, cmp_n, cmp_t, kn, vt):
    b, _, s = qt.shape
    n_chunk = cmp_n.shape[3]
    n_sel = s // SEL_LEN
    top_n = min(SEL_TOPN, n_sel)
    tk = KEY_TILE
    assert s % tk == 0 and n_chunk == n_sel * CMP_PER_SEL
    gw = HPG * HEAD_DIM
    return pl.pallas_call(
        functools.partial(_attn_kernel, seq=s, tk=tk, top_n=top_n),
        grid=(b, N_KV, s // Q_BLOCK),
        in_specs=[pl.BlockSpec((1, gw, Q_BLOCK), lambda bi, g, i: (bi, g, i)),
                  pl.BlockSpec((1, 1, GATE_GROUP_ROWS, Q_BLOCK), lambda bi, g, i: (bi, g, 0, i)),
                  pl.BlockSpec((1, 1, 1, n_chunk, HEAD_DIM), lambda bi, g, i: (bi, 0, g, 0, 0)),
                  pl.BlockSpec((1, 1, 1, HEAD_DIM, n_chunk), lambda bi, g, i: (bi, 1, g, 0, 0)),
                  pl.BlockSpec((1, 2, 1, s, LANES), lambda bi, g, i: (bi, 0, g, 0, 0)),
                  pl.BlockSpec((1, 2, 1, V_ROWS, s), lambda bi, g, i: (bi, 0, g, 0, 0))],
        out_specs=pl.BlockSpec((1, Q_BLOCK, gw), lambda bi, g, i: (bi, i, g)),
        out_shape=jax.ShapeDtypeStruct((b, s, D_ATTN), bf16),
        scratch_shapes=[pltpu.VMEM((Q_BLOCK // LANES, n_chunk, LANES), f32),
                        pltpu.VMEM((2 * n_sel + TILE_BLOCKS, Q_BLOCK), f32),
                        pltpu.VMEM((tk, HPG * Q_BLOCK), f32), pltpu.VMEM((tk, HPG * Q_BLOCK), f32),
                        pltpu.VMEM((Q_BLOCK, HPG * Q_BLOCK), f32)],
        compiler_params=_params(("parallel", "parallel", "arbitrary")),
        name="attention",
    )(qt, gates, cmp_n, cmp_t, kn, vt)


CONV_HALO = 32
CONV_ROWS = 32


def _conv_kernel(cur_ref, prev_ref, w_ref, b_ref, g_ref, bb_ref, o_ref, glu_ref):
    i = pl.program_id(1)
    ts = cur_ref.shape[1]
    dc = o_ref.shape[2]
    cur = cur_ref[0]
    prev = prev_ref[0]
    glu_prev = prev[:, :dc] * _sigmoid(prev[:, dc:])
    glu_ref[0:CONV_HALO] = jnp.where(i == 0, 0.0, glu_prev)
    glu_ref[CONV_HALO:CONV_HALO + ts] = cur[:, :dc] * _sigmoid(cur[:, dc:])
    lead = CONV_HALO - (CONV_WIDTH - 1)

    def chunk(r, _):
        r0 = pl.multiple_of(r * CONV_ROWS, CONV_ROWS)
        win = glu_ref[pl.ds(r0, CONV_ROWS + CONV_HALO), :]
        span = CONV_ROWS + CONV_HALO
        acc = jnp.zeros((CONV_ROWS, dc), f32)
        for ph in range(SUBLANES):
            turned = win if ph == 0 else pltpu.roll(win, span - ph, 0)
            for j in range(CONV_WIDTH):
                if (lead + j) % SUBLANES == ph:
                    at = lead + j - ph
                    acc = acc + w_ref[j:j + 1, :] * turned[at:at + CONV_ROWS]
        y = _layer_norm(acc + b_ref[...], g_ref[...], bb_ref[...])
        o_ref[0, pl.ds(r0, CONV_ROWS), :] = _silu(y).astype(bf16)
        return 0

    lax.fori_loop(0, ts // CONV_ROWS, chunk, 0)


def _conv(conv_in, w_dw, b_dw, ln_g, ln_b, ts):
    b, s, dc2 = conv_in.shape
    dc = dc2 // 2
    per = ts // CONV_HALO
    row = lambda a: a.reshape(1, dc)
    return pl.pallas_call(
        _conv_kernel,
        grid=(b, s // ts),
        in_specs=[pl.BlockSpec((1, ts, dc2), lambda bi, i: (bi, i, 0)),
                  pl.BlockSpec((1, CONV_HALO, dc2), lambda bi, i: (bi, jnp.maximum(i * per - 1, 0), 0)),
                  pl.BlockSpec((CONV_WIDTH, dc), lambda bi, i: (0, 0)),
                  pl.BlockSpec((1, dc), lambda bi, i: (0, 0)),
                  pl.BlockSpec((1, dc), lambda bi, i: (0, 0)),
                  pl.BlockSpec((1, dc), lambda bi, i: (0, 0))],
        out_specs=pl.BlockSpec((1, ts, dc), lambda bi, i: (bi, i, 0)),
        out_shape=jax.ShapeDtypeStruct((b, s, dc), bf16),
        scratch_shapes=[pltpu.VMEM((CONV_HALO + ts, dc), f32)],
        compiler_params=_params(("parallel", "parallel")),
        name="conv",
    )(conv_in, conv_in, w_dw.reshape(CONV_WIDTH, dc), row(b_dw), row(ln_g), row(ln_b))


ROUTE_HALF = 128


def _mix_route_kernel(a_ref, cv_ref, x_ref, mod_ref, wo_ref, lg_ref, lb_ref, wrh_ref, wrl_ref, rb_ref,
                      x1_ref, h2p_ref, idx_ref, wt_ref, cnt_ref):
    step = pl.program_id(0)

    @pl.when(step == 0)
    def _():
        cnt_ref[...] = jnp.zeros_like(cnt_ref)

    m = mod_ref[0]
    da = a_ref.shape[1]
    counts = jnp.zeros(cnt_ref.shape, f32)
    for first in range(0, x_ref.shape[0], ROUTE_HALF):
        rows = slice(first, first + ROUTE_HALF)
        mix = (jnp.dot(a_ref[rows, :], wo_ref[0:da, :], preferred_element_type=f32)
               + jnp.dot(cv_ref[rows, :], wo_ref[da:, :], preferred_element_type=f32))
        x1 = _layer_norm(DEEPNORM_ALPHA * x_ref[rows, :] + m[2:3] * mix, lg_ref[...], lb_ref[...])
        x1_ref[rows, :] = x1
        h2 = x1 * (1.0 + m[4:5]) + m[3:4]
        _store_packed_rows(h2p_ref, h2, first)
        h_hi = h2.astype(bf16)
        h_lo = (h2 - h_hi.astype(f32)).astype(bf16)
        logits = _dot_nt(wrh_ref[...], h_hi) + (_dot_nt(wrh_ref[...], h_lo) + _dot_nt(wrl_ref[...], h_hi))
        idx, wt, cnt = _route(_sigmoid(logits), rb_ref[...])
        idx_ref[:, rows] = idx
        wt_ref[:, rows] = wt
        counts = counts + cnt
    cnt_ref[...] += counts


def _route(score, bias):
    tm = score.shape[1]
    sel = score + bias
    eid = lax.broadcasted_iota(i32, (N_EXPERTS, tm), 0)
    per_group = N_EXPERTS // N_GROUPS
    gs = []
    for g in range(N_GROUPS):
        rows = slice(g * per_group, (g + 1) * per_group)
        v = sel[rows]
        e = g * per_group + lax.broadcasted_iota(i32, (per_group, tm), 0)
        m1 = jnp.max(v, axis=0, keepdims=True)
        i1 = jnp.min(jnp.where(v == m1, e, N_EXPERTS), axis=0, keepdims=True)
        m2 = jnp.max(jnp.where(e == i1, -jnp.inf, v), axis=0, keepdims=True)
        gs.append(m1 + m2)
    cands = []
    for g in range(N_GROUPS):
        rank = jnp.zeros((1, tm), i32)
        for o in range(N_GROUPS):
            if o == g:
                continue
            beats = (gs[o] > gs[g]) | (gs[o] == gs[g]) if o < g else gs[o] > gs[g]
            rank = rank + beats.astype(i32)
        drop = jnp.where(rank < TOPK_GROUPS, 0.0, -jnp.inf)
        cands.append(sel[g * per_group:(g + 1) * per_group] + drop)
    cand = jnp.concatenate(cands, axis=0)
    row_o = lax.broadcasted_iota(i32, (TOP_K, tm), 0)
    idx_out = jnp.zeros((TOP_K, tm), i32)
    wt_out = jnp.zeros((TOP_K, tm), f32)
    picked = jnp.zeros((N_EXPERTS, tm), f32)
    w_sum = jnp.zeros((1, tm), f32)
    for k in range(TOP_K):
        mx = jnp.max(cand, axis=0, keepdims=True)
        ik = jnp.min(jnp.where(cand == mx, eid, N_EXPERTS), axis=0, keepdims=True)
        pick = eid == ik
        wk = jnp.sum(jnp.where(pick, score, 0.0), axis=0, keepdims=True)
        cand = jnp.where(pick, -jnp.inf, cand)
        picked = jnp.where(pick, 1.0, picked)
        idx_out = jnp.where(row_o == k, ik, idx_out)
        wt_out = jnp.where(row_o == k, wk, wt_out)
        w_sum = w_sum + wk
    return idx_out, wt_out / w_sum * ROUTED_SCALE, jnp.sum(picked, axis=1, keepdims=True)


def _mix_route(attn, conv, x2, mod, w_out, ln_g, ln_b, w_router, router_bias, s, tm):
    t, d = x2.shape
    per = s // tm
    da = attn.shape[1]
    pack_rows = d // 2 // LANES
    row = lambda a: a.reshape(1, -1)
    tile = lambda w: pl.BlockSpec((tm, w), lambda i: (i, 0))
    full = lambda a: pl.BlockSpec(a.shape, lambda i: (0,) * a.ndim)
    assert tm % ROUTE_HALF == 0
    wr_hi = w_router.T.astype(bf16)
    wr_lo = (w_router.T - wr_hi.astype(f32)).astype(bf16)
    args = (attn, conv, x2, mod, w_out.astype(bf16), row(ln_g), row(ln_b), wr_hi, wr_lo,
            router_bias.reshape(N_EXPERTS, 1))
    per_token = lambda rows: pl.BlockSpec((rows, tm), lambda i: (0, i))
    return pl.pallas_call(
        _mix_route_kernel,
        grid=(t // tm,),
        in_specs=[tile(da), tile(conv.shape[1]), tile(d),
                  pl.BlockSpec((1, 6, d), lambda i: (i // per, 0, 0))] + [full(a) for a in args[4:]],
        out_specs=[tile(d), pl.BlockSpec((tm * pack_rows, LANES), lambda i: (i, 0)),
                   per_token(TOP_K), per_token(TOP_K), pl.BlockSpec((N_EXPERTS, 1), lambda i: (0, 0))],
        out_shape=[jax.ShapeDtypeStruct((t, d), f32),
                   jax.ShapeDtypeStruct((t * pack_rows, LANES), u32),
                   jax.ShapeDtypeStruct((TOP_K, t), i32), jax.ShapeDtypeStruct((TOP_K, t), f32),
                   jax.ShapeDtypeStruct((N_EXPERTS, 1), f32)],
        compiler_params=_params(("arbitrary",)),
        name="mix_route",
    )(*args)


def _positions_kernel(idx_ref, start_ref, o_ref, run_ref):
    step = pl.program_id(0)
    tm = idx_ref.shape[1]

    @pl.when(step == 0)
    def _():
        run_ref[...] = jnp.zeros_like(run_ref)

    idx = idx_ref[...]
    eid = lax.broadcasted_iota(i32, (N_EXPERTS, tm), 0)
    onehot = jnp.zeros((N_EXPERTS, tm), f32)
    for k in range(TOP_K):
        onehot = jnp.where(eid == idx[k:k + 1], 1.0, onehot)
    r = lax.broadcasted_iota(i32, (tm, tm), 0)
    c = lax.broadcasted_iota(i32, (tm, tm), 1)
    earlier = jnp.where(r < c, 1.0, 0.0).astype(bf16)
    prior = jnp.dot(onehot.astype(bf16), earlier, preferred_element_type=f32)
    pos = prior + run_ref[...] + start_ref[...]
    row_o = lax.broadcasted_iota(i32, (TOP_K, tm), 0)
    out = jnp.zeros((TOP_K, tm), i32)
    for k in range(TOP_K):
        dk = jnp.sum(jnp.where(eid == idx[k:k + 1], pos, 0.0), axis=0, keepdims=True)
        out = jnp.where(row_o == k, dk.astype(i32), out)
    o_ref[...] = out
    run_ref[...] += jnp.sum(onehot, axis=1, keepdims=True)


def _positions(idx, seg_start, tm):
    t = idx.shape[1]
    return pl.pallas_call(
        _positions_kernel,
        grid=(t // tm,),
        in_specs=[pl.BlockSpec((TOP_K, tm), lambda i: (0, i)),
                  pl.BlockSpec((N_EXPERTS, 1), lambda i: (0, 0))],
        out_specs=pl.BlockSpec((TOP_K, tm), lambda i: (0, i)),
        out_shape=jax.ShapeDtypeStruct((TOP_K, t), i32),
        scratch_shapes=[pltpu.VMEM((N_EXPERTS, 1), f32)],
        compiler_params=_params(("arbitrary",)),
        name="positions",
    )(idx, seg_start)


def _dispatch_kernel(dest_hbm, h_ref, o_hbm, dest_smem, sem_i, sem, *, r):
    step = pl.program_id(0)
    n = dest_smem.shape[0]
    load = pltpu.make_async_copy(dest_hbm.at[pl.ds(pl.multiple_of(step * n, n), n)], dest_smem, sem_i)
    load.start()
    load.wait()

    def issue(tok, _):
        src = pl.multiple_of(tok * r, r)
        for k in range(TOP_K):
            dst = pl.multiple_of(dest_smem[tok * TOP_K + k] * r, r)
            pltpu.make_async_copy(h_ref.at[pl.ds(src, r)], o_hbm.at[pl.ds(dst, r)], sem).start(priority=k % 2)
        return 0

    lax.fori_loop(0, n // TOP_K, issue, 0)
    everything = o_hbm.at[pl.ds(0, n * r)]
    pltpu.make_async_copy(everything, everything, sem).wait()


def _dispatch(dest_flat, h2p, n_buf, r, tm):
    t = h2p.shape[0] // r
    return pl.pallas_call(
        functools.partial(_dispatch_kernel, r=r),
        grid=(t // tm,),
        in_specs=[pl.BlockSpec(memory_space=pl.ANY),
                  pl.BlockSpec((tm * r, LANES), lambda i: (i, 0))],
        out_specs=pl.BlockSpec(memory_space=pl.ANY),
        out_shape=jax.ShapeDtypeStruct((n_buf * r, LANES), u32),
        scratch_shapes=[pltpu.SMEM((tm * TOP_K,), i32), pltpu.SemaphoreType.DMA, pltpu.SemaphoreType.DMA],
        compiler_params=_params(("arbitrary",)),
        name="dispatch",
    )(dest_flat, h2p)


def _experts_kernel(nb_ref, b0_ref, cnt_ref, xs_hbm, wg_ref, wu_ref, wd_ref, ys_hbm,
                    xbuf, ybuf, wg_s, wu_s, wd_s, slot_ref, sem_x, sem_y):
    e = pl.program_id(0)
    nb, b0, cnt = nb_ref[e], b0_ref[e], cnt_ref[e]
    d = wg_s.shape[0]
    rows = xbuf.shape[1]

    def x_copy(blk, slot):
        return pltpu.make_async_copy(xs_hbm.at[pl.ds(pl.multiple_of(blk * rows, rows), rows)],
                                     xbuf.at[slot], sem_x.at[slot])

    def y_copy(blk, slot):
        return pltpu.make_async_copy(ybuf.at[slot], ys_hbm.at[pl.ds(pl.multiple_of(blk * rows, rows), rows)],
                                     sem_y.at[slot])

    def request_next(slot):
        nxt = jnp.minimum(e + 1, pl.num_programs(0) - 1)

        @pl.when((e + 1 < pl.num_programs(0)) & (nb_ref[nxt] > 0))
        def _():
            x_copy(b0_ref[nxt], slot).start()
            slot_ref[0] = slot

    @pl.when(e == 0)
    def _():
        slot_ref[0] = 0

        @pl.when(nb > 0)
        def _():
            x_copy(b0, 0).start()

    @pl.when(nb == 0)
    def _():
        request_next(slot_ref[0])

    @pl.when(nb > 0)
    def _():
        wg_s[...] = wg_ref[0].astype(bf16)
        wu_s[...] = wu_ref[0].astype(bf16)
        wd_s[...] = wd_ref[0].astype(bf16)
        first = slot_ref[0]

        def block(i, _):
            slot = (first + i) % 2
            x_copy(b0 + i, slot).wait()

            @pl.when(i + 1 < nb)
            def _():
                x_copy(b0 + i + 1, 1 - slot).start()

            @pl.when(i + 1 == nb)
            def _():
                request_next(1 - slot)

            live = lax.broadcasted_iota(i32, (ROW_BLOCK, 1), 0) < cnt - i * ROW_BLOCK
            x = jnp.where(live, _load_packed_rows(xbuf.at[slot], ROW_BLOCK, d), 0.0).astype(bf16)
            hg = jnp.dot(x, wg_s[...], preferred_element_type=f32)
            hu = jnp.dot(x, wu_s[...], preferred_element_type=f32)
            hid = (_silu(hg) * hu).astype(bf16)
            y = jnp.dot(hid, wd_s[...], preferred_element_type=f32)
            out = i % 2

            @pl.when(i >= 2)
            def _():
                y_copy(b0 + i - 2, out).wait()

            _store_packed_rows(ybuf.at[out], y)
            y_copy(b0 + i, out).start()
            return 0

        lax.fori_loop(0, nb, block, 0)

        @pl.when(nb >= 2)
        def _():
            y_copy(b0 + nb - 2, nb % 2).wait()

        y_copy(b0 + nb - 1, (nb - 1) % 2).wait()


def _experts(n_blocks, first_block, counts, xs, w_gate, w_up, w_down):
    d, f = w_gate.shape[1], w_gate.shape[2]
    r = d // 2 // LANES
    wsel = lambda e, nb, b0, cnt: (e, 0, 0)
    return pl.pallas_call(
        _experts_kernel,
        grid_spec=pltpu.PrefetchScalarGridSpec(
            num_scalar_prefetch=3,
            grid=(w_gate.shape[0],),
            in_specs=[pl.BlockSpec(memory_space=pl.ANY),
                      pl.BlockSpec((1, d, f), wsel),
                      pl.BlockSpec((1, d, f), wsel),
                      pl.BlockSpec((1, f, d), wsel)],
            out_specs=pl.BlockSpec(memory_space=pl.ANY),
            scratch_shapes=[pltpu.VMEM((2, ROW_BLOCK * r, LANES), u32), pltpu.VMEM((2, ROW_BLOCK * r, LANES), u32),
                            pltpu.VMEM((d, f), bf16), pltpu.VMEM((d, f), bf16), pltpu.VMEM((f, d), bf16),
                            pltpu.SMEM((1,), i32), pltpu.SemaphoreType.DMA((2,)), pltpu.SemaphoreType.DMA((2,))]),
        out_shape=jax.ShapeDtypeStruct(xs.shape, u32),
        compiler_params=_params(("arbitrary",)),
        name="experts",
    )(n_blocks, first_block, counts, xs, w_gate, w_up, w_down)


def _combine_kernel(dest_hbm, ys_hbm, wt_ref, h_ref, x1_ref, mod_ref, wsg_ref, wsu_ref, wsd_ref,
                    lg_ref, lb_ref, o_ref, dest_a, dest_b, rows_a, rows_b, sem_d, sem_g):
    step = pl.program_id(0)
    n_steps = pl.num_programs(0)
    tm, d = x1_ref.shape
    r = d // 2 // LANES
    n = tm * TOP_K
    tables, rows = (dest_a, dest_b), (rows_a, rows_b)

    def table_copy(tile, slot):
        return pltpu.make_async_copy(dest_hbm.at[pl.ds(pl.multiple_of(tile * n, n), n)],
                                     tables[slot], sem_d.at[slot])

    def start_gathers(slot):
        def issue(tok, _):
            dst = pl.multiple_of(tok * r, r)
            for k in range(TOP_K):
                src = pl.multiple_of(tables[slot][tok * TOP_K + k] * r, r)
                pltpu.make_async_copy(ys_hbm.at[pl.ds(src, r)], rows[slot].at[k, pl.ds(dst, r)],
                                      sem_g.at[slot]).start(priority=k % 2)
            return 0

        lax.fori_loop(0, tm, issue, 0)

    @pl.when(step == 0)
    def _():
        table_copy(0, 0).start()
        table_copy(0, 0).wait()
        start_gathers(0)

        @pl.when(n_steps > 1)
        def _():
            table_copy(1, 1).start()

    def tile_body(slot):
        @pl.when(step + 1 < n_steps)
        def _():
            table_copy(step + 1, 1 - slot).wait()
            start_gathers(1 - slot)

            @pl.when(step + 2 < n_steps)
            def _():
                table_copy(step + 2, slot).start()

        h = _load_packed_rows(h_ref, tm, d).astype(bf16)
        hg = jnp.dot(h, wsg_ref[...], preferred_element_type=f32)
        hu = jnp.dot(h, wsu_ref[...], preferred_element_type=f32)
        y = jnp.dot((_silu(hg) * hu).astype(bf16), wsd_ref[...], preferred_element_type=f32)

        pltpu.make_async_copy(rows[slot], rows[slot], sem_g.at[slot]).wait()
        wt = wt_ref[...]
        lo = [jnp.zeros((tm, LANES), f32) for _ in range(r)]
        hi = [jnp.zeros((tm, LANES), f32) for _ in range(r)]
        for k in range(TOP_K):
            wk = wt[:, k:k + 1]
            for c in range(r):
                pl_, ph_ = _unpack_words(rows[slot][k, pl.ds(c, tm, stride=r), :])
                lo[c] = lo[c] + wk * pl_
                hi[c] = hi[c] + wk * ph_
        y = y + jnp.concatenate(lo + hi, axis=1)
        m = mod_ref[0]
        o_ref[...] = _layer_norm(DEEPNORM_ALPHA * x1_ref[...] + m[5:6] * y, lg_ref[...], lb_ref[...])

    for slot in range(2):
        pl.when(step % 2 == slot)(functools.partial(tile_body, slot))


def _combine(dest_flat, ys, wts, h2p, x1, mod, w_s_gate, w_s_up, w_s_down, ln_g, ln_b, s, tm):
    t, d = x1.shape
    per = s // tm
    r = d // 2 // LANES
    row = lambda a: a.reshape(1, -1)
    tile = lambda w: pl.BlockSpec((tm, w), lambda i: (i, 0))
    full = lambda a: pl.BlockSpec(a.shape, lambda i: (0,) * a.ndim)
    tail = (w_s_gate.astype(bf16), w_s_up.astype(bf16), w_s_down.astype(bf16), row(ln_g), row(ln_b))
    return pl.pallas_call(
        _combine_kernel,
        grid=(t // tm,),
        in_specs=[pl.BlockSpec(memory_space=pl.ANY), pl.BlockSpec(memory_space=pl.ANY),
                  tile(LANES), pl.BlockSpec((tm * r, LANES), lambda i: (i, 0)), tile(d),
                  pl.BlockSpec((1, 6, d), lambda i: (i // per, 0, 0))] + [full(a) for a in tail],
        out_specs=tile(d),
        out_shape=jax.ShapeDtypeStruct((t, d), f32),
        scratch_shapes=[pltpu.SMEM((tm * TOP_K,), i32), pltpu.SMEM((tm * TOP_K,), i32),
                        pltpu.VMEM((TOP_K, tm * r, LANES), u32), pltpu.VMEM((TOP_K, tm * r, LANES), u32),
                        pltpu.SemaphoreType.DMA((2,)), pltpu.SemaphoreType.DMA((2,))],
        compiler_params=_params(("arbitrary",)),
        name="combine",
    )(dest_flat, ys, wts, h2p, x1, mod, *tail)


def _layer(x, mod, w_in, pe, w_cmp1, w_cmp2, w_dw, b_dw, conv_ln_g, conv_ln_b, w_out, ln1_g, ln1_b,
           w_router, router_bias, w_e_gate, w_e_up, w_e_down, w_s_gate, w_s_up, w_s_down, ln2_g, ln2_b):
    b, s, d = x.shape
    t = b * s
    tm = min(512, s)
    qt, vt, gt, kvc, kn, conv_in = _in_proj(x, mod, w_in, tm)
    cmp_n, cmp_t = _compress(kvc, pe, w_cmp1, w_cmp2)
    gates = gt[:, :3 * N_HEADS].reshape(b, N_KV, 3 * HPG, s)
    gates = jnp.pad(gates, ((0, 0), (0, 0), (0, GATE_GROUP_ROWS - 3 * HPG), (0, 0)))
    attn = _attention(qt, gates, cmp_n, cmp_t, kn, vt)
    conv = _conv(conv_in, w_dw, b_dw, conv_ln_g, conv_ln_b, tm)

    tr = min(256, s)
    x1, h2p, idx, wts, counts = _mix_route(attn.reshape(t, -1), conv.reshape(t, -1), x.reshape(t, d), mod,
                                           w_out, ln1_g, ln1_b, w_router, router_bias, s, tr)
    counts = counts[:, 0].astype(i32)
    padded = (counts + ROW_BLOCK - 1) // ROW_BLOCK * ROW_BLOCK
    seg_end = jnp.cumsum(padded)
    seg_start = seg_end - padded
    n_blk = -(-(t * TOP_K + N_EXPERTS * (ROW_BLOCK - 1)) // ROW_BLOCK)

    dest = _positions(idx, seg_start.astype(f32).reshape(N_EXPERTS, 1), tr)
    dest_flat = dest.T.reshape(-1)
    wts = jnp.zeros((t, LANES), f32).at[:, :TOP_K].set(wts.T)
    xs = _dispatch(dest_flat, h2p, n_blk * ROW_BLOCK, d // 2 // LANES, min(1024, s))
    ys = _experts(padded // ROW_BLOCK, seg_start // ROW_BLOCK, counts, xs, w_e_gate, w_e_up, w_e_down)
    out = _combine(dest_flat, ys, wts, h2p, x1, mod, w_s_gate, w_s_up, w_s_down, ln2_g, ln2_b, s, min(256, s))
    return out.reshape(b, s, d)


def kernel(x, c, w_ada, b_ada, w_in, pe_k, pe_v, w_cmp_k1, w_cmp_k2, w_cmp_v1, w_cmp_v2, w_dw, b_dw,
           conv_ln_g, conv_ln_b, w_out, ln1_g, ln1_b, w_router, router_bias, w_e_gate, w_e_up, w_e_down,
           w_s_gate, w_s_up, w_s_down, ln2_g, ln2_b):
    assert w_ada.shape[0] == DEPTH
    layer = lambda a: a.reshape(a.shape[1:])
    mod = _ada(c, layer(w_ada), layer(b_ada))
    return _layer(x, mod, layer(w_in), jnp.concatenate([pe_k, pe_v]),
                  jnp.concatenate([w_cmp_k1, w_cmp_v1]), jnp.concatenate([w_cmp_k2, w_cmp_v2]),
                  *[layer(a) for a in (w_dw, b_dw, conv_ln_g, conv_ln_b, w_out, ln1_g, ln1_b, w_router,
                                       router_bias, w_e_gate, w_e_up, w_e_down, w_s_gate, w_s_up, w_s_down,
                                       ln2_g, ln2_b)])
```

```python
import functools

import jax
import jax.numpy as jnp
from jax import lax
from jax.experimental import pallas as pl
from jax.experimental.pallas import tpu as pltpu

N_HEADS = 8
N_KV = 2
HPG = N_HEADS // N_KV
HEAD_DIM = 64
D_ATTN = N_HEADS * HEAD_DIM
D_KV = N_KV * HEAD_DIM
CONV_WIDTH = 31
CMP_LEN = 32
CMP_STRIDE = 16
CMP_HID = 256
SEL_LEN = 64
SEL_TOPN = 16
WINDOW = 512
Q_BLOCK = 256
N_EXPERTS = 256
TOP_K = 8
N_GROUPS = 8
TOPK_GROUPS = 4
ROUTED_SCALE = 2.5
LN_EPS = 1e-5
DEPTH = 1
DEEPNORM_ALPHA = (2 * DEPTH) ** 0.25

LANES = 128
SUBLANES = 8
ROW_BLOCK = 512
NEG = -1e30
HIGHEST = lax.Precision.HIGHEST
VMEM_LIMIT = 48 * 1024 * 1024

f32 = jnp.float32
bf16 = jnp.bfloat16
i32 = jnp.int32
u32 = jnp.uint32


def _params(sem, vmem=VMEM_LIMIT):
    return pltpu.CompilerParams(dimension_semantics=sem, vmem_limit_bytes=vmem)


def _sigmoid(v):
    return 1.0 / (1.0 + jnp.exp(-v))


def _silu(v):
    return v * _sigmoid(v)


def _layer_norm(v, g, b):
    mu = jnp.mean(v, axis=-1, keepdims=True)
    var = jnp.mean(jnp.square(v - mu), axis=-1, keepdims=True)
    return (v - mu) * lax.rsqrt(var + LN_EPS) * g + b


def _dot_nt(a, b):
    return lax.dot_general(a, b, (((1,), (1,)), ((), ())), preferred_element_type=f32)


def _store_packed_rows(ref, v, first=0):
    n, d = v.shape
    half = d // 2
    bits = lax.bitcast_convert_type(v.astype(bf16).astype(f32), u32)
    words = bits[:, half:] | (bits[:, :half] >> 16)
    r = half // LANES
    for c in range(r):
        ref[pl.ds(first * r + c, n, stride=r), :] = words[:, c * LANES:(c + 1) * LANES]


def _unpack_words(w):
    return (lax.bitcast_convert_type(w << 16, f32),
            lax.bitcast_convert_type(w & jnp.uint32(0xFFFF0000), f32))


def _load_packed_rows(ref, n, d):
    r = d // 2 // LANES
    parts = [_unpack_words(ref[pl.ds(c, n, stride=r), :]) for c in range(r)]
    return jnp.concatenate([p[0] for p in parts] + [p[1] for p in parts], axis=1)


def _ada_kernel(c_ref, w_ref, b_ref, o_ref):
    c = c_ref[...]
    o_ref[...] = jnp.dot(_silu(c), w_ref[...], precision=HIGHEST,
                         preferred_element_type=f32) + b_ref[...]


def _ada(c, w_ada, b_ada):
    b, d = c.shape
    n = w_ada.shape[1]
    rows = 8
    c_pad = jnp.zeros((rows, d), f32).at[:b].set(c)
    tn = 1024
    out = pl.pallas_call(
        _ada_kernel,
        grid=(n // tn,),
        in_specs=[pl.BlockSpec((rows, d), lambda j: (0, 0)),
                  pl.BlockSpec((d, tn), lambda j: (0, j)),
                  pl.BlockSpec((1, tn), lambda j: (0, j))],
        out_specs=pl.BlockSpec((rows, tn), lambda j: (0, j)),
        out_shape=jax.ShapeDtypeStruct((rows, n), f32),
        compiler_params=_params(("arbitrary",)),
        name="ada",
    )(c_pad, w_ada, b_ada.reshape(1, n))
    return out[:b].reshape(b, 6, d)


GATE_ROWS = 32
KEY_TILE = 512
TILE_BLOCKS = KEY_TILE // SEL_LEN
PATCH_BLOCKS = Q_BLOCK // SEL_LEN
PAIR_UNROLL = 2
CMP_PER_SEL = SEL_LEN // CMP_STRIDE
CMP_BACK = CMP_LEN // CMP_STRIDE - 1
assert SEL_LEN % CMP_STRIDE == 0 and CMP_LEN % CMP_STRIDE == 0 and CMP_BACK < CMP_PER_SEL
V_ROWS = HEAD_DIM + 16
Q_SCALE = HEAD_DIM ** -0.5 * 1.4426950408889634


def _in_proj_kernel(x_ref, mod_ref, wt_ref, wc_ref, wk_ref, wv_ref,
                    qt_ref, vt_ref, gt_ref, kvc_ref, kn_ref, cv_ref):
    m = mod_ref[0]
    h = (x_ref[0] * (1.0 + m[1:2]) + m[0:1]).astype(bf16)
    res_t = _dot_nt(wt_ref[...], h)
    qt_ref[0] = (res_t[0:D_ATTN] * Q_SCALE).astype(bf16)
    ones = jnp.ones((V_ROWS - HEAD_DIM, res_t.shape[1]), bf16)
    for j in range(2):
        for g in range(N_KV):
            off = D_ATTN + (j * N_KV + g) * HEAD_DIM
            vt_ref[0, j, g, 0:HEAD_DIM, :] = res_t[off:off + HEAD_DIM].astype(bf16)
            vt_ref[0, j, g, HEAD_DIM:V_ROWS, :] = ones
    gt_ref[0] = res_t[D_ATTN + 2 * D_KV:]
    kvc = jnp.dot(h, wc_ref[...], preferred_element_type=f32)
    kvc_ref[0, 0] = kvc[:, :D_KV]
    kvc_ref[0, 1] = kvc[:, D_KV:]
    kn = jnp.dot(h, wk_ref[...], preferred_element_type=f32).astype(bf16)
    tm = kn.shape[0]
    pos = pl.program_id(1) * tm + lax.broadcasted_iota(i32, (tm, LANES - HEAD_DIM), 0)
    lane = lax.broadcasted_iota(i32, (tm, LANES - HEAD_DIM), 1)
    onehot = jnp.where(lane == ((pos >> 6) & (TILE_BLOCKS - 1)), 1.0, 0.0).astype(bf16)
    for j in range(2):
        for g in range(N_KV):
            off = (j * N_KV + g) * HEAD_DIM
            kn_ref[0, j, g] = jnp.concatenate([kn[:, off:off + HEAD_DIM], onehot], axis=1)
    cv_ref[0] = jnp.dot(h, wv_ref[...], preferred_element_type=f32)


def _in_proj(x, mod, w_in, tm):
    b, s, d = x.shape
    o = 0
    wq = w_in[:, o:o + D_ATTN]; o += D_ATTN
    wkc = w_in[:, o:o + 2 * D_KV]; o += 2 * D_KV
    wk_s = w_in[:, o:o + D_KV]; o += D_KV
    wv_s = w_in[:, o:o + D_KV]; o += D_KV
    wk_w = w_in[:, o:o + D_KV]; o += D_KV
    wv_w = w_in[:, o:o + D_KV]; o += D_KV
    wg = w_in[:, o:o + 3 * N_HEADS]; o += 3 * N_HEADS
    wcv = w_in[:, o:]
    d_conv2 = wcv.shape[1]
    wg = jnp.zeros((d, GATE_ROWS), f32).at[:, :3 * N_HEADS].set(wg)
    wt = jnp.concatenate([wq, wv_s, wv_w, wg], axis=1).T
    ws = [w.astype(bf16) for w in (wt, wkc, jnp.concatenate([wk_s, wk_w], axis=1), wcv)]
    full = lambda a: pl.BlockSpec(a.shape, lambda bi, i: (0, 0))
    return pl.pallas_call(
        _in_proj_kernel,
        grid=(b, s // tm),
        in_specs=[pl.BlockSpec((1, tm, d), lambda bi, i: (bi, i, 0)),
                  pl.BlockSpec((1, 6, d), lambda bi, i: (bi, 0, 0))] + [full(w) for w in ws],
        out_specs=[pl.BlockSpec((1, D_ATTN, tm), lambda bi, i: (bi, 0, i)),
                   pl.BlockSpec((1, 2, N_KV, V_ROWS, tm), lambda bi, i: (bi, 0, 0, 0, i)),
                   pl.BlockSpec((1, GATE_ROWS, tm), lambda bi, i: (bi, 0, i)),
                   pl.BlockSpec((1, 2, tm, D_KV), lambda bi, i: (bi, 0, i, 0)),
                   pl.BlockSpec((1, 2, N_KV, tm, LANES), lambda bi, i: (bi, 0, 0, i, 0)),
                   pl.BlockSpec((1, tm, d_conv2), lambda bi, i: (bi, i, 0))],
        out_shape=[jax.ShapeDtypeStruct((b, D_ATTN, s), bf16),
                   jax.ShapeDtypeStruct((b, 2, N_KV, V_ROWS, s), bf16),
                   jax.ShapeDtypeStruct((b, GATE_ROWS, s), f32),
                   jax.ShapeDtypeStruct((b, 2, s, D_KV), f32),
                   jax.ShapeDtypeStruct((b, 2, N_KV, s, LANES), bf16),
                   jax.ShapeDtypeStruct((b, s, d_conv2), f32)],
        compiler_params=_params(("parallel", "parallel")),
        name="in_proj",
    )(x, mod, *ws)


def _compress_kernel(x_ref, pe_ref, w1_ref, w2_ref, w2t_ref, o_ref, ot_ref):
    n_chunk = o_ref.shape[3]
    for j in range(2):
        for g in range(N_KV):
            cols = slice(g * HEAD_DIM, (g + 1) * HEAD_DIM)
            a = jnp.zeros((n_chunk, CMP_HID), f32)
            bm = jnp.zeros((n_chunk, CMP_HID), f32)
            for l in range(CMP_STRIDE):
                xl = x_ref[0, j, pl.ds(l, n_chunk, stride=CMP_STRIDE), :][:, cols]
                a = a + jnp.dot((xl + pe_ref[j, l:l + 1, :]).astype(bf16), w1_ref[j, l],
                                preferred_element_type=f32)
                bm = bm + jnp.dot((xl + pe_ref[j, CMP_STRIDE + l:CMP_STRIDE + l + 1, :]).astype(bf16),
                                  w1_ref[j, CMP_STRIDE + l], preferred_element_type=f32)
            hid = a + pltpu.roll(bm, n_chunk - 1, 0)
            act = 0.5 * hid * (1.0 + jnp.tanh(0.7978845608028654 * (hid + 0.044715 * (hid * hid * hid))))
            act = act.astype(bf16)
            o_ref[0, j, g] = jnp.dot(act, w2_ref[j], preferred_element_type=f32).astype(bf16)
            ot_ref[0, j, g] = _dot_nt(w2t_ref[j], act).astype(bf16)


def _compress(kvc, pe, w1, w2):
    assert CMP_LEN == 2 * CMP_STRIDE
    b, _, s, width = kvc.shape
    n_chunk = s // CMP_STRIDE
    w1b = w1.reshape(2, CMP_LEN, HEAD_DIM, CMP_HID).astype(bf16)
    w2b = w2.astype(bf16)
    w2t = w2b.transpose(0, 2, 1)
    full = lambda a: pl.BlockSpec(a.shape, lambda bi: (0,) * a.ndim)
    return pl.pallas_call(
        _compress_kernel,
        grid=(b,),
        in_specs=[pl.BlockSpec((1, 2, s, width), lambda bi: (bi, 0, 0, 0)),
                  full(pe), full(w1b), full(w2b), full(w2t)],
        out_specs=[pl.BlockSpec((1, 2, N_KV, n_chunk, HEAD_DIM), lambda bi: (bi, 0, 0, 0, 0)),
                   pl.BlockSpec((1, 2, N_KV, HEAD_DIM, n_chunk), lambda bi: (bi, 0, 0, 0, 0))],
        out_shape=[jax.ShapeDtypeStruct((b, 2, N_KV, n_chunk, HEAD_DIM), bf16),
                   jax.ShapeDtypeStruct((b, 2, N_KV, HEAD_DIM, n_chunk), bf16)],
        compiler_params=_params(("parallel",)),
        name="compress",
    )(kvc, pe, w1b, w2b, w2t)


def _attn_kernel(q_ref, g_ref, kc_ref, vc_ref, kn_ref, vt_ref, o_ref, psum_ref, selbias_ref, sa_ref, sb_ref,
                 sd_ref, *, seq, tk, top_n):
    i = pl.program_id(2)
    s0 = i * Q_BLOCK
    n_cmp_rows = kc_ref.shape[3]
    n_sel = seq // SEL_LEN

    q4 = q_ref[0]
    qt = jnp.concatenate([q4[p * HEAD_DIM:(p + 1) * HEAD_DIM, :] for p in range(HPG)], axis=1)
    t_row = s0 + lax.broadcasted_iota(i32, (1, Q_BLOCK), 1)

    s_c = jnp.dot(kc_ref[0, 0, 0], qt, preferred_element_type=f32)
    cmp_end = lax.broadcasted_iota(i32, (n_cmp_rows, 1), 0) * CMP_STRIDE + (CMP_LEN - 1)
    bias_c = jnp.where(cmp_end <= t_row, 0.0, NEG)
    any_c = t_row >= CMP_LEN - 1
    p_sum = jnp.zeros((n_cmp_rows, Q_BLOCK), f32)
    pcs = []
    for p in range(HPG):
        sp = s_c[:, p * Q_BLOCK:(p + 1) * Q_BLOCK] + bias_c
        e = jnp.exp2(sp - jnp.max(sp, axis=0, keepdims=True))
        pn = e * jnp.where(any_c, 1.0 / jnp.sum(e, axis=0, keepdims=True), 0.0)
        p_sum = p_sum + pn
        pcs.append(pn.astype(bf16))
    o_c = jnp.dot(vc_ref[0, 0, 0], jnp.concatenate(pcs, axis=1), preferred_element_type=f32)

    cols = HPG * Q_BLOCK
    pad_rows = LANES - HEAD_DIM
    q_pad = jnp.concatenate([qt, jnp.zeros((pad_rows, cols), bf16)], axis=0)
    span = WINDOW + Q_BLOCK
    w0 = pl.multiple_of(jnp.maximum(s0 - WINDOW, 0), Q_BLOCK)
    s_w = jnp.dot(kn_ref[0, 1, 0, pl.ds(w0, span), :], q_pad, preferred_element_type=f32)
    wpos = w0 + lax.broadcasted_iota(i32, (span, 1), 0)
    bias_w = jnp.where((wpos <= t_row) & (wpos > t_row - WINDOW), 0.0, NEG)
    pws = []
    for p in range(HPG):
        sp = s_w[:, p * Q_BLOCK:(p + 1) * Q_BLOCK] + bias_w
        pws.append(jnp.exp2(sp - jnp.max(sp, axis=0, keepdims=True)).astype(bf16))
    acc_w = jnp.dot(vt_ref[0, 1, 0, :, pl.ds(w0, span)], jnp.concatenate(pws, axis=1),
                    preferred_element_type=f32)
    o_w = acc_w[0:HEAD_DIM] * (1.0 / acc_w[HEAD_DIM:HEAD_DIM + 1])
    d0 = pl.multiple_of(s0, Q_BLOCK)
    s_d = jnp.dot(kn_ref[0, 0, 0, pl.ds(d0, Q_BLOCK), :], q_pad, preferred_element_type=f32)

    blk = lax.broadcasted_iota(i32, (n_sel, Q_BLOCK), 0)
    for lt in range(Q_BLOCK // LANES):
        psum_ref[lt] = p_sum[:, lt * LANES:(lt + 1) * LANES]

    def every(first):
        return jnp.concatenate([psum_ref[lt, pl.ds(first, n_sel, stride=CMP_PER_SEL), :]
                                for lt in range(Q_BLOCK // LANES)], axis=1)

    imp = every(0)
    for r in range(1, CMP_PER_SEL):
        imp = imp + every(r)
    for back in range(1, CMP_BACK + 1):
        imp = imp + jnp.where(blk >= 1, pltpu.roll(every(CMP_PER_SEL - back), 1, 0), 0.0)

    cur = t_row >> 6
    forced = (blk == 0) | (blk == cur) | (blk == cur - 1)
    vals = jnp.where(forced, jnp.inf, jnp.where(blk <= cur, imp, -jnp.inf))
    sel = jnp.zeros((n_sel, Q_BLOCK), f32)
    for _ in range(top_n):
        mx = jnp.max(vals, axis=0, keepdims=True)
        first = jnp.min(jnp.where(vals == mx, blk, n_sel), axis=0, keepdims=True)
        pick = blk == first
        sel = jnp.where(pick & (mx > -jnp.inf), 1.0, sel)
        vals = jnp.where(pick, -jnp.inf, vals)
    first_blk = s0 >> 6
    full_at = n_sel + TILE_BLOCKS
    selbias_ref[0:n_sel] = jnp.where((sel > 0.5) & (blk < first_blk), 0.0, NEG)
    selbias_ref[n_sel:full_at] = jnp.full((TILE_BLOCKS, Q_BLOCK), NEG, f32)
    selbias_ref[full_at:full_at + n_sel] = jnp.where(sel > 0.5, 0.0, NEG)
    last_tile = seq // tk - 1

    def scores(kt, s_ref):
        k0 = pl.multiple_of(jnp.minimum(kt, last_tile) * tk, tk)
        sb = selbias_ref[pl.ds(pl.multiple_of(kt * TILE_BLOCKS, TILE_BLOCKS), TILE_BLOCKS), :]
        rows = jnp.concatenate([jnp.concatenate([sb] * HPG, axis=1),
                                jnp.zeros((pad_rows - TILE_BLOCKS, cols), f32)], axis=0).astype(bf16)
        s_ref[...] = jnp.dot(kn_ref[0, 0, 0, pl.ds(k0, tk), :], jnp.concatenate([qt, rows], axis=0),
                             preferred_element_type=f32)

    def fold(scores_of_head, v_t, carry):
        m_i, acc = carry
        es, ms, alphas = [], [], []
        for p in range(HPG):
            c = slice(p * Q_BLOCK, (p + 1) * Q_BLOCK)
            sp = scores_of_head(c)
            m_new = jnp.maximum(m_i[:, c], jnp.max(sp, axis=0, keepdims=True))
            es.append(jnp.exp2(sp - m_new).astype(bf16))
            alphas.append(jnp.exp2(m_i[:, c] - m_new))
            ms.append(m_new)
        pv = jnp.dot(v_t, jnp.concatenate(es, axis=1), preferred_element_type=f32)
        return jnp.concatenate(ms, axis=1), jnp.concatenate(alphas, axis=1) * acc + pv

    def sel_tile(kt, s_ref, carry):
        k0 = pl.multiple_of(jnp.minimum(kt, last_tile) * tk, tk)
        return fold(lambda c: s_ref[:, c], vt_ref[0, 0, 0, :, pl.ds(k0, tk)], carry)

    def sel_pair(i, carry):
        scores(2 * i + 1, sb_ref)
        carry = sel_tile(2 * i, sa_ref, carry)
        scores(2 * i + 2, sa_ref)
        return sel_tile(2 * i + 1, sb_ref, carry)

    n_sweep = (s0 + tk - 1) // tk
    n_pairs = (n_sweep + 1) // 2
    init = (jnp.full((1, cols), NEG, f32), jnp.zeros((V_ROWS, cols), f32))
    scores(0, sa_ref)
    def sel_pairs(j, carry):
        for u in range(PAIR_UNROLL):
            carry = sel_pair(PAIR_UNROLL * j + u, carry)
        return carry

    group = selbias_ref[pl.ds(full_at + pl.multiple_of((first_blk >> 3) << 3, SUBLANES), SUBLANES), :]
    own = group[0:PATCH_BLOCKS]
    for at in range(PATCH_BLOCKS, SUBLANES, PATCH_BLOCKS):
        own = jnp.where((first_blk & (SUBLANES - 1)) == at, group[at:at + PATCH_BLOCKS], own)
    bias_d = jnp.concatenate([jnp.broadcast_to(own[j:j + 1], (SEL_LEN, Q_BLOCK)) for j in range(PATCH_BLOCKS)],
                             axis=0)
    bias_d = jnp.where(s0 + lax.broadcasted_iota(i32, (Q_BLOCK, 1), 0) <= t_row, bias_d, NEG)
    for p in range(HPG):
        c = slice(p * Q_BLOCK, (p + 1) * Q_BLOCK)
        sd_ref[:, c] = s_d[:, c] + bias_d

    carry = lax.fori_loop(0, n_pairs // PAIR_UNROLL, sel_pairs, init)
    carry = lax.fori_loop(n_pairs // PAIR_UNROLL * PAIR_UNROLL, n_pairs, sel_pair, carry)
    _, acc_s = fold(lambda c: sd_ref[:, c], vt_ref[0, 0, 0, :, pl.ds(d0, Q_BLOCK)], carry)
    o_s = acc_s[0:HEAD_DIM] * (1.0 / acc_s[HEAD_DIM:HEAD_DIM + 1])

    gate = _sigmoid(g_ref[0, 0])
    outs = []
    for p in range(HPG):
        c = slice(p * Q_BLOCK, (p + 1) * Q_BLOCK)
        outs.append(gate[3 * p:3 * p + 1] * o_c[:, c] + gate[3 * p + 1:3 * p + 2] * o_s[:, c]
                    + gate[3 * p + 2:3 * p + 3] * o_w[:, c])
    o_ref[0] = jnp.concatenate(outs, axis=0).T.astype(bf16)


GATE_GROUP_ROWS = 16


def _attention(qt, gates, cmp_n, cmp_t, kn, vt):
    b, _, s = qt.shape
    n_chunk = cmp_n.shape[3]
    n_sel = s // SEL_LEN
    top_n = min(SEL_TOPN, n_sel)
    tk = KEY_TILE
    assert s % tk == 0 and n_chunk == n_sel * CMP_PER_SEL
    gw = HPG * HEAD_DIM
    return pl.pallas_call(
        functools.partial(_attn_kernel, seq=s, tk=tk, top_n=top_n),
        grid=(b, N_KV, s // Q_BLOCK),
        in_specs=[pl.BlockSpec((1, gw, Q_BLOCK), lambda bi, g, i: (bi, g, i)),
                  pl.BlockSpec((1, 1, GATE_GROUP_ROWS, Q_BLOCK), lambda bi, g, i: (bi, g, 0, i)),
                  pl.BlockSpec((1, 1, 1, n_chunk, HEAD_DIM), lambda bi, g, i: (bi, 0, g, 0, 0)),
                  pl.BlockSpec((1, 1, 1, HEAD_DIM, n_chunk), lambda bi, g, i: (bi, 1, g, 0, 0)),
                  pl.BlockSpec((1, 2, 1, s, LANES), lambda bi, g, i: (bi, 0, g, 0, 0)),
                  pl.BlockSpec((1, 2, 1, V_ROWS, s), lambda bi, g, i: (bi, 0, g, 0, 0))],
        out_specs=pl.BlockSpec((1, Q_BLOCK, gw), lambda bi, g, i: (bi, i, g)),
        out_shape=jax.ShapeDtypeStruct((b, s, D_ATTN), bf16),
        scratch_shapes=[pltpu.VMEM((Q_BLOCK // LANES, n_chunk, LANES), f32),
                        pltpu.VMEM((2 * n_sel + TILE_BLOCKS, Q_BLOCK), f32),
                        pltpu.VMEM((tk, HPG * Q_BLOCK), f32), pltpu.VMEM((tk, HPG * Q_BLOCK), f32),
                        pltpu.VMEM((Q_BLOCK, HPG * Q_BLOCK), f32)],
        compiler_params=_params(("parallel", "parallel", "arbitrary")),
        name="attention",
    )(qt, gates, cmp_n, cmp_t, kn, vt)


CONV_HALO = 32
CONV_ROWS = 32


def _conv_kernel(cur_ref, prev_ref, w_ref, b_ref, g_ref, bb_ref, o_ref, glu_ref):
    i = pl.program_id(1)
    ts = cur_ref.shape[1]
    dc = o_ref.shape[2]
    cur = cur_ref[0]
    prev = prev_ref[0]
    glu_prev = prev[:, :dc] * _sigmoid(prev[:, dc:])
    glu_ref[0:CONV_HALO] = jnp.where(i == 0, 0.0, glu_prev)
    glu_ref[CONV_HALO:CONV_HALO + ts] = cur[:, :dc] * _sigmoid(cur[:, dc:])
    lead = CONV_HALO - (CONV_WIDTH - 1)

    def chunk(r, _):
        r0 = pl.multiple_of(r * CONV_ROWS, CONV_ROWS)
        win = glu_ref[pl.ds(r0, CONV_ROWS + CONV_HALO), :]
        span = CONV_ROWS + CONV_HALO
        acc = jnp.zeros((CONV_ROWS, dc), f32)
        for ph in range(SUBLANES):
            turned = win if ph == 0 else pltpu.roll(win, span - ph, 0)
            for j in range(CONV_WIDTH):
                if (lead + j) % SUBLANES == ph:
                    at = lead + j - ph
                    acc = acc + w_ref[j:j + 1, :] * turned[at:at + CONV_ROWS]
        y = _layer_norm(acc + b_ref[...], g_ref[...], bb_ref[...])
        o_ref[0, pl.ds(r0, CONV_ROWS), :] = _silu(y).astype(bf16)
        return 0

    lax.fori_loop(0, ts // CONV_ROWS, chunk, 0)


def _conv(conv_in, w_dw, b_dw, ln_g, ln_b, ts):
    b, s, dc2 = conv_in.shape
    dc = dc2 // 2
    per = ts // CONV_HALO
    row = lambda a: a.reshape(1, dc)
    return pl.pallas_call(
        _conv_kernel,
        grid=(b, s // ts),
        in_specs=[pl.BlockSpec((1, ts, dc2), lambda bi, i: (bi, i, 0)),
                  pl.BlockSpec((1, CONV_HALO, dc2), lambda bi, i: (bi, jnp.maximum(i * per - 1, 0), 0)),
                  pl.BlockSpec((CONV_WIDTH, dc), lambda bi, i: (0, 0)),
                  pl.BlockSpec((1, dc), lambda bi, i: (0, 0)),
                  pl.BlockSpec((1, dc), lambda bi, i: (0, 0)),
                  pl.BlockSpec((1, dc), lambda bi, i: (0, 0))],
        out_specs=pl.BlockSpec((1, ts, dc), lambda bi, i: (bi, i, 0)),
        out_shape=jax.ShapeDtypeStruct((b, s, dc), bf16),
        scratch_shapes=[pltpu.VMEM((CONV_HALO + ts, dc), f32)],
        compiler_params=_params(("parallel", "parallel")),
        name="conv",
    )(conv_in, conv_in, w_dw.reshape(CONV_WIDTH, dc), row(b_dw), row(ln_g), row(ln_b))


ROUTE_HALF = 128


def _mix_route_kernel(a_ref, cv_ref, x_ref, mod_ref, wo_ref, lg_ref, lb_ref, wrh_ref, wrl_ref, rb_ref,
                      x1_ref, h2p_ref, idx_ref, wt_ref, cnt_ref):
    step = pl.program_id(0)

    @pl.when(step == 0)
    def _():
        cnt_ref[...] = jnp.zeros_like(cnt_ref)

    m = mod_ref[0]
    da = a_ref.shape[1]
    counts = jnp.zeros(cnt_ref.shape, f32)
    for first in range(0, x_ref.shape[0], ROUTE_HALF):
        rows = slice(first, first + ROUTE_HALF)
        mix = (jnp.dot(a_ref[rows, :], wo_ref[0:da, :], preferred_element_type=f32)
               + jnp.dot(cv_ref[rows, :], wo_ref[da:, :], preferred_element_type=f32))
        x1 = _layer_norm(DEEPNORM_ALPHA * x_ref[rows, :] + m[2:3] * mix, lg_ref[...], lb_ref[...])
        x1_ref[rows, :] = x1
        h2 = x1 * (1.0 + m[4:5]) + m[3:4]
        _store_packed_rows(h2p_ref, h2, first)
        h_hi = h2.astype(bf16)
        h_lo = (h2 - h_hi.astype(f32)).astype(bf16)
        logits = _dot_nt(wrh_ref[...], h_hi) + (_dot_nt(wrh_ref[...], h_lo) + _dot_nt(wrl_ref[...], h_hi))
        idx, wt, cnt = _route(_sigmoid(logits), rb_ref[...])
        idx_ref[:, rows] = idx
        wt_ref[:, rows] = wt
        counts = counts + cnt
    cnt_ref[...] += counts


def _route(score, bias):
    tm = score.shape[1]
    sel = score + bias
    eid = lax.broadcasted_iota(i32, (N_EXPERTS, tm), 0)
    per_group = N_EXPERTS // N_GROUPS
    gs = []
    for g in range(N_GROUPS):
        rows = slice(g * per_group, (g + 1) * per_group)
        v = sel[rows]
        e = g * per_group + lax.broadcasted_iota(i32, (per_group, tm), 0)
        m1 = jnp.max(v, axis=0, keepdims=True)
        i1 = jnp.min(jnp.where(v == m1, e, N_EXPERTS), axis=0, keepdims=True)
        m2 = jnp.max(jnp.where(e == i1, -jnp.inf, v), axis=0, keepdims=True)
        gs.append(m1 + m2)
    cands = []
    for g in range(N_GROUPS):
        rank = jnp.zeros((1, tm), i32)
        for o in range(N_GROUPS):
            if o == g:
                continue
            beats = (gs[o] > gs[g]) | (gs[o] == gs[g]) if o < g else gs[o] > gs[g]
            rank = rank + beats.astype(i32)
        drop = jnp.where(rank < TOPK_GROUPS, 0.0, -jnp.inf)
        cands.append(sel[g * per_group:(g + 1) * per_group] + drop)
    cand = jnp.concatenate(cands, axis=0)
    row_o = lax.broadcasted_iota(i32, (TOP_K, tm), 0)
    idx_out = jnp.zeros((TOP_K, tm), i32)
    wt_out = jnp.zeros((TOP_K, tm), f32)
    picked = jnp.zeros((N_EXPERTS, tm), f32)
    w_sum = jnp.zeros((1, tm), f32)
    for k in range(TOP_K):
        mx = jnp.max(cand, axis=0, keepdims=True)
        ik = jnp.min(jnp.where(cand == mx, eid, N_EXPERTS), axis=0, keepdims=True)
        pick = eid == ik
        wk = jnp.sum(jnp.where(pick, score, 0.0), axis=0, keepdims=True)
        cand = jnp.where(pick, -jnp.inf, cand)
        picked = jnp.where(pick, 1.0, picked)
        idx_out = jnp.where(row_o == k, ik, idx_out)
        wt_out = jnp.where(row_o == k, wk, wt_out)
        w_sum = w_sum + wk
    return idx_out, wt_out / w_sum * ROUTED_SCALE, jnp.sum(picked, axis=1, keepdims=True)


def _mix_route(attn, conv, x2, mod, w_out, ln_g, ln_b, w_router, router_bias, s, tm):
    t, d = x2.shape
    per = s // tm
    da = attn.shape[1]
    pack_rows = d // 2 // LANES
    row = lambda a: a.reshape(1, -1)
    tile = lambda w: pl.BlockSpec((tm, w), lambda i: (i, 0))
    full = lambda a: pl.BlockSpec(a.shape, lambda i: (0,) * a.ndim)
    assert tm % ROUTE_HALF == 0
    wr_hi = w_router.T.astype(bf16)
    wr_lo = (w_router.T - wr_hi.astype(f32)).astype(bf16)
    args = (attn, conv, x2, mod, w_out.astype(bf16), row(ln_g), row(ln_b), wr_hi, wr_lo,
            router_bias.reshape(N_EXPERTS, 1))
    per_token = lambda rows: pl.BlockSpec((rows, tm), lambda i: (0, i))
    return pl.pallas_call(
        _mix_route_kernel,
        grid=(t // tm,),
        in_specs=[tile(da), tile(conv.shape[1]), tile(d),
                  pl.BlockSpec((1, 6, d), lambda i: (i // per, 0, 0))] + [full(a) for a in args[4:]],
        out_specs=[tile(d), pl.BlockSpec((tm * pack_rows, LANES), lambda i: (i, 0)),
                   per_token(TOP_K), per_token(TOP_K), pl.BlockSpec((N_EXPERTS, 1), lambda i: (0, 0))],
        out_shape=[jax.ShapeDtypeStruct((t, d), f32),
                   jax.ShapeDtypeStruct((t * pack_rows, LANES), u32),
                   jax.ShapeDtypeStruct((TOP_K, t), i32), jax.ShapeDtypeStruct((TOP_K, t), f32),
                   jax.ShapeDtypeStruct((N_EXPERTS, 1), f32)],
        compiler_params=_params(("arbitrary",)),
        name="mix_route",
    )(*args)


def _positions_kernel(idx_ref, start_ref, o_ref, run_ref):
    step = pl.program_id(0)
    tm = idx_ref.shape[1]

    @pl.when(step == 0)
    def _():
        run_ref[...] = jnp.zeros_like(run_ref)

    idx = idx_ref[...]
    eid = lax.broadcasted_iota(i32, (N_EXPERTS, tm), 0)
    onehot = jnp.zeros((N_EXPERTS, tm), f32)
    for k in range(TOP_K):
        onehot = jnp.where(eid == idx[k:k + 1], 1.0, onehot)
    r = lax.broadcasted_iota(i32, (tm, tm), 0)
    c = lax.broadcasted_iota(i32, (tm, tm), 1)
    earlier = jnp.where(r < c, 1.0, 0.0).astype(bf16)
    prior = jnp.dot(onehot.astype(bf16), earlier, preferred_element_type=f32)
    pos = prior + run_ref[...] + start_ref[...]
    row_o = lax.broadcasted_iota(i32, (TOP_K, tm), 0)
    out = jnp.zeros((TOP_K, tm), i32)
    for k in range(TOP_K):
        dk = jnp.sum(jnp.where(eid == idx[k:k + 1], pos, 0.0), axis=0, keepdims=True)
        out = jnp.where(row_o == k, dk.astype(i32), out)
    o_ref[...] = out
    run_ref[...] += jnp.sum(onehot, axis=1, keepdims=True)


def _positions(idx, seg_start, tm):
    t = idx.shape[1]
    return pl.pallas_call(
        _positions_kernel,
        grid=(t // tm,),
        in_specs=[pl.BlockSpec((TOP_K, tm), lambda i: (0, i)),
                  pl.BlockSpec((N_EXPERTS, 1), lambda i: (0, 0))],
        out_specs=pl.BlockSpec((TOP_K, tm), lambda i: (0, i)),
        out_shape=jax.ShapeDtypeStruct((TOP_K, t), i32),
        scratch_shapes=[pltpu.VMEM((N_EXPERTS, 1), f32)],
        compiler_params=_params(("arbitrary",)),
        name="positions",
    )(idx, seg_start)


def _dispatch_kernel(dest_hbm, h_ref, o_hbm, dest_smem, sem_i, sem, *, r):
    step = pl.program_id(0)
    n = dest_smem.shape[0]
    load = pltpu.make_async_copy(dest_hbm.at[pl.ds(pl.multiple_of(step * n, n), n)], dest_smem, sem_i)
    load.start()
    load.wait()

    def issue(tok, _):
        src = pl.multiple_of(tok * r, r)
        for k in range(TOP_K):
            dst = pl.multiple_of(dest_smem[tok * TOP_K + k] * r, r)
            pltpu.make_async_copy(h_ref.at[pl.ds(src, r)], o_hbm.at[pl.ds(dst, r)], sem).start(priority=k % 2)
        return 0

    lax.fori_loop(0, n // TOP_K, issue, 0)
    everything = o_hbm.at[pl.ds(0, n * r)]
    pltpu.make_async_copy(everything, everything, sem).wait()


def _dispatch(dest_flat, h2p, n_buf, r, tm):
    t = h2p.shape[0] // r
    return pl.pallas_call(
        functools.partial(_dispatch_kernel, r=r),
        grid=(t // tm,),
        in_specs=[pl.BlockSpec(memory_space=pl.ANY),
                  pl.BlockSpec((tm * r, LANES), lambda i: (i, 0))],
        out_specs=pl.BlockSpec(memory_space=pl.ANY),
        out_shape=jax.ShapeDtypeStruct((n_buf * r, LANES), u32),
        scratch_shapes=[pltpu.SMEM((tm * TOP_K,), i32), pltpu.SemaphoreType.DMA, pltpu.SemaphoreType.DMA],
        compiler_params=_params(("arbitrary",)),
        name="dispatch",
    )(dest_flat, h2p)


def _experts_kernel(be_ref, bv_ref, nu_ref, x_ref, wg_ref, wu_ref, wd_ref, o_ref, wg_s, wu_s, wd_s):
    j = pl.program_id(0)
    prev = be_ref[jnp.maximum(j - 1, 0)]
    used = j < nu_ref[0]
    d = wg_s.shape[0]

    @pl.when(used & ((j == 0) | (be_ref[j] != prev)))
    def _():
        wg_s[...] = wg_ref[0].astype(bf16)
        wu_s[...] = wu_ref[0].astype(bf16)
        wd_s[...] = wd_ref[0].astype(bf16)

    @pl.when(used)
    def _():
        live = lax.broadcasted_iota(i32, (ROW_BLOCK, 1), 0) < bv_ref[j]
        x = jnp.where(live, _load_packed_rows(x_ref, ROW_BLOCK, d), 0.0).astype(bf16)
        hg = jnp.dot(x, wg_s[...], preferred_element_type=f32)
        hu = jnp.dot(x, wu_s[...], preferred_element_type=f32)
        hid = (_silu(hg) * hu).astype(bf16)
        _store_packed_rows(o_ref, jnp.dot(hid, wd_s[...], preferred_element_type=f32))


def _experts(blk_e, blk_valid, n_used, xs, w_gate, w_up, w_down):
    d, f = w_gate.shape[1], w_gate.shape[2]
    r = d // 2 // LANES
    n_blk = xs.shape[0] // r // ROW_BLOCK
    rows = lambda j, be, bv, nu: (jnp.minimum(j, nu[0] - 1), 0)
    wsel = lambda j, be, bv, nu: (be[j], 0, 0)
    return pl.pallas_call(
        _experts_kernel,
        grid_spec=pltpu.PrefetchScalarGridSpec(
            num_scalar_prefetch=3,
            grid=(n_blk,),
            in_specs=[pl.BlockSpec((ROW_BLOCK * r, LANES), rows),
                      pl.BlockSpec((1, d, f), wsel),
                      pl.BlockSpec((1, d, f), wsel),
                      pl.BlockSpec((1, f, d), wsel)],
            out_specs=pl.BlockSpec((ROW_BLOCK * r, LANES), rows),
            scratch_shapes=[pltpu.VMEM((d, f), bf16), pltpu.VMEM((d, f), bf16), pltpu.VMEM((f, d), bf16)]),
        out_shape=jax.ShapeDtypeStruct(xs.shape, u32),
        compiler_params=_params(("arbitrary",)),
        name="experts",
    )(blk_e, blk_valid, n_used, xs, w_gate, w_up, w_down)


COMBINE_CHUNK = 64


def _combine_kernel(dest_hbm, ys_hbm, wt_ref, h_ref, x1_ref, mod_ref, wsg_ref, wsu_ref, wsd_ref,
                    lg_ref, lb_ref, o_ref, dest_a, dest_b, rows_a, rows_b, sem_d, sem_g):
    step = pl.program_id(0)
    n_steps = pl.num_programs(0)
    tm, d = x1_ref.shape
    r = d // 2 // LANES
    n = tm * TOP_K
    tables, rows = (dest_a, dest_b), (rows_a, rows_b)

    def table_copy(tile, slot):
        return pltpu.make_async_copy(dest_hbm.at[pl.ds(pl.multiple_of(tile * n, n), n)],
                                     tables[slot], sem_d.at[slot])

    def start_gathers(slot):
        def issue(tok, _):
            dst = pl.multiple_of(tok * r, r)
            for k in range(TOP_K):
                src = pl.multiple_of(tables[slot][tok * TOP_K + k] * r, r)
                pltpu.make_async_copy(ys_hbm.at[pl.ds(src, r)], rows[slot].at[k, pl.ds(dst, r)],
                                      sem_g.at[slot]).start(priority=k % 2)
            return 0

        lax.fori_loop(0, tm, issue, 0)

    @pl.when(step == 0)
    def _():
        table_copy(0, 0).start()
        table_copy(0, 0).wait()
        start_gathers(0)
        second = jnp.minimum(1, n_steps - 1)
        table_copy(second, 1).start()

        @pl.when(n_steps == 1)
        def _():
            table_copy(second, 1).wait()

    def tile_body(slot):
        other = 1 - slot

        @pl.when(step + 2 < n_steps)
        def _():
            table_copy(step + 2, slot).start()

        @pl.when(step + 1 < n_steps)
        def _():
            table_copy(step + 1, other).wait()

        h = _load_packed_rows(h_ref, tm, d).astype(bf16)
        hg = jnp.dot(h, wsg_ref[...], preferred_element_type=f32)
        hu = jnp.dot(h, wsu_ref[...], preferred_element_type=f32)
        shared = jnp.dot((_silu(hg) * hu).astype(bf16), wsd_ref[...], preferred_element_type=f32)
        pltpu.make_async_copy(rows[slot], rows[slot], sem_g.at[slot]).wait()
        m = mod_ref[0]
        for first in range(0, tm, COMBINE_CHUNK):
            for tok in range(first, first + COMBINE_CHUNK):
                for k in range(TOP_K):
                    src = pl.multiple_of(tables[other][tok * TOP_K + k] * r, r)
                    pltpu.make_async_copy(ys_hbm.at[pl.ds(src, r)], rows[other].at[k, pl.ds(tok * r, r)],
                                          sem_g.at[other]).start(priority=k % 2)
            sl = slice(first, first + COMBINE_CHUNK)
            wt = wt_ref[sl, :]
            lo = [jnp.zeros((COMBINE_CHUNK, LANES), f32) for _ in range(r)]
            hi = [jnp.zeros((COMBINE_CHUNK, LANES), f32) for _ in range(r)]
            for k in range(TOP_K):
                wk = wt[:, k:k + 1]
                for c in range(r):
                    pl_, ph_ = _unpack_words(rows[slot][k, pl.ds(first * r + c, COMBINE_CHUNK, stride=r), :])
                    lo[c] = lo[c] + wk * pl_
                    hi[c] = hi[c] + wk * ph_
            y = shared[sl] + jnp.concatenate(lo + hi, axis=1)
            o_ref[sl, :] = _layer_norm(DEEPNORM_ALPHA * x1_ref[sl, :] + m[5:6] * y, lg_ref[...], lb_ref[...])

        @pl.when(step + 1 == n_steps)
        def _():
            pltpu.make_async_copy(rows[other], rows[other], sem_g.at[other]).wait()

    for slot in range(2):
        pl.when(step % 2 == slot)(functools.partial(tile_body, slot))


def _combine(dest_flat, ys, wts, h2p, x1, mod, w_s_gate, w_s_up, w_s_down, ln_g, ln_b, s, tm):
    t, d = x1.shape
    per = s // tm
    r = d // 2 // LANES
    row = lambda a: a.reshape(1, -1)
    tile = lambda w: pl.BlockSpec((tm, w), lambda i: (i, 0))
    full = lambda a: pl.BlockSpec(a.shape, lambda i: (0,) * a.ndim)
    tail = (w_s_gate.astype(bf16), w_s_up.astype(bf16), w_s_down.astype(bf16), row(ln_g), row(ln_b))
    return pl.pallas_call(
        _combine_kernel,
        grid=(t // tm,),
        in_specs=[pl.BlockSpec(memory_space=pl.ANY), pl.BlockSpec(memory_space=pl.ANY),
                  tile(LANES), pl.BlockSpec((tm * r, LANES), lambda i: (i, 0)), tile(d),
                  pl.BlockSpec((1, 6, d), lambda i: (i // per, 0, 0))] + [full(a) for a in tail],
        out_specs=tile(d),
        out_shape=jax.ShapeDtypeStruct((t, d), f32),
        scratch_shapes=[pltpu.SMEM((tm * TOP_K,), i32), pltpu.SMEM((tm * TOP_K,), i32),
                        pltpu.VMEM((TOP_K, tm * r, LANES), u32), pltpu.VMEM((TOP_K, tm * r, LANES), u32),
                        pltpu.SemaphoreType.DMA((2,)), pltpu.SemaphoreType.DMA((2,))],
        compiler_params=_params(("arbitrary",)),
        name="combine",
    )(dest_flat, ys, wts, h2p, x1, mod, *tail)


def _layer(x, mod, w_in, pe, w_cmp1, w_cmp2, w_dw, b_dw, conv_ln_g, conv_ln_b, w_out, ln1_g, ln1_b,
           w_router, router_bias, w_e_gate, w_e_up, w_e_down, w_s_gate, w_s_up, w_s_down, ln2_g, ln2_b):
    b, s, d = x.shape
    t = b * s
    tm = min(512, s)
    qt, vt, gt, kvc, kn, conv_in = _in_proj(x, mod, w_in, tm)
    cmp_n, cmp_t = _compress(kvc, pe, w_cmp1, w_cmp2)
    gates = gt[:, :3 * N_HEADS].reshape(b, N_KV, 3 * HPG, s)
    gates = jnp.pad(gates, ((0, 0), (0, 0), (0, GATE_GROUP_ROWS - 3 * HPG), (0, 0)))
    attn = _attention(qt, gates, cmp_n, cmp_t, kn, vt)
    conv = _conv(conv_in, w_dw, b_dw, conv_ln_g, conv_ln_b, tm)

    tr = min(256, s)
    x1, h2p, idx, wts, counts = _mix_route(attn.reshape(t, -1), conv.reshape(t, -1), x.reshape(t, d), mod,
                                           w_out, ln1_g, ln1_b, w_router, router_bias, s, tr)
    counts = counts[:, 0].astype(i32)
    padded = (counts + ROW_BLOCK - 1) // ROW_BLOCK * ROW_BLOCK
    seg_end = jnp.cumsum(padded)
    seg_start = seg_end - padded
    n_blk = -(-(t * TOP_K + N_EXPERTS * (ROW_BLOCK - 1)) // ROW_BLOCK)
    blk_row0 = jnp.arange(n_blk, dtype=i32) * ROW_BLOCK
    owns = (blk_row0[:, None] >= seg_start[None, :]) & (blk_row0[:, None] < seg_end[None, :])
    blk_e = jnp.where(blk_row0 < seg_end[-1], jnp.argmax(owns, axis=1), N_EXPERTS - 1).astype(i32)
    live_end = jnp.sum(jnp.where(owns, (seg_start + counts)[None, :], 0), axis=1)
    blk_valid = jnp.clip(live_end - blk_row0, 0, ROW_BLOCK).astype(i32)
    n_used = (seg_end[-1:] // ROW_BLOCK).astype(i32)

    dest = _positions(idx, seg_start.astype(f32).reshape(N_EXPERTS, 1), tr)
    dest_flat = dest.T.reshape(-1)
    wts = jnp.zeros((t, LANES), f32).at[:, :TOP_K].set(wts.T)
    xs = _dispatch(dest_flat, h2p, n_blk * ROW_BLOCK, d // 2 // LANES, min(1024, s))
    ys = _experts(blk_e, blk_valid, n_used, xs, w_e_gate, w_e_up, w_e_down)
    out = _combine(dest_flat, ys, wts, h2p, x1, mod, w_s_gate, w_s_up, w_s_down, ln2_g, ln2_b, s, min(256, s))
    return out.reshape(b, s, d)


def kernel(x, c, w_ada, b_ada, w_in, pe_k, pe_v, w_cmp_k1, w_cmp_k2, w_cmp_v1, w_cmp_v2, w_dw, b_dw,
           conv_ln_g, conv_ln_b, w_out, ln1_g, ln1_b, w_router, router_bias, w_e_gate, w_e_up, w_e_down,
           w_s_gate, w_s_up, w_s_down, ln2_g, ln2_b):
    assert w_ada.shape[0] == DEPTH
    layer = lambda a: a.reshape(a.shape[1:])
    mod = _ada(c, layer(w_ada), layer(b_ada))
    return _layer(x, mod, layer(w_in), jnp.concatenate([pe_k, pe_v]),
                  jnp.concatenate([w_cmp_k1, w_cmp_v1]), jnp.concatenate([w_cmp_k2, w_cmp_v2]),
                  *[layer(a) for a in (w_dw, b_dw, conv_ln_g, conv_ln_b, w_out, ln1_g, ln1_b, w_router,
                                       router_bias, w_e_gate, w_e_up, w_e_down, w_s_gate, w_s_up, w_s_down,
                                       ln2_g, ln2_b)])
```

```python
import functools

import jax
import jax.numpy as jnp
from jax import lax
from jax.experimental import pallas as pl
from jax.experimental.pallas import tpu as pltpu

N_HEADS = 8
N_KV = 2
HPG = N_HEADS // N_KV
HEAD_DIM = 64
D_ATTN = N_HEADS * HEAD_DIM
D_KV = N_KV * HEAD_DIM
CONV_WIDTH = 31
CMP_LEN = 32
CMP_STRIDE = 16
CMP_HID = 256
SEL_LEN = 64
SEL_TOPN = 16
WINDOW = 512
Q_BLOCK = 256
N_EXPERTS = 256
TOP_K = 8
N_GROUPS = 8
TOPK_GROUPS = 4
ROUTED_SCALE = 2.5
LN_EPS = 1e-5
DEPTH = 1
DEEPNORM_ALPHA = (2 * DEPTH) ** 0.25

LANES = 128
SUBLANES = 8
ROW_BLOCK = 512
NEG = -1e30
HIGHEST = lax.Precision.HIGHEST
VMEM_LIMIT = 48 * 1024 * 1024

f32 = jnp.float32
bf16 = jnp.bfloat16
i32 = jnp.int32
u32 = jnp.uint32


def _params(sem, vmem=VMEM_LIMIT):
    return pltpu.CompilerParams(dimension_semantics=sem, vmem_limit_bytes=vmem)


def _sigmoid(v):
    return 1.0 / (1.0 + jnp.exp(-v))


def _silu(v):
    return v * _sigmoid(v)


def _layer_norm(v, g, b):
    mu = jnp.mean(v, axis=-1, keepdims=True)
    var = jnp.mean(jnp.square(v - mu), axis=-1, keepdims=True)
    return (v - mu) * lax.rsqrt(var + LN_EPS) * g + b


def _dot_nt(a, b):
    return lax.dot_general(a, b, (((1,), (1,)), ((), ())), preferred_element_type=f32)


def _store_packed_rows(ref, v, first=0):
    n, d = v.shape
    half = d // 2
    bits = lax.bitcast_convert_type(v.astype(bf16).astype(f32), u32)
    words = bits[:, half:] | (bits[:, :half] >> 16)
    r = half // LANES
    for c in range(r):
        ref[pl.ds(first * r + c, n, stride=r), :] = words[:, c * LANES:(c + 1) * LANES]


def _unpack_words(w):
    return (lax.bitcast_convert_type(w << 16, f32),
            lax.bitcast_convert_type(w & jnp.uint32(0xFFFF0000), f32))


def _load_packed_rows(ref, n, d):
    r = d // 2 // LANES
    parts = [_unpack_words(ref[pl.ds(c, n, stride=r), :]) for c in range(r)]
    return jnp.concatenate([p[0] for p in parts] + [p[1] for p in parts], axis=1)


def _ada_kernel(c_ref, w_ref, b_ref, o_ref):
    c = c_ref[...]
    o_ref[...] = jnp.dot(_silu(c), w_ref[...], precision=HIGHEST,
                         preferred_element_type=f32) + b_ref[...]


def _ada(c, w_ada, b_ada):
    b, d = c.shape
    n = w_ada.shape[1]
    rows = 8
    c_pad = jnp.zeros((rows, d), f32).at[:b].set(c)
    tn = 1024
    out = pl.pallas_call(
        _ada_kernel,
        grid=(n // tn,),
        in_specs=[pl.BlockSpec((rows, d), lambda j: (0, 0)),
                  pl.BlockSpec((d, tn), lambda j: (0, j)),
                  pl.BlockSpec((1, tn), lambda j: (0, j))],
        out_specs=pl.BlockSpec((rows, tn), lambda j: (0, j)),
        out_shape=jax.ShapeDtypeStruct((rows, n), f32),
        compiler_params=_params(("arbitrary",)),
        name="ada",
    )(c_pad, w_ada, b_ada.reshape(1, n))
    return out[:b].reshape(b, 6, d)


GATE_ROWS = 32
KEY_TILE = 512
TILE_BLOCKS = KEY_TILE // SEL_LEN
PATCH_BLOCKS = Q_BLOCK // SEL_LEN
PAIR_UNROLL = 2
MAX_FORCED = 3
CMP_PER_SEL = SEL_LEN // CMP_STRIDE
CMP_BACK = CMP_LEN // CMP_STRIDE - 1
assert SEL_LEN % CMP_STRIDE == 0 and CMP_LEN % CMP_STRIDE == 0 and CMP_BACK < CMP_PER_SEL
V_ROWS = HEAD_DIM + 16
Q_SCALE = HEAD_DIM ** -0.5 * 1.4426950408889634


def _in_proj_kernel(x_ref, mod_ref, wt_ref, wc_ref, wk_ref, wv_ref,
                    qt_ref, vt_ref, gt_ref, kvc_ref, kn_ref, cv_ref):
    m = mod_ref[0]
    h = (x_ref[0] * (1.0 + m[1:2]) + m[0:1]).astype(bf16)
    res_t = _dot_nt(wt_ref[...], h)
    qt_ref[0] = (res_t[0:D_ATTN] * Q_SCALE).astype(bf16)
    ones = jnp.ones((V_ROWS - HEAD_DIM, res_t.shape[1]), bf16)
    for j in range(2):
        for g in range(N_KV):
            off = D_ATTN + (j * N_KV + g) * HEAD_DIM
            vt_ref[0, j, g, 0:HEAD_DIM, :] = res_t[off:off + HEAD_DIM].astype(bf16)
            vt_ref[0, j, g, HEAD_DIM:V_ROWS, :] = ones
    gt_ref[0] = res_t[D_ATTN + 2 * D_KV:]
    kvc = jnp.dot(h, wc_ref[...], preferred_element_type=f32)
    kvc_ref[0, 0] = kvc[:, :D_KV]
    kvc_ref[0, 1] = kvc[:, D_KV:]
    kn = jnp.dot(h, wk_ref[...], preferred_element_type=f32).astype(bf16)
    tm = kn.shape[0]
    pos = pl.program_id(1) * tm + lax.broadcasted_iota(i32, (tm, LANES - HEAD_DIM), 0)
    lane = lax.broadcasted_iota(i32, (tm, LANES - HEAD_DIM), 1)
    onehot = jnp.where(lane == ((pos >> 6) & (TILE_BLOCKS - 1)), 1.0, 0.0).astype(bf16)
    for j in range(2):
        for g in range(N_KV):
            off = (j * N_KV + g) * HEAD_DIM
            kn_ref[0, j, g] = jnp.concatenate([kn[:, off:off + HEAD_DIM], onehot], axis=1)
    cv_ref[0] = jnp.dot(h, wv_ref[...], preferred_element_type=f32)


def _in_proj(x, mod, w_in, tm):
    b, s, d = x.shape
    o = 0
    wq = w_in[:, o:o + D_ATTN]; o += D_ATTN
    wkc = w_in[:, o:o + 2 * D_KV]; o += 2 * D_KV
    wk_s = w_in[:, o:o + D_KV]; o += D_KV
    wv_s = w_in[:, o:o + D_KV]; o += D_KV
    wk_w = w_in[:, o:o + D_KV]; o += D_KV
    wv_w = w_in[:, o:o + D_KV]; o += D_KV
    wg = w_in[:, o:o + 3 * N_HEADS]; o += 3 * N_HEADS
    wcv = w_in[:, o:]
    d_conv2 = wcv.shape[1]
    wg = jnp.zeros((d, GATE_ROWS), f32).at[:, :3 * N_HEADS].set(wg)
    wt = jnp.concatenate([wq, wv_s, wv_w, wg], axis=1).T
    ws = [w.astype(bf16) for w in (wt, wkc, jnp.concatenate([wk_s, wk_w], axis=1), wcv)]
    full = lambda a: pl.BlockSpec(a.shape, lambda bi, i: (0, 0))
    return pl.pallas_call(
        _in_proj_kernel,
        grid=(b, s // tm),
        in_specs=[pl.BlockSpec((1, tm, d), lambda bi, i: (bi, i, 0)),
                  pl.BlockSpec((1, 6, d), lambda bi, i: (bi, 0, 0))] + [full(w) for w in ws],
        out_specs=[pl.BlockSpec((1, D_ATTN, tm), lambda bi, i: (bi, 0, i)),
                   pl.BlockSpec((1, 2, N_KV, V_ROWS, tm), lambda bi, i: (bi, 0, 0, 0, i)),
                   pl.BlockSpec((1, GATE_ROWS, tm), lambda bi, i: (bi, 0, i)),
                   pl.BlockSpec((1, 2, tm, D_KV), lambda bi, i: (bi, 0, i, 0)),
                   pl.BlockSpec((1, 2, N_KV, tm, LANES), lambda bi, i: (bi, 0, 0, i, 0)),
                   pl.BlockSpec((1, tm, d_conv2), lambda bi, i: (bi, i, 0))],
        out_shape=[jax.ShapeDtypeStruct((b, D_ATTN, s), bf16),
                   jax.ShapeDtypeStruct((b, 2, N_KV, V_ROWS, s), bf16),
                   jax.ShapeDtypeStruct((b, GATE_ROWS, s), f32),
                   jax.ShapeDtypeStruct((b, 2, s, D_KV), f32),
                   jax.ShapeDtypeStruct((b, 2, N_KV, s, LANES), bf16),
                   jax.ShapeDtypeStruct((b, s, d_conv2), f32)],
        compiler_params=_params(("parallel", "parallel")),
        name="in_proj",
    )(x, mod, *ws)


def _compress_kernel(x_ref, pe_ref, w1_ref, w2_ref, w2t_ref, o_ref, ot_ref):
    n_chunk = o_ref.shape[3]
    for j in range(2):
        for g in range(N_KV):
            cols = slice(g * HEAD_DIM, (g + 1) * HEAD_DIM)
            a = jnp.zeros((n_chunk, CMP_HID), f32)
            bm = jnp.zeros((n_chunk, CMP_HID), f32)
            for l in range(CMP_STRIDE):
                xl = x_ref[0, j, pl.ds(l, n_chunk, stride=CMP_STRIDE), :][:, cols]
                a = a + jnp.dot((xl + pe_ref[j, l:l + 1, :]).astype(bf16), w1_ref[j, l],
                                preferred_element_type=f32)
                bm = bm + jnp.dot((xl + pe_ref[j, CMP_STRIDE + l:CMP_STRIDE + l + 1, :]).astype(bf16),
                                  w1_ref[j, CMP_STRIDE + l], preferred_element_type=f32)
            hid = a + pltpu.roll(bm, n_chunk - 1, 0)
            act = 0.5 * hid * (1.0 + jnp.tanh(0.7978845608028654 * (hid + 0.044715 * (hid * hid * hid))))
            act = act.astype(bf16)
            o_ref[0, j, g] = jnp.dot(act, w2_ref[j], preferred_element_type=f32).astype(bf16)
            ot_ref[0, j, g] = _dot_nt(w2t_ref[j], act).astype(bf16)


def _compress(kvc, pe, w1, w2):
    assert CMP_LEN == 2 * CMP_STRIDE
    b, _, s, width = kvc.shape
    n_chunk = s // CMP_STRIDE
    w1b = w1.reshape(2, CMP_LEN, HEAD_DIM, CMP_HID).astype(bf16)
    w2b = w2.astype(bf16)
    w2t = w2b.transpose(0, 2, 1)
    full = lambda a: pl.BlockSpec(a.shape, lambda bi: (0,) * a.ndim)
    return pl.pallas_call(
        _compress_kernel,
        grid=(b,),
        in_specs=[pl.BlockSpec((1, 2, s, width), lambda bi: (bi, 0, 0, 0)),
                  full(pe), full(w1b), full(w2b), full(w2t)],
        out_specs=[pl.BlockSpec((1, 2, N_KV, n_chunk, HEAD_DIM), lambda bi: (bi, 0, 0, 0, 0)),
                   pl.BlockSpec((1, 2, N_KV, HEAD_DIM, n_chunk), lambda bi: (bi, 0, 0, 0, 0))],
        out_shape=[jax.ShapeDtypeStruct((b, 2, N_KV, n_chunk, HEAD_DIM), bf16),
                   jax.ShapeDtypeStruct((b, 2, N_KV, HEAD_DIM, n_chunk), bf16)],
        compiler_params=_params(("parallel",)),
        name="compress",
    )(kvc, pe, w1b, w2b, w2t)


def _attn_kernel(q_ref, g_ref, kc_ref, vc_ref, kn_ref, vt_ref, o_ref, psum_ref, selbias_ref, sa_ref, sb_ref,
                 sd_ref, *, seq, tk, top_n):
    i = pl.program_id(2)
    s0 = i * Q_BLOCK
    n_cmp_rows = kc_ref.shape[3]
    n_sel = seq // SEL_LEN

    q4 = q_ref[0]
    qt = jnp.concatenate([q4[p * HEAD_DIM:(p + 1) * HEAD_DIM, :] for p in range(HPG)], axis=1)
    t_row = s0 + lax.broadcasted_iota(i32, (1, Q_BLOCK), 1)

    s_c = jnp.dot(kc_ref[0, 0, 0], qt, preferred_element_type=f32)
    cmp_end = lax.broadcasted_iota(i32, (n_cmp_rows, 1), 0) * CMP_STRIDE + (CMP_LEN - 1)
    bias_c = jnp.where(cmp_end <= t_row, 0.0, NEG)
    any_c = t_row >= CMP_LEN - 1
    p_sum = jnp.zeros((n_cmp_rows, Q_BLOCK), f32)
    pcs = []
    for p in range(HPG):
        sp = s_c[:, p * Q_BLOCK:(p + 1) * Q_BLOCK] + bias_c
        e = jnp.exp2(sp - jnp.max(sp, axis=0, keepdims=True))
        pn = e * jnp.where(any_c, 1.0 / jnp.sum(e, axis=0, keepdims=True), 0.0)
        p_sum = p_sum + pn
        pcs.append(pn.astype(bf16))
    o_c = jnp.dot(vc_ref[0, 0, 0], jnp.concatenate(pcs, axis=1), preferred_element_type=f32)

    cols = HPG * Q_BLOCK
    pad_rows = LANES - HEAD_DIM
    q_pad = jnp.concatenate([qt, jnp.zeros((pad_rows, cols), bf16)], axis=0)
    span = WINDOW + Q_BLOCK
    w0 = pl.multiple_of(jnp.maximum(s0 - WINDOW, 0), Q_BLOCK)
    s_w = jnp.dot(kn_ref[0, 1, 0, pl.ds(w0, span), :], q_pad, preferred_element_type=f32)
    wpos = w0 + lax.broadcasted_iota(i32, (span, 1), 0)
    bias_w = jnp.where((wpos <= t_row) & (wpos > t_row - WINDOW), 0.0, NEG)
    pws = []
    for p in range(HPG):
        sp = s_w[:, p * Q_BLOCK:(p + 1) * Q_BLOCK] + bias_w
        pws.append(jnp.exp2(sp - jnp.max(sp, axis=0, keepdims=True)).astype(bf16))
    acc_w = jnp.dot(vt_ref[0, 1, 0, :, pl.ds(w0, span)], jnp.concatenate(pws, axis=1),
                    preferred_element_type=f32)
    o_w = acc_w[0:HEAD_DIM] * (1.0 / acc_w[HEAD_DIM:HEAD_DIM + 1])
    d0 = pl.multiple_of(s0, Q_BLOCK)
    s_d = jnp.dot(kn_ref[0, 0, 0, pl.ds(d0, Q_BLOCK), :], q_pad, preferred_element_type=f32)

    blk = lax.broadcasted_iota(i32, (n_sel, Q_BLOCK), 0)
    for lt in range(Q_BLOCK // LANES):
        psum_ref[lt] = p_sum[:, lt * LANES:(lt + 1) * LANES]

    def every(first):
        return jnp.concatenate([psum_ref[lt, pl.ds(first, n_sel, stride=CMP_PER_SEL), :]
                                for lt in range(Q_BLOCK // LANES)], axis=1)

    imp = every(0)
    for r in range(1, CMP_PER_SEL):
        imp = imp + every(r)
    for back in range(1, CMP_BACK + 1):
        imp = imp + jnp.where(blk >= 1, pltpu.roll(every(CMP_PER_SEL - back), 1, 0), 0.0)

    cur = t_row >> 6
    first_blk = s0 >> 6
    forced = (blk == 0) | (blk == cur) | (blk == cur - 1)
    state = (jnp.where(forced | (blk > cur), -jnp.inf, imp), jnp.where(forced, 1.0, 0.0))

    def pick_round(state, room):
        vals, sel = state
        mx = jnp.max(vals, axis=0, keepdims=True)
        first = jnp.min(jnp.where(vals == mx, blk, n_sel), axis=0, keepdims=True)
        pick = blk == first
        return jnp.where(pick, -jnp.inf, vals), jnp.where(pick & (mx > -jnp.inf) & room, 1.0, sel)

    for _ in range(top_n - MAX_FORCED):
        state = pick_round(state, True)

    def short_of_forced(state):
        n_forced = jnp.sum(jnp.where(forced, 1.0, 0.0), axis=0, keepdims=True)
        for extra in range(MAX_FORCED - 1):
            state = pick_round(state, n_forced + extra < MAX_FORCED)
        return state

    _, sel = lax.cond(first_blk < MAX_FORCED - 1, short_of_forced, lambda st: st, state)
    full_at = n_sel + TILE_BLOCKS
    selbias_ref[0:n_sel] = jnp.where((sel > 0.5) & (blk < first_blk), 0.0, NEG)
    selbias_ref[n_sel:full_at] = jnp.full((TILE_BLOCKS, Q_BLOCK), NEG, f32)
    selbias_ref[full_at:full_at + n_sel] = jnp.where(sel > 0.5, 0.0, NEG)
    last_tile = seq // tk - 1

    def scores(kt, s_ref):
        k0 = pl.multiple_of(jnp.minimum(kt, last_tile) * tk, tk)
        sb = selbias_ref[pl.ds(pl.multiple_of(kt * TILE_BLOCKS, TILE_BLOCKS), TILE_BLOCKS), :]
        rows = jnp.concatenate([jnp.concatenate([sb] * HPG, axis=1),
                                jnp.zeros((pad_rows - TILE_BLOCKS, cols), f32)], axis=0).astype(bf16)
        s_ref[...] = jnp.dot(kn_ref[0, 0, 0, pl.ds(k0, tk), :], jnp.concatenate([qt, rows], axis=0),
                             preferred_element_type=f32)

    def fold(scores_of_head, v_t, carry):
        m_i, acc = carry
        es, ms, alphas = [], [], []
        for p in range(HPG):
            c = slice(p * Q_BLOCK, (p + 1) * Q_BLOCK)
            sp = scores_of_head(c)
            m_new = jnp.maximum(m_i[:, c], jnp.max(sp, axis=0, keepdims=True))
            es.append(jnp.exp2(sp - m_new).astype(bf16))
            alphas.append(jnp.exp2(m_i[:, c] - m_new))
            ms.append(m_new)
        pv = jnp.dot(v_t, jnp.concatenate(es, axis=1), preferred_element_type=f32)
        return jnp.concatenate(ms, axis=1), jnp.concatenate(alphas, axis=1) * acc + pv

    def sel_tile(kt, s_ref, carry):
        k0 = pl.multiple_of(jnp.minimum(kt, last_tile) * tk, tk)
        return fold(lambda c: s_ref[:, c], vt_ref[0, 0, 0, :, pl.ds(k0, tk)], carry)

    def sel_pair(i, carry):
        scores(2 * i + 1, sb_ref)
        carry = sel_tile(2 * i, sa_ref, carry)
        scores(2 * i + 2, sa_ref)
        return sel_tile(2 * i + 1, sb_ref, carry)

    n_sweep = (s0 + tk - 1) // tk
    n_pairs = n_sweep // 2
    init = (jnp.full((1, cols), NEG, f32), jnp.zeros((V_ROWS, cols), f32))
    scores(0, sa_ref)
    def sel_pairs(j, carry):
        for u in range(PAIR_UNROLL):
            carry = sel_pair(PAIR_UNROLL * j + u, carry)
        return carry

    group = selbias_ref[pl.ds(full_at + pl.multiple_of((first_blk >> 3) << 3, SUBLANES), SUBLANES), :]
    own = group[0:PATCH_BLOCKS]
    for at in range(PATCH_BLOCKS, SUBLANES, PATCH_BLOCKS):
        own = jnp.where((first_blk & (SUBLANES - 1)) == at, group[at:at + PATCH_BLOCKS], own)
    bias_d = jnp.concatenate([jnp.broadcast_to(own[j:j + 1], (SEL_LEN, Q_BLOCK)) for j in range(PATCH_BLOCKS)],
                             axis=0)
    bias_d = jnp.where(s0 + lax.broadcasted_iota(i32, (Q_BLOCK, 1), 0) <= t_row, bias_d, NEG)
    for p in range(HPG):
        c = slice(p * Q_BLOCK, (p + 1) * Q_BLOCK)
        sd_ref[:, c] = s_d[:, c] + bias_d

    carry = lax.fori_loop(0, n_pairs // PAIR_UNROLL, sel_pairs, init)
    carry = lax.fori_loop(n_pairs // PAIR_UNROLL * PAIR_UNROLL, n_pairs, sel_pair, carry)
    carry = lax.cond(n_sweep % 2 == 1, lambda c: sel_tile(n_sweep - 1, sa_ref, c), lambda c: c, carry)
    _, acc_s = fold(lambda c: sd_ref[:, c], vt_ref[0, 0, 0, :, pl.ds(d0, Q_BLOCK)], carry)
    o_s = acc_s[0:HEAD_DIM] * (1.0 / acc_s[HEAD_DIM:HEAD_DIM + 1])

    gate = _sigmoid(g_ref[0, 0])
    outs = []
    for p in range(HPG):
        c = slice(p * Q_BLOCK, (p + 1) * Q_BLOCK)
        outs.append(gate[3 * p:3 * p + 1] * o_c[:, c] + gate[3 * p + 1:3 * p + 2] * o_s[:, c]
                    + gate[3 * p + 2:3 * p + 3] * o_w[:, c])
    o_ref[0] = jnp.concatenate(outs, axis=0).T.astype(bf16)


GATE_GROUP_ROWS = 16


def _attention(qt, gates, cmp_n, cmp_t, kn, vt):
    b, _, s = qt.shape
    n_chunk = cmp_n.shape[3]
    n_sel = s // SEL_LEN
    top_n = min(SEL_TOPN, n_sel)
    tk = KEY_TILE
    assert s % tk == 0 and n_chunk == n_sel * CMP_PER_SEL
    gw = HPG * HEAD_DIM
    return pl.pallas_call(
        functools.partial(_attn_kernel, seq=s, tk=tk, top_n=top_n),
        grid=(b, N_KV, s // Q_BLOCK),
        in_specs=[pl.BlockSpec((1, gw, Q_BLOCK), lambda bi, g, i: (bi, g, i)),
                  pl.BlockSpec((1, 1, GATE_GROUP_ROWS, Q_BLOCK), lambda bi, g, i: (bi, g, 0, i)),
                  pl.BlockSpec((1, 1, 1, n_chunk, HEAD_DIM), lambda bi, g, i: (bi, 0, g, 0, 0)),
                  pl.BlockSpec((1, 1, 1, HEAD_DIM, n_chunk), lambda bi, g, i: (bi, 1, g, 0, 0)),
                  pl.BlockSpec((1, 2, 1, s, LANES), lambda bi, g, i: (bi, 0, g, 0, 0)),
                  pl.BlockSpec((1, 2, 1, V_ROWS, s), lambda bi, g, i: (bi, 0, g, 0, 0))],
        out_specs=pl.BlockSpec((1, Q_BLOCK, gw), lambda bi, g, i: (bi, i, g)),
        out_shape=jax.ShapeDtypeStruct((b, s, D_ATTN), bf16),
        scratch_shapes=[pltpu.VMEM((Q_BLOCK // LANES, n_chunk, LANES), f32),
                        pltpu.VMEM((2 * n_sel + TILE_BLOCKS, Q_BLOCK), f32),
                        pltpu.VMEM((tk, HPG * Q_BLOCK), f32), pltpu.VMEM((tk, HPG * Q_BLOCK), f32),
                        pltpu.VMEM((Q_BLOCK, HPG * Q_BLOCK), f32)],
        compiler_params=_params(("parallel", "parallel", "arbitrary")),
        name="attention",
    )(qt, gates, cmp_n, cmp_t, kn, vt)


CONV_HALO = 32
CONV_ROWS = 32


def _conv_kernel(cur_ref, prev_ref, w_ref, b_ref, g_ref, bb_ref, o_ref, glu_ref):
    i = pl.program_id(1)
    ts = cur_ref.shape[1]
    dc = o_ref.shape[2]
    cur = cur_ref[0]
    prev = prev_ref[0]
    glu_prev = prev[:, :dc] * _sigmoid(prev[:, dc:])
    glu_ref[0:CONV_HALO] = jnp.where(i == 0, 0.0, glu_prev)
    glu_ref[CONV_HALO:CONV_HALO + ts] = cur[:, :dc] * _sigmoid(cur[:, dc:])
    lead = CONV_HALO - (CONV_WIDTH - 1)

    def chunk(r, _):
        r0 = pl.multiple_of(r * CONV_ROWS, CONV_ROWS)
        win = glu_ref[pl.ds(r0, CONV_ROWS + CONV_HALO), :]
        span = CONV_ROWS + CONV_HALO
        acc = jnp.zeros((CONV_ROWS, dc), f32)
        for ph in range(SUBLANES):
            turned = win if ph == 0 else pltpu.roll(win, span - ph, 0)
            for j in range(CONV_WIDTH):
                if (lead + j) % SUBLANES == ph:
                    at = lead + j - ph
                    acc = acc + w_ref[j:j + 1, :] * turned[at:at + CONV_ROWS]
        y = _layer_norm(acc + b_ref[...], g_ref[...], bb_ref[...])
        o_ref[0, pl.ds(r0, CONV_ROWS), :] = _silu(y).astype(bf16)
        return 0

    lax.fori_loop(0, ts // CONV_ROWS, chunk, 0)


def _conv(conv_in, w_dw, b_dw, ln_g, ln_b, ts):
    b, s, dc2 = conv_in.shape
    dc = dc2 // 2
    per = ts // CONV_HALO
    row = lambda a: a.reshape(1, dc)
    return pl.pallas_call(
        _conv_kernel,
        grid=(b, s // ts),
        in_specs=[pl.BlockSpec((1, ts, dc2), lambda bi, i: (bi, i, 0)),
                  pl.BlockSpec((1, CONV_HALO, dc2), lambda bi, i: (bi, jnp.maximum(i * per - 1, 0), 0)),
                  pl.BlockSpec((CONV_WIDTH, dc), lambda bi, i: (0, 0)),
                  pl.BlockSpec((1, dc), lambda bi, i: (0, 0)),
                  pl.BlockSpec((1, dc), lambda bi, i: (0, 0)),
                  pl.BlockSpec((1, dc), lambda bi, i: (0, 0))],
        out_specs=pl.BlockSpec((1, ts, dc), lambda bi, i: (bi, i, 0)),
        out_shape=jax.ShapeDtypeStruct((b, s, dc), bf16),
        scratch_shapes=[pltpu.VMEM((CONV_HALO + ts, dc), f32)],
        compiler_params=_params(("parallel", "parallel")),
        name="conv",
    )(conv_in, conv_in, w_dw.reshape(CONV_WIDTH, dc), row(b_dw), row(ln_g), row(ln_b))


ROUTE_HALF = 128


def _mix_route_kernel(a_ref, cv_ref, x_ref, mod_ref, wo_ref, lg_ref, lb_ref, wrh_ref, wrl_ref, rb_ref,
                      x1_ref, h2p_ref, idx_ref, wt_ref, cnt_ref):
    step = pl.program_id(0)

    @pl.when(step == 0)
    def _():
        cnt_ref[...] = jnp.zeros_like(cnt_ref)

    m = mod_ref[0]
    da = a_ref.shape[1]
    counts = jnp.zeros(cnt_ref.shape, f32)
    for first in range(0, x_ref.shape[0], ROUTE_HALF):
        rows = slice(first, first + ROUTE_HALF)
        mix = (jnp.dot(a_ref[rows, :], wo_ref[0:da, :], preferred_element_type=f32)
               + jnp.dot(cv_ref[rows, :], wo_ref[da:, :], preferred_element_type=f32))
        x1 = _layer_norm(DEEPNORM_ALPHA * x_ref[rows, :] + m[2:3] * mix, lg_ref[...], lb_ref[...])
        x1_ref[rows, :] = x1
        h2 = x1 * (1.0 + m[4:5]) + m[3:4]
        _store_packed_rows(h2p_ref, h2, first)
        h_hi = h2.astype(bf16)
        h_lo = (h2 - h_hi.astype(f32)).astype(bf16)
        logits = _dot_nt(wrh_ref[...], h_hi) + (_dot_nt(wrh_ref[...], h_lo) + _dot_nt(wrl_ref[...], h_hi))
        idx, wt, cnt = _route(_sigmoid(logits), rb_ref[...])
        idx_ref[:, rows] = idx
        wt_ref[:, rows] = wt
        counts = counts + cnt
    cnt_ref[...] += counts


def _route(score, bias):
    tm = score.shape[1]
    sel = score + bias
    eid = lax.broadcasted_iota(i32, (N_EXPERTS, tm), 0)
    per_group = N_EXPERTS // N_GROUPS
    gs = []
    for g in range(N_GROUPS):
        rows = slice(g * per_group, (g + 1) * per_group)
        v = sel[rows]
        e = g * per_group + lax.broadcasted_iota(i32, (per_group, tm), 0)
        m1 = jnp.max(v, axis=0, keepdims=True)
        i1 = jnp.min(jnp.where(v == m1, e, N_EXPERTS), axis=0, keepdims=True)
        m2 = jnp.max(jnp.where(e == i1, -jnp.inf, v), axis=0, keepdims=True)
        gs.append(m1 + m2)
    cands = []
    for g in range(N_GROUPS):
        rank = jnp.zeros((1, tm), i32)
        for o in range(N_GROUPS):
            if o == g:
                continue
            beats = (gs[o] > gs[g]) | (gs[o] == gs[g]) if o < g else gs[o] > gs[g]
            rank = rank + beats.astype(i32)
        drop = jnp.where(rank < TOPK_GROUPS, 0.0, -jnp.inf)
        cands.append(sel[g * per_group:(g + 1) * per_group] + drop)
    cand = jnp.concatenate(cands, axis=0)
    row_o = lax.broadcasted_iota(i32, (TOP_K, tm), 0)
    idx_out = jnp.zeros((TOP_K, tm), i32)
    wt_out = jnp.zeros((TOP_K, tm), f32)
    picked = jnp.zeros((N_EXPERTS, tm), f32)
    w_sum = jnp.zeros((1, tm), f32)
    for k in range(TOP_K):
        mx = jnp.max(cand, axis=0, keepdims=True)
        ik = jnp.min(jnp.where(cand == mx, eid, N_EXPERTS), axis=0, keepdims=True)
        pick = eid == ik
        wk = jnp.sum(jnp.where(pick, score, 0.0), axis=0, keepdims=True)
        cand = jnp.where(pick, -jnp.inf, cand)
        picked = jnp.where(pick, 1.0, picked)
        idx_out = jnp.where(row_o == k, ik, idx_out)
        wt_out = jnp.where(row_o == k, wk, wt_out)
        w_sum = w_sum + wk
    return idx_out, wt_out / w_sum * ROUTED_SCALE, jnp.sum(picked, axis=1, keepdims=True)


def _mix_route(attn, conv, x2, mod, w_out, ln_g, ln_b, w_router, router_bias, s, tm):
    t, d = x2.shape
    per = s // tm
    da = attn.shape[1]
    pack_rows = d // 2 // LANES
    row = lambda a: a.reshape(1, -1)
    tile = lambda w: pl.BlockSpec((tm, w), lambda i: (i, 0))
    full = lambda a: pl.BlockSpec(a.shape, lambda i: (0,) * a.ndim)
    assert tm % ROUTE_HALF == 0
    wr_hi = w_router.T.astype(bf16)
    wr_lo = (w_router.T - wr_hi.astype(f32)).astype(bf16)
    args = (attn, conv, x2, mod, w_out.astype(bf16), row(ln_g), row(ln_b), wr_hi, wr_lo,
            router_bias.reshape(N_EXPERTS, 1))
    per_token = lambda rows: pl.BlockSpec((rows, tm), lambda i: (0, i))
    return pl.pallas_call(
        _mix_route_kernel,
        grid=(t // tm,),
        in_specs=[tile(da), tile(conv.shape[1]), tile(d),
                  pl.BlockSpec((1, 6, d), lambda i: (i // per, 0, 0))] + [full(a) for a in args[4:]],
        out_specs=[tile(d), pl.BlockSpec((tm * pack_rows, LANES), lambda i: (i, 0)),
                   per_token(TOP_K), per_token(TOP_K), pl.BlockSpec((N_EXPERTS, 1), lambda i: (0, 0))],
        out_shape=[jax.ShapeDtypeStruct((t, d), f32),
                   jax.ShapeDtypeStruct((t * pack_rows, LANES), u32),
                   jax.ShapeDtypeStruct((TOP_K, t), i32), jax.ShapeDtypeStruct((TOP_K, t), f32),
                   jax.ShapeDtypeStruct((N_EXPERTS, 1), f32)],
        compiler_params=_params(("arbitrary",)),
        name="mix_route",
    )(*args)


def _positions_kernel(idx_ref, start_ref, o_ref, run_ref):
    step = pl.program_id(0)
    tm = idx_ref.shape[1]

    @pl.when(step == 0)
    def _():
        run_ref[...] = jnp.zeros_like(run_ref)

    idx = idx_ref[...]
    eid = lax.broadcasted_iota(i32, (N_EXPERTS, tm), 0)
    onehot = jnp.zeros((N_EXPERTS, tm), f32)
    for k in range(TOP_K):
        onehot = jnp.where(eid == idx[k:k + 1], 1.0, onehot)
    r = lax.broadcasted_iota(i32, (tm, tm), 0)
    c = lax.broadcasted_iota(i32, (tm, tm), 1)
    earlier = jnp.where(r < c, 1.0, 0.0).astype(bf16)
    prior = jnp.dot(onehot.astype(bf16), earlier, preferred_element_type=f32)
    pos = prior + run_ref[...] + start_ref[...]
    row_o = lax.broadcasted_iota(i32, (TOP_K, tm), 0)
    out = jnp.zeros((TOP_K, tm), i32)
    for k in range(TOP_K):
        dk = jnp.sum(jnp.where(eid == idx[k:k + 1], pos, 0.0), axis=0, keepdims=True)
        out = jnp.where(row_o == k, dk.astype(i32), out)
    o_ref[...] = out
    run_ref[...] += jnp.sum(onehot, axis=1, keepdims=True)


def _positions(idx, seg_start, tm):
    t = idx.shape[1]
    return pl.pallas_call(
        _positions_kernel,
        grid=(t // tm,),
        in_specs=[pl.BlockSpec((TOP_K, tm), lambda i: (0, i)),
                  pl.BlockSpec((N_EXPERTS, 1), lambda i: (0, 0))],
        out_specs=pl.BlockSpec((TOP_K, tm), lambda i: (0, i)),
        out_shape=jax.ShapeDtypeStruct((TOP_K, t), i32),
        scratch_shapes=[pltpu.VMEM((N_EXPERTS, 1), f32)],
        compiler_params=_params(("arbitrary",)),
        name="positions",
    )(idx, seg_start)


def _dispatch_kernel(dest_hbm, h_ref, o_hbm, dest_smem, sem_i, sem, *, r):
    step = pl.program_id(0)
    n = dest_smem.shape[0]
    load = pltpu.make_async_copy(dest_hbm.at[pl.ds(pl.multiple_of(step * n, n), n)], dest_smem, sem_i)
    load.start()
    load.wait()

    def issue(tok, _):
        src = pl.multiple_of(tok * r, r)
        for k in range(TOP_K):
            dst = pl.multiple_of(dest_smem[tok * TOP_K + k] * r, r)
            pltpu.make_async_copy(h_ref.at[pl.ds(src, r)], o_hbm.at[pl.ds(dst, r)], sem).start(priority=k % 2)
        return 0

    lax.fori_loop(0, n // TOP_K, issue, 0)
    everything = o_hbm.at[pl.ds(0, n * r)]
    pltpu.make_async_copy(everything, everything, sem).wait()


def _dispatch(dest_flat, h2p, n_buf, r, tm):
    t = h2p.shape[0] // r
    return pl.pallas_call(
        functools.partial(_dispatch_kernel, r=r),
        grid=(t // tm,),
        in_specs=[pl.BlockSpec(memory_space=pl.ANY),
                  pl.BlockSpec((tm * r, LANES), lambda i: (i, 0))],
        out_specs=pl.BlockSpec(memory_space=pl.ANY),
        out_shape=jax.ShapeDtypeStruct((n_buf * r, LANES), u32),
        scratch_shapes=[pltpu.SMEM((tm * TOP_K,), i32), pltpu.SemaphoreType.DMA, pltpu.SemaphoreType.DMA],
        compiler_params=_params(("arbitrary",)),
        name="dispatch",
    )(dest_flat, h2p)


def _experts_kernel(be_ref, bv_ref, nu_ref, x_ref, wg_ref, wu_ref, wd_ref, o_ref, wg_s, wu_s, wd_s):
    j = pl.program_id(0)
    prev = be_ref[jnp.maximum(j - 1, 0)]
    used = j < nu_ref[0]
    d = wg_s.shape[0]

    @pl.when(used & ((j == 0) | (be_ref[j] != prev)))
    def _():
        wg_s[...] = wg_ref[0].astype(bf16)
        wu_s[...] = wu_ref[0].astype(bf16)
        wd_s[...] = wd_ref[0].astype(bf16)

    def rows_through(n):
        live = lax.broadcasted_iota(i32, (n, 1), 0) < bv_ref[j]
        x = jnp.where(live, _load_packed_rows(x_ref, n, d), 0.0).astype(bf16)
        hg = jnp.dot(x, wg_s[...], preferred_element_type=f32)
        hu = jnp.dot(x, wu_s[...], preferred_element_type=f32)
        hid = (_silu(hg) * hu).astype(bf16)
        _store_packed_rows(o_ref, jnp.dot(hid, wd_s[...], preferred_element_type=f32))

    half = ROW_BLOCK // 2

    @pl.when(used & (bv_ref[j] > half))
    def _():
        rows_through(ROW_BLOCK)

    @pl.when(used & (bv_ref[j] <= half))
    def _():
        rows_through(half)
        r = d // 2 // LANES
        o_ref[half * r:ROW_BLOCK * r, :] = jnp.zeros(((ROW_BLOCK - half) * r, LANES), u32)


def _experts(blk_e, blk_valid, n_used, xs, w_gate, w_up, w_down):
    d, f = w_gate.shape[1], w_gate.shape[2]
    r = d // 2 // LANES
    n_blk = xs.shape[0] // r // ROW_BLOCK
    rows = lambda j, be, bv, nu: (jnp.minimum(j, nu[0] - 1), 0)
    wsel = lambda j, be, bv, nu: (be[j], 0, 0)
    return pl.pallas_call(
        _experts_kernel,
        grid_spec=pltpu.PrefetchScalarGridSpec(
            num_scalar_prefetch=3,
            grid=(n_blk,),
            in_specs=[pl.BlockSpec((ROW_BLOCK * r, LANES), rows),
                      pl.BlockSpec((1, d, f), wsel),
                      pl.BlockSpec((1, d, f), wsel),
                      pl.BlockSpec((1, f, d), wsel)],
            out_specs=pl.BlockSpec((ROW_BLOCK * r, LANES), rows),
            scratch_shapes=[pltpu.VMEM((d, f), bf16), pltpu.VMEM((d, f), bf16), pltpu.VMEM((f, d), bf16)]),
        out_shape=jax.ShapeDtypeStruct(xs.shape, u32),
        compiler_params=_params(("arbitrary",)),
        name="experts",
    )(blk_e, blk_valid, n_used, xs, w_gate, w_up, w_down)


COMBINE_CHUNK = 64


def _combine_kernel(dest_hbm, ys_hbm, wt_ref, h_ref, x1_ref, mod_ref, wsg_ref, wsu_ref, wsd_ref,
                    lg_ref, lb_ref, o_ref, dest_a, dest_b, rows_a, rows_b, sem_d, sem_g):
    step = pl.program_id(0)
    n_steps = pl.num_programs(0)
    tm, d = x1_ref.shape
    r = d // 2 // LANES
    n = tm * TOP_K
    tables, rows = (dest_a, dest_b), (rows_a, rows_b)

    def table_copy(tile, slot):
        return pltpu.make_async_copy(dest_hbm.at[pl.ds(pl.multiple_of(tile * n, n), n)],
                                     tables[slot], sem_d.at[slot])

    def start_gathers(slot):
        def issue(tok, _):
            dst = pl.multiple_of(tok * r, r)
            for k in range(TOP_K):
                src = pl.multiple_of(tables[slot][tok * TOP_K + k] * r, r)
                pltpu.make_async_copy(ys_hbm.at[pl.ds(src, r)], rows[slot].at[k, pl.ds(dst, r)],
                                      sem_g.at[slot]).start(priority=k % 2)
            return 0

        lax.fori_loop(0, tm, issue, 0)

    @pl.when(step == 0)
    def _():
        table_copy(0, 0).start()
        table_copy(0, 0).wait()
        start_gathers(0)
        second = jnp.minimum(1, n_steps - 1)
        table_copy(second, 1).start()

        @pl.when(n_steps == 1)
        def _():
            table_copy(second, 1).wait()

    def tile_body(slot):
        other = 1 - slot

        @pl.when(step + 2 < n_steps)
        def _():
            table_copy(step + 2, slot).start()

        @pl.when(step + 1 < n_steps)
        def _():
            table_copy(step + 1, other).wait()

        h = _load_packed_rows(h_ref, tm, d).astype(bf16)
        hg = jnp.dot(h, wsg_ref[...], preferred_element_type=f32)
        hu = jnp.dot(h, wsu_ref[...], preferred_element_type=f32)
        shared = jnp.dot((_silu(hg) * hu).astype(bf16), wsd_ref[...], preferred_element_type=f32)
        pltpu.make_async_copy(rows[slot], rows[slot], sem_g.at[slot]).wait()
        m = mod_ref[0]
        for first in range(0, tm, COMBINE_CHUNK):
            for tok in range(first, first + COMBINE_CHUNK):
                for k in range(TOP_K):
                    src = pl.multiple_of(tables[other][tok * TOP_K + k] * r, r)
                    pltpu.make_async_copy(ys_hbm.at[pl.ds(src, r)], rows[other].at[k, pl.ds(tok * r, r)],
                                          sem_g.at[other]).start(priority=k % 2)
            sl = slice(first, first + COMBINE_CHUNK)
            wt = wt_ref[sl, :]
            lo = [jnp.zeros((COMBINE_CHUNK, LANES), f32) for _ in range(r)]
            hi = [jnp.zeros((COMBINE_CHUNK, LANES), f32) for _ in range(r)]
            for k in range(TOP_K):
                wk = wt[:, k:k + 1]
                for c in range(r):
                    pl_, ph_ = _unpack_words(rows[slot][k, pl.ds(first * r + c, COMBINE_CHUNK, stride=r), :])
                    lo[c] = lo[c] + wk * pl_
                    hi[c] = hi[c] + wk * ph_
            y = shared[sl] + jnp.concatenate(lo + hi, axis=1)
            o_ref[sl, :] = _layer_norm(DEEPNORM_ALPHA * x1_ref[sl, :] + m[5:6] * y, lg_ref[...], lb_ref[...])

        @pl.when(step + 1 == n_steps)
        def _():
            pltpu.make_async_copy(rows[other], rows[other], sem_g.at[other]).wait()

    for slot in range(2):
        pl.when(step % 2 == slot)(functools.partial(tile_body, slot))


def _combine(dest_flat, ys, wts, h2p, x1, mod, w_s_gate, w_s_up, w_s_down, ln_g, ln_b, s, tm):
    t, d = x1.shape
    per = s // tm
    r = d // 2 // LANES
    row = lambda a: a.reshape(1, -1)
    tile = lambda w: pl.BlockSpec((tm, w), lambda i: (i, 0))
    full = lambda a: pl.BlockSpec(a.shape, lambda i: (0,) * a.ndim)
    tail = (w_s_gate.astype(bf16), w_s_up.astype(bf16), w_s_down.astype(bf16), row(ln_g), row(ln_b))
    return pl.pallas_call(
        _combine_kernel,
        grid=(t // tm,),
        in_specs=[pl.BlockSpec(memory_space=pl.ANY), pl.BlockSpec(memory_space=pl.ANY),
                  tile(LANES), pl.BlockSpec((tm * r, LANES), lambda i: (i, 0)), tile(d),
                  pl.BlockSpec((1, 6, d), lambda i: (i // per, 0, 0))] + [full(a) for a in tail],
        out_specs=tile(d),
        out_shape=jax.ShapeDtypeStruct((t, d), f32),
        scratch_shapes=[pltpu.SMEM((tm * TOP_K,), i32), pltpu.SMEM((tm * TOP_K,), i32),
                        pltpu.VMEM((TOP_K, tm * r, LANES), u32), pltpu.VMEM((TOP_K, tm * r, LANES), u32),
                        pltpu.SemaphoreType.DMA((2,)), pltpu.SemaphoreType.DMA((2,))],
        compiler_params=_params(("arbitrary",)),
        name="combine",
    )(dest_flat, ys, wts, h2p, x1, mod, *tail)


def _layer(x, mod, w_in, pe, w_cmp1, w_cmp2, w_dw, b_dw, conv_ln_g, conv_ln_b, w_out, ln1_g, ln1_b,
           w_router, router_bias, w_e_gate, w_e_up, w_e_down, w_s_gate, w_s_up, w_s_down, ln2_g, ln2_b):
    b, s, d = x.shape
    t = b * s
    tm = min(512, s)
    qt, vt, gt, kvc, kn, conv_in = _in_proj(x, mod, w_in, tm)
    cmp_n, cmp_t = _compress(kvc, pe, w_cmp1, w_cmp2)
    gates = gt[:, :3 * N_HEADS].reshape(b, N_KV, 3 * HPG, s)
    gates = jnp.pad(gates, ((0, 0), (0, 0), (0, GATE_GROUP_ROWS - 3 * HPG), (0, 0)))
    attn = _attention(qt, gates, cmp_n, cmp_t, kn, vt)
    conv = _conv(conv_in, w_dw, b_dw, conv_ln_g, conv_ln_b, tm)

    tr = min(256, s)
    x1, h2p, idx, wts, counts = _mix_route(attn.reshape(t, -1), conv.reshape(t, -1), x.reshape(t, d), mod,
                                           w_out, ln1_g, ln1_b, w_router, router_bias, s, tr)
    counts = counts[:, 0].astype(i32)
    padded = (counts + ROW_BLOCK - 1) // ROW_BLOCK * ROW_BLOCK
    seg_end = jnp.cumsum(padded)
    seg_start = seg_end - padded
    n_blk = -(-(t * TOP_K + N_EXPERTS * (ROW_BLOCK - 1)) // ROW_BLOCK)
    blk_row0 = jnp.arange(n_blk, dtype=i32) * ROW_BLOCK
    owns = (blk_row0[:, None] >= seg_start[None, :]) & (blk_row0[:, None] < seg_end[None, :])
    blk_e = jnp.where(blk_row0 < seg_end[-1], jnp.argmax(owns, axis=1), N_EXPERTS - 1).astype(i32)
    live_end = jnp.sum(jnp.where(owns, (seg_start + counts)[None, :], 0), axis=1)
    blk_valid = jnp.clip(live_end - blk_row0, 0, ROW_BLOCK).astype(i32)
    n_used = (seg_end[-1:] // ROW_BLOCK).astype(i32)

    dest = _positions(idx, seg_start.astype(f32).reshape(N_EXPERTS, 1), tr)
    dest_flat = dest.T.reshape(-1)
    wts = jnp.zeros((t, LANES), f32).at[:, :TOP_K].set(wts.T)
    xs = _dispatch(dest_flat, h2p, n_blk * ROW_BLOCK, d // 2 // LANES, min(1024, s))
    ys = _experts(blk_e, blk_valid, n_used, xs, w_e_gate, w_e_up, w_e_down)
    out = _combine(dest_flat, ys, wts, h2p, x1, mod, w_s_gate, w_s_up, w_s_down, ln2_g, ln2_b, s, min(256, s))
    return out.reshape(b, s, d)


def kernel(x, c, w_ada, b_ada, w_in, pe_k, pe_v, w_cmp_k1, w_cmp_k2, w_cmp_v1, w_cmp_v2, w_dw, b_dw,
           conv_ln_g, conv_ln_b, w_out, ln1_g, ln1_b, w_router, router_bias, w_e_gate, w_e_up, w_e_down,
           w_s_gate, w_s_up, w_s_down, ln2_g, ln2_b):
    assert w_ada.shape[0] == DEPTH
    layer = lambda a: a.reshape(a.shape[1:])
    mod = _ada(c, layer(w_ada), layer(b_ada))
    return _layer(x, mod, layer(w_in), jnp.concatenate([pe_k, pe_v]),
                  jnp.concatenate([w_cmp_k1, w_cmp_v1]), jnp.concatenate([w_cmp_k2, w_cmp_v2]),
                  *[layer(a) for a in (w_dw, b_dw, conv_ln_g, conv_ln_b, w_out, ln1_g, ln1_b, w_router,
                                       router_bias, w_e_gate, w_e_up, w_e_down, w_s_gate, w_s_up, w_s_down,
                                       ln2_g, ln2_b)])
```

```python
import functools

import jax
import jax.numpy as jnp
from jax import lax
from jax.experimental import pallas as pl
from jax.experimental.pallas import tpu as pltpu

N_HEADS = 8
N_KV = 2
HPG = N_HEADS // N_KV
HEAD_DIM = 64
D_ATTN = N_HEADS * HEAD_DIM
D_KV = N_KV * HEAD_DIM
CONV_WIDTH = 31
CMP_LEN = 32
CMP_STRIDE = 16
CMP_HID = 256
SEL_LEN = 64
SEL_TOPN = 16
WINDOW = 512
Q_BLOCK = 256
N_EXPERTS = 256
TOP_K = 8
N_GROUPS = 8
TOPK_GROUPS = 4
ROUTED_SCALE = 2.5
LN_EPS = 1e-5
DEPTH = 1
DEEPNORM_ALPHA = (2 * DEPTH) ** 0.25

LANES = 128
SUBLANES = 8
ROW_BLOCK = 512
NEG = -1e30
HIGHEST = lax.Precision.HIGHEST
VMEM_LIMIT = 48 * 1024 * 1024

f32 = jnp.float32
bf16 = jnp.bfloat16
i32 = jnp.int32
u32 = jnp.uint32


def _params(sem, vmem=VMEM_LIMIT):
    return pltpu.CompilerParams(dimension_semantics=sem, vmem_limit_bytes=vmem)


def _sigmoid(v):
    return 1.0 / (1.0 + jnp.exp(-v))


def _silu(v):
    return v * _sigmoid(v)


def _layer_norm(v, g, b):
    mu = jnp.mean(v, axis=-1, keepdims=True)
    var = jnp.mean(jnp.square(v - mu), axis=-1, keepdims=True)
    return (v - mu) * lax.rsqrt(var + LN_EPS) * g + b


def _dot_nt(a, b):
    return lax.dot_general(a, b, (((1,), (1,)), ((), ())), preferred_element_type=f32)


def _store_packed_rows(ref, v, first=0):
    n, d = v.shape
    half = d // 2
    bits = lax.bitcast_convert_type(v.astype(bf16).astype(f32), u32)
    words = bits[:, half:] | (bits[:, :half] >> 16)
    r = half // LANES
    for c in range(r):
        ref[pl.ds(first * r + c, n, stride=r), :] = words[:, c * LANES:(c + 1) * LANES]


def _unpack_words(w):
    return (lax.bitcast_convert_type(w << 16, f32),
            lax.bitcast_convert_type(w & jnp.uint32(0xFFFF0000), f32))


def _load_packed_rows(ref, n, d):
    r = d // 2 // LANES
    parts = [_unpack_words(ref[pl.ds(c, n, stride=r), :]) for c in range(r)]
    return jnp.concatenate([p[0] for p in parts] + [p[1] for p in parts], axis=1)


def _ada_kernel(c_ref, w_ref, b_ref, o_ref):
    c = c_ref[...]
    o_ref[...] = jnp.dot(_silu(c), w_ref[...], precision=HIGHEST,
                         preferred_element_type=f32) + b_ref[...]


def _ada(c, w_ada, b_ada):
    b, d = c.shape
    n = w_ada.shape[1]
    rows = -(-b // SUBLANES) * SUBLANES
    c_pad = jnp.zeros((rows, d), f32).at[:b].set(c)
    tn = d
    out = pl.pallas_call(
        _ada_kernel,
        grid=(n // tn,),
        in_specs=[pl.BlockSpec((rows, d), lambda j: (0, 0)),
                  pl.BlockSpec((d, tn), lambda j: (0, j)),
                  pl.BlockSpec((1, tn), lambda j: (0, j))],
        out_specs=pl.BlockSpec((rows, tn), lambda j: (0, j)),
        out_shape=jax.ShapeDtypeStruct((rows, n), f32),
        compiler_params=_params(("arbitrary",)),
        name="ada",
    )(c_pad, w_ada, b_ada.reshape(1, n))
    return out[:b].reshape(b, 6, d)


GATE_ROWS = 32
KEY_TILE = 512
TILE_BLOCKS = KEY_TILE // SEL_LEN
SEL_SHIFT = SEL_LEN.bit_length() - 1
assert 1 << SEL_SHIFT == SEL_LEN and TILE_BLOCKS & (TILE_BLOCKS - 1) == 0
PATCH_BLOCKS = Q_BLOCK // SEL_LEN
PAIR_UNROLL = 2
CMP_PER_SEL = SEL_LEN // CMP_STRIDE
CMP_BACK = CMP_LEN // CMP_STRIDE - 1
assert SEL_LEN % CMP_STRIDE == 0 and CMP_LEN % CMP_STRIDE == 0 and CMP_BACK < CMP_PER_SEL
V_ROWS = HEAD_DIM + 16
Q_SCALE = HEAD_DIM ** -0.5 * 1.4426950408889634


def _in_proj_kernel(x_ref, mod_ref, wt_ref, wc_ref, wk_ref, wv_ref,
                    qt_ref, vt_ref, gt_ref, kvc_ref, kn_ref, cv_ref):
    m = mod_ref[0]
    h = (x_ref[0] * (1.0 + m[1:2]) + m[0:1]).astype(bf16)
    res_t = _dot_nt(wt_ref[...], h)
    qt_ref[0] = (res_t[0:D_ATTN] * Q_SCALE).astype(bf16)
    ones = jnp.ones((V_ROWS - HEAD_DIM, res_t.shape[1]), bf16)
    for j in range(2):
        for g in range(N_KV):
            off = D_ATTN + (j * N_KV + g) * HEAD_DIM
            vt_ref[0, j, g, 0:HEAD_DIM, :] = res_t[off:off + HEAD_DIM].astype(bf16)
            vt_ref[0, j, g, HEAD_DIM:V_ROWS, :] = ones
    gt_ref[0] = res_t[D_ATTN + 2 * D_KV:]
    kvc = jnp.dot(h, wc_ref[...], preferred_element_type=f32)
    kvc_ref[0, 0] = kvc[:, :D_KV]
    kvc_ref[0, 1] = kvc[:, D_KV:]
    kn = jnp.dot(h, wk_ref[...], preferred_element_type=f32).astype(bf16)
    tm = kn.shape[0]
    pos = pl.program_id(1) * tm + lax.broadcasted_iota(i32, (tm, LANES - HEAD_DIM), 0)
    lane = lax.broadcasted_iota(i32, (tm, LANES - HEAD_DIM), 1)
    onehot = jnp.where(lane == ((pos >> SEL_SHIFT) & (TILE_BLOCKS - 1)), 1.0, 0.0).astype(bf16)
    for j in range(2):
        for g in range(N_KV):
            off = (j * N_KV + g) * HEAD_DIM
            kn_ref[0, j, g] = jnp.concatenate([kn[:, off:off + HEAD_DIM], onehot], axis=1)
    cv_ref[0] = jnp.dot(h, wv_ref[...], preferred_element_type=f32)


def _in_proj(x, mod, w_in, tm):
    b, s, d = x.shape
    o = 0
    wq = w_in[:, o:o + D_ATTN]; o += D_ATTN
    wkc = w_in[:, o:o + 2 * D_KV]; o += 2 * D_KV
    wk_s = w_in[:, o:o + D_KV]; o += D_KV
    wv_s = w_in[:, o:o + D_KV]; o += D_KV
    wk_w = w_in[:, o:o + D_KV]; o += D_KV
    wv_w = w_in[:, o:o + D_KV]; o += D_KV
    wg = w_in[:, o:o + 3 * N_HEADS]; o += 3 * N_HEADS
    wcv = w_in[:, o:]
    d_conv2 = wcv.shape[1]
    wg = jnp.zeros((d, GATE_ROWS), f32).at[:, :3 * N_HEADS].set(wg)
    wt = jnp.concatenate([wq, wv_s, wv_w, wg], axis=1).T
    ws = [w.astype(bf16) for w in (wt, wkc, jnp.concatenate([wk_s, wk_w], axis=1), wcv)]
    full = lambda a: pl.BlockSpec(a.shape, lambda bi, i: (0, 0))
    return pl.pallas_call(
        _in_proj_kernel,
        grid=(b, s // tm),
        in_specs=[pl.BlockSpec((1, tm, d), lambda bi, i: (bi, i, 0)),
                  pl.BlockSpec((1, 6, d), lambda bi, i: (bi, 0, 0))] + [full(w) for w in ws],
        out_specs=[pl.BlockSpec((1, D_ATTN, tm), lambda bi, i: (bi, 0, i)),
                   pl.BlockSpec((1, 2, N_KV, V_ROWS, tm), lambda bi, i: (bi, 0, 0, 0, i)),
                   pl.BlockSpec((1, GATE_ROWS, tm), lambda bi, i: (bi, 0, i)),
                   pl.BlockSpec((1, 2, tm, D_KV), lambda bi, i: (bi, 0, i, 0)),
                   pl.BlockSpec((1, 2, N_KV, tm, LANES), lambda bi, i: (bi, 0, 0, i, 0)),
                   pl.BlockSpec((1, tm, d_conv2), lambda bi, i: (bi, i, 0))],
        out_shape=[jax.ShapeDtypeStruct((b, D_ATTN, s), bf16),
                   jax.ShapeDtypeStruct((b, 2, N_KV, V_ROWS, s), bf16),
                   jax.ShapeDtypeStruct((b, GATE_ROWS, s), f32),
                   jax.ShapeDtypeStruct((b, 2, s, D_KV), f32),
                   jax.ShapeDtypeStruct((b, 2, N_KV, s, LANES), bf16),
                   jax.ShapeDtypeStruct((b, s, d_conv2), f32)],
        compiler_params=_params(("parallel", "parallel")),
        name="in_proj",
    )(x, mod, *ws)


def _compress_kernel(x_ref, pe_ref, w1_ref, w2_ref, w2t_ref, o_ref, ot_ref):
    n_chunk = o_ref.shape[3]
    for j in range(2):
        for g in range(N_KV):
            cols = slice(g * HEAD_DIM, (g + 1) * HEAD_DIM)
            a = jnp.zeros((n_chunk, CMP_HID), f32)
            bm = jnp.zeros((n_chunk, CMP_HID), f32)
            for l in range(CMP_STRIDE):
                xl = x_ref[0, j, pl.ds(l, n_chunk, stride=CMP_STRIDE), :][:, cols]
                a = a + jnp.dot((xl + pe_ref[j, l:l + 1, :]).astype(bf16), w1_ref[j, l],
                                preferred_element_type=f32)
                bm = bm + jnp.dot((xl + pe_ref[j, CMP_STRIDE + l:CMP_STRIDE + l + 1, :]).astype(bf16),
                                  w1_ref[j, CMP_STRIDE + l], preferred_element_type=f32)
            hid = a + pltpu.roll(bm, n_chunk - 1, 0)
            act = 0.5 * hid * (1.0 + jnp.tanh(0.7978845608028654 * (hid + 0.044715 * (hid * hid * hid))))
            act = act.astype(bf16)
            o_ref[0, j, g] = jnp.dot(act, w2_ref[j], preferred_element_type=f32).astype(bf16)
            ot_ref[0, j, g] = _dot_nt(w2t_ref[j], act).astype(bf16)


def _compress(kvc, pe, w1, w2):
    assert CMP_LEN == 2 * CMP_STRIDE
    b, _, s, width = kvc.shape
    n_chunk = s // CMP_STRIDE
    w1b = w1.reshape(2, CMP_LEN, HEAD_DIM, CMP_HID).astype(bf16)
    w2b = w2.astype(bf16)
    w2t = w2b.transpose(0, 2, 1)
    full = lambda a: pl.BlockSpec(a.shape, lambda bi: (0,) * a.ndim)
    return pl.pallas_call(
        _compress_kernel,
        grid=(b,),
        in_specs=[pl.BlockSpec((1, 2, s, width), lambda bi: (bi, 0, 0, 0)),
                  full(pe), full(w1b), full(w2b), full(w2t)],
        out_specs=[pl.BlockSpec((1, 2, N_KV, n_chunk, HEAD_DIM), lambda bi: (bi, 0, 0, 0, 0)),
                   pl.BlockSpec((1, 2, N_KV, HEAD_DIM, n_chunk), lambda bi: (bi, 0, 0, 0, 0))],
        out_shape=[jax.ShapeDtypeStruct((b, 2, N_KV, n_chunk, HEAD_DIM), bf16),
                   jax.ShapeDtypeStruct((b, 2, N_KV, HEAD_DIM, n_chunk), bf16)],
        compiler_params=_params(("parallel",)),
        name="compress",
    )(kvc, pe, w1b, w2b, w2t)


def _attn_kernel(q_ref, g_ref, kc_ref, vc_ref, kn_ref, vt_ref, o_ref, psum_ref, selbias_ref, sa_ref, sb_ref,
                 sd_ref, *, seq, tk, top_n):
    i = pl.program_id(2)
    s0 = i * Q_BLOCK
    n_cmp_rows = kc_ref.shape[3]
    n_sel = seq // SEL_LEN

    q4 = q_ref[0]
    qt = jnp.concatenate([q4[p * HEAD_DIM:(p + 1) * HEAD_DIM, :] for p in range(HPG)], axis=1)
    t_row = s0 + lax.broadcasted_iota(i32, (1, Q_BLOCK), 1)

    s_c = jnp.dot(kc_ref[0, 0, 0], qt, preferred_element_type=f32)
    cmp_end = lax.broadcasted_iota(i32, (n_cmp_rows, 1), 0) * CMP_STRIDE + (CMP_LEN - 1)
    bias_c = jnp.where(cmp_end <= t_row, 0.0, NEG)
    any_c = t_row >= CMP_LEN - 1
    p_sum = jnp.zeros((n_cmp_rows, Q_BLOCK), f32)
    pcs = []
    for p in range(HPG):
        sp = s_c[:, p * Q_BLOCK:(p + 1) * Q_BLOCK] + bias_c
        e = jnp.exp2(sp - jnp.max(sp, axis=0, keepdims=True))
        pn = e * jnp.where(any_c, 1.0 / jnp.sum(e, axis=0, keepdims=True), 0.0)
        p_sum = p_sum + pn
        pcs.append(pn.astype(bf16))
    o_c = jnp.dot(vc_ref[0, 0, 0], jnp.concatenate(pcs, axis=1), preferred_element_type=f32)

    cols = HPG * Q_BLOCK
    pad_rows = LANES - HEAD_DIM
    q_pad = jnp.concatenate([qt, jnp.zeros((pad_rows, cols), bf16)], axis=0)
    span = WINDOW + Q_BLOCK
    w0 = pl.multiple_of(jnp.maximum(s0 - WINDOW, 0), Q_BLOCK)
    s_w = jnp.dot(kn_ref[0, 1, 0, pl.ds(w0, span), :], q_pad, preferred_element_type=f32)
    wpos = w0 + lax.broadcasted_iota(i32, (span, 1), 0)
    bias_w = jnp.where((wpos <= t_row) & (wpos > t_row - WINDOW), 0.0, NEG)
    pws = []
    for p in range(HPG):
        sp = s_w[:, p * Q_BLOCK:(p + 1) * Q_BLOCK] + bias_w
        pws.append(jnp.exp2(sp - jnp.max(sp, axis=0, keepdims=True)).astype(bf16))
    acc_w = jnp.dot(vt_ref[0, 1, 0, :, pl.ds(w0, span)], jnp.concatenate(pws, axis=1),
                    preferred_element_type=f32)
    o_w = acc_w[0:HEAD_DIM] * (1.0 / acc_w[HEAD_DIM:HEAD_DIM + 1])
    d0 = pl.multiple_of(s0, Q_BLOCK)
    s_d = jnp.dot(kn_ref[0, 0, 0, pl.ds(d0, Q_BLOCK), :], q_pad, preferred_element_type=f32)

    blk = lax.broadcasted_iota(i32, (n_sel, Q_BLOCK), 0)
    for lt in range(Q_BLOCK // LANES):
        psum_ref[lt] = p_sum[:, lt * LANES:(lt + 1) * LANES]

    def every(first):
        return jnp.concatenate([psum_ref[lt, pl.ds(first, n_sel, stride=CMP_PER_SEL), :]
                                for lt in range(Q_BLOCK // LANES)], axis=1)

    imp = every(0)
    for r in range(1, CMP_PER_SEL):
        imp = imp + every(r)
    for back in range(1, CMP_BACK + 1):
        imp = imp + jnp.where(blk >= 1, pltpu.roll(every(CMP_PER_SEL - back), 1, 0), 0.0)

    cur = t_row >> SEL_SHIFT
    forced = (blk == 0) | (blk == cur) | (blk == cur - 1)
    vals = jnp.where(forced, jnp.inf, jnp.where(blk <= cur, imp, -jnp.inf))
    sel = jnp.zeros((n_sel, Q_BLOCK), f32)
    for _ in range(top_n):
        mx = jnp.max(vals, axis=0, keepdims=True)
        first = jnp.min(jnp.where(vals == mx, blk, n_sel), axis=0, keepdims=True)
        pick = blk == first
        sel = jnp.where(pick & (mx > -jnp.inf), 1.0, sel)
        vals = jnp.where(pick, -jnp.inf, vals)
    first_blk = s0 >> SEL_SHIFT
    full_at = n_sel + TILE_BLOCKS
    selbias_ref[0:n_sel] = jnp.where((sel > 0.5) & (blk < first_blk), 0.0, NEG)
    selbias_ref[n_sel:full_at] = jnp.full((TILE_BLOCKS, Q_BLOCK), NEG, f32)
    selbias_ref[full_at:full_at + n_sel] = jnp.where(sel > 0.5, 0.0, NEG)
    last_tile = seq // tk - 1

    def scores(kt, s_ref):
        k0 = pl.multiple_of(jnp.minimum(kt, last_tile) * tk, tk)
        sb = selbias_ref[pl.ds(pl.multiple_of(kt * TILE_BLOCKS, TILE_BLOCKS), TILE_BLOCKS), :]
        rows = jnp.concatenate([jnp.concatenate([sb] * HPG, axis=1),
                                jnp.zeros((pad_rows - TILE_BLOCKS, cols), f32)], axis=0).astype(bf16)
        s_ref[...] = jnp.dot(kn_ref[0, 0, 0, pl.ds(k0, tk), :], jnp.concatenate([qt, rows], axis=0),
                             preferred_element_type=f32)

    def fold(scores_of_head, v_t, carry):
        m_i, acc = carry
        es, ms, alphas = [], [], []
        for p in range(HPG):
            c = slice(p * Q_BLOCK, (p + 1) * Q_BLOCK)
            sp = scores_of_head(c)
            m_new = jnp.maximum(m_i[:, c], jnp.max(sp, axis=0, keepdims=True))
            es.append(jnp.exp2(sp - m_new).astype(bf16))
            alphas.append(jnp.exp2(m_i[:, c] - m_new))
            ms.append(m_new)
        pv = jnp.dot(v_t, jnp.concatenate(es, axis=1), preferred_element_type=f32)
        return jnp.concatenate(ms, axis=1), jnp.concatenate(alphas, axis=1) * acc + pv

    def sel_tile(kt, s_ref, carry):
        k0 = pl.multiple_of(jnp.minimum(kt, last_tile) * tk, tk)
        return fold(lambda c: s_ref[:, c], vt_ref[0, 0, 0, :, pl.ds(k0, tk)], carry)

    def sel_pair(i, carry):
        scores(2 * i + 1, sb_ref)
        carry = sel_tile(2 * i, sa_ref, carry)
        scores(2 * i + 2, sa_ref)
        return sel_tile(2 * i + 1, sb_ref, carry)

    n_sweep = (s0 + tk - 1) // tk
    n_pairs = (n_sweep + 1) // 2
    init = (jnp.full((1, cols), NEG, f32), jnp.zeros((V_ROWS, cols), f32))
    scores(0, sa_ref)
    def sel_pairs(j, carry):
        for u in range(PAIR_UNROLL):
            carry = sel_pair(PAIR_UNROLL * j + u, carry)
        return carry

    group_at = pl.multiple_of(first_blk & ~(SUBLANES - 1), SUBLANES)
    group = selbias_ref[pl.ds(full_at + group_at, SUBLANES), :]
    own = group[0:PATCH_BLOCKS]
    for at in range(PATCH_BLOCKS, SUBLANES, PATCH_BLOCKS):
        own = jnp.where((first_blk & (SUBLANES - 1)) == at, group[at:at + PATCH_BLOCKS], own)
    bias_d = jnp.concatenate([jnp.broadcast_to(own[j:j + 1], (SEL_LEN, Q_BLOCK)) for j in range(PATCH_BLOCKS)],
                             axis=0)
    bias_d = jnp.where(s0 + lax.broadcasted_iota(i32, (Q_BLOCK, 1), 0) <= t_row, bias_d, NEG)
    for p in range(HPG):
        c = slice(p * Q_BLOCK, (p + 1) * Q_BLOCK)
        sd_ref[:, c] = s_d[:, c] + bias_d

    carry = lax.fori_loop(0, n_pairs // PAIR_UNROLL, sel_pairs, init)
    carry = lax.fori_loop(n_pairs // PAIR_UNROLL * PAIR_UNROLL, n_pairs, sel_pair, carry)
    _, acc_s = fold(lambda c: sd_ref[:, c], vt_ref[0, 0, 0, :, pl.ds(d0, Q_BLOCK)], carry)
    o_s = acc_s[0:HEAD_DIM] * (1.0 / acc_s[HEAD_DIM:HEAD_DIM + 1])

    gate = _sigmoid(g_ref[0, 0])
    outs = []
    for p in range(HPG):
        c = slice(p * Q_BLOCK, (p + 1) * Q_BLOCK)
        outs.append(gate[3 * p:3 * p + 1] * o_c[:, c] + gate[3 * p + 1:3 * p + 2] * o_s[:, c]
                    + gate[3 * p + 2:3 * p + 3] * o_w[:, c])
    o_ref[0] = jnp.concatenate(outs, axis=0).T.astype(bf16)


GATE_GROUP_ROWS = 16


def _attention(qt, gates, cmp_n, cmp_t, kn, vt):
    b, _, s = qt.shape
    n_chunk = cmp_n.shape[3]
    n_sel = s // SEL_LEN
    top_n = min(SEL_TOPN, n_sel)
    tk = KEY_TILE
    assert s % tk == 0 and n_chunk == n_sel * CMP_PER_SEL
    gw = HPG * HEAD_DIM
    return pl.pallas_call(
        functools.partial(_attn_kernel, seq=s, tk=tk, top_n=top_n),
        grid=(b, N_KV, s // Q_BLOCK),
        in_specs=[pl.BlockSpec((1, gw, Q_BLOCK), lambda bi, g, i: (bi, g, i)),
                  pl.BlockSpec((1, 1, GATE_GROUP_ROWS, Q_BLOCK), lambda bi, g, i: (bi, g, 0, i)),
                  pl.BlockSpec((1, 1, 1, n_chunk, HEAD_DIM), lambda bi, g, i: (bi, 0, g, 0, 0)),
                  pl.BlockSpec((1, 1, 1, HEAD_DIM, n_chunk), lambda bi, g, i: (bi, 1, g, 0, 0)),
                  pl.BlockSpec((1, 2, 1, s, LANES), lambda bi, g, i: (bi, 0, g, 0, 0)),
                  pl.BlockSpec((1, 2, 1, V_ROWS, s), lambda bi, g, i: (bi, 0, g, 0, 0))],
        out_specs=pl.BlockSpec((1, Q_BLOCK, gw), lambda bi, g, i: (bi, i, g)),
        out_shape=jax.ShapeDtypeStruct((b, s, D_ATTN), bf16),
        scratch_shapes=[pltpu.VMEM((Q_BLOCK // LANES, n_chunk, LANES), f32),
                        pltpu.VMEM((2 * n_sel + TILE_BLOCKS, Q_BLOCK), f32),
                        pltpu.VMEM((tk, HPG * Q_BLOCK), f32), pltpu.VMEM((tk, HPG * Q_BLOCK), f32),
                        pltpu.VMEM((Q_BLOCK, HPG * Q_BLOCK), f32)],
        compiler_params=_params(("parallel", "parallel", "arbitrary")),
        name="attention",
    )(qt, gates, cmp_n, cmp_t, kn, vt)


CONV_HALO = 32
CONV_ROWS = 32


def _conv_kernel(cur_ref, prev_ref, w_ref, b_ref, g_ref, bb_ref, o_ref, glu_ref):
    i = pl.program_id(1)
    ts = cur_ref.shape[1]
    dc = o_ref.shape[2]
    cur = cur_ref[0]
    prev = prev_ref[0]
    glu_prev = prev[:, :dc] * _sigmoid(prev[:, dc:])
    glu_ref[0:CONV_HALO] = jnp.where(i == 0, 0.0, glu_prev)
    glu_ref[CONV_HALO:CONV_HALO + ts] = cur[:, :dc] * _sigmoid(cur[:, dc:])
    lead = CONV_HALO - (CONV_WIDTH - 1)

    def chunk(r, _):
        r0 = pl.multiple_of(r * CONV_ROWS, CONV_ROWS)
        win = glu_ref[pl.ds(r0, CONV_ROWS + CONV_HALO), :]
        span = CONV_ROWS + CONV_HALO
        acc = jnp.zeros((CONV_ROWS, dc), f32)
        for ph in range(SUBLANES):
            turned = win if ph == 0 else pltpu.roll(win, span - ph, 0)
            for j in range(CONV_WIDTH):
                if (lead + j) % SUBLANES == ph:
                    at = lead + j - ph
                    acc = acc + w_ref[j:j + 1, :] * turned[at:at + CONV_ROWS]
        y = _layer_norm(acc + b_ref[...], g_ref[...], bb_ref[...])
        o_ref[0, pl.ds(r0, CONV_ROWS), :] = _silu(y).astype(bf16)
        return 0

    lax.fori_loop(0, ts // CONV_ROWS, chunk, 0)


def _conv(conv_in, w_dw, b_dw, ln_g, ln_b, ts):
    b, s, dc2 = conv_in.shape
    dc = dc2 // 2
    per = ts // CONV_HALO
    row = lambda a: a.reshape(1, dc)
    return pl.pallas_call(
        _conv_kernel,
        grid=(b, s // ts),
        in_specs=[pl.BlockSpec((1, ts, dc2), lambda bi, i: (bi, i, 0)),
                  pl.BlockSpec((1, CONV_HALO, dc2), lambda bi, i: (bi, jnp.maximum(i * per - 1, 0), 0)),
                  pl.BlockSpec((CONV_WIDTH, dc), lambda bi, i: (0, 0)),
                  pl.BlockSpec((1, dc), lambda bi, i: (0, 0)),
                  pl.BlockSpec((1, dc), lambda bi, i: (0, 0)),
                  pl.BlockSpec((1, dc), lambda bi, i: (0, 0))],
        out_specs=pl.BlockSpec((1, ts, dc), lambda bi, i: (bi, i, 0)),
        out_shape=jax.ShapeDtypeStruct((b, s, dc), bf16),
        scratch_shapes=[pltpu.VMEM((CONV_HALO + ts, dc), f32)],
        compiler_params=_params(("parallel", "parallel")),
        name="conv",
    )(conv_in, conv_in, w_dw.reshape(CONV_WIDTH, dc), row(b_dw), row(ln_g), row(ln_b))


ROUTE_HALF = 128


def _mix_route_kernel(a_ref, cv_ref, x_ref, mod_ref, wo_ref, lg_ref, lb_ref, wrh_ref, wrl_ref, rb_ref,
                      x1_ref, h2p_ref, idx_ref, wt_ref, cnt_ref):
    step = pl.program_id(0)

    @pl.when(step == 0)
    def _():
        cnt_ref[...] = jnp.zeros_like(cnt_ref)

    m = mod_ref[0]
    da = a_ref.shape[1]
    counts = jnp.zeros(cnt_ref.shape, f32)
    for first in range(0, x_ref.shape[0], ROUTE_HALF):
        rows = slice(first, first + ROUTE_HALF)
        mix = (jnp.dot(a_ref[rows, :], wo_ref[0:da, :], preferred_element_type=f32)
               + jnp.dot(cv_ref[rows, :], wo_ref[da:, :], preferred_element_type=f32))
        x1 = _layer_norm(DEEPNORM_ALPHA * x_ref[rows, :] + m[2:3] * mix, lg_ref[...], lb_ref[...])
        x1_ref[rows, :] = x1
        h2 = x1 * (1.0 + m[4:5]) + m[3:4]
        _store_packed_rows(h2p_ref, h2, first)
        h_hi = h2.astype(bf16)
        h_lo = (h2 - h_hi.astype(f32)).astype(bf16)
        logits = _dot_nt(wrh_ref[...], h_hi) + (_dot_nt(wrh_ref[...], h_lo) + _dot_nt(wrl_ref[...], h_hi))
        idx, wt, cnt = _route(_sigmoid(logits), rb_ref[...])
        idx_ref[:, rows] = idx
        wt_ref[:, rows] = wt
        counts = counts + cnt
    cnt_ref[...] += counts


def _route(score, bias):
    tm = score.shape[1]
    sel = score + bias
    eid = lax.broadcasted_iota(i32, (N_EXPERTS, tm), 0)
    per_group = N_EXPERTS // N_GROUPS
    gs = []
    for g in range(N_GROUPS):
        rows = slice(g * per_group, (g + 1) * per_group)
        v = sel[rows]
        e = g * per_group + lax.broadcasted_iota(i32, (per_group, tm), 0)
        m1 = jnp.max(v, axis=0, keepdims=True)
        i1 = jnp.min(jnp.where(v == m1, e, N_EXPERTS), axis=0, keepdims=True)
        m2 = jnp.max(jnp.where(e == i1, -jnp.inf, v), axis=0, keepdims=True)
        gs.append(m1 + m2)
    cands = []
    for g in range(N_GROUPS):
        rank = jnp.zeros((1, tm), i32)
        for o in range(N_GROUPS):
            if o == g:
                continue
            beats = (gs[o] > gs[g]) | (gs[o] == gs[g]) if o < g else gs[o] > gs[g]
            rank = rank + beats.astype(i32)
        drop = jnp.where(rank < TOPK_GROUPS, 0.0, -jnp.inf)
        cands.append(sel[g * per_group:(g + 1) * per_group] + drop)
    cand = jnp.concatenate(cands, axis=0)
    row_o = lax.broadcasted_iota(i32, (TOP_K, tm), 0)
    idx_out = jnp.zeros((TOP_K, tm), i32)
    wt_out = jnp.zeros((TOP_K, tm), f32)
    picked = jnp.zeros((N_EXPERTS, tm), f32)
    w_sum = jnp.zeros((1, tm), f32)
    for k in range(TOP_K):
        mx = jnp.max(cand, axis=0, keepdims=True)
        ik = jnp.min(jnp.where(cand == mx, eid, N_EXPERTS), axis=0, keepdims=True)
        pick = eid == ik
        wk = jnp.sum(jnp.where(pick, score, 0.0), axis=0, keepdims=True)
        cand = jnp.where(pick, -jnp.inf, cand)
        picked = jnp.where(pick, 1.0, picked)
        idx_out = jnp.where(row_o == k, ik, idx_out)
        wt_out = jnp.where(row_o == k, wk, wt_out)
        w_sum = w_sum + wk
    return idx_out, wt_out / w_sum * ROUTED_SCALE, jnp.sum(picked, axis=1, keepdims=True)


def _mix_route(attn, conv, x2, mod, w_out, ln_g, ln_b, w_router, router_bias, s, tm):
    t, d = x2.shape
    per = s // tm
    da = attn.shape[1]
    pack_rows = d // 2 // LANES
    row = lambda a: a.reshape(1, -1)
    tile = lambda w: pl.BlockSpec((tm, w), lambda i: (i, 0))
    full = lambda a: pl.BlockSpec(a.shape, lambda i: (0,) * a.ndim)
    assert tm % ROUTE_HALF == 0
    wr_hi = w_router.T.astype(bf16)
    wr_lo = (w_router.T - wr_hi.astype(f32)).astype(bf16)
    args = (attn, conv, x2, mod, w_out.astype(bf16), row(ln_g), row(ln_b), wr_hi, wr_lo,
            router_bias.reshape(N_EXPERTS, 1))
    per_token = lambda rows: pl.BlockSpec((rows, tm), lambda i: (0, i))
    return pl.pallas_call(
        _mix_route_kernel,
        grid=(t // tm,),
        in_specs=[tile(da), tile(conv.shape[1]), tile(d),
                  pl.BlockSpec((1, 6, d), lambda i: (i // per, 0, 0))] + [full(a) for a in args[4:]],
        out_specs=[tile(d), pl.BlockSpec((tm * pack_rows, LANES), lambda i: (i, 0)),
                   per_token(TOP_K), per_token(TOP_K), pl.BlockSpec((N_EXPERTS, 1), lambda i: (0, 0))],
        out_shape=[jax.ShapeDtypeStruct((t, d), f32),
                   jax.ShapeDtypeStruct((t * pack_rows, LANES), u32),
                   jax.ShapeDtypeStruct((TOP_K, t), i32), jax.ShapeDtypeStruct((TOP_K, t), f32),
                   jax.ShapeDtypeStruct((N_EXPERTS, 1), f32)],
        compiler_params=_params(("arbitrary",)),
        name="mix_route",
    )(*args)


def _positions_kernel(idx_ref, start_ref, o_ref, run_ref):
    step = pl.program_id(0)
    tm = idx_ref.shape[1]

    @pl.when(step == 0)
    def _():
        run_ref[...] = jnp.zeros_like(run_ref)

    idx = idx_ref[...]
    eid = lax.broadcasted_iota(i32, (N_EXPERTS, tm), 0)
    onehot = jnp.zeros((N_EXPERTS, tm), f32)
    for k in range(TOP_K):
        onehot = jnp.where(eid == idx[k:k + 1], 1.0, onehot)
    r = lax.broadcasted_iota(i32, (tm, tm), 0)
    c = lax.broadcasted_iota(i32, (tm, tm), 1)
    earlier = jnp.where(r < c, 1.0, 0.0).astype(bf16)
    prior = jnp.dot(onehot.astype(bf16), earlier, preferred_element_type=f32)
    pos = prior + run_ref[...] + start_ref[...]
    row_o = lax.broadcasted_iota(i32, (TOP_K, tm), 0)
    out = jnp.zeros((TOP_K, tm), i32)
    for k in range(TOP_K):
        dk = jnp.sum(jnp.where(eid == idx[k:k + 1], pos, 0.0), axis=0, keepdims=True)
        out = jnp.where(row_o == k, dk.astype(i32), out)
    o_ref[...] = out
    run_ref[...] += jnp.sum(onehot, axis=1, keepdims=True)


def _positions(idx, seg_start, tm):
    t = idx.shape[1]
    return pl.pallas_call(
        _positions_kernel,
        grid=(t // tm,),
        in_specs=[pl.BlockSpec((TOP_K, tm), lambda i: (0, i)),
                  pl.BlockSpec((N_EXPERTS, 1), lambda i: (0, 0))],
        out_specs=pl.BlockSpec((TOP_K, tm), lambda i: (0, i)),
        out_shape=jax.ShapeDtypeStruct((TOP_K, t), i32),
        scratch_shapes=[pltpu.VMEM((N_EXPERTS, 1), f32)],
        compiler_params=_params(("arbitrary",)),
        name="positions",
    )(idx, seg_start)


def _dispatch_kernel(dest_hbm, h_ref, o_hbm, dest_smem, sem_i, sem, *, r):
    step = pl.program_id(0)
    n = dest_smem.shape[0]
    load = pltpu.make_async_copy(dest_hbm.at[pl.ds(pl.multiple_of(step * n, n), n)], dest_smem, sem_i)
    load.start()
    load.wait()

    def issue(tok, _):
        src = pl.multiple_of(tok * r, r)
        for k in range(TOP_K):
            dst = pl.multiple_of(dest_smem[tok * TOP_K + k] * r, r)
            pltpu.make_async_copy(h_ref.at[pl.ds(src, r)], o_hbm.at[pl.ds(dst, r)], sem).start(priority=k % 2)
        return 0

    lax.fori_loop(0, n // TOP_K, issue, 0)
    everything = o_hbm.at[pl.ds(0, n * r)]
    pltpu.make_async_copy(everything, everything, sem).wait()


def _dispatch(dest_flat, h2p, n_buf, r, tm):
    t = h2p.shape[0] // r
    return pl.pallas_call(
        functools.partial(_dispatch_kernel, r=r),
        grid=(t // tm,),
        in_specs=[pl.BlockSpec(memory_space=pl.ANY),
                  pl.BlockSpec((tm * r, LANES), lambda i: (i, 0))],
        out_specs=pl.BlockSpec(memory_space=pl.ANY),
        out_shape=jax.ShapeDtypeStruct((n_buf * r, LANES), u32),
        scratch_shapes=[pltpu.SMEM((tm * TOP_K,), i32), pltpu.SemaphoreType.DMA, pltpu.SemaphoreType.DMA],
        compiler_params=_params(("arbitrary",)),
        name="dispatch",
    )(dest_flat, h2p)


def _experts_kernel(be_ref, bv_ref, nu_ref, x_ref, wg_ref, wu_ref, wd_ref, o_ref, wg_s, wu_s, wd_s):
    j = pl.program_id(0)
    prev = be_ref[jnp.maximum(j - 1, 0)]
    used = j < nu_ref[0]
    d = wg_s.shape[0]

    @pl.when(used & ((j == 0) | (be_ref[j] != prev)))
    def _():
        wg_s[...] = wg_ref[0].astype(bf16)
        wu_s[...] = wu_ref[0].astype(bf16)
        wd_s[...] = wd_ref[0].astype(bf16)

    @pl.when(used)
    def _():
        live = lax.broadcasted_iota(i32, (ROW_BLOCK, 1), 0) < bv_ref[j]
        x = jnp.where(live, _load_packed_rows(x_ref, ROW_BLOCK, d), 0.0).astype(bf16)
        hg = jnp.dot(x, wg_s[...], preferred_element_type=f32)
        hu = jnp.dot(x, wu_s[...], preferred_element_type=f32)
        hid = (_silu(hg) * hu).astype(bf16)
        _store_packed_rows(o_ref, jnp.dot(hid, wd_s[...], preferred_element_type=f32))


def _experts(blk_e, blk_valid, n_used, xs, w_gate, w_up, w_down):
    d, f = w_gate.shape[1], w_gate.shape[2]
    r = d // 2 // LANES
    n_blk = xs.shape[0] // r // ROW_BLOCK
    rows = lambda j, be, bv, nu: (jnp.minimum(j, nu[0] - 1), 0)
    wsel = lambda j, be, bv, nu: (be[j], 0, 0)
    return pl.pallas_call(
        _experts_kernel,
        grid_spec=pltpu.PrefetchScalarGridSpec(
            num_scalar_prefetch=3,
            grid=(n_blk,),
            in_specs=[pl.BlockSpec((ROW_BLOCK * r, LANES), rows),
                      pl.BlockSpec((1, d, f), wsel),
                      pl.BlockSpec((1, d, f), wsel),
                      pl.BlockSpec((1, f, d), wsel)],
            out_specs=pl.BlockSpec((ROW_BLOCK * r, LANES), rows),
            scratch_shapes=[pltpu.VMEM((d, f), bf16), pltpu.VMEM((d, f), bf16), pltpu.VMEM((f, d), bf16)]),
        out_shape=jax.ShapeDtypeStruct(xs.shape, u32),
        compiler_params=_params(("arbitrary",)),
        name="experts",
    )(blk_e, blk_valid, n_used, xs, w_gate, w_up, w_down)


COMBINE_CHUNK = 64


def _combine_kernel(dest_hbm, ys_hbm, wt_ref, h_ref, x1_ref, mod_ref, wsg_ref, wsu_ref, wsd_ref,
                    lg_ref, lb_ref, o_ref, dest_a, dest_b, rows_a, rows_b, sem_d, sem_g):
    step = pl.program_id(0)
    n_steps = pl.num_programs(0)
    tm, d = x1_ref.shape
    r = d // 2 // LANES
    n = tm * TOP_K
    tables, rows = (dest_a, dest_b), (rows_a, rows_b)

    def table_copy(tile, slot):
        return pltpu.make_async_copy(dest_hbm.at[pl.ds(pl.multiple_of(tile * n, n), n)],
                                     tables[slot], sem_d.at[slot])

    def start_gathers(slot):
        def issue(tok, _):
            dst = pl.multiple_of(tok * r, r)
            for k in range(TOP_K):
                src = pl.multiple_of(tables[slot][tok * TOP_K + k] * r, r)
                pltpu.make_async_copy(ys_hbm.at[pl.ds(src, r)], rows[slot].at[k, pl.ds(dst, r)],
                                      sem_g.at[slot]).start(priority=k % 2)
            return 0

        lax.fori_loop(0, tm, issue, 0)

    @pl.when(step == 0)
    def _():
        table_copy(0, 0).start()
        table_copy(0, 0).wait()
        start_gathers(0)
        second = jnp.minimum(1, n_steps - 1)
        table_copy(second, 1).start()

        @pl.when(n_steps == 1)
        def _():
            table_copy(second, 1).wait()

    def tile_body(slot):
        other = 1 - slot

        @pl.when(step + 2 < n_steps)
        def _():
            table_copy(step + 2, slot).start()

        @pl.when(step + 1 < n_steps)
        def _():
            table_copy(step + 1, other).wait()

        h = _load_packed_rows(h_ref, tm, d).astype(bf16)
        hg = jnp.dot(h, wsg_ref[...], preferred_element_type=f32)
        hu = jnp.dot(h, wsu_ref[...], preferred_element_type=f32)
        shared = jnp.dot((_silu(hg) * hu).astype(bf16), wsd_ref[...], preferred_element_type=f32)
        pltpu.make_async_copy(rows[slot], rows[slot], sem_g.at[slot]).wait()
        m = mod_ref[0]
        for first in range(0, tm, COMBINE_CHUNK):
            for tok in range(first, first + COMBINE_CHUNK):
                for k in range(TOP_K):
                    src = pl.multiple_of(tables[other][tok * TOP_K + k] * r, r)
                    pltpu.make_async_copy(ys_hbm.at[pl.ds(src, r)], rows[other].at[k, pl.ds(tok * r, r)],
                                          sem_g.at[other]).start(priority=k % 2)
            sl = slice(first, first + COMBINE_CHUNK)
            wt = wt_ref[sl, :]
            lo = [jnp.zeros((COMBINE_CHUNK, LANES), f32) for _ in range(r)]
            hi = [jnp.zeros((COMBINE_CHUNK, LANES), f32) for _ in range(r)]
            for k in range(TOP_K):
                wk = wt[:, k:k + 1]
                for c in range(r):
                    pl_, ph_ = _unpack_words(rows[slot][k, pl.ds(first * r + c, COMBINE_CHUNK, stride=r), :])
                    lo[c] = lo[c] + wk * pl_
                    hi[c] = hi[c] + wk * ph_
            y = shared[sl] + jnp.concatenate(lo + hi, axis=1)
            o_ref[sl, :] = _layer_norm(DEEPNORM_ALPHA * x1_ref[sl, :] + m[5:6] * y, lg_ref[...], lb_ref[...])

        @pl.when(step + 1 == n_steps)
        def _():
            pltpu.make_async_copy(rows[other], rows[other], sem_g.at[other]).wait()

    for slot in range(2):
        pl.when(step % 2 == slot)(functools.partial(tile_body, slot))


def _combine(dest_flat, ys, wts, h2p, x1, mod, w_s_gate, w_s_up, w_s_down, ln_g, ln_b, s, tm):
    t, d = x1.shape
    per = s // tm
    r = d // 2 // LANES
    row = lambda a: a.reshape(1, -1)
    tile = lambda w: pl.BlockSpec((tm, w), lambda i: (i, 0))
    full = lambda a: pl.BlockSpec(a.shape, lambda i: (0,) * a.ndim)
    tail = (w_s_gate.astype(bf16), w_s_up.astype(bf16), w_s_down.astype(bf16), row(ln_g), row(ln_b))
    return pl.pallas_call(
        _combine_kernel,
        grid=(t // tm,),
        in_specs=[pl.BlockSpec(memory_space=pl.ANY), pl.BlockSpec(memory_space=pl.ANY),
                  tile(LANES), pl.BlockSpec((tm * r, LANES), lambda i: (i, 0)), tile(d),
                  pl.BlockSpec((1, 6, d), lambda i: (i // per, 0, 0))] + [full(a) for a in tail],
        out_specs=tile(d),
        out_shape=jax.ShapeDtypeStruct((t, d), f32),
        scratch_shapes=[pltpu.SMEM((tm * TOP_K,), i32), pltpu.SMEM((tm * TOP_K,), i32),
                        pltpu.VMEM((TOP_K, tm * r, LANES), u32), pltpu.VMEM((TOP_K, tm * r, LANES), u32),
                        pltpu.SemaphoreType.DMA((2,)), pltpu.SemaphoreType.DMA((2,))],
        compiler_params=_params(("arbitrary",)),
        name="combine",
    )(dest_flat, ys, wts, h2p, x1, mod, *tail)


def _token_tiles(s):
    return {"proj": min(512, s), "route": min(256, s), "dispatch": min(1024, s), "combine": min(256, s)}


def _layer(x, mod, w_in, pe, w_cmp1, w_cmp2, w_dw, b_dw, conv_ln_g, conv_ln_b, w_out, ln1_g, ln1_b,
           w_router, router_bias, w_e_gate, w_e_up, w_e_down, w_s_gate, w_s_up, w_s_down, ln2_g, ln2_b):
    b, s, d = x.shape
    t = b * s
    tiles = _token_tiles(s)
    qt, vt, gt, kvc, kn, conv_in = _in_proj(x, mod, w_in, tiles["proj"])
    cmp_n, cmp_t = _compress(kvc, pe, w_cmp1, w_cmp2)
    gates = gt[:, :3 * N_HEADS].reshape(b, N_KV, 3 * HPG, s)
    gates = jnp.pad(gates, ((0, 0), (0, 0), (0, GATE_GROUP_ROWS - 3 * HPG), (0, 0)))
    attn = _attention(qt, gates, cmp_n, cmp_t, kn, vt)
    conv = _conv(conv_in, w_dw, b_dw, conv_ln_g, conv_ln_b, tiles["proj"])

    tr = tiles["route"]
    x1, h2p, idx, wts, counts = _mix_route(attn.reshape(t, -1), conv.reshape(t, -1), x.reshape(t, d), mod,
                                           w_out, ln1_g, ln1_b, w_router, router_bias, s, tr)
    counts = counts[:, 0].astype(i32)
    padded = (counts + ROW_BLOCK - 1) // ROW_BLOCK * ROW_BLOCK
    seg_end = jnp.cumsum(padded)
    seg_start = seg_end - padded
    n_blk = -(-(t * TOP_K + N_EXPERTS * (ROW_BLOCK - 1)) // ROW_BLOCK)
    blk_row0 = jnp.arange(n_blk, dtype=i32) * ROW_BLOCK
    owns = (blk_row0[:, None] >= seg_start[None, :]) & (blk_row0[:, None] < seg_end[None, :])
    blk_e = jnp.where(blk_row0 < seg_end[-1], jnp.argmax(owns, axis=1), N_EXPERTS - 1).astype(i32)
    live_end = jnp.sum(jnp.where(owns, (seg_start + counts)[None, :], 0), axis=1)
    blk_valid = jnp.clip(live_end - blk_row0, 0, ROW_BLOCK).astype(i32)
    n_used = (seg_end[-1:] // ROW_BLOCK).astype(i32)

    dest = _positions(idx, seg_start.astype(f32).reshape(N_EXPERTS, 1), tr)
    dest_flat = dest.T.reshape(-1)
    wts = jnp.zeros((t, LANES), f32).at[:, :TOP_K].set(wts.T)
    xs = _dispatch(dest_flat, h2p, n_blk * ROW_BLOCK, d // 2 // LANES, tiles["dispatch"])
    ys = _experts(blk_e, blk_valid, n_used, xs, w_e_gate, w_e_up, w_e_down)
    out = _combine(dest_flat, ys, wts, h2p, x1, mod, w_s_gate, w_s_up, w_s_down, ln2_g, ln2_b, s,
                   tiles["combine"])
    return out.reshape(b, s, d)


def kernel(x, c, w_ada, b_ada, w_in, pe_k, pe_v, w_cmp_k1, w_cmp_k2, w_cmp_v1, w_cmp_v2, w_dw, b_dw,
           conv_ln_g, conv_ln_b, w_out, ln1_g, ln1_b, w_router, router_bias, w_e_gate, w_e_up, w_e_down,
           w_s_gate, w_s_up, w_s_down, ln2_g, ln2_b):
    assert w_ada.shape[0] == DEPTH
    layer = lambda a: a.reshape(a.shape[1:])
    mod = _ada(c, layer(w_ada), layer(b_ada))
    return _layer(x, mod, layer(w_in), jnp.concatenate([pe_k, pe_v]),
                  jnp.concatenate([w_cmp_k1, w_cmp_v1]), jnp.concatenate([w_cmp_k2, w_cmp_v2]),
                  *[layer(a) for a in (w_dw, b_dw, conv_ln_g, conv_ln_b, w_out, ln1_g, ln1_b, w_router,
                                       router_bias, w_e_gate, w_e_up, w_e_down, w_s_gate, w_s_up, w_s_down,
                                       ln2_g, ln2_b)])
```

```python
import functools

import jax
import jax.numpy as jnp
from jax import lax
from jax.experimental import pallas as pl
from jax.experimental.pallas import tpu as pltpu

N_HEADS = 8
N_KV = 2
HPG = N_HEADS // N_KV
HEAD_DIM = 64
D_ATTN = N_HEADS * HEAD_DIM
D_KV = N_KV * HEAD_DIM
CONV_WIDTH = 31
CMP_LEN = 32
CMP_STRIDE = 16
CMP_HID = 256
SEL_LEN = 64
SEL_TOPN = 16
WINDOW = 512
Q_BLOCK = 256
N_EXPERTS = 256
TOP_K = 8
N_GROUPS = 8
TOPK_GROUPS = 4
ROUTED_SCALE = 2.5
LN_EPS = 1e-5
DEPTH = 1
DEEPNORM_ALPHA = (2 * DEPTH) ** 0.25

LANES = 128
SUBLANES = 8
ROW_BLOCK = 512
NEG = -1e30
HIGHEST = lax.Precision.HIGHEST
VMEM_LIMIT = 48 * 1024 * 1024

f32 = jnp.float32
bf16 = jnp.bfloat16
i32 = jnp.int32
u32 = jnp.uint32


def _params(sem, vmem=VMEM_LIMIT):
    return pltpu.CompilerParams(dimension_semantics=sem, vmem_limit_bytes=vmem)


def _sigmoid(v):
    return 1.0 / (1.0 + jnp.exp(-v))


def _silu(v):
    return v * _sigmoid(v)


def _layer_norm(v, g, b):
    mu = jnp.mean(v, axis=-1, keepdims=True)
    var = jnp.mean(jnp.square(v - mu), axis=-1, keepdims=True)
    return (v - mu) * lax.rsqrt(var + LN_EPS) * g + b


def _dot_nt(a, b):
    return lax.dot_general(a, b, (((1,), (1,)), ((), ())), preferred_element_type=f32)


def _store_packed_rows(ref, v, first=0):
    n, d = v.shape
    half = d // 2
    bits = lax.bitcast_convert_type(v.astype(bf16).astype(f32), u32)
    words = bits[:, half:] | (bits[:, :half] >> 16)
    r = half // LANES
    for c in range(r):
        ref[pl.ds(first * r + c, n, stride=r), :] = words[:, c * LANES:(c + 1) * LANES]


def _unpack_words(w):
    return (lax.bitcast_convert_type(w << 16, f32),
            lax.bitcast_convert_type(w & jnp.uint32(0xFFFF0000), f32))


def _load_packed_rows(ref, n, d):
    r = d // 2 // LANES
    parts = [_unpack_words(ref[pl.ds(c, n, stride=r), :]) for c in range(r)]
    return jnp.concatenate([p[0] for p in parts] + [p[1] for p in parts], axis=1)


def _ada_kernel(c_ref, w_ref, b_ref, o_ref):
    c = c_ref[...]
    o_ref[...] = jnp.dot(_silu(c), w_ref[...], precision=HIGHEST,
                         preferred_element_type=f32) + b_ref[...]


def _ada(c, w_ada, b_ada):
    b, d = c.shape
    n = w_ada.shape[1]
    rows = -(-b // SUBLANES) * SUBLANES
    c_pad = jnp.zeros((rows, d), f32).at[:b].set(c)
    tn = d
    out = pl.pallas_call(
        _ada_kernel,
        grid=(n // tn,),
        in_specs=[pl.BlockSpec((rows, d), lambda j: (0, 0)),
                  pl.BlockSpec((d, tn), lambda j: (0, j)),
                  pl.BlockSpec((1, tn), lambda j: (0, j))],
        out_specs=pl.BlockSpec((rows, tn), lambda j: (0, j)),
        out_shape=jax.ShapeDtypeStruct((rows, n), f32),
        compiler_params=_params(("arbitrary",)),
        name="ada",
    )(c_pad, w_ada, b_ada.reshape(1, n))
    return out[:b].reshape(b, 6, d)


GATE_ROWS = 32
KEY_TILE = 512
TILE_BLOCKS = KEY_TILE // SEL_LEN
SEL_SHIFT = SEL_LEN.bit_length() - 1
assert 1 << SEL_SHIFT == SEL_LEN and TILE_BLOCKS & (TILE_BLOCKS - 1) == 0
PATCH_BLOCKS = Q_BLOCK // SEL_LEN
PAIR_UNROLL = 2
CMP_PER_SEL = SEL_LEN // CMP_STRIDE
CMP_BACK = CMP_LEN // CMP_STRIDE - 1
assert SEL_LEN % CMP_STRIDE == 0 and CMP_LEN % CMP_STRIDE == 0 and CMP_BACK < CMP_PER_SEL
V_ROWS = HEAD_DIM + 16
Q_SCALE = HEAD_DIM ** -0.5 * 1.4426950408889634


def _in_proj_kernel(x_ref, mod_ref, wt_ref, wc_ref, wk_ref, wv_ref,
                    qt_ref, vt_ref, gt_ref, kvc_ref, kn_ref, cv_ref):
    m = mod_ref[0]
    h = (x_ref[0] * (1.0 + m[1:2]) + m[0:1]).astype(bf16)
    res_t = _dot_nt(wt_ref[...], h)
    qt_ref[0] = (res_t[0:D_ATTN] * Q_SCALE).astype(bf16)
    ones = jnp.ones((V_ROWS - HEAD_DIM, res_t.shape[1]), bf16)
    for j in range(2):
        for g in range(N_KV):
            off = D_ATTN + (j * N_KV + g) * HEAD_DIM
            vt_ref[0, j, g, 0:HEAD_DIM, :] = res_t[off:off + HEAD_DIM].astype(bf16)
            vt_ref[0, j, g, HEAD_DIM:V_ROWS, :] = ones
    gt_ref[0] = res_t[D_ATTN + 2 * D_KV:]
    kvc = jnp.dot(h, wc_ref[...], preferred_element_type=f32)
    kvc_ref[0, 0] = kvc[:, :D_KV]
    kvc_ref[0, 1] = kvc[:, D_KV:]
    kn = jnp.dot(h, wk_ref[...], preferred_element_type=f32).astype(bf16)
    tm = kn.shape[0]
    pos = pl.program_id(1) * tm + lax.broadcasted_iota(i32, (tm, LANES - HEAD_DIM), 0)
    lane = lax.broadcasted_iota(i32, (tm, LANES - HEAD_DIM), 1)
    onehot = jnp.where(lane == ((pos >> SEL_SHIFT) & (TILE_BLOCKS - 1)), 1.0, 0.0).astype(bf16)
    for j in range(2):
        for g in range(N_KV):
            off = (j * N_KV + g) * HEAD_DIM
            kn_ref[0, j, g] = jnp.concatenate([kn[:, off:off + HEAD_DIM], onehot], axis=1)
    cv_ref[0] = jnp.dot(h, wv_ref[...], preferred_element_type=f32)


def _in_proj(x, mod, w_in, tm):
    b, s, d = x.shape
    o = 0
    wq = w_in[:, o:o + D_ATTN]; o += D_ATTN
    wkc = w_in[:, o:o + 2 * D_KV]; o += 2 * D_KV
    wk_s = w_in[:, o:o + D_KV]; o += D_KV
    wv_s = w_in[:, o:o + D_KV]; o += D_KV
    wk_w = w_in[:, o:o + D_KV]; o += D_KV
    wv_w = w_in[:, o:o + D_KV]; o += D_KV
    wg = w_in[:, o:o + 3 * N_HEADS]; o += 3 * N_HEADS
    wcv = w_in[:, o:]
    d_conv2 = wcv.shape[1]
    wg = jnp.zeros((d, GATE_ROWS), f32).at[:, :3 * N_HEADS].set(wg)
    wt = jnp.concatenate([wq, wv_s, wv_w, wg], axis=1).T
    ws = [w.astype(bf16) for w in (wt, wkc, jnp.concatenate([wk_s, wk_w], axis=1), wcv)]
    full = lambda a: pl.BlockSpec(a.shape, lambda bi, i: (0, 0))
    return pl.pallas_call(
        _in_proj_kernel,
        grid=(b, s // tm),
        in_specs=[pl.BlockSpec((1, tm, d), lambda bi, i: (bi, i, 0)),
                  pl.BlockSpec((1, 6, d), lambda bi, i: (bi, 0, 0))] + [full(w) for w in ws],
        out_specs=[pl.BlockSpec((1, D_ATTN, tm), lambda bi, i: (bi, 0, i)),
                   pl.BlockSpec((1, 2, N_KV, V_ROWS, tm), lambda bi, i: (bi, 0, 0, 0, i)),
                   pl.BlockSpec((1, GATE_ROWS, tm), lambda bi, i: (bi, 0, i)),
                   pl.BlockSpec((1, 2, tm, D_KV), lambda bi, i: (bi, 0, i, 0)),
                   pl.BlockSpec((1, 2, N_KV, tm, LANES), lambda bi, i: (bi, 0, 0, i, 0)),
                   pl.BlockSpec((1, tm, d_conv2), lambda bi, i: (bi, i, 0))],
        out_shape=[jax.ShapeDtypeStruct((b, D_ATTN, s), bf16),
                   jax.ShapeDtypeStruct((b, 2, N_KV, V_ROWS, s), bf16),
                   jax.ShapeDtypeStruct((b, GATE_ROWS, s), f32),
                   jax.ShapeDtypeStruct((b, 2, s, D_KV), f32),
                   jax.ShapeDtypeStruct((b, 2, N_KV, s, LANES), bf16),
                   jax.ShapeDtypeStruct((b, s, d_conv2), f32)],
        compiler_params=_params(("parallel", "parallel")),
        name="in_proj",
    )(x, mod, *ws)


def _compress_kernel(x_ref, pe_ref, w1_ref, w2_ref, w2t_ref, o_ref, ot_ref):
    n_chunk = o_ref.shape[3]
    for j in range(2):
        for g in range(N_KV):
            cols = slice(g * HEAD_DIM, (g + 1) * HEAD_DIM)
            a = jnp.zeros((n_chunk, CMP_HID), f32)
            bm = jnp.zeros((n_chunk, CMP_HID), f32)
            for l in range(CMP_STRIDE):
                xl = x_ref[0, j, pl.ds(l, n_chunk, stride=CMP_STRIDE), :][:, cols]
                a = a + jnp.dot((xl + pe_ref[j, l:l + 1, :]).astype(bf16), w1_ref[j, l],
                                preferred_element_type=f32)
                bm = bm + jnp.dot((xl + pe_ref[j, CMP_STRIDE + l:CMP_STRIDE + l + 1, :]).astype(bf16),
                                  w1_ref[j, CMP_STRIDE + l], preferred_element_type=f32)
            hid = a + pltpu.roll(bm, n_chunk - 1, 0)
            act = 0.5 * hid * (1.0 + jnp.tanh(0.7978845608028654 * (hid + 0.044715 * (hid * hid * hid))))
            act = act.astype(bf16)
            o_ref[0, j, g] = jnp.dot(act, w2_ref[j], preferred_element_type=f32).astype(bf16)
            ot_ref[0, j, g] = _dot_nt(w2t_ref[j], act).astype(bf16)


def _compress(kvc, pe, w1, w2):
    assert CMP_LEN == 2 * CMP_STRIDE
    b, _, s, width = kvc.shape
    n_chunk = s // CMP_STRIDE
    w1b = w1.reshape(2, CMP_LEN, HEAD_DIM, CMP_HID).astype(bf16)
    w2b = w2.astype(bf16)
    w2t = w2b.transpose(0, 2, 1)
    full = lambda a: pl.BlockSpec(a.shape, lambda bi: (0,) * a.ndim)
    return pl.pallas_call(
        _compress_kernel,
        grid=(b,),
        in_specs=[pl.BlockSpec((1, 2, s, width), lambda bi: (bi, 0, 0, 0)),
                  full(pe), full(w1b), full(w2b), full(w2t)],
        out_specs=[pl.BlockSpec((1, 2, N_KV, n_chunk, HEAD_DIM), lambda bi: (bi, 0, 0, 0, 0)),
                   pl.BlockSpec((1, 2, N_KV, HEAD_DIM, n_chunk), lambda bi: (bi, 0, 0, 0, 0))],
        out_shape=[jax.ShapeDtypeStruct((b, 2, N_KV, n_chunk, HEAD_DIM), bf16),
                   jax.ShapeDtypeStruct((b, 2, N_KV, HEAD_DIM, n_chunk), bf16)],
        compiler_params=_params(("parallel",)),
        name="compress",
    )(kvc, pe, w1b, w2b, w2t)


def _attn_kernel(q_ref, g_ref, kc_ref, vc_ref, kn_ref, vt_ref, o_ref, psum_ref, selbias_ref, sa_ref, sb_ref,
                 sd_ref, *, seq, tk, top_n):
    i = pl.program_id(2)
    s0 = i * Q_BLOCK
    n_cmp_rows = kc_ref.shape[3]
    n_sel = seq // SEL_LEN

    q4 = q_ref[0]
    qt = jnp.concatenate([q4[p * HEAD_DIM:(p + 1) * HEAD_DIM, :] for p in range(HPG)], axis=1)
    t_row = s0 + lax.broadcasted_iota(i32, (1, Q_BLOCK), 1)

    s_c = jnp.dot(kc_ref[0, 0, 0], qt, preferred_element_type=f32)
    cmp_end = lax.broadcasted_iota(i32, (n_cmp_rows, 1), 0) * CMP_STRIDE + (CMP_LEN - 1)
    bias_c = jnp.where(cmp_end <= t_row, 0.0, NEG)
    any_c = t_row >= CMP_LEN - 1
    p_sum = jnp.zeros((n_cmp_rows, Q_BLOCK), f32)
    pcs = []
    for p in range(HPG):
        sp = s_c[:, p * Q_BLOCK:(p + 1) * Q_BLOCK] + bias_c
        e = jnp.exp2(sp - jnp.max(sp, axis=0, keepdims=True))
        pn = e * jnp.where(any_c, 1.0 / jnp.sum(e, axis=0, keepdims=True), 0.0)
        p_sum = p_sum + pn
        pcs.append(pn.astype(bf16))
    o_c = jnp.dot(vc_ref[0, 0, 0], jnp.concatenate(pcs, axis=1), preferred_element_type=f32)

    cols = HPG * Q_BLOCK
    pad_rows = LANES - HEAD_DIM
    q_pad = jnp.concatenate([qt, jnp.zeros((pad_rows, cols), bf16)], axis=0)
    span = WINDOW + Q_BLOCK
    w0 = pl.multiple_of(jnp.maximum(s0 - WINDOW, 0), Q_BLOCK)
    s_w = jnp.dot(kn_ref[0, 1, 0, pl.ds(w0, span), :], q_pad, preferred_element_type=f32)
    wpos = w0 + lax.broadcasted_iota(i32, (span, 1), 0)
    bias_w = jnp.where((wpos <= t_row) & (wpos > t_row - WINDOW), 0.0, NEG)
    pws = []
    for p in range(HPG):
        sp = s_w[:, p * Q_BLOCK:(p + 1) * Q_BLOCK] + bias_w
        pws.append(jnp.exp2(sp - jnp.max(sp, axis=0, keepdims=True)).astype(bf16))
    acc_w = jnp.dot(vt_ref[0, 1, 0, :, pl.ds(w0, span)], jnp.concatenate(pws, axis=1),
                    preferred_element_type=f32)
    o_w = acc_w[0:HEAD_DIM] * (1.0 / acc_w[HEAD_DIM:HEAD_DIM + 1])
    d0 = pl.multiple_of(s0, Q_BLOCK)
    s_d = jnp.dot(kn_ref[0, 0, 0, pl.ds(d0, Q_BLOCK), :], q_pad, preferred_element_type=f32)

    blk = lax.broadcasted_iota(i32, (n_sel, Q_BLOCK), 0)
    for lt in range(Q_BLOCK // LANES):
        psum_ref[lt] = p_sum[:, lt * LANES:(lt + 1) * LANES]

    def every(first):
        return jnp.concatenate([psum_ref[lt, pl.ds(first, n_sel, stride=CMP_PER_SEL), :]
                                for lt in range(Q_BLOCK // LANES)], axis=1)

    imp = every(0)
    for r in range(1, CMP_PER_SEL):
        imp = imp + every(r)
    for back in range(1, CMP_BACK + 1):
        imp = imp + jnp.where(blk >= 1, pltpu.roll(every(CMP_PER_SEL - back), 1, 0), 0.0)

    cur = t_row >> SEL_SHIFT
    forced = (blk == 0) | (blk == cur) | (blk == cur - 1)
    vals = jnp.where(forced, jnp.inf, jnp.where(blk <= cur, imp, -jnp.inf))
    sel = jnp.zeros((n_sel, Q_BLOCK), f32)
    for _ in range(top_n):
        mx = jnp.max(vals, axis=0, keepdims=True)
        first = jnp.min(jnp.where(vals == mx, blk, n_sel), axis=0, keepdims=True)
        pick = blk == first
        sel = jnp.where(pick & (mx > -jnp.inf), 1.0, sel)
        vals = jnp.where(pick, -jnp.inf, vals)
    first_blk = s0 >> SEL_SHIFT
    full_at = n_sel + TILE_BLOCKS
    selbias_ref[0:n_sel] = jnp.where((sel > 0.5) & (blk < first_blk), 0.0, NEG)
    selbias_ref[n_sel:full_at] = jnp.full((TILE_BLOCKS, Q_BLOCK), NEG, f32)
    selbias_ref[full_at:full_at + n_sel] = jnp.where(sel > 0.5, 0.0, NEG)
    last_tile = seq // tk - 1

    def scores(kt, s_ref):
        k0 = pl.multiple_of(jnp.minimum(kt, last_tile) * tk, tk)
        sb = selbias_ref[pl.ds(pl.multiple_of(kt * TILE_BLOCKS, TILE_BLOCKS), TILE_BLOCKS), :]
        rows = jnp.concatenate([jnp.concatenate([sb] * HPG, axis=1),
                                jnp.zeros((pad_rows - TILE_BLOCKS, cols), f32)], axis=0).astype(bf16)
        s_ref[...] = jnp.dot(kn_ref[0, 0, 0, pl.ds(k0, tk), :], jnp.concatenate([qt, rows], axis=0),
                             preferred_element_type=f32)

    def fold(scores_of_head, v_t, carry):
        m_i, acc = carry
        es, ms, alphas = [], [], []
        for p in range(HPG):
            c = slice(p * Q_BLOCK, (p + 1) * Q_BLOCK)
            sp = scores_of_head(c)
            m_new = jnp.maximum(m_i[:, c], jnp.max(sp, axis=0, keepdims=True))
            es.append(jnp.exp2(sp - m_new).astype(bf16))
            alphas.append(jnp.exp2(m_i[:, c] - m_new))
            ms.append(m_new)
        pv = jnp.dot(v_t, jnp.concatenate(es, axis=1), preferred_element_type=f32)
        return jnp.concatenate(ms, axis=1), jnp.concatenate(alphas, axis=1) * acc + pv

    def sel_tile(kt, s_ref, carry):
        k0 = pl.multiple_of(jnp.minimum(kt, last_tile) * tk, tk)
        return fold(lambda c: s_ref[:, c], vt_ref[0, 0, 0, :, pl.ds(k0, tk)], carry)

    def sel_pair(i, carry):
        scores(2 * i + 1, sb_ref)
        carry = sel_tile(2 * i, sa_ref, carry)
        scores(2 * i + 2, sa_ref)
        return sel_tile(2 * i + 1, sb_ref, carry)

    n_sweep = (s0 + tk - 1) // tk
    n_pairs = (n_sweep + 1) // 2
    init = (jnp.full((1, cols), NEG, f32), jnp.zeros((V_ROWS, cols), f32))
    scores(0, sa_ref)
    def sel_pairs(j, carry):
        for u in range(PAIR_UNROLL):
            carry = sel_pair(PAIR_UNROLL * j + u, carry)
        return carry

    group_at = pl.multiple_of(first_blk & ~(SUBLANES - 1), SUBLANES)
    group = selbias_ref[pl.ds(full_at + group_at, SUBLANES), :]
    own = group[0:PATCH_BLOCKS]
    for at in range(PATCH_BLOCKS, SUBLANES, PATCH_BLOCKS):
        own = jnp.where((first_blk & (SUBLANES - 1)) == at, group[at:at + PATCH_BLOCKS], own)
    bias_d = jnp.concatenate([jnp.broadcast_to(own[j:j + 1], (SEL_LEN, Q_BLOCK)) for j in range(PATCH_BLOCKS)],
                             axis=0)
    bias_d = jnp.where(s0 + lax.broadcasted_iota(i32, (Q_BLOCK, 1), 0) <= t_row, bias_d, NEG)
    for p in range(HPG):
        c = slice(p * Q_BLOCK, (p + 1) * Q_BLOCK)
        sd_ref[:, c] = s_d[:, c] + bias_d

    carry = lax.fori_loop(0, n_pairs // PAIR_UNROLL, sel_pairs, init)
    carry = lax.fori_loop(n_pairs // PAIR_UNROLL * PAIR_UNROLL, n_pairs, sel_pair, carry)
    _, acc_s = fold(lambda c: sd_ref[:, c], vt_ref[0, 0, 0, :, pl.ds(d0, Q_BLOCK)], carry)
    o_s = acc_s[0:HEAD_DIM] * (1.0 / acc_s[HEAD_DIM:HEAD_DIM + 1])

    gate = _sigmoid(g_ref[0, 0])
    outs = []
    for p in range(HPG):
        c = slice(p * Q_BLOCK, (p + 1) * Q_BLOCK)
        outs.append(gate[3 * p:3 * p + 1] * o_c[:, c] + gate[3 * p + 1:3 * p + 2] * o_s[:, c]
                    + gate[3 * p + 2:3 * p + 3] * o_w[:, c])
    o_ref[0] = jnp.concatenate(outs, axis=0).T.astype(bf16)


GATE_GROUP_ROWS = 16


def _attention(qt, gates, cmp_n, cmp_t, kn, vt):
    b, _, s = qt.shape
    n_chunk = cmp_n.shape[3]
    n_sel = s // SEL_LEN
    top_n = min(SEL_TOPN, n_sel)
    tk = KEY_TILE
    assert s % tk == 0 and n_chunk == n_sel * CMP_PER_SEL
    gw = HPG * HEAD_DIM
    return pl.pallas_call(
        functools.partial(_attn_kernel, seq=s, tk=tk, top_n=top_n),
        grid=(b, N_KV, s // Q_BLOCK),
        in_specs=[pl.BlockSpec((1, gw, Q_BLOCK), lambda bi, g, i: (bi, g, i)),
                  pl.BlockSpec((1, 1, GATE_GROUP_ROWS, Q_BLOCK), lambda bi, g, i: (bi, g, 0, i)),
                  pl.BlockSpec((1, 1, 1, n_chunk, HEAD_DIM), lambda bi, g, i: (bi, 0, g, 0, 0)),
                  pl.BlockSpec((1, 1, 1, HEAD_DIM, n_chunk), lambda bi, g, i: (bi, 1, g, 0, 0)),
                  pl.BlockSpec((1, 2, 1, s, LANES), lambda bi, g, i: (bi, 0, g, 0, 0)),
                  pl.BlockSpec((1, 2, 1, V_ROWS, s), lambda bi, g, i: (bi, 0, g, 0, 0))],
        out_specs=pl.BlockSpec((1, Q_BLOCK, gw), lambda bi, g, i: (bi, i, g)),
        out_shape=jax.ShapeDtypeStruct((b, s, D_ATTN), bf16),
        scratch_shapes=[pltpu.VMEM((Q_BLOCK // LANES, n_chunk, LANES), f32),
                        pltpu.VMEM((2 * n_sel + TILE_BLOCKS, Q_BLOCK), f32),
                        pltpu.VMEM((tk, HPG * Q_BLOCK), f32), pltpu.VMEM((tk, HPG * Q_BLOCK), f32),
                        pltpu.VMEM((Q_BLOCK, HPG * Q_BLOCK), f32)],
        compiler_params=_params(("parallel", "parallel", "arbitrary")),
        name="attention",
    )(qt, gates, cmp_n, cmp_t, kn, vt)


CONV_HALO = 32
CONV_ROWS = 32


def _conv_kernel(cur_ref, prev_ref, w_ref, b_ref, g_ref, bb_ref, o_ref, glu_ref):
    i = pl.program_id(1)
    ts = cur_ref.shape[1]
    dc = o_ref.shape[2]
    cur = cur_ref[0]
    prev = prev_ref[0]
    glu_prev = prev[:, :dc] * _sigmoid(prev[:, dc:])
    glu_ref[0:CONV_HALO] = jnp.where(i == 0, 0.0, glu_prev)
    glu_ref[CONV_HALO:CONV_HALO + ts] = cur[:, :dc] * _sigmoid(cur[:, dc:])
    lead = CONV_HALO - (CONV_WIDTH - 1)

    def chunk(r, _):
        r0 = pl.multiple_of(r * CONV_ROWS, CONV_ROWS)
        win = glu_ref[pl.ds(r0, CONV_ROWS + CONV_HALO), :]
        span = CONV_ROWS + CONV_HALO
        acc = jnp.zeros((CONV_ROWS, dc), f32)
        for ph in range(SUBLANES):
            turned = win if ph == 0 else pltpu.roll(win, span - ph, 0)
            for j in range(CONV_WIDTH):
                if (lead + j) % SUBLANES == ph:
                    at = lead + j - ph
                    acc = acc + w_ref[j:j + 1, :] * turned[at:at + CONV_ROWS]
        y = _layer_norm(acc + b_ref[...], g_ref[...], bb_ref[...])
        o_ref[0, pl.ds(r0, CONV_ROWS), :] = _silu(y).astype(bf16)
        return 0

    lax.fori_loop(0, ts // CONV_ROWS, chunk, 0)


def _conv(conv_in, w_dw, b_dw, ln_g, ln_b, ts):
    b, s, dc2 = conv_in.shape
    dc = dc2 // 2
    per = ts // CONV_HALO
    row = lambda a: a.reshape(1, dc)
    return pl.pallas_call(
        _conv_kernel,
        grid=(b, s // ts),
        in_specs=[pl.BlockSpec((1, ts, dc2), lambda bi, i: (bi, i, 0)),
                  pl.BlockSpec((1, CONV_HALO, dc2), lambda bi, i: (bi, jnp.maximum(i * per - 1, 0), 0)),
                  pl.BlockSpec((CONV_WIDTH, dc), lambda bi, i: (0, 0)),
                  pl.BlockSpec((1, dc), lambda bi, i: (0, 0)),
                  pl.BlockSpec((1, dc), lambda bi, i: (0, 0)),
                  pl.BlockSpec((1, dc), lambda bi, i: (0, 0))],
        out_specs=pl.BlockSpec((1, ts, dc), lambda bi, i: (bi, i, 0)),
        out_shape=jax.ShapeDtypeStruct((b, s, dc), bf16),
        scratch_shapes=[pltpu.VMEM((CONV_HALO + ts, dc), f32)],
        compiler_params=_params(("parallel", "parallel")),
        name="conv",
    )(conv_in, conv_in, w_dw.reshape(CONV_WIDTH, dc), row(b_dw), row(ln_g), row(ln_b))


ROUTE_HALF = 128


def _mix_route_kernel(a_ref, cv_ref, x_ref, mod_ref, wo_ref, lg_ref, lb_ref, wrh_ref, wrl_ref, rb_ref,
                      x1_ref, h2p_ref, idx_ref, wt_ref, cnt_ref):
    step = pl.program_id(0)

    @pl.when(step == 0)
    def _():
        cnt_ref[...] = jnp.zeros_like(cnt_ref)

    m = mod_ref[0]
    da = a_ref.shape[1]
    counts = jnp.zeros(cnt_ref.shape, f32)
    for first in range(0, x_ref.shape[0], ROUTE_HALF):
        rows = slice(first, first + ROUTE_HALF)
        mix = (jnp.dot(a_ref[rows, :], wo_ref[0:da, :], preferred_element_type=f32)
               + jnp.dot(cv_ref[rows, :], wo_ref[da:, :], preferred_element_type=f32))
        x1 = _layer_norm(DEEPNORM_ALPHA * x_ref[rows, :] + m[2:3] * mix, lg_ref[...], lb_ref[...])
        x1_ref[rows, :] = x1
        h2 = x1 * (1.0 + m[4:5]) + m[3:4]
        _store_packed_rows(h2p_ref, h2, first)
        h_hi = h2.astype(bf16)
        h_lo = (h2 - h_hi.astype(f32)).astype(bf16)
        logits = _dot_nt(wrh_ref[...], h_hi) + (_dot_nt(wrh_ref[...], h_lo) + _dot_nt(wrl_ref[...], h_hi))
        idx, wt, cnt = _route(_sigmoid(logits), rb_ref[...])
        idx_ref[:, rows] = idx
        wt_ref[:, rows] = wt
        counts = counts + cnt
    cnt_ref[...] += counts


def _route(score, bias):
    tm = score.shape[1]
    sel = score + bias
    eid = lax.broadcasted_iota(i32, (N_EXPERTS, tm), 0)
    per_group = N_EXPERTS // N_GROUPS
    gs = []
    for g in range(N_GROUPS):
        rows = slice(g * per_group, (g + 1) * per_group)
        v = sel[rows]
        e = g * per_group + lax.broadcasted_iota(i32, (per_group, tm), 0)
        m1 = jnp.max(v, axis=0, keepdims=True)
        i1 = jnp.min(jnp.where(v == m1, e, N_EXPERTS), axis=0, keepdims=True)
        m2 = jnp.max(jnp.where(e == i1, -jnp.inf, v), axis=0, keepdims=True)
        gs.append(m1 + m2)
    cands = []
    for g in range(N_GROUPS):
        rank = jnp.zeros((1, tm), i32)
        for o in range(N_GROUPS):
            if o == g:
                continue
            beats = (gs[o] > gs[g]) | (gs[o] == gs[g]) if o < g else gs[o] > gs[g]
            rank = rank + beats.astype(i32)
        drop = jnp.where(rank < TOPK_GROUPS, 0.0, -jnp.inf)
        cands.append(sel[g * per_group:(g + 1) * per_group] + drop)
    cand = jnp.concatenate(cands, axis=0)
    row_o = lax.broadcasted_iota(i32, (TOP_K, tm), 0)
    idx_out = jnp.zeros((TOP_K, tm), i32)
    wt_out = jnp.zeros((TOP_K, tm), f32)
    picked = jnp.zeros((N_EXPERTS, tm), f32)
    w_sum = jnp.zeros((1, tm), f32)
    for k in range(TOP_K):
        mx = jnp.max(cand, axis=0, keepdims=True)
        ik = jnp.min(jnp.where(cand == mx, eid, N_EXPERTS), axis=0, keepdims=True)
        pick = eid == ik
        wk = jnp.sum(jnp.where(pick, score, 0.0), axis=0, keepdims=True)
        cand = jnp.where(pick, -jnp.inf, cand)
        picked = jnp.where(pick, 1.0, picked)
        idx_out = jnp.where(row_o == k, ik, idx_out)
        wt_out = jnp.where(row_o == k, wk, wt_out)
        w_sum = w_sum + wk
    return idx_out, wt_out / w_sum * ROUTED_SCALE, jnp.sum(picked, axis=1, keepdims=True)


def _mix_route(attn, conv, x2, mod, w_out, ln_g, ln_b, w_router, router_bias, s, tm):
    t, d = x2.shape
    per = s // tm
    da = attn.shape[1]
    pack_rows = d // 2 // LANES
    row = lambda a: a.reshape(1, -1)
    tile = lambda w: pl.BlockSpec((tm, w), lambda i: (i, 0))
    full = lambda a: pl.BlockSpec(a.shape, lambda i: (0,) * a.ndim)
    assert tm % ROUTE_HALF == 0
    wr_hi = w_router.T.astype(bf16)
    wr_lo = (w_router.T - wr_hi.astype(f32)).astype(bf16)
    args = (attn, conv, x2, mod, w_out.astype(bf16), row(ln_g), row(ln_b), wr_hi, wr_lo,
            router_bias.reshape(N_EXPERTS, 1))
    per_token = lambda rows: pl.BlockSpec((rows, tm), lambda i: (0, i))
    return pl.pallas_call(
        _mix_route_kernel,
        grid=(t // tm,),
        in_specs=[tile(da), tile(conv.shape[1]), tile(d),
                  pl.BlockSpec((1, 6, d), lambda i: (i // per, 0, 0))] + [full(a) for a in args[4:]],
        out_specs=[tile(d), pl.BlockSpec((tm * pack_rows, LANES), lambda i: (i, 0)),
                   per_token(TOP_K), per_token(TOP_K), pl.BlockSpec((N_EXPERTS, 1), lambda i: (0, 0))],
        out_shape=[jax.ShapeDtypeStruct((t, d), f32),
                   jax.ShapeDtypeStruct((t * pack_rows, LANES), u32),
                   jax.ShapeDtypeStruct((TOP_K, t), i32), jax.ShapeDtypeStruct((TOP_K, t), f32),
                   jax.ShapeDtypeStruct((N_EXPERTS, 1), f32)],
        compiler_params=_params(("arbitrary",)),
        name="mix_route",
    )(*args)


def _positions_kernel(idx_ref, start_ref, o_ref, run_ref):
    step = pl.program_id(0)
    tm = idx_ref.shape[1]

    @pl.when(step == 0)
    def _():
        run_ref[...] = jnp.zeros_like(run_ref)

    idx = idx_ref[...]
    eid = lax.broadcasted_iota(i32, (N_EXPERTS, tm), 0)
    onehot = jnp.zeros((N_EXPERTS, tm), f32)
    for k in range(TOP_K):
        onehot = jnp.where(eid == idx[k:k + 1], 1.0, onehot)
    r = lax.broadcasted_iota(i32, (tm, tm), 0)
    c = lax.broadcasted_iota(i32, (tm, tm), 1)
    earlier = jnp.where(r < c, 1.0, 0.0).astype(bf16)
    prior = jnp.dot(onehot.astype(bf16), earlier, preferred_element_type=f32)
    pos = prior + run_ref[...] + start_ref[...]
    row_o = lax.broadcasted_iota(i32, (TOP_K, tm), 0)
    out = jnp.zeros((TOP_K, tm), i32)
    for k in range(TOP_K):
        dk = jnp.sum(jnp.where(eid == idx[k:k + 1], pos, 0.0), axis=0, keepdims=True)
        out = jnp.where(row_o == k, dk.astype(i32), out)
    o_ref[...] = out
    run_ref[...] += jnp.sum(onehot, axis=1, keepdims=True)


def _positions(idx, seg_start, tm):
    t = idx.shape[1]
    return pl.pallas_call(
        _positions_kernel,
        grid=(t // tm,),
        in_specs=[pl.BlockSpec((TOP_K, tm), lambda i: (0, i)),
                  pl.BlockSpec((N_EXPERTS, 1), lambda i: (0, 0))],
        out_specs=pl.BlockSpec((TOP_K, tm), lambda i: (0, i)),
        out_shape=jax.ShapeDtypeStruct((TOP_K, t), i32),
        scratch_shapes=[pltpu.VMEM((N_EXPERTS, 1), f32)],
        compiler_params=_params(("arbitrary",)),
        name="positions",
    )(idx, seg_start)


def _dispatch_kernel(last_ref, nu_ref, dest_hbm, h_ref, o_hbm, dest_smem, zero_ref, sem_i, sem, sem_z, *, r):
    step = pl.program_id(0)
    n = dest_smem.shape[0]
    blk_rows = zero_ref.shape[0]

    @pl.when(step == 0)
    def _():
        zero_ref[...] = jnp.zeros(zero_ref.shape, u32)

        def fill(blk):
            return pltpu.make_async_copy(zero_ref, o_hbm.at[pl.ds(pl.multiple_of(blk * blk_rows, blk_rows),
                                                                   blk_rows)], sem_z)

        def each_block(act):
            def per_expert(e, _):
                @pl.when(last_ref[e] >= 0)
                def _():
                    act(fill(last_ref[e]))
                return 0

            def per_tail(blk, _):
                act(fill(blk))
                return 0

            lax.fori_loop(0, last_ref.shape[0], per_expert, 0)
            lax.fori_loop(nu_ref[0], o_hbm.shape[0] // blk_rows, per_tail, 0)

        each_block(lambda copy: copy.start())
        each_block(lambda copy: copy.wait())

    load = pltpu.make_async_copy(dest_hbm.at[pl.ds(pl.multiple_of(step * n, n), n)], dest_smem, sem_i)
    load.start()
    load.wait()

    def issue(tok, _):
        src = pl.multiple_of(tok * r, r)
        for k in range(TOP_K):
            dst = pl.multiple_of(dest_smem[tok * TOP_K + k] * r, r)
            pltpu.make_async_copy(h_ref.at[pl.ds(src, r)], o_hbm.at[pl.ds(dst, r)], sem).start(priority=k % 2)
        return 0

    lax.fori_loop(0, n // TOP_K, issue, 0)
    everything = o_hbm.at[pl.ds(0, n * r)]
    pltpu.make_async_copy(everything, everything, sem).wait()


def _dispatch(last_block, n_used, dest_flat, h2p, n_buf, r, tm):
    t = h2p.shape[0] // r
    return pl.pallas_call(
        functools.partial(_dispatch_kernel, r=r),
        grid_spec=pltpu.PrefetchScalarGridSpec(
            num_scalar_prefetch=2,
            grid=(t // tm,),
            in_specs=[pl.BlockSpec(memory_space=pl.ANY),
                      pl.BlockSpec((tm * r, LANES), lambda i, last, nu: (i, 0))],
            out_specs=pl.BlockSpec(memory_space=pl.ANY),
            scratch_shapes=[pltpu.SMEM((tm * TOP_K,), i32), pltpu.VMEM((ROW_BLOCK * r, LANES), u32),
                            pltpu.SemaphoreType.DMA, pltpu.SemaphoreType.DMA, pltpu.SemaphoreType.DMA]),
        out_shape=jax.ShapeDtypeStruct((n_buf * r, LANES), u32),
        compiler_params=_params(("arbitrary",)),
        name="dispatch",
    )(last_block, n_used, dest_flat, h2p)


def _experts_kernel(be_ref, bv_ref, nu_ref, x_ref, wg_ref, wu_ref, wd_ref, o_ref, wg_s, wu_s, wd_s):
    j = pl.program_id(0)
    prev = be_ref[jnp.maximum(j - 1, 0)]
    used = j < nu_ref[0]
    d = wg_s.shape[0]

    @pl.when(used & ((j == 0) | (be_ref[j] != prev)))
    def _():
        wg_s[...] = wg_ref[0].astype(bf16)
        wu_s[...] = wu_ref[0].astype(bf16)
        wd_s[...] = wd_ref[0].astype(bf16)

    @pl.when(used)
    def _():
        live = lax.broadcasted_iota(i32, (ROW_BLOCK, 1), 0) < bv_ref[j]
        x = jnp.where(live, _load_packed_rows(x_ref, ROW_BLOCK, d), 0.0).astype(bf16)
        hg = jnp.dot(x, wg_s[...], preferred_element_type=f32)
        hu = jnp.dot(x, wu_s[...], preferred_element_type=f32)
        hid = (_silu(hg) * hu).astype(bf16)
        _store_packed_rows(o_ref, jnp.dot(hid, wd_s[...], preferred_element_type=f32))

    @pl.when(jnp.logical_not(used))
    def _():
        o_ref[...] = jnp.zeros(o_ref.shape, u32)


def _experts(blk_e, blk_valid, n_used, xs, w_gate, w_up, w_down):
    d, f = w_gate.shape[1], w_gate.shape[2]
    r = d // 2 // LANES
    n_blk = xs.shape[0] // r // ROW_BLOCK
    rows = lambda j, be, bv, nu: (jnp.minimum(j, nu[0] - 1), 0)
    wsel = lambda j, be, bv, nu: (be[j], 0, 0)
    return pl.pallas_call(
        _experts_kernel,
        grid_spec=pltpu.PrefetchScalarGridSpec(
            num_scalar_prefetch=3,
            grid=(n_blk,),
            in_specs=[pl.BlockSpec((ROW_BLOCK * r, LANES), rows),
                      pl.BlockSpec((1, d, f), wsel),
                      pl.BlockSpec((1, d, f), wsel),
                      pl.BlockSpec((1, f, d), wsel)],
            out_specs=pl.BlockSpec((ROW_BLOCK * r, LANES), lambda j, be, bv, nu: (j, 0)),
            scratch_shapes=[pltpu.VMEM((d, f), bf16), pltpu.VMEM((d, f), bf16), pltpu.VMEM((f, d), bf16)]),
        out_shape=jax.ShapeDtypeStruct(xs.shape, u32),
        compiler_params=_params(("arbitrary",)),
        name="experts",
    )(blk_e, blk_valid, n_used, xs, w_gate, w_up, w_down)


COMBINE_CHUNK = 64


def _combine_kernel(dest_hbm, ys_hbm, wt_ref, h_ref, x1_ref, mod_ref, wsg_ref, wsu_ref, wsd_ref,
                    lg_ref, lb_ref, o_ref, dest_a, dest_b, rows_a, rows_b, sem_d, sem_g):
    step = pl.program_id(0)
    n_steps = pl.num_programs(0)
    tm, d = x1_ref.shape
    r = d // 2 // LANES
    n = tm * TOP_K
    tables, rows = (dest_a, dest_b), (rows_a, rows_b)

    def table_copy(tile, slot):
        return pltpu.make_async_copy(dest_hbm.at[pl.ds(pl.multiple_of(tile * n, n), n)],
                                     tables[slot], sem_d.at[slot])

    def start_gathers(slot):
        def issue(tok, _):
            dst = pl.multiple_of(tok * r, r)
            for k in range(TOP_K):
                src = pl.multiple_of(tables[slot][tok * TOP_K + k] * r, r)
                pltpu.make_async_copy(ys_hbm.at[pl.ds(src, r)], rows[slot].at[k, pl.ds(dst, r)],
                                      sem_g.at[slot]).start(priority=k % 2)
            return 0

        lax.fori_loop(0, tm, issue, 0)

    @pl.when(step == 0)
    def _():
        table_copy(0, 0).start()
        table_copy(0, 0).wait()
        start_gathers(0)
        second = jnp.minimum(1, n_steps - 1)
        table_copy(second, 1).start()

        @pl.when(n_steps == 1)
        def _():
            table_copy(second, 1).wait()

    def tile_body(slot):
        other = 1 - slot

        @pl.when(step + 2 < n_steps)
        def _():
            table_copy(step + 2, slot).start()

        @pl.when(step + 1 < n_steps)
        def _():
            table_copy(step + 1, other).wait()

        h = _load_packed_rows(h_ref, tm, d).astype(bf16)
        hg = jnp.dot(h, wsg_ref[...], preferred_element_type=f32)
        hu = jnp.dot(h, wsu_ref[...], preferred_element_type=f32)
        shared = jnp.dot((_silu(hg) * hu).astype(bf16), wsd_ref[...], preferred_element_type=f32)
        pltpu.make_async_copy(rows[slot], rows[slot], sem_g.at[slot]).wait()
        m = mod_ref[0]
        for first in range(0, tm, COMBINE_CHUNK):
            for tok in range(first, first + COMBINE_CHUNK):
                for k in range(TOP_K):
                    src = pl.multiple_of(tables[other][tok * TOP_K + k] * r, r)
                    pltpu.make_async_copy(ys_hbm.at[pl.ds(src, r)], rows[other].at[k, pl.ds(tok * r, r)],
                                          sem_g.at[other]).start(priority=k % 2)
            sl = slice(first, first + COMBINE_CHUNK)
            wt = wt_ref[sl, :]
            lo = [jnp.zeros((COMBINE_CHUNK, LANES), f32) for _ in range(r)]
            hi = [jnp.zeros((COMBINE_CHUNK, LANES), f32) for _ in range(r)]
            for k in range(TOP_K):
                wk = wt[:, k:k + 1]
                for c in range(r):
                    pl_, ph_ = _unpack_words(rows[slot][k, pl.ds(first * r + c, COMBINE_CHUNK, stride=r), :])
                    lo[c] = lo[c] + wk * pl_
                    hi[c] = hi[c] + wk * ph_
            y = shared[sl] + jnp.concatenate(lo + hi, axis=1)
            o_ref[sl, :] = _layer_norm(DEEPNORM_ALPHA * x1_ref[sl, :] + m[5:6] * y, lg_ref[...], lb_ref[...])

        @pl.when(step + 1 == n_steps)
        def _():
            pltpu.make_async_copy(rows[other], rows[other], sem_g.at[other]).wait()

    for slot in range(2):
        pl.when(step % 2 == slot)(functools.partial(tile_body, slot))


def _combine(dest_flat, ys, wts, h2p, x1, mod, w_s_gate, w_s_up, w_s_down, ln_g, ln_b, s, tm):
    t, d = x1.shape
    per = s // tm
    r = d // 2 // LANES
    row = lambda a: a.reshape(1, -1)
    tile = lambda w: pl.BlockSpec((tm, w), lambda i: (i, 0))
    full = lambda a: pl.BlockSpec(a.shape, lambda i: (0,) * a.ndim)
    tail = (w_s_gate.astype(bf16), w_s_up.astype(bf16), w_s_down.astype(bf16), row(ln_g), row(ln_b))
    return pl.pallas_call(
        _combine_kernel,
        grid=(t // tm,),
        in_specs=[pl.BlockSpec(memory_space=pl.ANY), pl.BlockSpec(memory_space=pl.ANY),
                  tile(LANES), pl.BlockSpec((tm * r, LANES), lambda i: (i, 0)), tile(d),
                  pl.BlockSpec((1, 6, d), lambda i: (i // per, 0, 0))] + [full(a) for a in tail],
        out_specs=tile(d),
        out_shape=jax.ShapeDtypeStruct((t, d), f32),
        scratch_shapes=[pltpu.SMEM((tm * TOP_K,), i32), pltpu.SMEM((tm * TOP_K,), i32),
                        pltpu.VMEM((TOP_K, tm * r, LANES), u32), pltpu.VMEM((TOP_K, tm * r, LANES), u32),
                        pltpu.SemaphoreType.DMA((2,)), pltpu.SemaphoreType.DMA((2,))],
        compiler_params=_params(("arbitrary",)),
        name="combine",
    )(dest_flat, ys, wts, h2p, x1, mod, *tail)


def _token_tiles(s):
    return {"proj": min(512, s), "route": min(256, s), "dispatch": min(1024, s), "combine": min(256, s)}


def _layer(x, mod, w_in, pe, w_cmp1, w_cmp2, w_dw, b_dw, conv_ln_g, conv_ln_b, w_out, ln1_g, ln1_b,
           w_router, router_bias, w_e_gate, w_e_up, w_e_down, w_s_gate, w_s_up, w_s_down, ln2_g, ln2_b):
    b, s, d = x.shape
    t = b * s
    tiles = _token_tiles(s)
    qt, vt, gt, kvc, kn, conv_in = _in_proj(x, mod, w_in, tiles["proj"])
    cmp_n, cmp_t = _compress(kvc, pe, w_cmp1, w_cmp2)
    gates = gt[:, :3 * N_HEADS].reshape(b, N_KV, 3 * HPG, s)
    gates = jnp.pad(gates, ((0, 0), (0, 0), (0, GATE_GROUP_ROWS - 3 * HPG), (0, 0)))
    attn = _attention(qt, gates, cmp_n, cmp_t, kn, vt)
    conv = _conv(conv_in, w_dw, b_dw, conv_ln_g, conv_ln_b, tiles["proj"])

    tr = tiles["route"]
    x1, h2p, idx, wts, counts = _mix_route(attn.reshape(t, -1), conv.reshape(t, -1), x.reshape(t, d), mod,
                                           w_out, ln1_g, ln1_b, w_router, router_bias, s, tr)
    counts = counts[:, 0].astype(i32)
    padded = (counts + ROW_BLOCK - 1) // ROW_BLOCK * ROW_BLOCK
    seg_end = jnp.cumsum(padded)
    seg_start = seg_end - padded
    n_blk = -(-(t * TOP_K + N_EXPERTS * (ROW_BLOCK - 1)) // ROW_BLOCK)
    blk_row0 = jnp.arange(n_blk, dtype=i32) * ROW_BLOCK
    owns = (blk_row0[:, None] >= seg_start[None, :]) & (blk_row0[:, None] < seg_end[None, :])
    blk_e = jnp.where(blk_row0 < seg_end[-1], jnp.argmax(owns, axis=1), N_EXPERTS - 1).astype(i32)
    live_end = jnp.sum(jnp.where(owns, (seg_start + counts)[None, :], 0), axis=1)
    blk_valid = jnp.clip(live_end - blk_row0, 0, ROW_BLOCK).astype(i32)
    n_used = (seg_end[-1:] // ROW_BLOCK).astype(i32)

    dest = _positions(idx, seg_start.astype(f32).reshape(N_EXPERTS, 1), tr)
    dest_flat = dest.T.reshape(-1)
    wts = jnp.zeros((t, LANES), f32).at[:, :TOP_K].set(wts.T)
    last_block = jnp.where(padded > 0, seg_end // ROW_BLOCK - 1, -1).astype(i32)
    xs = _dispatch(last_block, n_used, dest_flat, h2p, n_blk * ROW_BLOCK, d // 2 // LANES, tiles["dispatch"])
    ys = _experts(blk_e, blk_valid, n_used, xs, w_e_gate, w_e_up, w_e_down)
    out = _combine(dest_flat, ys, wts, h2p, x1, mod, w_s_gate, w_s_up, w_s_down, ln2_g, ln2_b, s,
                   tiles["combine"])
    return out.reshape(b, s, d)


def kernel(x, c, w_ada, b_ada, w_in, pe_k, pe_v, w_cmp_k1, w_cmp_k2, w_cmp_v1, w_cmp_v2, w_dw, b_dw,
           conv_ln_g, conv_ln_b, w_out, ln1_g, ln1_b, w_router, router_bias, w_e_gate, w_e_up, w_e_down,
           w_s_gate, w_s_up, w_s_down, ln2_g, ln2_b):
    assert w_ada.shape[0] == DEPTH
    layer = lambda a: a.reshape(a.shape[1:])
    mod = _ada(c, layer(w_ada), layer(b_ada))
    return _layer(x, mod, layer(w_in), jnp.concatenate([pe_k, pe_v]),
                  jnp.concatenate([w_cmp_k1, w_cmp_v1]), jnp.concatenate([w_cmp_k2, w_cmp_v2]),
                  *[layer(a) for a in (w_dw, b_dw, conv_ln_g, conv_ln_b, w_out, ln1_g, ln1_b, w_router,
                                       router_bias, w_e_gate, w_e_up, w_e_down, w_s_gate, w_s_up, w_s_down,
                                       ln2_g, ln2_b)])
```
